```python
import math
import jax
import jax.numpy as jnp
from jax import lax
import numpy as np

D_MODEL = 2048
BATCH = 2
SEQ = 8192
DEPTH = 1
DEC_BATCH = 32
DEC_SEQ = 32
PAST_LEN = 2048

CHUNK = 64
N_HEADS = 32
N_KV_HEADS = 8
HEAD_DIM = 64
Q_PER_KV = N_HEADS // N_KV_HEADS
ATTN_WIDTH = N_HEADS * HEAD_DIM
KV_WIDTH = N_KV_HEADS * HEAD_DIM
WINDOW = 128
N_PREV_CHUNKS = WINDOW // CHUNK
ROPE_THETA = 500000.0
ROT_DIM = HEAD_DIM // 4
D_INNER = 2 * D_MODEL
SSM_HEAD_DIM = 64
SSM_HEADS = D_INNER // SSM_HEAD_DIM
SSM_GROUPS = 8
HEADS_PER_GROUP = SSM_HEADS // SSM_GROUPS
D_STATE = 128
CONV_W = 4
CONV_DIM = D_INNER + 2 * SSM_GROUPS * D_STATE
IN_WIDTH = ATTN_WIDTH + 2 * KV_WIDTH + D_INNER + CONV_DIM + SSM_HEADS
N_EXPERTS = 32
TOP_K = 4
D_FF = D_MODEL
SWIGLU_LIMIT = 7.0
SWIGLU_ALPHA = 1.702
MOE_BLOCK = 128
EPS = 1e-6
NEG_INF = -1e30

kernel_name = "hybrid_swa_ssd_moe_stream_step"


def rms_norm(x, w):
    xf = x.astype(jnp.float32)
    y = xf * lax.rsqrt(jnp.mean(xf * xf, axis=-1, keepdims=True) + EPS)
    return (y * w.astype(jnp.float32)).astype(x.dtype)


def apply_partial_rope(x, pos):
    half = ROT_DIM // 2
    inv_freq = ROPE_THETA ** (-jnp.arange(half, dtype=jnp.float32) * 2.0 / ROT_DIM)
    ang = pos.astype(jnp.float32)[:, None] * inv_freq[None, :]
    cos = jnp.cos(ang)[None, :, None, :]
    sin = jnp.sin(ang)[None, :, None, :]
    xr = x[..., :ROT_DIM].astype(jnp.float32)
    x1, x2 = xr[..., :half], xr[..., half:]
    rot = jnp.concatenate([x1 * cos - x2 * sin, x2 * cos + x1 * sin], axis=-1)
    return jnp.concatenate([rot.astype(x.dtype), x[..., ROT_DIM:]], axis=-1)


def sink_attend(q, k, v, sinks, key_valid):
    s = jnp.einsum('nqkgd,nskd->nkgqs', q.astype(jnp.float32), k.astype(jnp.float32)) * (HEAD_DIM ** -0.5)
    if key_valid is not None:
        s = jnp.where(key_valid[:, None, None, None, :], s, NEG_INF)
    sink = sinks.astype(jnp.float32).reshape(1, N_KV_HEADS, Q_PER_KV, 1, 1)
    m = jnp.maximum(jnp.max(s, axis=-1, keepdims=True), sink)
    p = jnp.exp(s - m)
    probs = p / (jnp.sum(p, axis=-1, keepdims=True) + jnp.exp(sink - m))
    o = jnp.einsum('nkgqs,nskd->nqkgd', probs, v.astype(jnp.float32))
    return o.astype(q.dtype)


def window_attention_prompt(q, k, v, sinks):
    bsz, seq = q.shape[:2]
    nc = seq // CHUNK
    band_len = (N_PREV_CHUNKS + 1) * CHUNK
    qb = q.reshape(bsz * nc, CHUNK, N_KV_HEADS, Q_PER_KV, HEAD_DIM)

    def band(t):
        tc = t.reshape(bsz, nc, CHUNK, N_KV_HEADS, HEAD_DIM)
        tp = jnp.pad(tc, ((0, 0), (N_PREV_CHUNKS, 0), (0, 0), (0, 0), (0, 0)))
        tb = jnp.concatenate([tp[:, j:j + nc] for j in range(N_PREV_CHUNKS + 1)], axis=2)
        return tb.reshape(bsz * nc, band_len, N_KV_HEADS, HEAD_DIM)

    kb, vb = band(k), band(v)
    key_chunk = jnp.arange(nc)[:, None] - N_PREV_CHUNKS + jnp.arange(N_PREV_CHUNKS + 1)[None, :]
    valid = jnp.repeat(key_chunk >= 0, CHUNK, axis=1)
    valid = jnp.broadcast_to(valid[None], (bsz, nc, band_len)).reshape(bsz * nc, band_len)
    o = sink_attend(qb, kb, vb, sinks, valid)
    return o.reshape(bsz, seq, ATTN_WIDTH)


def causal_depthwise_conv(xbc, past, w, b):
    xp = jnp.concatenate([past.astype(xbc.dtype), xbc], axis=1)
    out = lax.conv_general_dilated(xp, w[:, None, :].astype(xbc.dtype), window_strides=(1,), padding='VALID',
                                   dimension_numbers=('NWC', 'WIO', 'NWC'), feature_group_count=CONV_DIM)
    return out + b.astype(xbc.dtype), xp[:, -(CONV_W - 1):]


def ssd_chunked_scan(x, dt, a_head, b_in, c_in, h0):
    bsz, seq = x.shape[:2]
    q = CHUNK if seq % CHUNK == 0 else seq
    nc = seq // q

    def chunks(t):
        t = t.reshape(bsz, nc, q, *t.shape[2:])
        return jnp.moveaxis(t, 1, 0)

    xdt = (x.astype(jnp.float32) * dt[..., None]).reshape(bsz, seq, SSM_GROUPS, HEADS_PER_GROUP, SSM_HEAD_DIM)
    log_a = (dt * a_head).reshape(bsz, seq, SSM_GROUPS, HEADS_PER_GROUP)
    tril = jnp.tril(jnp.ones((q, q), dtype=bool))[None, :, :, None, None]

    def step(h, inp):
        xc, ac, bc, cc = inp
        acum = jnp.cumsum(ac, axis=1)
        seg = acum[:, :, None] - acum[:, None, :]
        decay_ts = jnp.exp(jnp.where(tril, seg, -jnp.inf))
        cb = jnp.einsum('btgn,bsgn->btsg', cc, bc)
        y = jnp.einsum('btsg,btsgr,bsgrp->btgrp', cb, decay_ts, xc)
        y = y + jnp.einsum('btgn,bgrpn->btgrp', cc, h) * jnp.exp(acum)[..., None]
        decay_end = jnp.exp(acum[:, -1:] - acum)
        h = h * jnp.exp(acum[:, -1])[..., None, None] + jnp.einsum('bsgn,bsgr,bsgrp->bgrpn', bc, decay_end, xc)
        return h, y

    h_init = h0.astype(jnp.float32).reshape(bsz, SSM_GROUPS, HEADS_PER_GROUP, SSM_HEAD_DIM, D_STATE)
    h_fin, ys = lax.scan(step, h_init, (chunks(xdt), chunks(log_a),
                                        chunks(b_in.astype(jnp.float32)), chunks(c_in.astype(jnp.float32))))
    y = jnp.moveaxis(ys, 0, 1).reshape(bsz, seq, SSM_HEADS, SSM_HEAD_DIM)
    return y, h_fin.reshape(bsz, SSM_HEADS, SSM_HEAD_DIM, D_STATE)


def gated_group_rms_norm(y, z, w):
    g = (y.astype(jnp.float32) * jax.nn.silu(z.astype(jnp.float32)))
    gg = g.reshape(*g.shape[:-1], SSM_GROUPS, D_INNER // SSM_GROUPS)
    gg = gg * lax.rsqrt(jnp.mean(gg * gg, axis=-1, keepdims=True) + EPS)
    return (gg.reshape(g.shape) * w.astype(jnp.float32)).astype(y.dtype)


def moe(h, w_router, b_router, w_e_gate, b_e_gate, w_e_up, b_e_up, w_e_down, b_e_down):
    t = h.reshape(-1, D_MODEL)
    n_tok = t.shape[0]
    logits = (t @ w_router + b_router).astype(jnp.float32)
    top_val, top_idx = lax.top_k(logits, TOP_K)
    gate = jax.nn.softmax(top_val, axis=-1)
    n_assign = n_tok * TOP_K
    e_flat = top_idx.reshape(-1).astype(jnp.int32)
    tok_flat = jnp.arange(n_assign, dtype=jnp.int32) // TOP_K
    order = jnp.argsort(e_flat)
    e_sorted = e_flat[order]
    counts = jnp.bincount(e_flat, length=N_EXPERTS).astype(jnp.int32)
    start = jnp.cumsum(counts) - counts
    padded = (counts + MOE_BLOCK - 1) // MOE_BLOCK * MOE_BLOCK
    pad_end = jnp.cumsum(padded)
    pad_start = pad_end - padded
    dest = pad_start[e_sorted] + jnp.arange(n_assign, dtype=jnp.int32) - start[e_sorted]
    n_blocks = -(-(n_assign + N_EXPERTS * (MOE_BLOCK - 1)) // MOE_BLOCK)
    n_rows = n_blocks * MOE_BLOCK
    row_tok = jnp.zeros((n_rows,), jnp.int32).at[dest].set(tok_flat[order])
    row_w = jnp.zeros((n_rows,), jnp.float32).at[dest].set(gate.reshape(-1)[order])
    block_e = jnp.minimum(jnp.searchsorted(pad_end, jnp.arange(n_blocks, dtype=jnp.int32) * MOE_BLOCK,
                                           side='right'), N_EXPERTS - 1)

    def expert_block(args):
        tok, wts, e = args
        xb = t[tok]
        g = jnp.minimum(xb @ w_e_gate[e] + b_e_gate[e], SWIGLU_LIMIT)
        u = jnp.clip(xb @ w_e_up[e] + b_e_up[e], -SWIGLU_LIMIT, SWIGLU_LIMIT)
        act = (u + 1.0) * (g * jax.nn.sigmoid(SWIGLU_ALPHA * g))
        out = act @ w_e_down[e] + b_e_down[e]
        return out * wts[:, None].astype(out.dtype)

    outs = lax.map(expert_block, (row_tok.reshape(n_blocks, MOE_BLOCK), row_w.reshape(n_blocks, MOE_BLOCK), block_e))
    y = jnp.zeros_like(t).at[row_tok].add(outs.reshape(n_rows, D_MODEL).astype(t.dtype))
    return y.reshape(h.shape)


def hybrid_layer(x, pos, k_past, v_past, conv_past, ssm_past,
                 norm1, w_in, q_norm, k_norm, sinks, conv_w, conv_b, dt_bias, a_log, d_skip, ssm_norm,
                 w_gate, b_gate, w_attn_up, w_ssm_up, w_out, norm2, w_router, b_router,
                 w_e_gate, b_e_gate, w_e_up, b_e_up, w_e_down, b_e_down):
    bsz, seq, _ = x.shape
    h = rms_norm(x, norm1)
    proj = h @ w_in
    o1 = ATTN_WIDTH
    o2 = o1 + KV_WIDTH
    o3 = o2 + KV_WIDTH
    o4 = o3 + D_INNER
    o5 = o4 + CONV_DIM
    q_p, k_p, v_p, z, xbc, dt_raw = jnp.split(proj, [o1, o2, o3, o4, o5], axis=-1)

    q = apply_partial_rope(rms_norm(q_p.reshape(bsz, seq, N_HEADS, HEAD_DIM), q_norm), pos)
    k = apply_partial_rope(rms_norm(k_p.reshape(bsz, seq, N_KV_HEADS, HEAD_DIM), k_norm), pos)
    v = v_p.reshape(bsz, seq, N_KV_HEADS, HEAD_DIM)
    if k_past is None:
        attn = window_attention_prompt(q, k, v, sinks)
        n_keep = WINDOW
        k_all, v_all = k, v
    else:
        k_all = jnp.concatenate([k_past.astype(k.dtype), k], axis=1)
        v_all = jnp.concatenate([v_past.astype(v.dtype), v], axis=1)
        qb = q.reshape(bsz, seq, N_KV_HEADS, Q_PER_KV, HEAD_DIM)
        attn = sink_attend(qb, k_all, v_all, sinks, None).reshape(bsz, seq, ATTN_WIDTH)
        n_keep = k_past.shape[1]
    new_k = k_all[:, -n_keep:]
    new_v = v_all[:, -n_keep:]

    if conv_past is None:
        conv_past = jnp.zeros((bsz, CONV_W - 1, CONV_DIM), x.dtype)
        ssm_past = jnp.zeros((bsz, SSM_HEADS, SSM_HEAD_DIM, D_STATE), x.dtype)
    conv_out, new_conv = causal_depthwise_conv(xbc, conv_past, conv_w, conv_b)
    conv_out = jax.nn.silu(conv_out)
    xs, bs, cs = jnp.split(conv_out, [D_INNER, D_INNER + SSM_GROUPS * D_STATE], axis=-1)
    xs = xs.reshape(bsz, seq, SSM_HEADS, SSM_HEAD_DIM)
    bs = bs.reshape(bsz, seq, SSM_GROUPS, D_STATE)
    cs = cs.reshape(bsz, seq, SSM_GROUPS, D_STATE)
    dt = jax.nn.softplus(dt_raw.astype(jnp.float32) + dt_bias.astype(jnp.float32))
    a_head = -jnp.exp(a_log.astype(jnp.float32))
    y_ssm, new_ssm = ssd_chunked_scan(xs, dt, a_head, bs, cs, ssm_past)
    y_ssm = y_ssm + xs.astype(jnp.float32) * d_skip.astype(jnp.float32)[:, None]
    y_ssm = y_ssm.astype(x.dtype).reshape(bsz, seq, D_INNER)
    ssm_out = gated_group_rms_norm(y_ssm, z, ssm_norm)

    gates = jax.nn.sigmoid((h @ w_gate + b_gate).astype(jnp.float32)).astype(x.dtype)
    g_attn, g_ssm = jnp.split(gates, 2, axis=-1)
    merged = g_attn * (attn @ w_attn_up) + g_ssm * (ssm_out @ w_ssm_up)
    x = x + merged @ w_out

    x = x + moe(rms_norm(x, norm2), w_router, b_router, w_e_gate, b_e_gate, w_e_up, b_e_up, w_e_down, b_e_down)
    return x, new_k, new_v, new_conv, new_ssm.astype(x.dtype)


def setup_inputs(seed: int = 0) -> dict:
    key = jax.random.key(seed)
    ks = iter(jax.random.split(key, 40))
    f32 = jnp.float32

    def nrm(shape, scale):
        return scale * jax.random.normal(next(ks), shape, f32)

    def gain(shape):
        return 1.0 + nrm(shape, 0.02)

    kv_len = min(WINDOW, PAST_LEN)
    dt0 = jnp.exp(jax.random.uniform(next(ks), (DEPTH, SSM_HEADS), f32, math.log(1e-3), math.log(1e-1)))
    dt_bias = dt0 + jnp.log(-jnp.expm1(-dt0))
    a_log = jnp.log(jax.random.uniform(next(ks), (DEPTH, SSM_HEADS), f32, 1.0, 16.0))
    return {
        'x_prompt': nrm((BATCH, SEQ, D_MODEL), 1.0),
        'x_sample': nrm((DEC_BATCH, DEC_SEQ, D_MODEL), 1.0),
        'cache_k': nrm((DEPTH, DEC_BATCH, kv_len, N_KV_HEADS, HEAD_DIM), 1.0),
        'cache_v': nrm((DEPTH, DEC_BATCH, kv_len, N_KV_HEADS, HEAD_DIM), 1.0),
        'state_conv': nrm((DEPTH, DEC_BATCH, CONV_W - 1, CONV_DIM), 1.0),
        'state_ssm': nrm((DEPTH, DEC_BATCH, SSM_HEADS, SSM_HEAD_DIM, D_STATE), 0.1),
        'norm1': gain((DEPTH, D_MODEL)),
        'w_in': nrm((DEPTH, D_MODEL, IN_WIDTH), D_MODEL ** -0.5),
        'q_norm': gain((DEPTH, HEAD_DIM)),
        'k_norm': gain((DEPTH, HEAD_DIM)),
        'sinks': nrm((DEPTH, N_HEADS), 0.5),
        'conv_w': nrm((DEPTH, CONV_W, CONV_DIM), CONV_W ** -0.5),
        'conv_b': nrm((DEPTH, CONV_DIM), 0.02),
        'dt_bias': dt_bias,
        'a_log': a_log,
        'd_skip': 1.0 + nrm((DEPTH, SSM_HEADS), 0.1),
        'ssm_norm': gain((DEPTH, D_INNER)),
        'w_gate': nrm((DEPTH, D_MODEL, 2 * D_MODEL), D_MODEL ** -0.5),
        'b_gate': nrm((DEPTH, 2 * D_MODEL), 0.02),
        'w_attn_up': nrm((DEPTH, ATTN_WIDTH, D_MODEL), ATTN_WIDTH ** -0.5),
        'w_ssm_up': nrm((DEPTH, D_INNER, D_MODEL), D_INNER ** -0.5),
        'w_out': nrm((DEPTH, D_MODEL, D_MODEL), D_MODEL ** -0.5),
        'norm2': gain((DEPTH, D_MODEL)),
        'w_router': nrm((DEPTH, D_MODEL, N_EXPERTS), D_MODEL ** -0.5),
        'b_router': nrm((DEPTH, N_EXPERTS), 0.01),
        'w_e_gate': nrm((DEPTH, N_EXPERTS, D_MODEL, D_FF), D_MODEL ** -0.5),
        'b_e_gate': nrm((DEPTH, N_EXPERTS, D_FF), 0.02),
        'w_e_up': nrm((DEPTH, N_EXPERTS, D_MODEL, D_FF), D_MODEL ** -0.5),
        'b_e_up': nrm((DEPTH, N_EXPERTS, D_FF), 0.02),
        'w_e_down': nrm((DEPTH, N_EXPERTS, D_FF, D_MODEL), D_FF ** -0.5),
        'b_e_down': nrm((DEPTH, N_EXPERTS, D_MODEL), 0.02),
    }


def reference(x_prompt, x_sample, cache_k, cache_v, state_conv, state_ssm,
              norm1, w_in, q_norm, k_norm, sinks, conv_w, conv_b, dt_bias, a_log, d_skip, ssm_norm,
              w_gate, b_gate, w_attn_up, w_ssm_up, w_out, norm2, w_router, b_router,
              w_e_gate, b_e_gate, w_e_up, b_e_up, w_e_down, b_e_down):
    pos_prompt = jnp.arange(x_prompt.shape[1], dtype=jnp.int32)
    pos_sample = PAST_LEN + jnp.arange(x_sample.shape[1], dtype=jnp.int32)
    yp, ys = x_prompt, x_sample
    kp, vp, cp, sp = [], [], [], []
    kq, vq, cq, sq = [], [], [], []
    for l in range(DEPTH):
        wl = (norm1[l], w_in[l], q_norm[l], k_norm[l], sinks[l], conv_w[l], conv_b[l], dt_bias[l], a_log[l],
              d_skip[l], ssm_norm[l], w_gate[l], b_gate[l], w_attn_up[l], w_ssm_up[l], w_out[l], norm2[l],
              w_router[l], b_router[l], w_e_gate[l], b_e_gate[l], w_e_up[l], b_e_up[l], w_e_down[l], b_e_down[l])
        yp, k1, v1, c1, s1 = hybrid_layer(yp, pos_prompt, None, None, None, None, *wl)
        ys, k2, v2, c2, s2 = hybrid_layer(ys, pos_sample, cache_k[l], cache_v[l], state_conv[l], state_ssm[l], *wl)
        kp.append(k1); vp.append(v1); cp.append(c1); sp.append(s1)
        kq.append(k2); vq.append(v2); cq.append(c2); sq.append(s2)
    return (yp, ys,
            jnp.stack(kp), jnp.stack(vp), jnp.stack(cp), jnp.stack(sp),
            jnp.stack(kq), jnp.stack(vq), jnp.stack(cq), jnp.stack(sq))
```

```python
import functools
import math

import jax
import jax.numpy as jnp
from jax import lax
from jax.experimental import pallas as pl
from jax.experimental.pallas import tpu as pltpu

F32 = jnp.float32
BF16 = jnp.bfloat16
I32 = jnp.int32

D_MODEL = 2048
CHUNK = 64
N_HEADS = 32
N_KV_HEADS = 8
HEAD_DIM = 64
Q_PER_KV = N_HEADS // N_KV_HEADS
ATTN_WIDTH = N_HEADS * HEAD_DIM
KV_WIDTH = N_KV_HEADS * HEAD_DIM
WINDOW = 128
N_PREV_CHUNKS = WINDOW // CHUNK
ROPE_THETA = 500000.0
ROT_DIM = HEAD_DIM // 4
D_INNER = 2 * D_MODEL
SSM_HEAD_DIM = 64
SSM_HEADS = D_INNER // SSM_HEAD_DIM
SSM_GROUPS = 8
D_STATE = 128
CONV_W = 4
BC_WIDTH = SSM_GROUPS * D_STATE
CONV_DIM = D_INNER + 2 * BC_WIDTH
N_EXPERTS = 32
TOP_K = 4
D_FF = D_MODEL
SWIGLU_LIMIT = 7.0
SWIGLU_ALPHA = 1.702
EPS = 1e-6
NEG_INF = -1e30
PAST_LEN = 2048

LANES = 128
VMEM_LIMIT = 56 * 1024 * 1024
MOE_TM = 512
MOE_TF = 256


def _pick(n, cands):
    for c in cands:
        if n % c == 0:
            return c
    return n


def _cparams(sem):
    return pltpu.CompilerParams(dimension_semantics=sem, vmem_limit_bytes=VMEM_LIMIT)


def _split2(x):
    hi = x.astype(BF16)
    lo = (x - hi.astype(F32)).astype(BF16)
    return hi, lo


def _split3(x):
    hi = x.astype(BF16)
    r = x - hi.astype(F32)
    mid = r.astype(BF16)
    lo = (r - mid.astype(F32)).astype(BF16)
    return hi, mid, lo


def _dot(a, b):
    return jnp.dot(a, b, preferred_element_type=F32)


def _dot_nt(a, b):
    return lax.dot_general(a, b, (((1,), (1,)), ((), ())), preferred_element_type=F32)


def _silu(x):
    return x * jax.nn.sigmoid(x)


def _rms_kernel(x_ref, w_ref, o_ref):
    x = x_ref[...]
    ms = jnp.mean(x * x, axis=-1, keepdims=True)
    o_ref[...] = (x * lax.rsqrt(ms + EPS) * w_ref[...]).astype(o_ref.dtype)


def _rmsnorm(x, w, out_dtype):
    m, d = x.shape
    tm = _pick(m, (512, 256, 128, 64, 32))
    return pl.pallas_call(
        _rms_kernel,
        grid=(m // tm,),
        in_specs=[pl.BlockSpec((tm, d), lambda i: (i, 0)), pl.BlockSpec((1, d), lambda i: (0, 0))],
        out_specs=pl.BlockSpec((tm, d), lambda i: (i, 0)),
        out_shape=jax.ShapeDtypeStruct((m, d), out_dtype),
        compiler_params=_cparams(("parallel",)),
        name="rmsnorm",
    )(x, w.reshape(1, d))


def _mm_kernel(x_ref, w_ref, o_ref):
    o_ref[...] = _dot(x_ref[...], w_ref[...]).astype(o_ref.dtype)


def _matmul(x, w, out_dtype, name):
    m, k = x.shape
    n = w.shape[1]
    tm = _pick(m, (1024, 512, 256, 128, 64, 32))
    tn = _pick(n, (1024, 512, 256, 128))
    return pl.pallas_call(
        _mm_kernel,
        grid=(m // tm, n // tn),
        in_specs=[pl.BlockSpec((tm, k), lambda i, j: (i, 0)), pl.BlockSpec((k, tn), lambda i, j: (0, j))],
        out_specs=pl.BlockSpec((tm, tn), lambda i, j: (i, j)),
        out_shape=jax.ShapeDtypeStruct((m, n), out_dtype),
        compiler_params=_cparams(("parallel", "parallel")),
        name=name,
    )(x, w)


def _qk_prep_kernel(q_ref, k_ref, cos_ref, s1_ref, s2_ref, qn_ref, kn_ref, g_ref, gt_ref, qo_ref, ko_ref):
    cos = cos_ref[...]
    s1 = s1_ref[...]
    s2 = s2_ref[...]

    def norm_rope(x, nw, width):
        g = g_ref[0:width, :]
        gt = gt_ref[:, 0:width]
        sq_hi, sq_lo = _split2(x * x)
        ssum = _dot(sq_hi, g) + _dot(sq_lo, g)
        r = lax.rsqrt(ssum * (1.0 / HEAD_DIM) + EPS)
        r_hi, r_lo = _split2(r)
        y = x * (_dot(r_hi, gt) + _dot(r_lo, gt)) * nw
        outs = []
        for s in range(width // LANES):
            blk = y[:, s * LANES:(s + 1) * LANES]
            outs.append(blk * cos + pltpu.roll(blk, ROT_DIM // 2, 1) * s1
                        + pltpu.roll(blk, LANES - ROT_DIM // 2, 1) * s2)
        return jnp.concatenate(outs, axis=1)

    q = norm_rope(q_ref[...], qn_ref[...], ATTN_WIDTH)
    qo_ref[...] = (q * (HEAD_DIM ** -0.5)).astype(qo_ref.dtype)
    ko_ref[...] = norm_rope(k_ref[...], kn_ref[...], KV_WIDTH)


def _qk_prep(q, k, pos, q_norm, k_norm):
    m = q.shape[0]
    tm = _pick(m, (256, 128, 64, 32))
    half = ROT_DIM // 2
    inv_freq = ROPE_THETA ** (-jnp.arange(half, dtype=F32) * 2.0 / ROT_DIM)
    ang = pos.astype(F32)[:, None] * inv_freq[None, :]
    cos, sin = jnp.cos(ang), jnp.sin(ang)
    ones = jnp.ones((m, HEAD_DIM - ROT_DIM), F32)
    zeros = jnp.zeros((m, HEAD_DIM - ROT_DIM), F32)
    zh = jnp.zeros((m, half), F32)
    cos_t = jnp.tile(jnp.concatenate([cos, cos, ones], axis=1), (1, LANES // HEAD_DIM))
    s1_t = jnp.tile(jnp.concatenate([zh, sin, zeros], axis=1), (1, LANES // HEAD_DIM))
    s2_t = jnp.tile(jnp.concatenate([-sin, zh, zeros], axis=1), (1, LANES // HEAD_DIM))
    head_of_lane = jnp.arange(ATTN_WIDTH) // HEAD_DIM
    g = (head_of_lane[:, None] == jnp.arange(LANES)[None, :]).astype(BF16)
    gt = g.T
    qn = jnp.tile(q_norm.astype(F32), N_HEADS).reshape(1, ATTN_WIDTH)
    kn = jnp.tile(k_norm.astype(F32), N_KV_HEADS).reshape(1, KV_WIDTH)
    row = lambda w: pl.BlockSpec((tm, w), lambda i: (i, 0))
    full = lambda a: pl.BlockSpec(a.shape, lambda i: (0, 0))
    return pl.pallas_call(
        _qk_prep_kernel,
        grid=(m // tm,),
        in_specs=[row(ATTN_WIDTH), row(KV_WIDTH), row(LANES), row(LANES), row(LANES),
                  full(qn), full(kn), full(g), full(gt)],
        out_specs=[row(ATTN_WIDTH), row(KV_WIDTH)],
        out_shape=[jax.ShapeDtypeStruct((m, ATTN_WIDTH), BF16), jax.ShapeDtypeStruct((m, KV_WIDTH), F32)],
        compiler_params=_cparams(("parallel",)),
        name="qk_prep",
    )(q, k, cos_t, s1_t, s2_t, qn, kn, g, gt)


def _attend(q, kk, vv, sinks_ref, valid):
    tq = q.shape[0]
    outs = []
    for j in range(N_KV_HEADS):
        kj = kk[:, j * HEAD_DIM:(j + 1) * HEAD_DIM]
        vj = vv[:, j * HEAD_DIM:(j + 1) * HEAD_DIM]
        heads = [Q_PER_KV * j + g for g in range(Q_PER_KV)]
        q4 = jnp.concatenate([q[:, h * HEAD_DIM:(h + 1) * HEAD_DIM] for h in heads], axis=0)
        s = _dot_nt(q4, kj)
        if valid is not None:
            s = jnp.where(valid, s, NEG_INF)
        sink = jnp.concatenate([jnp.full((tq, 1), sinks_ref[h], F32) for h in heads], axis=0)
        m = jnp.maximum(jnp.max(s, axis=-1, keepdims=True), sink)
        p = jnp.exp(s - m)
        denom = jnp.sum(p, axis=-1, keepdims=True) + jnp.exp(sink - m)
        o = _dot(p.astype(BF16), vj) / denom
        for g in range(Q_PER_KV):
            outs.append(o[g * tq:(g + 1) * tq, :])
    return jnp.concatenate(outs, axis=1)


def _attn_prompt_kernel(sinks_ref, q_ref, k0_ref, k1_ref, k2_ref, v0_ref, v1_ref, v2_ref, o_ref):
    c = pl.program_id(1)
    kk = jnp.concatenate([k0_ref[...], k1_ref[...], k2_ref[...]], axis=0).astype(BF16)
    vv = jnp.concatenate([v0_ref[...], v1_ref[...], v2_ref[...]], axis=0).astype(BF16)
    key_chunk = lax.broadcasted_iota(I32, (1, 3 * CHUNK), 1) // CHUNK + (c - N_PREV_CHUNKS)
    o_ref[...] = _attend(q_ref[...], kk, vv, sinks_ref, key_chunk >= 0).astype(o_ref.dtype)


def _attn_prompt(q, k, v, sinks, bsz, seq):
    nc = seq // CHUNK

    def kv_spec(back):
        return pl.BlockSpec((CHUNK, KV_WIDTH), lambda b, c, s: (b * nc + jnp.maximum(c - back, 0), 0))

    grid_spec = pltpu.PrefetchScalarGridSpec(
        num_scalar_prefetch=1,
        grid=(bsz, nc),
        in_specs=[pl.BlockSpec((CHUNK, ATTN_WIDTH), lambda b, c, s: (b * nc + c, 0)),
                  kv_spec(2), kv_spec(1), kv_spec(0), kv_spec(2), kv_spec(1), kv_spec(0)],
        out_specs=pl.BlockSpec((CHUNK, ATTN_WIDTH), lambda b, c, s: (b * nc + c, 0)),
    )
    return pl.pallas_call(
        _attn_prompt_kernel,
        grid_spec=grid_spec,
        out_shape=jax.ShapeDtypeStruct((bsz * seq, ATTN_WIDTH), BF16),
        compiler_params=_cparams(("parallel", "parallel")),
        name="attn_prompt",
    )(sinks, q, k, k, k, v, v, v)


def _attn_sample_kernel(sinks_ref, q_ref, kn_ref, vn_ref, kc_ref, vc_ref, o_ref):
    kk = jnp.concatenate([kc_ref[0], kn_ref[...]], axis=0).astype(BF16)
    vv = jnp.concatenate([vc_ref[0], vn_ref[...]], axis=0).astype(BF16)
    o_ref[...] = _attend(q_ref[...], kk, vv, sinks_ref, None).astype(o_ref.dtype)


def _attn_sample(q, k, v, cache_k, cache_v, sinks, row0, bsz, seq):
    blk0 = row0 // seq
    kv_len = cache_k.shape[1]
    new = lambda w: pl.BlockSpec((seq, w), lambda b, s: (blk0 + b, 0))
    cache = pl.BlockSpec((1, kv_len, KV_WIDTH), lambda b, s: (b, 0, 0))
    grid_spec = pltpu.PrefetchScalarGridSpec(
        num_scalar_prefetch=1,
        grid=(bsz,),
        in_specs=[new(ATTN_WIDTH), new(KV_WIDTH), new(KV_WIDTH), cache, cache],
        out_specs=pl.BlockSpec((seq, ATTN_WIDTH), lambda b, s: (b, 0)),
    )
    return pl.pallas_call(
        _attn_sample_kernel,
        grid_spec=grid_spec,
        out_shape=jax.ShapeDtypeStruct((bsz * seq, ATTN_WIDTH), BF16),
        compiler_params=_cparams(("parallel",)),
        name="attn_sample",
    )(sinks, q, k, v, cache_k, cache_v)


def _pad_rows(x, rows):
    if x.shape[0] == rows:
        return x
    return jnp.concatenate([x, jnp.zeros((rows - x.shape[0], x.shape[1]), x.dtype)], axis=0)


def _ssd_kernel(*refs, clen, has_past):
    if has_past:
        (xbc_ref, dt_ref, z_ref, cw_ref, cb_ref, dtb_ref, alog_ref, dskip_ref, nw_ref, cpast_ref, hpast_ref,
         y_ref, hout_ref, xpad, u_ref, g_ref, ht_ref) = refs
    else:
        (xbc_ref, dt_ref, z_ref, cw_ref, cb_ref, dtb_ref, alog_ref, dskip_ref, nw_ref,
         y_ref, hout_ref, xpad, u_ref, g_ref, ht_ref) = refs
    L = clen
    c = pl.program_id(1)
    nc = pl.num_programs(1)
    n_pairs = SSM_HEADS // 2
    pairs_per_group = n_pairs // SSM_GROUPS
    gw = D_INNER // SSM_GROUPS

    @pl.when(c == 0)
    def _init():
        if has_past:
            xpad[5:8, :] = cpast_ref[0]
            for p in range(n_pairs):
                both = jnp.concatenate([hpast_ref[0, 2 * p], hpast_ref[0, 2 * p + 1]], axis=0)
                ht_ref[:, p * LANES:(p + 1) * LANES] = both.T
        else:
            xpad[0:8, :] = jnp.zeros((8, CONV_DIM), F32)
            ht_ref[...] = jnp.zeros(ht_ref.shape, F32)

    xpad[8:8 + L, :] = xbc_ref[...]
    cblk = 512
    for j in range(CONV_DIM // cblk):
        sl = slice(j * cblk, (j + 1) * cblk)
        acc = cb_ref[:, sl] + xpad[8:8 + L, sl] * cw_ref[3:4, sl]
        acc = acc + xpad[7:7 + L, sl] * cw_ref[2:3, sl]
        acc = acc + xpad[6:6 + L, sl] * cw_ref[1:2, sl]
        acc = acc + xpad[5:5 + L, sl] * cw_ref[0:1, sl]
        u_ref[:, sl] = _silu(acc)
    xpad[5:8, :] = xpad[5 + L:8 + L, :]

    dtx = dt_ref[...] + dtb_ref[...]
    dt = jnp.maximum(dtx, 0.0) + jnp.log(1.0 + jnp.exp(-jnp.abs(dtx)))
    loga = dt * (-jnp.exp(alog_ref[...]))
    t_idx = lax.broadcasted_iota(I32, (L, L), 0)
    s_idx = lax.broadcasted_iota(I32, (L, L), 1)
    incl = (s_idx <= t_idx).astype(BF16)
    acum = sum(_dot(incl, part) for part in _split3(loga))
    eacum = jnp.exp(acum)
    a_last = acum[L - 1:L, :]
    e_last = jnp.exp(a_last)
    wend = dt * jnp.exp(a_last - acum)
    acum_t = _pad_rows(acum, LANES).T[:, 0:L]

    lane_m = lax.broadcasted_iota(I32, (L, 2 * L), 1)
    row_m = lax.broadcasted_iota(I32, (L, 2 * L), 0)
    left_m = lane_m < L
    causal_m = jnp.where(left_m, lane_m, lane_m - L) <= row_m
    left = lax.broadcasted_iota(I32, (L, LANES), 1) < SSM_HEAD_DIM
    left_n = lax.broadcasted_iota(I32, (D_STATE, LANES), 1) < SSM_HEAD_DIM
    left_1 = lax.broadcasted_iota(I32, (1, LANES), 1) < SSM_HEAD_DIM

    for grp in range(SSM_GROUPS):
        b_g = u_ref[:, D_INNER + grp * D_STATE:D_INNER + (grp + 1) * D_STATE]
        c_g = u_ref[:, D_INNER + BC_WIDTH + grp * D_STATE:D_INNER + BC_WIDTH + (grp + 1) * D_STATE]
        b_bf = b_g.astype(BF16)
        cb = _dot_nt(c_g.astype(BF16), b_bf)
        cb2 = jnp.concatenate([cb, cb], axis=1)
        bt_bf = _pad_rows(b_g, LANES).T[:, 0:L].astype(BF16)
        for pp in range(pairs_per_group):
            p = grp * pairs_per_group + pp
            h0, h1 = 2 * p, 2 * p + 1
            sl = slice(p * LANES, (p + 1) * LANES)
            x_pair = u_ref[:, sl]
            colpair = lambda qty: jnp.where(left, qty[:, h0:h0 + 1], qty[:, h1:h1 + 1])
            xdt = x_pair * colpair(dt)
            col_a = jnp.where(left_m, acum[:, h0:h0 + 1], acum[:, h1:h1 + 1])
            row_a = jnp.concatenate([acum_t[h0:h0 + 1, :], acum_t[h1:h1 + 1, :]], axis=1)
            dec = jnp.exp(jnp.where(causal_m, col_a - row_a, NEG_INF))
            m_pair = (cb2 * dec).astype(BF16)
            e0 = (eacum[:, h0:h0 + 1] * c_g).astype(BF16)
            e1 = (eacum[:, h1:h1 + 1] * c_g).astype(BF16)
            lhs = jnp.concatenate([m_pair, e0, e1], axis=1)
            h_pair = ht_ref[:, sl]
            rhs = jnp.concatenate([jnp.where(left, xdt, 0.0), jnp.where(left, 0.0, xdt),
                                   jnp.where(left_n, h_pair, 0.0), jnp.where(left_n, 0.0, h_pair)],
                                  axis=0).astype(BF16)
            y = _dot(lhs, rhs) + x_pair * dskip_ref[:, sl]
            g_ref[:, sl] = y * _silu(z_ref[:, sl].astype(F32))
            xw = (x_pair * colpair(wend)).astype(BF16)
            row_dec = jnp.where(left_1, e_last[:, h0:h0 + 1], e_last[:, h1:h1 + 1])
            ht_ref[:, sl] = h_pair * row_dec + _dot(bt_bf, xw)
        gsl = slice(grp * gw, (grp + 1) * gw)
        gg = g_ref[:, gsl]
        ms = jnp.mean(gg * gg, axis=-1, keepdims=True)
        y_ref[:, gsl] = (gg * lax.rsqrt(ms + EPS) * nw_ref[:, gsl]).astype(y_ref.dtype)

    @pl.when(c == nc - 1)
    def _fin():
        for p in range(n_pairs):
            both = ht_ref[:, p * LANES:(p + 1) * LANES].T
            hout_ref[0, 2 * p] = both[0:SSM_HEAD_DIM, :]
            hout_ref[0, 2 * p + 1] = both[SSM_HEAD_DIM:2 * SSM_HEAD_DIM, :]


def _ssd(xbc, dt_raw, z, conv_w, conv_b, dt_bias, a_log, d_skip, ssm_norm, row0, bsz, seq, clen,
         conv_past=None, ssm_past=None):
    nc = seq // clen
    blk0 = row0 // clen
    has_past = conv_past is not None
    pad = lambda a: jnp.pad(a.astype(F32), (0, LANES - SSM_HEADS)).reshape(1, LANES)
    params = [conv_w.astype(F32), conv_b.astype(F32).reshape(1, CONV_DIM), pad(dt_bias), pad(a_log),
              jnp.repeat(d_skip.astype(F32), SSM_HEAD_DIM).reshape(1, D_INNER), ssm_norm.astype(F32).reshape(1, D_INNER)]
    rows = lambda w: pl.BlockSpec((clen, w), lambda b, c: (blk0 + b * nc + c, 0))
    full = lambda a: pl.BlockSpec(a.shape, lambda b, c: (0,) * a.ndim)
    in_specs = [rows(CONV_DIM), rows(LANES), rows(D_INNER)] + [full(a) for a in params]
    args = [xbc, dt_raw, z] + params
    if has_past:
        in_specs += [pl.BlockSpec((1, CONV_W - 1, CONV_DIM), lambda b, c: (b, 0, 0)),
                     pl.BlockSpec((1, SSM_HEADS, SSM_HEAD_DIM, D_STATE), lambda b, c: (b, 0, 0, 0))]
        args += [conv_past.astype(F32), ssm_past.astype(F32)]
    return pl.pallas_call(
        functools.partial(_ssd_kernel, clen=clen, has_past=has_past),
        grid=(bsz, nc),
        in_specs=in_specs,
        out_specs=[pl.BlockSpec((clen, D_INNER), lambda b, c: (b * nc + c, 0)),
                   pl.BlockSpec((1, SSM_HEADS, SSM_HEAD_DIM, D_STATE), lambda b, c: (b, 0, 0, 0))],
        out_shape=[jax.ShapeDtypeStruct((bsz * seq, D_INNER), BF16),
                   jax.ShapeDtypeStruct((bsz, SSM_HEADS, SSM_HEAD_DIM, D_STATE), F32)],
        scratch_shapes=[pltpu.VMEM((8 + clen, CONV_DIM), F32), pltpu.VMEM((clen, CONV_DIM), F32),
                        pltpu.VMEM((clen, D_INNER), F32), pltpu.VMEM((D_STATE, D_INNER), F32)],
        compiler_params=_cparams(("parallel", "arbitrary")),
        name="ssd_past" if has_past else "ssd_prompt",
    )(*args)


def _merge_kernel(h_ref, a_ref, s_ref, wga_ref, wgs_ref, bga_ref, bgs_ref, wa_ref, ws_ref, o_ref):
    h = h_ref[...]
    g_a = jax.nn.sigmoid(_dot(h, wga_ref[...]) + bga_ref[...])
    g_s = jax.nn.sigmoid(_dot(h, wgs_ref[...]) + bgs_ref[...])
    o_ref[...] = (g_a * _dot(a_ref[...], wa_ref[...]) + g_s * _dot(s_ref[...], ws_ref[...])).astype(o_ref.dtype)


def _merge(h, attn, ssm, w_gate, b_gate, w_attn_up, w_ssm_up):
    m = h.shape[0]
    tm = _pick(m, (512, 256, 128, 64, 32))
    tn = 512
    nj = D_MODEL // tn
    bg = b_gate.astype(F32).reshape(1, 2 * D_MODEL)
    return pl.pallas_call(
        _merge_kernel,
        grid=(m // tm, nj),
        in_specs=[pl.BlockSpec((tm, D_MODEL), lambda i, j: (i, 0)),
                  pl.BlockSpec((tm, ATTN_WIDTH), lambda i, j: (i, 0)),
                  pl.BlockSpec((tm, D_INNER), lambda i, j: (i, 0)),
                  pl.BlockSpec((D_MODEL, tn), lambda i, j: (0, j)),
                  pl.BlockSpec((D_MODEL, tn), lambda i, j: (0, j + nj)),
                  pl.BlockSpec((1, tn), lambda i, j: (0, j)),
                  pl.BlockSpec((1, tn), lambda i, j: (0, j + nj)),
                  pl.BlockSpec((ATTN_WIDTH, tn), lambda i, j: (0, j)),
                  pl.BlockSpec((D_INNER, tn), lambda i, j: (0, j))],
        out_specs=pl.BlockSpec((tm, tn), lambda i, j: (i, j)),
        out_shape=jax.ShapeDtypeStruct((m, D_MODEL), BF16),
        compiler_params=_cparams(("parallel", "parallel")),
        name="merge",
    )(h, attn, ssm, w_gate, w_gate, bg, bg, w_attn_up, w_ssm_up)


def _outproj_router_kernel(mg_ref, wo_ref, x_ref, n2_ref, wr_ref, br_ref, x1_ref, h2_ref, idx_ref, gate_ref):
    x1 = x_ref[...] + _dot(mg_ref[...], wo_ref[...])
    x1_ref[...] = x1
    ms = jnp.mean(x1 * x1, axis=-1, keepdims=True)
    h2 = x1 * lax.rsqrt(ms + EPS) * n2_ref[...]
    h2_ref[...] = h2
    h_hi, h_lo = _split2(h2)
    w_hi, w_lo = _split2(wr_ref[...])
    logits = _dot(h_hi, w_hi) + (_dot(h_hi, w_lo) + _dot(h_lo, w_hi)) + br_ref[...]
    tm = logits.shape[0]
    lane = lax.broadcasted_iota(I32, (tm, LANES), 1)
    logits = jnp.where(lane < N_EXPERTS, logits, -jnp.inf)
    idx_out = jnp.zeros((tm, LANES), I32)
    val_out = jnp.zeros((tm, LANES), F32)
    top = None
    for k in range(TOP_K):
        v = jnp.max(logits, axis=-1, keepdims=True)
        i = jnp.min(jnp.where(logits == v, lane, LANES), axis=-1, keepdims=True)
        if k == 0:
            top = v
        idx_out = jnp.where(lane == k, i, idx_out)
        val_out = jnp.where(lane == k, jnp.exp(v - top), val_out)
        logits = jnp.where(lane == i, -jnp.inf, logits)
    idx_ref[...] = idx_out
    gate_ref[...] = val_out / jnp.sum(val_out, axis=-1, keepdims=True)


def _outproj_router(merged, w_out, x, norm2, w_router, b_router):
    m = x.shape[0]
    tm = _pick(m, (256, 128, 64, 32))
    wr = jnp.pad(w_router.astype(F32), ((0, 0), (0, LANES - N_EXPERTS)))
    br = jnp.pad(b_router.astype(F32), (0, LANES - N_EXPERTS)).reshape(1, LANES)
    row = lambda w: pl.BlockSpec((tm, w), lambda i: (i, 0))
    full = lambda r, c: pl.BlockSpec((r, c), lambda i: (0, 0))
    return pl.pallas_call(
        _outproj_router_kernel,
        grid=(m // tm,),
        in_specs=[row(D_MODEL), full(D_MODEL, D_MODEL), row(D_MODEL), full(1, D_MODEL),
                  full(D_MODEL, LANES), full(1, LANES)],
        out_specs=[row(D_MODEL), row(D_MODEL), row(LANES), row(LANES)],
        out_shape=[jax.ShapeDtypeStruct((m, D_MODEL), F32), jax.ShapeDtypeStruct((m, D_MODEL), F32),
                   jax.ShapeDtypeStruct((m, LANES), I32), jax.ShapeDtypeStruct((m, LANES), F32)],
        compiler_params=_cparams(("parallel",)),
        name="outproj_router",
    )(merged, w_out, x, norm2.astype(F32).reshape(1, D_MODEL), wr, br)


def _rank_kernel(idx_ref, rank_ref, cnt_ref, base_ref):
    i = pl.program_id(0)
    tt = idx_ref.shape[0]

    @pl.when(i == 0)
    def _():
        base_ref[...] = jnp.zeros(base_ref.shape, F32)

    idx = idx_ref[...]
    lane = lax.broadcasted_iota(I32, (tt, LANES), 1)
    sel = [lane == idx[:, k:k + 1] for k in range(TOP_K)]
    onehot = jnp.zeros((tt, LANES), F32)
    for k in range(TOP_K):
        onehot = jnp.where(sel[k], 1.0, onehot)
    r_idx = lax.broadcasted_iota(I32, (tt, tt), 0)
    c_idx = lax.broadcasted_iota(I32, (tt, tt), 1)
    before = (c_idx < r_idx).astype(BF16)
    rank_all = _dot(before, onehot.astype(BF16)) + base_ref[0:1, :]
    out = jnp.zeros((tt, LANES), F32)
    for k in range(TOP_K):
        out = jnp.where(lane == k, jnp.sum(jnp.where(sel[k], rank_all, 0.0), axis=-1, keepdims=True), out)
    rank_ref[...] = out.astype(I32)
    base_ref[0:1, :] = base_ref[0:1, :] + jnp.sum(onehot, axis=0, keepdims=True)
    cnt_ref[...] = base_ref[...].astype(I32)


def _ranks(idx):
    m = idx.shape[0]
    tt = _pick(m, (512, 256, 128, 64, 32))
    return pl.pallas_call(
        _rank_kernel,
        grid=(m // tt,),
        in_specs=[pl.BlockSpec((tt, LANES), lambda i: (i, 0))],
        out_specs=[pl.BlockSpec((tt, LANES), lambda i: (i, 0)), pl.BlockSpec((8, LANES), lambda i: (0, 0))],
        out_shape=[jax.ShapeDtypeStruct((m, LANES), I32), jax.ShapeDtypeStruct((8, LANES), I32)],
        scratch_shapes=[pltpu.VMEM((8, LANES), F32)],
        compiler_params=_cparams(("arbitrary",)),
        name="expert_ranks",
    )(idx)


def _dispatch_kernel(dest_ref, h_hbm, xs_in_hbm, xs_hbm, sem, *, tt):
    del xs_in_hbm
    i = pl.program_id(0)
    n = tt * TOP_K

    def issue(a, carry):
        flat = i * n + a
        src = i * tt + a // TOP_K
        pltpu.make_async_copy(h_hbm.at[pl.ds(src, 1), :], xs_hbm.at[pl.ds(dest_ref[flat], 1), :], sem).start()
        return carry

    lax.fori_loop(0, n, issue, 0)

    def drain(a, carry):
        pltpu.make_async_copy(h_hbm.at[pl.ds(0, 1), :], xs_hbm.at[pl.ds(0, 1), :], sem).wait()
        return carry

    lax.fori_loop(0, n, drain, 0)


def _dispatch(dest_flat, h2, n_rows):
    m = h2.shape[0]
    tt = _pick(m, (256, 128, 64, 32))
    xs0 = jnp.zeros((n_rows, D_MODEL), F32)
    grid_spec = pltpu.PrefetchScalarGridSpec(
        num_scalar_prefetch=1,
        grid=(m // tt,),
        in_specs=[pl.BlockSpec(memory_space=pl.ANY), pl.BlockSpec(memory_space=pl.ANY)],
        out_specs=pl.BlockSpec(memory_space=pl.ANY),
        scratch_shapes=[pltpu.SemaphoreType.DMA],
    )
    return pl.pallas_call(
        functools.partial(_dispatch_kernel, tt=tt),
        grid_spec=grid_spec,
        out_shape=jax.ShapeDtypeStruct((n_rows, D_MODEL), F32),
        input_output_aliases={2: 0},
        compiler_params=_cparams(("arbitrary",)),
        name="dispatch",
    )(dest_flat, h2, xs0)


def _expert_kernel(be_ref, nu_ref, xs_ref, wg_ref, wu_ref, wd_ref, bg_ref, bu_ref, bd_ref, ys_ref, xb_ref, acc_ref):
    i = pl.program_id(0)
    j = pl.program_id(1)
    nj = pl.num_programs(1)

    @pl.when(i < nu_ref[0])
    def _():
        @pl.when(j == 0)
        def _():
            xb_ref[...] = xs_ref[...].astype(BF16)

        xb = xb_ref[...]
        g = jnp.minimum(_dot(xb, wg_ref[0].astype(BF16)) + bg_ref[0], SWIGLU_LIMIT)
        u = jnp.clip(_dot(xb, wu_ref[0].astype(BF16)) + bu_ref[0], -SWIGLU_LIMIT, SWIGLU_LIMIT)
        act = (u + 1.0) * (g * jax.nn.sigmoid(SWIGLU_ALPHA * g))
        contrib = _dot(act.astype(BF16), wd_ref[0].astype(BF16))

        @pl.when(j == 0)
        def _():
            acc_ref[...] = contrib

        @pl.when(j > 0)
        def _():
            acc_ref[...] += contrib

        @pl.when(j == nj - 1)
        def _():
            ys_ref[...] = acc_ref[...] + bd_ref[0]

    @pl.when(jnp.logical_and(i >= nu_ref[0], j == nj - 1))
    def _():
        ys_ref[...] = jnp.zeros(ys_ref.shape, F32)


def _experts(block_e, n_used, xs, w_e_gate, b_e_gate, w_e_up, b_e_up, w_e_down, b_e_down):
    n_rows = xs.shape[0]
    tm, tf = MOE_TM, MOE_TF
    n_blocks = n_rows // tm
    nj = D_FF // tf

    def blk(i, be, nu):
        return jnp.minimum(i, nu[0] - 1)

    def ftile(i, j, nu):
        return jnp.where(i < nu[0], j, nj - 1)

    grid_spec = pltpu.PrefetchScalarGridSpec(
        num_scalar_prefetch=2,
        grid=(n_blocks, nj),
        in_specs=[pl.BlockSpec((tm, D_MODEL), lambda i, j, be, nu: (blk(i, be, nu), 0)),
                  pl.BlockSpec((1, D_MODEL, tf), lambda i, j, be, nu: (be[blk(i, be, nu)], 0, ftile(i, j, nu))),
                  pl.BlockSpec((1, D_MODEL, tf), lambda i, j, be, nu: (be[blk(i, be, nu)], 0, ftile(i, j, nu))),
                  pl.BlockSpec((1, tf, D_MODEL), lambda i, j, be, nu: (be[blk(i, be, nu)], ftile(i, j, nu), 0)),
                  pl.BlockSpec((1, 1, tf), lambda i, j, be, nu: (be[blk(i, be, nu)], 0, ftile(i, j, nu))),
                  pl.BlockSpec((1, 1, tf), lambda i, j, be, nu: (be[blk(i, be, nu)], 0, ftile(i, j, nu))),
                  pl.BlockSpec((1, 1, D_MODEL), lambda i, j, be, nu: (be[blk(i, be, nu)], 0, 0))],
        out_specs=pl.BlockSpec((tm, D_MODEL), lambda i, j, be, nu: (i, 0)),
        scratch_shapes=[pltpu.VMEM((tm, D_MODEL), BF16), pltpu.VMEM((tm, D_MODEL), F32)],
    )
    return pl.pallas_call(
        _expert_kernel,
        grid_spec=grid_spec,
        out_shape=jax.ShapeDtypeStruct((n_rows, D_MODEL), F32),
        compiler_params=_cparams(("arbitrary", "arbitrary")),
        name="experts",
    )(block_e, n_used, xs, w_e_gate, w_e_up, w_e_down,
      b_e_gate.reshape(N_EXPERTS, 1, D_FF), b_e_up.reshape(N_EXPERTS, 1, D_FF),
      b_e_down.reshape(N_EXPERTS, 1, D_MODEL))


def _combine_kernel(dest_ref, x1_ref, gate_ref, ys_hbm, o_ref, buf, sem, *, tt):
    i = pl.program_id(0)
    n = tt * TOP_K

    def issue(a, carry):
        r = a // TOP_K
        k = a % TOP_K
        pltpu.make_async_copy(ys_hbm.at[pl.ds(dest_ref[i * n + a], 1), :], buf.at[k, pl.ds(r, 1), :], sem).start()
        return carry

    lax.fori_loop(0, n, issue, 0)

    def drain(a, carry):
        pltpu.make_async_copy(ys_hbm.at[pl.ds(0, 1), :], buf.at[0, pl.ds(0, 1), :], sem).wait()
        return carry

    lax.fori_loop(0, n, drain, 0)
    gate = gate_ref[...]
    acc = x1_ref[...]
    for k in range(TOP_K):
        acc = acc + gate[:, k:k + 1] * buf[k]
    o_ref[...] = acc


def _combine(dest_flat, x1, gate, ys):
    m = x1.shape[0]
    tt = _pick(m, (128, 64, 32))
    grid_spec = pltpu.PrefetchScalarGridSpec(
        num_scalar_prefetch=1,
        grid=(m // tt,),
        in_specs=[pl.BlockSpec((tt, D_MODEL), lambda i, d: (i, 0)),
                  pl.BlockSpec((tt, LANES), lambda i, d: (i, 0)),
                  pl.BlockSpec(memory_space=pl.ANY)],
        out_specs=pl.BlockSpec((tt, D_MODEL), lambda i, d: (i, 0)),
        scratch_shapes=[pltpu.VMEM((TOP_K, tt, D_MODEL), F32), pltpu.SemaphoreType.DMA],
    )
    return pl.pallas_call(
        functools.partial(_combine_kernel, tt=tt),
        grid_spec=grid_spec,
        out_shape=jax.ShapeDtypeStruct((m, D_MODEL), F32),
        compiler_params=_cparams(("arbitrary",)),
        name="combine",
    )(dest_flat, x1, gate, ys)


def _moe(x1, h2, idx, gate, w_e_gate, b_e_gate, w_e_up, b_e_up, w_e_down, b_e_down):
    m = x1.shape[0]
    tm = MOE_TM
    ranks, cnt = _ranks(idx)
    counts = cnt[0, :N_EXPERTS]
    padded = (counts + tm - 1) // tm * tm
    pad_end = jnp.cumsum(padded)
    pad_start = pad_end - padded
    e_sel = idx[:, :TOP_K]
    dest = (pad_start[e_sel] + ranks[:, :TOP_K]).astype(I32).reshape(-1)
    n_blocks = -(-(m * TOP_K + N_EXPERTS * (tm - 1)) // tm)
    block_e = jnp.minimum(jnp.searchsorted(pad_end, jnp.arange(n_blocks, dtype=I32) * tm, side='right'),
                          N_EXPERTS - 1).astype(I32)
    n_used = (pad_end[-1] // tm).astype(I32).reshape(1)
    xs = _dispatch(dest, h2, n_blocks * tm)
    ys = _experts(block_e, n_used, xs, w_e_gate, b_e_gate, w_e_up, b_e_up, w_e_down, b_e_down)
    return _combine(dest, x1, gate, ys)


def kernel(x_prompt, x_sample, cache_k, cache_v, state_conv, state_ssm, norm1, w_in, q_norm, k_norm, sinks, conv_w,
           conv_b, dt_bias, a_log, d_skip, ssm_norm, w_gate, b_gate, w_attn_up, w_ssm_up, w_out, norm2, w_router,
           b_router, w_e_gate, b_e_gate, w_e_up, b_e_up, w_e_down, b_e_down):
    bp, sp, _ = x_prompt.shape
    bs, ss, _ = x_sample.shape
    tp, ts = bp * sp, bs * ss
    kv_len = cache_k.shape[2]
    l = 0

    x = jnp.concatenate([x_prompt.reshape(tp, D_MODEL), x_sample.reshape(ts, D_MODEL)], axis=0)
    pos = jnp.concatenate([jnp.tile(jnp.arange(sp, dtype=I32), bp), jnp.tile(PAST_LEN + jnp.arange(ss, dtype=I32), bs)])

    o1 = ATTN_WIDTH
    o2 = o1 + 2 * KV_WIDTH
    o3 = o2 + D_INNER
    o4 = o3 + CONV_DIM
    w = w_in[l]
    h = _rmsnorm(x, norm1[l], BF16)
    q_p = _matmul(h, w[:, :o1].astype(BF16), F32, "proj_q")
    kv_p = _matmul(h, w[:, o1:o2].astype(BF16), F32, "proj_kv")
    z = _matmul(h, w[:, o2:o3].astype(BF16), BF16, "proj_z")
    xbc = _matmul(h, w[:, o3:o4].astype(BF16), F32, "proj_xbc")
    dt_raw = _matmul(h, jnp.pad(w[:, o4:], ((0, 0), (0, LANES - SSM_HEADS))).astype(BF16), F32, "proj_dt")
    k_p, v = kv_p[:, :KV_WIDTH], kv_p[:, KV_WIDTH:]

    q_rot, k_rot = _qk_prep(q_p, k_p, pos, q_norm[l], k_norm[l])
    sk = sinks[l].astype(F32)
    attn_p = _attn_prompt(q_rot, k_rot, v, sk, bp, sp)
    ck = cache_k[l].reshape(bs, kv_len, KV_WIDTH)
    cv = cache_v[l].reshape(bs, kv_len, KV_WIDTH)
    attn_s = _attn_sample(q_rot, k_rot, v, ck, cv, sk, tp, bs, ss)
    attn = jnp.concatenate([attn_p, attn_s], axis=0)

    ssd_w = (conv_w[l], conv_b[l], dt_bias[l], a_log[l], d_skip[l], ssm_norm[l])
    ssm_p, hfin_p = _ssd(xbc, dt_raw, z, *ssd_w, 0, bp, sp, CHUNK)
    ssm_s, hfin_s = _ssd(xbc, dt_raw, z, *ssd_w, tp, bs, ss, ss, state_conv[l], state_ssm[l])
    ssm = jnp.concatenate([ssm_p, ssm_s], axis=0)

    merged = _merge(h, attn, ssm, w_gate[l].astype(BF16), b_gate[l], w_attn_up[l].astype(BF16), w_ssm_up[l].astype(BF16))
    x1, h2, idx, gate = _outproj_router(merged, w_out[l].astype(BF16), x, norm2[l], w_router[l], b_router[l])
    y = _moe(x1, h2, idx, gate, w_e_gate[l], b_e_gate[l], w_e_up[l], b_e_up[l], w_e_down[l], b_e_down[l])

    kp4 = k_rot[:tp].reshape(bp, sp, N_KV_HEADS, HEAD_DIM)
    vp4 = v[:tp].reshape(bp, sp, N_KV_HEADS, HEAD_DIM)
    ks4 = k_rot[tp:].reshape(bs, ss, N_KV_HEADS, HEAD_DIM)
    vs4 = v[tp:].reshape(bs, ss, N_KV_HEADS, HEAD_DIM)
    new_k_s = jnp.concatenate([cache_k[l], ks4], axis=1)[:, -kv_len:]
    new_v_s = jnp.concatenate([cache_v[l], vs4], axis=1)[:, -kv_len:]
    xbc_p = xbc[:tp].reshape(bp, sp, CONV_DIM)
    xbc_s = jnp.concatenate([state_conv[l], xbc[tp:].reshape(bs, ss, CONV_DIM)], axis=1)
    return (y[:tp].reshape(bp, sp, D_MODEL), y[tp:].reshape(bs, ss, D_MODEL),
            kp4[:, -WINDOW:][None], vp4[:, -WINDOW:][None], xbc_p[:, -(CONV_W - 1):][None], hfin_p[None],
            new_k_s[None], new_v_s[None], xbc_s[:, -(CONV_W - 1):][None], hfin_s[None])
```

```python
import functools
import math

import jax
import jax.numpy as jnp
from jax import lax
from jax.experimental import pallas as pl
from jax.experimental.pallas import tpu as pltpu

F32 = jnp.float32
BF16 = jnp.bfloat16
I32 = jnp.int32

D_MODEL = 2048
CHUNK = 64
N_HEADS = 32
N_KV_HEADS = 8
HEAD_DIM = 64
Q_PER_KV = N_HEADS // N_KV_HEADS
ATTN_WIDTH = N_HEADS * HEAD_DIM
KV_WIDTH = N_KV_HEADS * HEAD_DIM
WINDOW = 128
N_PREV_CHUNKS = WINDOW // CHUNK
ROPE_THETA = 500000.0
ROT_DIM = HEAD_DIM // 4
D_INNER = 2 * D_MODEL
SSM_HEAD_DIM = 64
SSM_HEADS = D_INNER // SSM_HEAD_DIM
SSM_GROUPS = 8
D_STATE = 128
CONV_W = 4
BC_WIDTH = SSM_GROUPS * D_STATE
CONV_DIM = D_INNER + 2 * BC_WIDTH
N_EXPERTS = 32
TOP_K = 4
D_FF = D_MODEL
SWIGLU_LIMIT = 7.0
SWIGLU_ALPHA = 1.702
EPS = 1e-6
NEG_INF = -1e30
PAST_LEN = 2048

LANES = 128
VMEM_LIMIT = 56 * 1024 * 1024
EXPERT_VMEM_LIMIT = 60 * 1024 * 1024
MOE_TM = 512
MOE_TF = 256


def _pick(n, cands):
    for c in cands:
        if n % c == 0:
            return c
    return n


def _cparams(sem):
    return pltpu.CompilerParams(dimension_semantics=sem, vmem_limit_bytes=VMEM_LIMIT)


def _split2(x):
    hi = x.astype(BF16)
    lo = (x - hi.astype(F32)).astype(BF16)
    return hi, lo


def _split3(x):
    hi = x.astype(BF16)
    r = x - hi.astype(F32)
    mid = r.astype(BF16)
    lo = (r - mid.astype(F32)).astype(BF16)
    return hi, mid, lo


def _dot(a, b):
    return jnp.dot(a, b, preferred_element_type=F32)


def _dot_nt(a, b):
    return lax.dot_general(a, b, (((1,), (1,)), ((), ())), preferred_element_type=F32)


def _silu(x):
    return x * jax.nn.sigmoid(x)


def _rms_kernel(x_ref, w_ref, o_ref):
    x = x_ref[...]
    ms = jnp.mean(x * x, axis=-1, keepdims=True)
    o_ref[...] = (x * lax.rsqrt(ms + EPS) * w_ref[...]).astype(o_ref.dtype)


def _rmsnorm(x, w, out_dtype):
    m, d = x.shape
    tm = _pick(m, (512, 256, 128, 64, 32))
    return pl.pallas_call(
        _rms_kernel,
        grid=(m // tm,),
        in_specs=[pl.BlockSpec((tm, d), lambda i: (i, 0)), pl.BlockSpec((1, d), lambda i: (0, 0))],
        out_specs=pl.BlockSpec((tm, d), lambda i: (i, 0)),
        out_shape=jax.ShapeDtypeStruct((m, d), out_dtype),
        compiler_params=_cparams(("parallel",)),
        name="rmsnorm",
    )(x, w.reshape(1, d))


def _mm_kernel(x_ref, w_ref, o_ref):
    o_ref[...] = _dot(x_ref[...], w_ref[...]).astype(o_ref.dtype)


def _matmul(x, w, out_dtype, name):
    m, k = x.shape
    n = w.shape[1]
    tm = _pick(m, (1024, 512, 256, 128, 64, 32))
    tn = _pick(n, (1024, 512, 256, 128))
    return pl.pallas_call(
        _mm_kernel,
        grid=(m // tm, n // tn),
        in_specs=[pl.BlockSpec((tm, k), lambda i, j: (i, 0)), pl.BlockSpec((k, tn), lambda i, j: (0, j))],
        out_specs=pl.BlockSpec((tm, tn), lambda i, j: (i, j)),
        out_shape=jax.ShapeDtypeStruct((m, n), out_dtype),
        compiler_params=_cparams(("parallel", "parallel")),
        name=name,
    )(x, w)


def _qk_prep_kernel(q_ref, k_ref, cos_ref, s1_ref, s2_ref, qn_ref, kn_ref, g_ref, gt_ref, qo_ref, ko_ref):
    cos = cos_ref[...]
    s1 = s1_ref[...]
    s2 = s2_ref[...]

    def norm_rope(x, nw, width):
        g = g_ref[0:width, :]
        gt = gt_ref[:, 0:width]
        sq_hi, sq_lo = _split2(x * x)
        ssum = _dot(sq_hi, g) + _dot(sq_lo, g)
        r = lax.rsqrt(ssum * (1.0 / HEAD_DIM) + EPS)
        r_hi, r_lo = _split2(r)
        y = x * (_dot(r_hi, gt) + _dot(r_lo, gt)) * nw
        outs = []
        for s in range(width // LANES):
            blk = y[:, s * LANES:(s + 1) * LANES]
            outs.append(blk * cos + pltpu.roll(blk, ROT_DIM // 2, 1) * s1
                        + pltpu.roll(blk, LANES - ROT_DIM // 2, 1) * s2)
        return jnp.concatenate(outs, axis=1)

    q = norm_rope(q_ref[...], qn_ref[...], ATTN_WIDTH)
    qo_ref[...] = (q * (HEAD_DIM ** -0.5)).astype(qo_ref.dtype)
    ko_ref[...] = norm_rope(k_ref[...], kn_ref[...], KV_WIDTH)


def _qk_prep(q, k, pos, q_norm, k_norm):
    m = q.shape[0]
    tm = _pick(m, (256, 128, 64, 32))
    half = ROT_DIM // 2
    inv_freq = ROPE_THETA ** (-jnp.arange(half, dtype=F32) * 2.0 / ROT_DIM)
    ang = pos.astype(F32)[:, None] * inv_freq[None, :]
    cos, sin = jnp.cos(ang), jnp.sin(ang)
    ones = jnp.ones((m, HEAD_DIM - ROT_DIM), F32)
    zeros = jnp.zeros((m, HEAD_DIM - ROT_DIM), F32)
    zh = jnp.zeros((m, half), F32)
    cos_t = jnp.tile(jnp.concatenate([cos, cos, ones], axis=1), (1, LANES // HEAD_DIM))
    s1_t = jnp.tile(jnp.concatenate([zh, sin, zeros], axis=1), (1, LANES // HEAD_DIM))
    s2_t = jnp.tile(jnp.concatenate([-sin, zh, zeros], axis=1), (1, LANES // HEAD_DIM))
    head_of_lane = jnp.arange(ATTN_WIDTH) // HEAD_DIM
    g = (head_of_lane[:, None] == jnp.arange(LANES)[None, :]).astype(BF16)
    gt = g.T
    qn = jnp.tile(q_norm.astype(F32), N_HEADS).reshape(1, ATTN_WIDTH)
    kn = jnp.tile(k_norm.astype(F32), N_KV_HEADS).reshape(1, KV_WIDTH)
    row = lambda w: pl.BlockSpec((tm, w), lambda i: (i, 0))
    full = lambda a: pl.BlockSpec(a.shape, lambda i: (0, 0))
    return pl.pallas_call(
        _qk_prep_kernel,
        grid=(m // tm,),
        in_specs=[row(ATTN_WIDTH), row(KV_WIDTH), row(LANES), row(LANES), row(LANES),
                  full(qn), full(kn), full(g), full(gt)],
        out_specs=[row(ATTN_WIDTH), row(KV_WIDTH)],
        out_shape=[jax.ShapeDtypeStruct((m, ATTN_WIDTH), BF16), jax.ShapeDtypeStruct((m, KV_WIDTH), F32)],
        compiler_params=_cparams(("parallel",)),
        name="qk_prep",
    )(q, k, cos_t, s1_t, s2_t, qn, kn, g, gt)


def _attend(q, kk, vv, sinks_ref, valid):
    tq = q.shape[0]
    outs = []
    for j in range(N_KV_HEADS):
        kj = kk[:, j * HEAD_DIM:(j + 1) * HEAD_DIM]
        vj = vv[:, j * HEAD_DIM:(j + 1) * HEAD_DIM]
        heads = [Q_PER_KV * j + g for g in range(Q_PER_KV)]
        q4 = jnp.concatenate([q[:, h * HEAD_DIM:(h + 1) * HEAD_DIM] for h in heads], axis=0)
        s = _dot_nt(q4, kj)
        if valid is not None:
            s = jnp.where(valid, s, NEG_INF)
        sink = jnp.concatenate([jnp.full((tq, 1), sinks_ref[h], F32) for h in heads], axis=0)
        m = jnp.maximum(jnp.max(s, axis=-1, keepdims=True), sink)
        p = jnp.exp(s - m)
        denom = jnp.sum(p, axis=-1, keepdims=True) + jnp.exp(sink - m)
        o = _dot(p.astype(BF16), vj) / denom
        for g in range(Q_PER_KV):
            outs.append(o[g * tq:(g + 1) * tq, :])
    return jnp.concatenate(outs, axis=1)


def _attn_prompt_kernel(sinks_ref, q_ref, k0_ref, k1_ref, k2_ref, v0_ref, v1_ref, v2_ref, o_ref):
    c = pl.program_id(1)
    kk = jnp.concatenate([k0_ref[...], k1_ref[...], k2_ref[...]], axis=0).astype(BF16)
    vv = jnp.concatenate([v0_ref[...], v1_ref[...], v2_ref[...]], axis=0).astype(BF16)
    key_chunk = lax.broadcasted_iota(I32, (1, 3 * CHUNK), 1) // CHUNK + (c - N_PREV_CHUNKS)
    o_ref[...] = _attend(q_ref[...], kk, vv, sinks_ref, key_chunk >= 0).astype(o_ref.dtype)


def _attn_prompt(q, k, kv, sinks, bsz, seq):
    nc = seq // CHUNK

    def kv_spec(back, col):
        return pl.BlockSpec((CHUNK, KV_WIDTH), lambda b, c, s: (b * nc + jnp.maximum(c - back, 0), col))

    grid_spec = pltpu.PrefetchScalarGridSpec(
        num_scalar_prefetch=1,
        grid=(bsz, nc),
        in_specs=[pl.BlockSpec((CHUNK, ATTN_WIDTH), lambda b, c, s: (b * nc + c, 0)),
                  kv_spec(2, 0), kv_spec(1, 0), kv_spec(0, 0), kv_spec(2, 1), kv_spec(1, 1), kv_spec(0, 1)],
        out_specs=pl.BlockSpec((CHUNK, ATTN_WIDTH), lambda b, c, s: (b * nc + c, 0)),
    )
    return pl.pallas_call(
        _attn_prompt_kernel,
        grid_spec=grid_spec,
        out_shape=jax.ShapeDtypeStruct((bsz * seq, ATTN_WIDTH), BF16),
        compiler_params=_cparams(("parallel", "parallel")),
        name="attn_prompt",
    )(sinks, q, k, k, k, kv, kv, kv)


def _attn_sample_kernel(sinks_ref, q_ref, kn_ref, vn_ref, kc_ref, vc_ref, o_ref):
    kk = jnp.concatenate([kc_ref[0], kn_ref[...]], axis=0).astype(BF16)
    vv = jnp.concatenate([vc_ref[0], vn_ref[...]], axis=0).astype(BF16)
    o_ref[...] = _attend(q_ref[...], kk, vv, sinks_ref, None).astype(o_ref.dtype)


def _attn_sample(q, k, kv, cache_k, cache_v, sinks, row0, bsz, seq):
    blk0 = row0 // seq
    kv_len = cache_k.shape[1]
    new = lambda w, col: pl.BlockSpec((seq, w), lambda b, s: (blk0 + b, col))
    cache = pl.BlockSpec((1, kv_len, KV_WIDTH), lambda b, s: (b, 0, 0))
    grid_spec = pltpu.PrefetchScalarGridSpec(
        num_scalar_prefetch=1,
        grid=(bsz,),
        in_specs=[new(ATTN_WIDTH, 0), new(KV_WIDTH, 0), new(KV_WIDTH, 1), cache, cache],
        out_specs=pl.BlockSpec((seq, ATTN_WIDTH), lambda b, s: (b, 0)),
    )
    return pl.pallas_call(
        _attn_sample_kernel,
        grid_spec=grid_spec,
        out_shape=jax.ShapeDtypeStruct((bsz * seq, ATTN_WIDTH), BF16),
        compiler_params=_cparams(("parallel",)),
        name="attn_sample",
    )(sinks, q, k, kv, cache_k, cache_v)


def _pad_rows(x, rows):
    if x.shape[0] == rows:
        return x
    return jnp.concatenate([x, jnp.zeros((rows - x.shape[0], x.shape[1]), x.dtype)], axis=0)


def _ssd_kernel(*refs, clen, has_past):
    if has_past:
        (xbc_ref, dt_ref, z_ref, cw_ref, cb_ref, dtb_ref, alog_ref, dskip_ref, nw_ref, cpast_ref, hpast_ref,
         y_ref, hout_ref, xpad, u_ref, g_ref, ht_ref) = refs
    else:
        (xbc_ref, dt_ref, z_ref, cw_ref, cb_ref, dtb_ref, alog_ref, dskip_ref, nw_ref,
         y_ref, hout_ref, xpad, u_ref, g_ref, ht_ref) = refs
    L = clen
    c = pl.program_id(1)
    nc = pl.num_programs(1)
    n_pairs = SSM_HEADS // 2
    pairs_per_group = n_pairs // SSM_GROUPS
    gw = D_INNER // SSM_GROUPS

    @pl.when(c == 0)
    def _init():
        if has_past:
            xpad[5:8, :] = cpast_ref[0]
            for p in range(n_pairs):
                both = jnp.concatenate([hpast_ref[0, 2 * p], hpast_ref[0, 2 * p + 1]], axis=0)
                ht_ref[:, p * LANES:(p + 1) * LANES] = both.T
        else:
            xpad[0:8, :] = jnp.zeros((8, CONV_DIM), F32)
            ht_ref[...] = jnp.zeros(ht_ref.shape, F32)

    xpad[8:8 + L, :] = xbc_ref[...]
    cblk = 512
    for j in range(CONV_DIM // cblk):
        sl = slice(j * cblk, (j + 1) * cblk)
        acc = cb_ref[:, sl] + xpad[8:8 + L, sl] * cw_ref[3:4, sl]
        acc = acc + xpad[7:7 + L, sl] * cw_ref[2:3, sl]
        acc = acc + xpad[6:6 + L, sl] * cw_ref[1:2, sl]
        acc = acc + xpad[5:5 + L, sl] * cw_ref[0:1, sl]
        u_ref[:, sl] = _silu(acc)
    xpad[5:8, :] = xpad[5 + L:8 + L, :]

    dtx = dt_ref[...] + dtb_ref[...]
    dt = jnp.maximum(dtx, 0.0) + jnp.log(1.0 + jnp.exp(-jnp.abs(dtx)))
    loga = dt * (-jnp.exp(alog_ref[...]))
    t_idx = lax.broadcasted_iota(I32, (L, L), 0)
    s_idx = lax.broadcasted_iota(I32, (L, L), 1)
    incl = (s_idx <= t_idx).astype(BF16)
    acum = sum(_dot(incl, part) for part in _split3(loga))
    eacum = jnp.exp(acum)
    a_last = acum[L - 1:L, :]
    e_last = jnp.exp(a_last)
    wend = dt * jnp.exp(a_last - acum)
    acum_t = _pad_rows(acum, LANES).T[:, 0:L]

    lane_m = lax.broadcasted_iota(I32, (L, 2 * L), 1)
    row_m = lax.broadcasted_iota(I32, (L, 2 * L), 0)
    left_m = lane_m < L
    causal_m = jnp.where(left_m, lane_m, lane_m - L) <= row_m
    left = lax.broadcasted_iota(I32, (L, LANES), 1) < SSM_HEAD_DIM
    left_n = lax.broadcasted_iota(I32, (D_STATE, LANES), 1) < SSM_HEAD_DIM
    left_1 = lax.broadcasted_iota(I32, (1, LANES), 1) < SSM_HEAD_DIM

    for grp in range(SSM_GROUPS):
        b_g = u_ref[:, D_INNER + grp * D_STATE:D_INNER + (grp + 1) * D_STATE]
        c_g = u_ref[:, D_INNER + BC_WIDTH + grp * D_STATE:D_INNER + BC_WIDTH + (grp + 1) * D_STATE]
        b_bf = b_g.astype(BF16)
        cb = _dot_nt(c_g.astype(BF16), b_bf)
        cb2 = jnp.concatenate([cb, cb], axis=1)
        bt_bf = _pad_rows(b_g, LANES).T[:, 0:L].astype(BF16)
        for pp in range(pairs_per_group):
            p = grp * pairs_per_group + pp
            h0, h1 = 2 * p, 2 * p + 1
            sl = slice(p * LANES, (p + 1) * LANES)
            x_pair = u_ref[:, sl]
            colpair = lambda qty: jnp.where(left, qty[:, h0:h0 + 1], qty[:, h1:h1 + 1])
            xdt = x_pair * colpair(dt)
            col_a = jnp.where(left_m, acum[:, h0:h0 + 1], acum[:, h1:h1 + 1])
            row_a = jnp.concatenate([acum_t[h0:h0 + 1, :], acum_t[h1:h1 + 1, :]], axis=1)
            dec = jnp.exp(jnp.where(causal_m, col_a - row_a, NEG_INF))
            m_pair = (cb2 * dec).astype(BF16)
            e0 = (eacum[:, h0:h0 + 1] * c_g).astype(BF16)
            e1 = (eacum[:, h1:h1 + 1] * c_g).astype(BF16)
            lhs = jnp.concatenate([m_pair, e0, e1], axis=1)
            h_pair = ht_ref[:, sl]
            rhs = jnp.concatenate([jnp.where(left, xdt, 0.0), jnp.where(left, 0.0, xdt),
                                   jnp.where(left_n, h_pair, 0.0), jnp.where(left_n, 0.0, h_pair)],
                                  axis=0).astype(BF16)
            y = _dot(lhs, rhs) + x_pair * dskip_ref[:, sl]
            g_ref[:, sl] = y * _silu(z_ref[:, sl].astype(F32))
            xw = (x_pair * colpair(wend)).astype(BF16)
            row_dec = jnp.where(left_1, e_last[:, h0:h0 + 1], e_last[:, h1:h1 + 1])
            ht_ref[:, sl] = h_pair * row_dec + _dot(bt_bf, xw)
        gsl = slice(grp * gw, (grp + 1) * gw)
        gg = g_ref[:, gsl]
        ms = jnp.mean(gg * gg, axis=-1, keepdims=True)
        y_ref[:, gsl] = (gg * lax.rsqrt(ms + EPS) * nw_ref[:, gsl]).astype(y_ref.dtype)

    @pl.when(c == nc - 1)
    def _fin():
        for p in range(n_pairs):
            both = ht_ref[:, p * LANES:(p + 1) * LANES].T
            hout_ref[0, 2 * p] = both[0:SSM_HEAD_DIM, :]
            hout_ref[0, 2 * p + 1] = both[SSM_HEAD_DIM:2 * SSM_HEAD_DIM, :]


def _ssd(xbc, dt_raw, z, conv_w, conv_b, dt_bias, a_log, d_skip, ssm_norm, row0, bsz, seq, clen,
         conv_past=None, ssm_past=None):
    nc = seq // clen
    blk0 = row0 // clen
    has_past = conv_past is not None
    pad = lambda a: jnp.pad(a.astype(F32), (0, LANES - SSM_HEADS)).reshape(1, LANES)
    params = [conv_w.astype(F32), conv_b.astype(F32).reshape(1, CONV_DIM), pad(dt_bias), pad(a_log),
              jnp.repeat(d_skip.astype(F32), SSM_HEAD_DIM).reshape(1, D_INNER), ssm_norm.astype(F32).reshape(1, D_INNER)]
    rows = lambda w: pl.BlockSpec((clen, w), lambda b, c: (blk0 + b * nc + c, 0))
    full = lambda a: pl.BlockSpec(a.shape, lambda b, c: (0,) * a.ndim)
    in_specs = [rows(CONV_DIM), rows(LANES), rows(D_INNER)] + [full(a) for a in params]
    args = [xbc, dt_raw, z] + params
    if has_past:
        in_specs += [pl.BlockSpec((1, CONV_W - 1, CONV_DIM), lambda b, c: (b, 0, 0)),
                     pl.BlockSpec((1, SSM_HEADS, SSM_HEAD_DIM, D_STATE), lambda b, c: (b, 0, 0, 0))]
        args += [conv_past.astype(F32), ssm_past.astype(F32)]
    return pl.pallas_call(
        functools.partial(_ssd_kernel, clen=clen, has_past=has_past),
        grid=(bsz, nc),
        in_specs=in_specs,
        out_specs=[pl.BlockSpec((clen, D_INNER), lambda b, c: (b * nc + c, 0)),
                   pl.BlockSpec((1, SSM_HEADS, SSM_HEAD_DIM, D_STATE), lambda b, c: (b, 0, 0, 0))],
        out_shape=[jax.ShapeDtypeStruct((bsz * seq, D_INNER), BF16),
                   jax.ShapeDtypeStruct((bsz, SSM_HEADS, SSM_HEAD_DIM, D_STATE), F32)],
        scratch_shapes=[pltpu.VMEM((8 + clen, CONV_DIM), F32), pltpu.VMEM((clen, CONV_DIM), F32),
                        pltpu.VMEM((clen, D_INNER), F32), pltpu.VMEM((D_STATE, D_INNER), F32)],
        compiler_params=_cparams(("parallel", "arbitrary")),
        name="ssd_past" if has_past else "ssd_prompt",
    )(*args)


def _merge_kernel(h_ref, a_ref, s_ref, wga_ref, wgs_ref, bga_ref, bgs_ref, wa_ref, ws_ref, o_ref):
    h = h_ref[...]
    g_a = jax.nn.sigmoid(_dot(h, wga_ref[...]) + bga_ref[...])
    g_s = jax.nn.sigmoid(_dot(h, wgs_ref[...]) + bgs_ref[...])
    o_ref[...] = (g_a * _dot(a_ref[...], wa_ref[...]) + g_s * _dot(s_ref[...], ws_ref[...])).astype(o_ref.dtype)


def _merge(h, attn, ssm, w_gate, b_gate, w_attn_up, w_ssm_up):
    m = h.shape[0]
    tm = _pick(m, (512, 256, 128, 64, 32))
    tn = 512
    nj = D_MODEL // tn
    bg = b_gate.astype(F32).reshape(1, 2 * D_MODEL)
    return pl.pallas_call(
        _merge_kernel,
        grid=(m // tm, nj),
        in_specs=[pl.BlockSpec((tm, D_MODEL), lambda i, j: (i, 0)),
                  pl.BlockSpec((tm, ATTN_WIDTH), lambda i, j: (i, 0)),
                  pl.BlockSpec((tm, D_INNER), lambda i, j: (i, 0)),
                  pl.BlockSpec((D_MODEL, tn), lambda i, j: (0, j)),
                  pl.BlockSpec((D_MODEL, tn), lambda i, j: (0, j + nj)),
                  pl.BlockSpec((1, tn), lambda i, j: (0, j)),
                  pl.BlockSpec((1, tn), lambda i, j: (0, j + nj)),
                  pl.BlockSpec((ATTN_WIDTH, tn), lambda i, j: (0, j)),
                  pl.BlockSpec((D_INNER, tn), lambda i, j: (0, j))],
        out_specs=pl.BlockSpec((tm, tn), lambda i, j: (i, j)),
        out_shape=jax.ShapeDtypeStruct((m, D_MODEL), BF16),
        compiler_params=_cparams(("parallel", "parallel")),
        name="merge",
    )(h, attn, ssm, w_gate, w_gate, bg, bg, w_attn_up, w_ssm_up)


def _outproj_router_kernel(mg_ref, wo_ref, x_ref, n2_ref, wr_ref, br_ref, x1_ref, h2_ref, idx_ref, gate_ref):
    x1 = x_ref[...] + _dot(mg_ref[...], wo_ref[...])
    x1_ref[...] = x1
    ms = jnp.mean(x1 * x1, axis=-1, keepdims=True)
    h2 = x1 * lax.rsqrt(ms + EPS) * n2_ref[...]
    h2_ref[...] = _pack_bf16_pairs(h2)
    h_hi, h_lo = _split2(h2)
    w_hi, w_lo = _split2(wr_ref[...])
    logits = _dot(h_hi, w_hi) + (_dot(h_hi, w_lo) + _dot(h_lo, w_hi)) + br_ref[...]
    tm = logits.shape[0]
    lane = lax.broadcasted_iota(I32, (tm, LANES), 1)
    logits = jnp.where(lane < N_EXPERTS, logits, -jnp.inf)
    idx_out = jnp.zeros((tm, LANES), I32)
    val_out = jnp.zeros((tm, LANES), F32)
    top = None
    for k in range(TOP_K):
        v = jnp.max(logits, axis=-1, keepdims=True)
        i = jnp.min(jnp.where(logits == v, lane, LANES), axis=-1, keepdims=True)
        if k == 0:
            top = v
        idx_out = jnp.where(lane == k, i, idx_out)
        val_out = jnp.where(lane == k, jnp.exp(v - top), val_out)
        logits = jnp.where(lane == i, -jnp.inf, logits)
    idx_ref[...] = idx_out
    gate_ref[...] = val_out / jnp.sum(val_out, axis=-1, keepdims=True)


def _outproj_router(merged, w_out, x, norm2, w_router, b_router):
    m = x.shape[0]
    tm = _pick(m, (256, 128, 64, 32))
    wr = jnp.pad(w_router.astype(F32), ((0, 0), (0, LANES - N_EXPERTS)))
    br = jnp.pad(b_router.astype(F32), (0, LANES - N_EXPERTS)).reshape(1, LANES)
    row = lambda w: pl.BlockSpec((tm, w), lambda i: (i, 0))
    full = lambda r, c: pl.BlockSpec((r, c), lambda i: (0, 0))
    return pl.pallas_call(
        _outproj_router_kernel,
        grid=(m // tm,),
        in_specs=[row(D_MODEL), full(D_MODEL, D_MODEL), row(D_MODEL), full(1, D_MODEL),
                  full(D_MODEL, LANES), full(1, LANES)],
        out_specs=[row(D_MODEL), row(D_MODEL // 2), row(LANES), row(LANES)],
        out_shape=[jax.ShapeDtypeStruct((m, D_MODEL), F32), jax.ShapeDtypeStruct((m, D_MODEL // 2), I32),
                   jax.ShapeDtypeStruct((m, LANES), I32), jax.ShapeDtypeStruct((m, LANES), F32)],
        compiler_params=_cparams(("parallel",)),
        name="outproj_router",
    )(merged, w_out, x, norm2.astype(F32).reshape(1, D_MODEL), wr, br)


def _rank_kernel(idx_ref, rank_ref, cnt_ref, base_ref):
    i = pl.program_id(0)
    tt = idx_ref.shape[0]

    @pl.when(i == 0)
    def _():
        base_ref[...] = jnp.zeros(base_ref.shape, F32)

    idx = idx_ref[...]
    lane = lax.broadcasted_iota(I32, (tt, LANES), 1)
    sel = [lane == idx[:, k:k + 1] for k in range(TOP_K)]
    onehot = jnp.zeros((tt, LANES), F32)
    for k in range(TOP_K):
        onehot = jnp.where(sel[k], 1.0, onehot)
    r_idx = lax.broadcasted_iota(I32, (tt, tt), 0)
    c_idx = lax.broadcasted_iota(I32, (tt, tt), 1)
    before = (c_idx < r_idx).astype(BF16)
    rank_all = _dot(before, onehot.astype(BF16)) + base_ref[0:1, :]
    out = jnp.zeros((tt, LANES), F32)
    for k in range(TOP_K):
        out = jnp.where(lane == k, jnp.sum(jnp.where(sel[k], rank_all, 0.0), axis=-1, keepdims=True), out)
    rank_ref[...] = out.astype(I32)
    base_ref[0:1, :] = base_ref[0:1, :] + jnp.sum(onehot, axis=0, keepdims=True)
    cnt_ref[...] = base_ref[...].astype(I32)


def _ranks(idx):
    m = idx.shape[0]
    tt = _pick(m, (512, 256, 128, 64, 32))
    return pl.pallas_call(
        _rank_kernel,
        grid=(m // tt,),
        in_specs=[pl.BlockSpec((tt, LANES), lambda i: (i, 0))],
        out_specs=[pl.BlockSpec((tt, LANES), lambda i: (i, 0)), pl.BlockSpec((8, LANES), lambda i: (0, 0))],
        out_shape=[jax.ShapeDtypeStruct((m, LANES), I32), jax.ShapeDtypeStruct((8, LANES), I32)],
        scratch_shapes=[pltpu.VMEM((8, LANES), F32)],
        compiler_params=_cparams(("arbitrary",)),
        name="expert_ranks",
    )(idx)


def _dispatch_kernel(dest_ref, h_ref, xs_in_hbm, xs_hbm, sem, *, tt):
    del xs_in_hbm
    base = pl.program_id(0) * (tt * TOP_K)

    def issue(r, carry):
        for k in range(TOP_K):
            dst = dest_ref[base + r * TOP_K + k]
            pltpu.make_async_copy(h_ref.at[pl.ds(r, 1), :], xs_hbm.at[pl.ds(dst, 1), :], sem).start()
        return carry

    lax.fori_loop(0, tt, issue, 0, unroll=2)
    for _ in range(TOP_K):
        pltpu.make_async_copy(h_ref, xs_hbm.at[pl.ds(0, tt), :], sem).wait()


def _dispatch(dest_flat, h2, n_rows):
    m = h2.shape[0]
    tt = _pick(m, (256, 128, 64, 32))
    width = h2.shape[1]
    xs0 = jnp.zeros((n_rows, width), h2.dtype)
    grid_spec = pltpu.PrefetchScalarGridSpec(
        num_scalar_prefetch=1,
        grid=(m // tt,),
        in_specs=[pl.BlockSpec((tt, width), lambda i, d: (i, 0)), pl.BlockSpec(memory_space=pl.ANY)],
        out_specs=pl.BlockSpec(memory_space=pl.ANY),
        scratch_shapes=[pltpu.SemaphoreType.DMA],
    )
    return pl.pallas_call(
        functools.partial(_dispatch_kernel, tt=tt),
        grid_spec=grid_spec,
        out_shape=jax.ShapeDtypeStruct((n_rows, width), h2.dtype),
        input_output_aliases={2: 0},
        compiler_params=_cparams(("arbitrary",)),
        name="dispatch",
    )(dest_flat, h2, xs0)


def _pack_bf16_pairs(x):
    c = x.shape[1] // 2
    bits = lax.bitcast_convert_type(x.astype(BF16).astype(F32), I32)
    return lax.shift_right_logical(bits[:, :c], 16) | bits[:, c:]


def _unpack_bf16_pairs(p):
    lo = lax.bitcast_convert_type(lax.shift_left(p, 16), F32).astype(BF16)
    hi = lax.bitcast_convert_type(p & jnp.int32(-65536), F32).astype(BF16)
    return lo, hi


def _expert_kernel(be_ref, nu_ref, xs_ref, bg_ref, bu_ref, bd_ref, wg_hbm, wu_hbm, wd_hbm, ys_ref,
                   cg, cu, cd, stg_g, stg_u, stg_d, xb_ref, sems, *, tf):
    i = pl.program_id(0)
    nj = D_FF // tf
    half = D_MODEL // 2
    e = be_ref[i]
    first = jnp.logical_or(i == 0, e != be_ref[jnp.maximum(i - 1, 0)])

    def tile_copies(t, slot):
        cols = pl.ds(pl.multiple_of(t * tf, tf), tf)
        return (pltpu.make_async_copy(wg_hbm.at[e, :, cols], stg_g.at[slot], sems.at[slot, 0]),
                pltpu.make_async_copy(wu_hbm.at[e, :, cols], stg_u.at[slot], sems.at[slot, 1]),
                pltpu.make_async_copy(wd_hbm.at[e, cols, :], stg_d.at[slot], sems.at[slot, 2]))

    @pl.when(i < nu_ref[0])
    def _():
        @pl.when(first)
        def _():
            for cp in tile_copies(0, 0):
                cp.start()

        lo, hi = _unpack_bf16_pairs(xs_ref[...])
        xb_ref[:, :half] = lo
        xb_ref[:, half:] = hi
        ys_ref[...] = jnp.broadcast_to(bd_ref[0], ys_ref.shape)

        def tile(t, carry):
            @pl.when(first)
            def _():
                slot = t % 2

                @pl.when(t + 1 < nj)
                def _():
                    for cp in tile_copies(t + 1, 1 - slot):
                        cp.start()

                for cp in tile_copies(t, slot):
                    cp.wait()
                cg[t] = stg_g[slot].astype(BF16)
                cu[t] = stg_u[slot].astype(BF16)
                cd[t] = stg_d[slot].astype(BF16)

            xb = xb_ref[...]
            g = jnp.minimum(_dot(xb, cg[t]) + bg_ref[0, t], SWIGLU_LIMIT)
            u = jnp.clip(_dot(xb, cu[t]) + bu_ref[0, t], -SWIGLU_LIMIT, SWIGLU_LIMIT)
            act = (u + 1.0) * (g * jax.nn.sigmoid(SWIGLU_ALPHA * g))
            ys_ref[...] += _dot(act.astype(BF16), cd[t])
            return carry

        lax.fori_loop(0, nj, tile, 0)

    @pl.when(i >= nu_ref[0])
    def _():
        ys_ref[...] = jnp.zeros(ys_ref.shape, F32)


def _experts(block_e, n_used, xs, w_e_gate, b_e_gate, w_e_up, b_e_up, w_e_down, b_e_down):
    n_rows = xs.shape[0]
    tm, tf = MOE_TM, MOE_TF
    n_blocks = n_rows // tm
    nj = D_FF // tf
    half = D_MODEL // 2

    def blk(i, nu):
        return jnp.minimum(i, nu[0] - 1)

    hbm = pl.BlockSpec(memory_space=pl.ANY)
    grid_spec = pltpu.PrefetchScalarGridSpec(
        num_scalar_prefetch=2,
        grid=(n_blocks,),
        in_specs=[pl.BlockSpec((tm, half), lambda i, be, nu: (blk(i, nu), 0)),
                  pl.BlockSpec((1, nj, 1, tf), lambda i, be, nu: (be[blk(i, nu)], 0, 0, 0)),
                  pl.BlockSpec((1, nj, 1, tf), lambda i, be, nu: (be[blk(i, nu)], 0, 0, 0)),
                  pl.BlockSpec((1, 1, D_MODEL), lambda i, be, nu: (be[blk(i, nu)], 0, 0)),
                  hbm, hbm, hbm],
        out_specs=pl.BlockSpec((tm, D_MODEL), lambda i, be, nu: (i, 0)),
        scratch_shapes=[pltpu.VMEM((nj, D_MODEL, tf), BF16), pltpu.VMEM((nj, D_MODEL, tf), BF16),
                        pltpu.VMEM((nj, tf, D_MODEL), BF16),
                        pltpu.VMEM((2, D_MODEL, tf), F32), pltpu.VMEM((2, D_MODEL, tf), F32),
                        pltpu.VMEM((2, tf, D_MODEL), F32),
                        pltpu.VMEM((tm, D_MODEL), BF16), pltpu.SemaphoreType.DMA((2, 3))],
    )
    return pl.pallas_call(
        functools.partial(_expert_kernel, tf=tf),
        grid_spec=grid_spec,
        out_shape=jax.ShapeDtypeStruct((n_rows, D_MODEL), F32),
        compiler_params=pltpu.CompilerParams(dimension_semantics=("arbitrary",), vmem_limit_bytes=EXPERT_VMEM_LIMIT),
        name="experts",
    )(block_e, n_used, xs, b_e_gate.reshape(N_EXPERTS, nj, 1, tf), b_e_up.reshape(N_EXPERTS, nj, 1, tf),
      b_e_down.reshape(N_EXPERTS, 1, D_MODEL), w_e_gate, w_e_up, w_e_down)


def _combine_kernel(dest_ref, x1_ref, gate_ref, ys_hbm, o_ref, buf, sems, *, tt, row0):
    i = pl.program_id(0)
    n_steps = pl.num_programs(0)
    slot = i % 2

    def gather(step, to_slot):
        base = (row0 + step * tt) * TOP_K

        def issue(r, carry):
            for k in range(TOP_K):
                src = dest_ref[base + r * TOP_K + k]
                pltpu.make_async_copy(ys_hbm.at[pl.ds(src, 1), :], buf.at[to_slot, k, pl.ds(r, 1), :],
                                      sems.at[to_slot]).start()
            return carry

        lax.fori_loop(0, tt, issue, 0, unroll=2)

    @pl.when(i == 0)
    def _():
        gather(0, 0)

    @pl.when(i + 1 < n_steps)
    def _():
        gather(i + 1, 1 - slot)

    for k in range(TOP_K):
        pltpu.make_async_copy(ys_hbm.at[pl.ds(0, tt), :], buf.at[slot, k], sems.at[slot]).wait()
    gate = gate_ref[...]
    acc = x1_ref[...]
    for k in range(TOP_K):
        acc = acc + gate[:, k:k + 1] * buf[slot, k]
    o_ref[...] = acc


def _combine(dest_flat, x1, gate, ys, row0, rows):
    tt = _pick(math.gcd(row0, rows), (128, 64, 32))
    blk0 = row0 // tt
    grid_spec = pltpu.PrefetchScalarGridSpec(
        num_scalar_prefetch=1,
        grid=(rows // tt,),
        in_specs=[pl.BlockSpec((tt, D_MODEL), lambda i, d: (blk0 + i, 0)),
                  pl.BlockSpec((tt, LANES), lambda i, d: (blk0 + i, 0)),
                  pl.BlockSpec(memory_space=pl.ANY)],
        out_specs=pl.BlockSpec((tt, D_MODEL), lambda i, d: (i, 0)),
        scratch_shapes=[pltpu.VMEM((2, TOP_K, tt, D_MODEL), F32), pltpu.SemaphoreType.DMA((2,))],
    )
    return pl.pallas_call(
        functools.partial(_combine_kernel, tt=tt, row0=row0),
        grid_spec=grid_spec,
        out_shape=jax.ShapeDtypeStruct((rows, D_MODEL), F32),
        compiler_params=_cparams(("arbitrary",)),
        name="combine",
    )(dest_flat, x1, gate, ys)


def _moe(x1, h2, idx, gate, w_e_gate, b_e_gate, w_e_up, b_e_up, w_e_down, b_e_down, n_first):
    m = x1.shape[0]
    tm = MOE_TM
    ranks, cnt = _ranks(idx)
    counts = cnt[0, :N_EXPERTS]
    padded = (counts + tm - 1) // tm * tm
    pad_end = jnp.cumsum(padded)
    pad_start = pad_end - padded
    e_sel = idx[:, :TOP_K]
    dest = (pad_start[e_sel] + ranks[:, :TOP_K]).astype(I32).reshape(-1)
    n_blocks = -(-(m * TOP_K + N_EXPERTS * (tm - 1)) // tm)
    block_row = jnp.arange(n_blocks, dtype=I32) * tm
    block_e = jnp.minimum(jnp.sum((pad_end[None, :] <= block_row[:, None]).astype(I32), axis=1), N_EXPERTS - 1)
    n_used = (pad_end[-1] // tm).astype(I32).reshape(1)
    xs = _dispatch(dest, h2, n_blocks * tm)
    ys = _experts(block_e, n_used, xs, w_e_gate, b_e_gate, w_e_up, b_e_up, w_e_down, b_e_down)
    return _combine(dest, x1, gate, ys, 0, n_first), _combine(dest, x1, gate, ys, n_first, m - n_first)


def kernel(x_prompt, x_sample, cache_k, cache_v, state_conv, state_ssm, norm1, w_in, q_norm, k_norm, sinks, conv_w,
           conv_b, dt_bias, a_log, d_skip, ssm_norm, w_gate, b_gate, w_attn_up, w_ssm_up, w_out, norm2, w_router,
           b_router, w_e_gate, b_e_gate, w_e_up, b_e_up, w_e_down, b_e_down):
    bp, sp, _ = x_prompt.shape
    bs, ss, _ = x_sample.shape
    tp, ts = bp * sp, bs * ss
    kv_len = cache_k.shape[2]
    l = 0

    x = jnp.concatenate([x_prompt.reshape(tp, D_MODEL), x_sample.reshape(ts, D_MODEL)], axis=0)
    pos = jnp.concatenate([jnp.tile(jnp.arange(sp, dtype=I32), bp), jnp.tile(PAST_LEN + jnp.arange(ss, dtype=I32), bs)])

    o1 = ATTN_WIDTH
    o2 = o1 + 2 * KV_WIDTH
    o3 = o2 + D_INNER
    o4 = o3 + CONV_DIM
    w = w_in[l]
    h = _rmsnorm(x, norm1[l], BF16)
    q_p = _matmul(h, w[:, :o1].astype(BF16), F32, "proj_q")
    kv_p = _matmul(h, w[:, o1:o2].astype(BF16), F32, "proj_kv")
    z = _matmul(h, w[:, o2:o3].astype(BF16), BF16, "proj_z")
    xbc = _matmul(h, w[:, o3:o4].astype(BF16), F32, "proj_xbc")
    dt_raw = _matmul(h, jnp.pad(w[:, o4:], ((0, 0), (0, LANES - SSM_HEADS))).astype(BF16), F32, "proj_dt")

    q_rot, k_rot = _qk_prep(q_p, kv_p, pos, q_norm[l], k_norm[l])
    sk = sinks[l].astype(F32)
    attn_p = _attn_prompt(q_rot, k_rot, kv_p, sk, bp, sp)
    ck = cache_k[l].reshape(bs, kv_len, KV_WIDTH)
    cv = cache_v[l].reshape(bs, kv_len, KV_WIDTH)
    attn_s = _attn_sample(q_rot, k_rot, kv_p, ck, cv, sk, tp, bs, ss)
    attn = jnp.concatenate([attn_p, attn_s], axis=0)

    ssd_w = (conv_w[l], conv_b[l], dt_bias[l], a_log[l], d_skip[l], ssm_norm[l])
    ssm_p, hfin_p = _ssd(xbc, dt_raw, z, *ssd_w, 0, bp, sp, CHUNK)
    ssm_s, hfin_s = _ssd(xbc, dt_raw, z, *ssd_w, tp, bs, ss, ss, state_conv[l], state_ssm[l])
    ssm = jnp.concatenate([ssm_p, ssm_s], axis=0)

    merged = _merge(h, attn, ssm, w_gate[l].astype(BF16), b_gate[l], w_attn_up[l].astype(BF16), w_ssm_up[l].astype(BF16))
    x1, h2, idx, gate = _outproj_router(merged, w_out[l].astype(BF16), x, norm2[l], w_router[l], b_router[l])
    y_p, y_s = _moe(x1, h2, idx, gate, w_e_gate[l], b_e_gate[l], w_e_up[l], b_e_up[l], w_e_down[l], b_e_down[l], tp)

    def prompt_tail(a, n):
        return jnp.stack([a[(b + 1) * sp - n:(b + 1) * sp] for b in range(bp)])

    heads = lambda a: a.reshape(a.shape[0], a.shape[1], N_KV_HEADS, HEAD_DIM)
    new_k_p = heads(prompt_tail(k_rot, WINDOW))
    new_v_p = heads(prompt_tail(kv_p, WINDOW)[:, :, KV_WIDTH:])
    ks4 = heads(k_rot[tp:].reshape(bs, ss, KV_WIDTH))
    vs4 = heads(kv_p[tp:, KV_WIDTH:].reshape(bs, ss, KV_WIDTH))
    new_k_s = jnp.concatenate([cache_k[l], ks4], axis=1)[:, -kv_len:]
    new_v_s = jnp.concatenate([cache_v[l], vs4], axis=1)[:, -kv_len:]
    xbc_s = jnp.concatenate([state_conv[l], xbc[tp:].reshape(bs, ss, CONV_DIM)], axis=1)
    return (y_p.reshape(bp, sp, D_MODEL), y_s.reshape(bs, ss, D_MODEL),
            new_k_p[None], new_v_p[None], prompt_tail(xbc, CONV_W - 1)[None], hfin_p[None],
            new_k_s[None], new_v_s[None], xbc_s[:, -(CONV_W - 1):][None], hfin_s[None])
```

```python
import functools
import math

import jax
import jax.numpy as jnp
from jax import lax
from jax.experimental import pallas as pl
from jax.experimental.pallas import tpu as pltpu

F32 = jnp.float32
BF16 = jnp.bfloat16
I32 = jnp.int32

D_MODEL = 2048
CHUNK = 64
N_HEADS = 32
N_KV_HEADS = 8
HEAD_DIM = 64
Q_PER_KV = N_HEADS // N_KV_HEADS
ATTN_WIDTH = N_HEADS * HEAD_DIM
KV_WIDTH = N_KV_HEADS * HEAD_DIM
WINDOW = 128
N_PREV_CHUNKS = WINDOW // CHUNK
ROPE_THETA = 500000.0
ROT_DIM = HEAD_DIM // 4
D_INNER = 2 * D_MODEL
SSM_HEAD_DIM = 64
SSM_HEADS = D_INNER // SSM_HEAD_DIM
SSM_GROUPS = 8
D_STATE = 128
CONV_W = 4
BC_WIDTH = SSM_GROUPS * D_STATE
CONV_DIM = D_INNER + 2 * BC_WIDTH
N_EXPERTS = 32
TOP_K = 4
D_FF = D_MODEL
SWIGLU_LIMIT = 7.0
SWIGLU_ALPHA = 1.702
EPS = 1e-6
NEG_INF = -1e30
PAST_LEN = 2048

LANES = 128
VMEM_LIMIT = 56 * 1024 * 1024
EXPERT_VMEM_LIMIT = 60 * 1024 * 1024
MOE_TM = 512
MOE_TF = 256


def _pick(n, cands):
    for c in cands:
        if n % c == 0:
            return c
    return n


def _cparams(sem):
    return pltpu.CompilerParams(dimension_semantics=sem, vmem_limit_bytes=VMEM_LIMIT)


def _split2(x):
    hi = x.astype(BF16)
    lo = (x - hi.astype(F32)).astype(BF16)
    return hi, lo


def _split3(x):
    hi = x.astype(BF16)
    r = x - hi.astype(F32)
    mid = r.astype(BF16)
    lo = (r - mid.astype(F32)).astype(BF16)
    return hi, mid, lo


def _dot(a, b):
    return jnp.dot(a, b, preferred_element_type=F32)


def _dot_nt(a, b):
    return lax.dot_general(a, b, (((1,), (1,)), ((), ())), preferred_element_type=F32)


def _silu(x):
    return x * jax.nn.sigmoid(x)


def _first_rows(n_first):
    return lambda i, *_: (jnp.minimum(i, n_first - 1), 0)


def _later_rows(n_first):
    return lambda i, *_: (jnp.maximum(i - n_first, 0), 0)


def _by_part(i, n_first, fn, first_refs, later_refs):
    @pl.when(i < n_first)
    def _():
        fn(*first_refs)

    @pl.when(i >= n_first)
    def _():
        fn(*later_refs)


def _rms_kernel(xa_ref, xb_ref, w_ref, o_ref, *, n_first):
    def body(x_ref):
        x = x_ref[...]
        ms = jnp.mean(x * x, axis=-1, keepdims=True)
        o_ref[...] = (x * lax.rsqrt(ms + EPS) * w_ref[...]).astype(o_ref.dtype)

    _by_part(pl.program_id(0), n_first, body, (xa_ref,), (xb_ref,))


def _rmsnorm(xa, xb, w, out_dtype):
    (ma, d), mb = xa.shape, xb.shape[0]
    tm = _pick(math.gcd(ma, mb), (512, 256, 128, 64, 32))
    nf = ma // tm
    return pl.pallas_call(
        functools.partial(_rms_kernel, n_first=nf),
        grid=((ma + mb) // tm,),
        in_specs=[pl.BlockSpec((tm, d), _first_rows(nf)), pl.BlockSpec((tm, d), _later_rows(nf)),
                  pl.BlockSpec((1, d), lambda i: (0, 0))],
        out_specs=pl.BlockSpec((tm, d), lambda i: (i, 0)),
        out_shape=jax.ShapeDtypeStruct((ma + mb, d), out_dtype),
        compiler_params=_cparams(("parallel",)),
        name="rmsnorm",
    )(xa, xb, w.reshape(1, d))


def _mm_kernel(x_ref, w_ref, o_ref):
    o_ref[...] = _dot(x_ref[...], w_ref[...]).astype(o_ref.dtype)


def _matmul(x, w, out_dtype, name):
    m, k = x.shape
    n = w.shape[1]
    tm = _pick(m, (1024, 512, 256, 128, 64, 32))
    tn = _pick(n, (1024, 512, 256, 128))
    return pl.pallas_call(
        _mm_kernel,
        grid=(m // tm, n // tn),
        in_specs=[pl.BlockSpec((tm, k), lambda i, j: (i, 0)), pl.BlockSpec((k, tn), lambda i, j: (0, j))],
        out_specs=pl.BlockSpec((tm, tn), lambda i, j: (i, j)),
        out_shape=jax.ShapeDtypeStruct((m, n), out_dtype),
        compiler_params=_cparams(("parallel", "parallel")),
        name=name,
    )(x, w)


def _qk_prep_kernel(q_ref, k_ref, cos_ref, s1_ref, s2_ref, qn_ref, kn_ref, g_ref, gt_ref, qo_ref, ko_ref):
    cos = cos_ref[...]
    s1 = s1_ref[...]
    s2 = s2_ref[...]

    def norm_rope(x, nw, width):
        g = g_ref[0:width, :]
        gt = gt_ref[:, 0:width]
        sq_hi, sq_lo = _split2(x * x)
        ssum = _dot(sq_hi, g) + _dot(sq_lo, g)
        r = lax.rsqrt(ssum * (1.0 / HEAD_DIM) + EPS)
        r_hi, r_lo = _split2(r)
        y = x * (_dot(r_hi, gt) + _dot(r_lo, gt)) * nw
        outs = []
        for s in range(width // LANES):
            blk = y[:, s * LANES:(s + 1) * LANES]
            outs.append(blk * cos + pltpu.roll(blk, ROT_DIM // 2, 1) * s1
                        + pltpu.roll(blk, LANES - ROT_DIM // 2, 1) * s2)
        return jnp.concatenate(outs, axis=1)

    q = norm_rope(q_ref[...], qn_ref[...], ATTN_WIDTH)
    qo_ref[...] = (q * (HEAD_DIM ** -0.5)).astype(qo_ref.dtype)
    ko_ref[...] = norm_rope(k_ref[...], kn_ref[...], KV_WIDTH)


def _qk_prep(q, k, pos, q_norm, k_norm):
    m = q.shape[0]
    tm = _pick(m, (256, 128, 64, 32))
    half = ROT_DIM // 2
    inv_freq = ROPE_THETA ** (-jnp.arange(half, dtype=F32) * 2.0 / ROT_DIM)
    ang = pos.astype(F32)[:, None] * inv_freq[None, :]
    cos, sin = jnp.cos(ang), jnp.sin(ang)
    ones = jnp.ones((m, HEAD_DIM - ROT_DIM), F32)
    zeros = jnp.zeros((m, HEAD_DIM - ROT_DIM), F32)
    zh = jnp.zeros((m, half), F32)
    cos_t = jnp.tile(jnp.concatenate([cos, cos, ones], axis=1), (1, LANES // HEAD_DIM))
    s1_t = jnp.tile(jnp.concatenate([zh, sin, zeros], axis=1), (1, LANES // HEAD_DIM))
    s2_t = jnp.tile(jnp.concatenate([-sin, zh, zeros], axis=1), (1, LANES // HEAD_DIM))
    head_of_lane = jnp.arange(ATTN_WIDTH) // HEAD_DIM
    g = (head_of_lane[:, None] == jnp.arange(LANES)[None, :]).astype(BF16)
    gt = g.T
    qn = jnp.tile(q_norm.astype(F32), N_HEADS).reshape(1, ATTN_WIDTH)
    kn = jnp.tile(k_norm.astype(F32), N_KV_HEADS).reshape(1, KV_WIDTH)
    row = lambda w: pl.BlockSpec((tm, w), lambda i: (i, 0))
    full = lambda a: pl.BlockSpec(a.shape, lambda i: (0, 0))
    return pl.pallas_call(
        _qk_prep_kernel,
        grid=(m // tm,),
        in_specs=[row(ATTN_WIDTH), row(KV_WIDTH), row(LANES), row(LANES), row(LANES),
                  full(qn), full(kn), full(g), full(gt)],
        out_specs=[row(ATTN_WIDTH), row(KV_WIDTH)],
        out_shape=[jax.ShapeDtypeStruct((m, ATTN_WIDTH), BF16), jax.ShapeDtypeStruct((m, KV_WIDTH), F32)],
        compiler_params=_cparams(("parallel",)),
        name="qk_prep",
    )(q, k, cos_t, s1_t, s2_t, qn, kn, g, gt)


def _attend(q, kk, vv, sinks_ref, valid):
    tq = q.shape[0]
    outs = []
    for j in range(N_KV_HEADS):
        kj = kk[:, j * HEAD_DIM:(j + 1) * HEAD_DIM]
        vj = vv[:, j * HEAD_DIM:(j + 1) * HEAD_DIM]
        heads = [Q_PER_KV * j + g for g in range(Q_PER_KV)]
        q4 = jnp.concatenate([q[:, h * HEAD_DIM:(h + 1) * HEAD_DIM] for h in heads], axis=0)
        s = _dot_nt(q4, kj)
        if valid is not None:
            s = jnp.where(valid, s, NEG_INF)
        sink = jnp.concatenate([jnp.full((tq, 1), sinks_ref[h], F32) for h in heads], axis=0)
        m = jnp.maximum(jnp.max(s, axis=-1, keepdims=True), sink)
        p = jnp.exp(s - m)
        denom = jnp.sum(p, axis=-1, keepdims=True) + jnp.exp(sink - m)
        o = _dot(p.astype(BF16), vj) / denom
        for g in range(Q_PER_KV):
            outs.append(o[g * tq:(g + 1) * tq, :])
    return jnp.concatenate(outs, axis=1)


def _attn_prompt_kernel(sinks_ref, q_ref, ka_ref, kb_ref, va_ref, vb_ref, o_ref, st_ref, pt_ref, rl_ref):
    i = pl.program_id(1)
    tq = q_ref.shape[0]
    tk = 2 * tq
    q = q_ref[...]
    k32 = jnp.concatenate([ka_ref[...], kb_ref[...]], axis=0)
    vt = jnp.concatenate([va_ref[...], vb_ref[...]], axis=0).T.astype(BF16)

    lane_k = lax.broadcasted_iota(I32, (tk, LANES), 1)
    kc = lax.broadcasted_iota(I32, (tk, LANES), 0) // CHUNK
    k_ind = jnp.where(lane_k == 0, jnp.where(kc == 0, 1.0, 0.0),
                      jnp.where(lane_k == 1, jnp.where(kc == N_PREV_CHUNKS + 1, 1.0, 0.0),
                                jnp.where(lane_k == 2, jnp.where(kc < N_PREV_CHUNKS, 1.0, 0.0), 0.0)))
    first_step = jnp.where(i == 0, 1.0, 0.0)
    k_ind = jnp.where(lane_k == 2, k_ind * first_step, k_ind).astype(BF16)
    lane_q = lax.broadcasted_iota(I32, (tq, LANES), 1)
    qc = lax.broadcasted_iota(I32, (tq, LANES), 0) // CHUNK
    q_msk = jnp.where(lane_q == 0, jnp.where(qc == 1, NEG_INF, 0.0),
                      jnp.where(lane_q == 1, jnp.where(qc == 0, NEG_INF, 0.0),
                                jnp.where(lane_q == 2, NEG_INF, 0.0))).astype(BF16)
    low = lane_k < HEAD_DIM

    for slab in range(KV_WIDTH // LANES):
        ks = k32[:, slab * LANES:(slab + 1) * LANES]
        kr = pltpu.roll(ks, HEAD_DIM, 1)
        for jj in range(2):
            j = 2 * slab + jj
            in_low = jnp.where(low, ks if jj == 0 else kr, 0.0).astype(BF16)
            in_high = jnp.where(low, 0.0, kr if jj == 0 else ks).astype(BF16)
            k_ext = (jnp.concatenate([in_low, k_ind], axis=1), jnp.concatenate([in_high, k_ind], axis=1))
            for g in range(Q_PER_KV):
                h = Q_PER_KV * j + g
                pair = h // 2
                q_ext = jnp.concatenate([q[:, pair * LANES:(pair + 1) * LANES], q_msk], axis=1)
                st_ref[h] = _dot_nt(k_ext[h % 2], q_ext)

    for h in range(N_HEADS):
        st = st_ref[h]
        sink = sinks_ref[h]
        m = jnp.maximum(jnp.max(st, axis=0, keepdims=True), sink)
        pt = jnp.exp(st - m)
        rl_ref[h:h + 1, :] = 1.0 / (jnp.sum(pt, axis=0, keepdims=True) + jnp.exp(sink - m))
        pt_ref[h] = pt.astype(BF16)

    for pair in range(N_HEADS // 2):
        pieces = []
        for h in (2 * pair, 2 * pair + 1):
            j = h // Q_PER_KV
            pieces.append(_dot(vt[j * HEAD_DIM:(j + 1) * HEAD_DIM, :], pt_ref[h]) * rl_ref[h:h + 1, :])
        o_ref[:, pair * LANES:(pair + 1) * LANES] = jnp.concatenate(pieces, axis=0).T.astype(o_ref.dtype)


def _attn_prompt(q, k, kv, sinks, bsz, seq):
    tq = 2 * CHUNK
    nb = seq // tq

    def kv_spec(back, col):
        return pl.BlockSpec((tq, KV_WIDTH), lambda b, i, s: (b * nb + jnp.maximum(i - back, 0), col))

    grid_spec = pltpu.PrefetchScalarGridSpec(
        num_scalar_prefetch=1,
        grid=(bsz, nb),
        in_specs=[pl.BlockSpec((tq, ATTN_WIDTH), lambda b, i, s: (b * nb + i, 0)),
                  kv_spec(1, 0), kv_spec(0, 0), kv_spec(1, 1), kv_spec(0, 1)],
        out_specs=pl.BlockSpec((tq, ATTN_WIDTH), lambda b, i, s: (b * nb + i, 0)),
        scratch_shapes=[pltpu.VMEM((N_HEADS, 2 * tq, tq), F32), pltpu.VMEM((N_HEADS, 2 * tq, tq), BF16),
                        pltpu.VMEM((N_HEADS, tq), F32)],
    )
    return pl.pallas_call(
        _attn_prompt_kernel,
        grid_spec=grid_spec,
        out_shape=jax.ShapeDtypeStruct((bsz * seq, ATTN_WIDTH), BF16),
        compiler_params=_cparams(("parallel", "parallel")),
        name="attn_prompt",
    )(sinks, q, k, k, kv, kv)


def _attn_sample_kernel(sinks_ref, q_ref, kn_ref, vn_ref, kc_ref, vc_ref, o_ref):
    kk = jnp.concatenate([kc_ref[0], kn_ref[...]], axis=0).astype(BF16)
    vv = jnp.concatenate([vc_ref[0], vn_ref[...]], axis=0).astype(BF16)
    o_ref[...] = _attend(q_ref[...], kk, vv, sinks_ref, None).astype(o_ref.dtype)


def _attn_sample(q, k, kv, cache_k, cache_v, sinks, row0, bsz, seq):
    blk0 = row0 // seq
    kv_len = cache_k.shape[1]
    new = lambda w, col: pl.BlockSpec((seq, w), lambda b, s: (blk0 + b, col))
    cache = pl.BlockSpec((1, kv_len, KV_WIDTH), lambda b, s: (b, 0, 0))
    grid_spec = pltpu.PrefetchScalarGridSpec(
        num_scalar_prefetch=1,
        grid=(bsz,),
        in_specs=[new(ATTN_WIDTH, 0), new(KV_WIDTH, 0), new(KV_WIDTH, 1), cache, cache],
        out_specs=pl.BlockSpec((seq, ATTN_WIDTH), lambda b, s: (b, 0)),
    )
    return pl.pallas_call(
        _attn_sample_kernel,
        grid_spec=grid_spec,
        out_shape=jax.ShapeDtypeStruct((bsz * seq, ATTN_WIDTH), BF16),
        compiler_params=_cparams(("parallel",)),
        name="attn_sample",
    )(sinks, q, k, kv, cache_k, cache_v)


def _pad_rows(x, rows):
    if x.shape[0] == rows:
        return x
    return jnp.concatenate([x, jnp.zeros((rows - x.shape[0], x.shape[1]), x.dtype)], axis=0)


def _ssd_kernel(*refs, clen, has_past):
    if has_past:
        (xbc_ref, dt_ref, z_ref, cw_ref, cb_ref, dtb_ref, alog_ref, dskip_ref, nw_ref, cpast_ref, hpast_ref,
         y_ref, hout_ref, xpad, u_ref, g_ref, ht_ref) = refs
    else:
        (xbc_ref, dt_ref, z_ref, cw_ref, cb_ref, dtb_ref, alog_ref, dskip_ref, nw_ref,
         y_ref, hout_ref, xpad, u_ref, g_ref, ht_ref) = refs
    L = clen
    c = pl.program_id(1)
    nc = pl.num_programs(1)
    n_pairs = SSM_HEADS // 2
    pairs_per_group = n_pairs // SSM_GROUPS
    gw = D_INNER // SSM_GROUPS

    @pl.when(c == 0)
    def _init():
        if has_past:
            xpad[5:8, :] = cpast_ref[0]
            for p in range(n_pairs):
                both = jnp.concatenate([hpast_ref[0, 2 * p], hpast_ref[0, 2 * p + 1]], axis=0)
                ht_ref[:, p * LANES:(p + 1) * LANES] = both.T
        else:
            xpad[0:8, :] = jnp.zeros((8, CONV_DIM), F32)
            ht_ref[...] = jnp.zeros(ht_ref.shape, F32)

    xpad[8:8 + L, :] = xbc_ref[...]
    cblk = 512
    for j in range(CONV_DIM // cblk):
        sl = slice(j * cblk, (j + 1) * cblk)
        acc = cb_ref[:, sl] + xpad[8:8 + L, sl] * cw_ref[3:4, sl]
        acc = acc + xpad[7:7 + L, sl] * cw_ref[2:3, sl]
        acc = acc + xpad[6:6 + L, sl] * cw_ref[1:2, sl]
        acc = acc + xpad[5:5 + L, sl] * cw_ref[0:1, sl]
        u_ref[:, sl] = _silu(acc)
    xpad[5:8, :] = xpad[5 + L:8 + L, :]

    dtx = dt_ref[...] + dtb_ref[...]
    dt = jnp.maximum(dtx, 0.0) + jnp.log(1.0 + jnp.exp(-jnp.abs(dtx)))
    loga = dt * (-jnp.exp(alog_ref[...]))
    t_idx = lax.broadcasted_iota(I32, (L, L), 0)
    s_idx = lax.broadcasted_iota(I32, (L, L), 1)
    incl = (s_idx <= t_idx).astype(BF16)
    acum = sum(_dot(incl, part) for part in _split3(loga))
    eacum = jnp.exp(acum)
    a_last = acum[L - 1:L, :]
    e_last = jnp.exp(a_last)
    wend = dt * jnp.exp(a_last - acum)
    acum_t = _pad_rows(acum, LANES).T[:, 0:L]

    lane_m = lax.broadcasted_iota(I32, (L, 2 * L), 1)
    row_m = lax.broadcasted_iota(I32, (L, 2 * L), 0)
    left_m = lane_m < L
    causal_m = jnp.where(left_m, lane_m, lane_m - L) <= row_m
    left = lax.broadcasted_iota(I32, (L, LANES), 1) < SSM_HEAD_DIM
    left_n = lax.broadcasted_iota(I32, (D_STATE, LANES), 1) < SSM_HEAD_DIM
    left_1 = lax.broadcasted_iota(I32, (1, LANES), 1) < SSM_HEAD_DIM

    for grp in range(SSM_GROUPS):
        b_g = u_ref[:, D_INNER + grp * D_STATE:D_INNER + (grp + 1) * D_STATE]
        c_g = u_ref[:, D_INNER + BC_WIDTH + grp * D_STATE:D_INNER + BC_WIDTH + (grp + 1) * D_STATE]
        b_bf = b_g.astype(BF16)
        cb = _dot_nt(c_g.astype(BF16), b_bf)
        cb2 = jnp.concatenate([cb, cb], axis=1)
        bt_bf = _pad_rows(b_g, LANES).T[:, 0:L].astype(BF16)
        for pp in range(pairs_per_group):
            p = grp * pairs_per_group + pp
            h0, h1 = 2 * p, 2 * p + 1
            sl = slice(p * LANES, (p + 1) * LANES)
            x_pair = u_ref[:, sl]
            colpair = lambda qty: jnp.where(left, qty[:, h0:h0 + 1], qty[:, h1:h1 + 1])
            xdt = x_pair * colpair(dt)
            col_a = jnp.where(left_m, acum[:, h0:h0 + 1], acum[:, h1:h1 + 1])
            row_a = jnp.concatenate([acum_t[h0:h0 + 1, :], acum_t[h1:h1 + 1, :]], axis=1)
            dec = jnp.exp(jnp.where(causal_m, col_a - row_a, NEG_INF))
            m_pair = (cb2 * dec).astype(BF16)
            e0 = (eacum[:, h0:h0 + 1] * c_g).astype(BF16)
            e1 = (eacum[:, h1:h1 + 1] * c_g).astype(BF16)
            lhs = jnp.concatenate([m_pair, e0, e1], axis=1)
            h_pair = ht_ref[:, sl]
            rhs = jnp.concatenate([jnp.where(left, xdt, 0.0), jnp.where(left, 0.0, xdt),
                                   jnp.where(left_n, h_pair, 0.0), jnp.where(left_n, 0.0, h_pair)],
                                  axis=0).astype(BF16)
            y = _dot(lhs, rhs) + x_pair * dskip_ref[:, sl]
            g_ref[:, sl] = y * _silu(z_ref[:, sl].astype(F32))
            xw = (x_pair * colpair(wend)).astype(BF16)
            row_dec = jnp.where(left_1, e_last[:, h0:h0 + 1], e_last[:, h1:h1 + 1])
            ht_ref[:, sl] = h_pair * row_dec + _dot(bt_bf, xw)
        gsl = slice(grp * gw, (grp + 1) * gw)
        gg = g_ref[:, gsl]
        ms = jnp.mean(gg * gg, axis=-1, keepdims=True)
        y_ref[:, gsl] = (gg * lax.rsqrt(ms + EPS) * nw_ref[:, gsl]).astype(y_ref.dtype)

    @pl.when(c == nc - 1)
    def _fin():
        for p in range(n_pairs):
            both = ht_ref[:, p * LANES:(p + 1) * LANES].T
            hout_ref[0, 2 * p] = both[0:SSM_HEAD_DIM, :]
            hout_ref[0, 2 * p + 1] = both[SSM_HEAD_DIM:2 * SSM_HEAD_DIM, :]


def _ssd(xbc, dt_raw, z, conv_w, conv_b, dt_bias, a_log, d_skip, ssm_norm, row0, bsz, seq, clen,
         conv_past=None, ssm_past=None):
    nc = seq // clen
    blk0 = row0 // clen
    has_past = conv_past is not None
    pad = lambda a: jnp.pad(a.astype(F32), (0, LANES - SSM_HEADS)).reshape(1, LANES)
    params = [conv_w.astype(F32), conv_b.astype(F32).reshape(1, CONV_DIM), pad(dt_bias), pad(a_log),
              jnp.repeat(d_skip.astype(F32), SSM_HEAD_DIM).reshape(1, D_INNER), ssm_norm.astype(F32).reshape(1, D_INNER)]
    rows = lambda w: pl.BlockSpec((clen, w), lambda b, c: (blk0 + b * nc + c, 0))
    full = lambda a: pl.BlockSpec(a.shape, lambda b, c: (0,) * a.ndim)
    in_specs = [rows(CONV_DIM), rows(LANES), rows(D_INNER)] + [full(a) for a in params]
    args = [xbc, dt_raw, z] + params
    if has_past:
        in_specs += [pl.BlockSpec((1, CONV_W - 1, CONV_DIM), lambda b, c: (b, 0, 0)),
                     pl.BlockSpec((1, SSM_HEADS, SSM_HEAD_DIM, D_STATE), lambda b, c: (b, 0, 0, 0))]
        args += [conv_past.astype(F32), ssm_past.astype(F32)]
    return pl.pallas_call(
        functools.partial(_ssd_kernel, clen=clen, has_past=has_past),
        grid=(bsz, nc),
        in_specs=in_specs,
        out_specs=[pl.BlockSpec((clen, D_INNER), lambda b, c: (b * nc + c, 0)),
                   pl.BlockSpec((1, SSM_HEADS, SSM_HEAD_DIM, D_STATE), lambda b, c: (b, 0, 0, 0))],
        out_shape=[jax.ShapeDtypeStruct((bsz * seq, D_INNER), BF16),
                   jax.ShapeDtypeStruct((bsz, SSM_HEADS, SSM_HEAD_DIM, D_STATE), F32)],
        scratch_shapes=[pltpu.VMEM((8 + clen, CONV_DIM), F32), pltpu.VMEM((clen, CONV_DIM), F32),
                        pltpu.VMEM((clen, D_INNER), F32), pltpu.VMEM((D_STATE, D_INNER), F32)],
        compiler_params=_cparams(("parallel", "arbitrary")),
        name="ssd_past" if has_past else "ssd_prompt",
    )(*args)


def _merge_kernel(h_ref, a1_ref, a2_ref, s1_ref, s2_ref, wga_ref, wgs_ref, bga_ref, bgs_ref, wa_ref, ws_ref, o_ref,
                  *, n_first):
    def body(a_ref, s_ref):
        h = h_ref[...]
        g_a = jax.nn.sigmoid(_dot(h, wga_ref[...]) + bga_ref[...])
        g_s = jax.nn.sigmoid(_dot(h, wgs_ref[...]) + bgs_ref[...])
        o_ref[...] = (g_a * _dot(a_ref[...], wa_ref[...]) + g_s * _dot(s_ref[...], ws_ref[...])).astype(o_ref.dtype)

    _by_part(pl.program_id(0), n_first, body, (a1_ref, s1_ref), (a2_ref, s2_ref))


def _merge(h, attn_parts, ssm_parts, w_gate, b_gate, w_attn_up, w_ssm_up):
    m = h.shape[0]
    m1, m2 = attn_parts[0].shape[0], attn_parts[1].shape[0]
    tm = _pick(math.gcd(m1, m2), (512, 256, 128, 64, 32))
    nf = m1 // tm
    tn = 512
    nj = D_MODEL // tn
    bg = b_gate.astype(F32).reshape(1, 2 * D_MODEL)
    return pl.pallas_call(
        functools.partial(_merge_kernel, n_first=nf),
        grid=(m // tm, nj),
        in_specs=[pl.BlockSpec((tm, D_MODEL), lambda i, j: (i, 0)),
                  pl.BlockSpec((tm, ATTN_WIDTH), _first_rows(nf)), pl.BlockSpec((tm, ATTN_WIDTH), _later_rows(nf)),
                  pl.BlockSpec((tm, D_INNER), _first_rows(nf)), pl.BlockSpec((tm, D_INNER), _later_rows(nf)),
                  pl.BlockSpec((D_MODEL, tn), lambda i, j: (0, j)),
                  pl.BlockSpec((D_MODEL, tn), lambda i, j: (0, j + nj)),
                  pl.BlockSpec((1, tn), lambda i, j: (0, j)),
                  pl.BlockSpec((1, tn), lambda i, j: (0, j + nj)),
                  pl.BlockSpec((ATTN_WIDTH, tn), lambda i, j: (0, j)),
                  pl.BlockSpec((D_INNER, tn), lambda i, j: (0, j))],
        out_specs=pl.BlockSpec((tm, tn), lambda i, j: (i, j)),
        out_shape=jax.ShapeDtypeStruct((m, D_MODEL), BF16),
        compiler_params=_cparams(("parallel", "parallel")),
        name="merge",
    )(h, attn_parts[0], attn_parts[1], ssm_parts[0], ssm_parts[1], w_gate, w_gate, bg, bg, w_attn_up, w_ssm_up)


def _outproj_router_kernel(mg_ref, wo_ref, xa_ref, xb_ref, n2_ref, wr_ref, br_ref, x1_ref, h2_ref, idx_ref, gate_ref,
                           *, n_first):
    def residual(x_ref):
        x1_ref[...] = x_ref[...] + _dot(mg_ref[...], wo_ref[...])

    _by_part(pl.program_id(0), n_first, residual, (xa_ref,), (xb_ref,))
    x1 = x1_ref[...]
    ms = jnp.mean(x1 * x1, axis=-1, keepdims=True)
    h2 = x1 * lax.rsqrt(ms + EPS) * n2_ref[...]
    h2_ref[...] = _pack_bf16_pairs(h2)
    h_hi, h_lo = _split2(h2)
    w_hi, w_lo = _split2(wr_ref[...])
    logits = _dot(h_hi, w_hi) + (_dot(h_hi, w_lo) + _dot(h_lo, w_hi)) + br_ref[...]
    tm = logits.shape[0]
    lane = lax.broadcasted_iota(I32, (tm, LANES), 1)
    logits = jnp.where(lane < N_EXPERTS, logits, -jnp.inf)
    idx_out = jnp.zeros((tm, LANES), I32)
    val_out = jnp.zeros((tm, LANES), F32)
    top = None
    for k in range(TOP_K):
        v = jnp.max(logits, axis=-1, keepdims=True)
        i = jnp.min(jnp.where(logits == v, lane, LANES), axis=-1, keepdims=True)
        if k == 0:
            top = v
        idx_out = jnp.where(lane == k, i, idx_out)
        val_out = jnp.where(lane == k, jnp.exp(v - top), val_out)
        logits = jnp.where(lane == i, -jnp.inf, logits)
    idx_ref[...] = idx_out
    gate_ref[...] = val_out / jnp.sum(val_out, axis=-1, keepdims=True)


def _outproj_router(merged, w_out, xa, xb, norm2, w_router, b_router):
    m = merged.shape[0]
    tm = _pick(math.gcd(xa.shape[0], xb.shape[0]), (256, 128, 64, 32))
    nf = xa.shape[0] // tm
    wr = jnp.pad(w_router.astype(F32), ((0, 0), (0, LANES - N_EXPERTS)))
    br = jnp.pad(b_router.astype(F32), (0, LANES - N_EXPERTS)).reshape(1, LANES)
    row = lambda w: pl.BlockSpec((tm, w), lambda i: (i, 0))
    full = lambda r, c: pl.BlockSpec((r, c), lambda i: (0, 0))
    return pl.pallas_call(
        functools.partial(_outproj_router_kernel, n_first=nf),
        grid=(m // tm,),
        in_specs=[row(D_MODEL), full(D_MODEL, D_MODEL),
                  pl.BlockSpec((tm, D_MODEL), _first_rows(nf)), pl.BlockSpec((tm, D_MODEL), _later_rows(nf)),
                  full(1, D_MODEL), full(D_MODEL, LANES), full(1, LANES)],
        out_specs=[row(D_MODEL), row(D_MODEL // 2), row(LANES), row(LANES)],
        out_shape=[jax.ShapeDtypeStruct((m, D_MODEL), F32), jax.ShapeDtypeStruct((m, D_MODEL // 2), I32),
                   jax.ShapeDtypeStruct((m, LANES), I32), jax.ShapeDtypeStruct((m, LANES), F32)],
        compiler_params=_cparams(("parallel",)),
        name="outproj_router",
    )(merged, w_out, xa, xb, norm2.astype(F32).reshape(1, D_MODEL), wr, br)


def _rank_kernel(idx_ref, rank_ref, cnt_ref, base_ref):
    i = pl.program_id(0)
    tt = idx_ref.shape[0]

    @pl.when(i == 0)
    def _():
        base_ref[...] = jnp.zeros(base_ref.shape, F32)

    idx = idx_ref[...]
    lane = lax.broadcasted_iota(I32, (tt, LANES), 1)
    sel = [lane == idx[:, k:k + 1] for k in range(TOP_K)]
    onehot = jnp.zeros((tt, LANES), F32)
    for k in range(TOP_K):
        onehot = jnp.where(sel[k], 1.0, onehot)
    r_idx = lax.broadcasted_iota(I32, (tt, tt), 0)
    c_idx = lax.broadcasted_iota(I32, (tt, tt), 1)
    before = (c_idx < r_idx).astype(BF16)
    rank_all = _dot(before, onehot.astype(BF16)) + base_ref[0:1, :]
    out = jnp.zeros((tt, LANES), F32)
    for k in range(TOP_K):
        out = jnp.where(lane == k, jnp.sum(jnp.where(sel[k], rank_all, 0.0), axis=-1, keepdims=True), out)
    rank_ref[...] = out.astype(I32)
    base_ref[0:1, :] = base_ref[0:1, :] + jnp.sum(onehot, axis=0, keepdims=True)
    cnt_ref[...] = base_ref[...].astype(I32)


def _ranks(idx):
    m = idx.shape[0]
    tt = _pick(m, (512, 256, 128, 64, 32))
    return pl.pallas_call(
        _rank_kernel,
        grid=(m // tt,),
        in_specs=[pl.BlockSpec((tt, LANES), lambda i: (i, 0))],
        out_specs=[pl.BlockSpec((tt, LANES), lambda i: (i, 0)), pl.BlockSpec((8, LANES), lambda i: (0, 0))],
        out_shape=[jax.ShapeDtypeStruct((m, LANES), I32), jax.ShapeDtypeStruct((8, LANES), I32)],
        scratch_shapes=[pltpu.VMEM((8, LANES), F32)],
        compiler_params=_cparams(("arbitrary",)),
        name="expert_ranks",
    )(idx)


def _dispatch_kernel(dest_ref, h_ref, xs_in_hbm, xs_hbm, sem, *, tt):
    del xs_in_hbm
    base = pl.program_id(0) * (tt * TOP_K)

    def issue(r, carry):
        for k in range(TOP_K):
            dst = dest_ref[base + r * TOP_K + k]
            pltpu.make_async_copy(h_ref.at[pl.ds(r, 1), :], xs_hbm.at[pl.ds(dst, 1), :], sem).start()
        return carry

    lax.fori_loop(0, tt, issue, 0, unroll=2)
    for _ in range(TOP_K):
        pltpu.make_async_copy(h_ref, xs_hbm.at[pl.ds(0, tt), :], sem).wait()


def _dispatch(dest_flat, h2, n_rows):
    m = h2.shape[0]
    tt = _pick(m, (256, 128, 64, 32))
    width = h2.shape[1]
    xs0 = jnp.zeros((n_rows, width), h2.dtype)
    grid_spec = pltpu.PrefetchScalarGridSpec(
        num_scalar_prefetch=1,
        grid=(m // tt,),
        in_specs=[pl.BlockSpec((tt, width), lambda i, d: (i, 0)), pl.BlockSpec(memory_space=pl.ANY)],
        out_specs=pl.BlockSpec(memory_space=pl.ANY),
        scratch_shapes=[pltpu.SemaphoreType.DMA],
    )
    return pl.pallas_call(
        functools.partial(_dispatch_kernel, tt=tt),
        grid_spec=grid_spec,
        out_shape=jax.ShapeDtypeStruct((n_rows, width), h2.dtype),
        input_output_aliases={2: 0},
        compiler_params=_cparams(("arbitrary",)),
        name="dispatch",
    )(dest_flat, h2, xs0)


def _pack_bf16_pairs(x):
    c = x.shape[1] // 2
    bits = lax.bitcast_convert_type(x.astype(BF16).astype(F32), I32)
    return lax.shift_right_logical(bits[:, :c], 16) | bits[:, c:]


def _unpack_bf16_pairs(p):
    lo = lax.bitcast_convert_type(lax.shift_left(p, 16), F32).astype(BF16)
    hi = lax.bitcast_convert_type(p & jnp.int32(-65536), F32).astype(BF16)
    return lo, hi


def _expert_kernel(be_ref, nu_ref, xs_ref, bg_ref, bu_ref, bd_ref, wg_hbm, wu_hbm, wd_hbm, ys_ref,
                   cg, cu, cd, stg_g, stg_u, stg_d, xb_ref, sems, *, tf):
    i = pl.program_id(0)
    nj = D_FF // tf
    half = D_MODEL // 2
    e = be_ref[i]
    first = jnp.logical_or(i == 0, e != be_ref[jnp.maximum(i - 1, 0)])

    def tile_copies(t, slot):
        cols = pl.ds(pl.multiple_of(t * tf, tf), tf)
        return (pltpu.make_async_copy(wg_hbm.at[e, :, cols], stg_g.at[slot], sems.at[slot, 0]),
                pltpu.make_async_copy(wu_hbm.at[e, :, cols], stg_u.at[slot], sems.at[slot, 1]),
                pltpu.make_async_copy(wd_hbm.at[e, cols, :], stg_d.at[slot], sems.at[slot, 2]))

    @pl.when(i < nu_ref[0])
    def _():
        @pl.when(first)
        def _():
            for cp in tile_copies(0, 0):
                cp.start()

        lo, hi = _unpack_bf16_pairs(xs_ref[...])
        xb_ref[:, :half] = lo
        xb_ref[:, half:] = hi
        ys_ref[...] = jnp.broadcast_to(bd_ref[0], ys_ref.shape)

        def tile(t, carry):
            @pl.when(first)
            def _():
                slot = t % 2

                @pl.when(t + 1 < nj)
                def _():
                    for cp in tile_copies(t + 1, 1 - slot):
                        cp.start()

                for cp in tile_copies(t, slot):
                    cp.wait()
                cg[t] = stg_g[slot].astype(BF16)
                cu[t] = stg_u[slot].astype(BF16)
                cd[t] = stg_d[slot].astype(BF16)

            xb = xb_ref[...]
            g = jnp.minimum(_dot(xb, cg[t]) + bg_ref[0, t], SWIGLU_LIMIT)
            u = jnp.clip(_dot(xb, cu[t]) + bu_ref[0, t], -SWIGLU_LIMIT, SWIGLU_LIMIT)
            act = (u + 1.0) * (g * jax.nn.sigmoid(SWIGLU_ALPHA * g))
            ys_ref[...] += _dot(act.astype(BF16), cd[t])
            return carry

        lax.fori_loop(0, nj, tile, 0)

    @pl.when(i >= nu_ref[0])
    def _():
        ys_ref[...] = jnp.zeros(ys_ref.shape, F32)


def _experts(block_e, n_used, xs, w_e_gate, b_e_gate, w_e_up, b_e_up, w_e_down, b_e_down):
    n_rows = xs.shape[0]
    tm, tf = MOE_TM, MOE_TF
    n_blocks = n_rows // tm
    nj = D_FF // tf
    half = D_MODEL // 2

    def blk(i, nu):
        return jnp.minimum(i, nu[0] - 1)

    hbm = pl.BlockSpec(memory_space=pl.ANY)
    grid_spec = pltpu.PrefetchScalarGridSpec(
        num_scalar_prefetch=2,
        grid=(n_blocks,),
        in_specs=[pl.BlockSpec((tm, half), lambda i, be, nu: (blk(i, nu), 0)),
                  pl.BlockSpec((1, nj, 1, tf), lambda i, be, nu: (be[blk(i, nu)], 0, 0, 0)),
                  pl.BlockSpec((1, nj, 1, tf), lambda i, be, nu: (be[blk(i, nu)], 0, 0, 0)),
                  pl.BlockSpec((1, 1, D_MODEL), lambda i, be, nu: (be[blk(i, nu)], 0, 0)),
                  hbm, hbm, hbm],
        out_specs=pl.BlockSpec((tm, D_MODEL), lambda i, be, nu: (i, 0)),
        scratch_shapes=[pltpu.VMEM((nj, D_MODEL, tf), BF16), pltpu.VMEM((nj, D_MODEL, tf), BF16),
                        pltpu.VMEM((nj, tf, D_MODEL), BF16),
                        pltpu.VMEM((2, D_MODEL, tf), F32), pltpu.VMEM((2, D_MODEL, tf), F32),
                        pltpu.VMEM((2, tf, D_MODEL), F32),
                        pltpu.VMEM((tm, D_MODEL), BF16), pltpu.SemaphoreType.DMA((2, 3))],
    )
    return pl.pallas_call(
        functools.partial(_expert_kernel, tf=tf),
        grid_spec=grid_spec,
        out_shape=jax.ShapeDtypeStruct((n_rows, D_MODEL), F32),
        compiler_params=pltpu.CompilerParams(dimension_semantics=("arbitrary",), vmem_limit_bytes=EXPERT_VMEM_LIMIT),
        name="experts",
    )(block_e, n_used, xs, b_e_gate.reshape(N_EXPERTS, nj, 1, tf), b_e_up.reshape(N_EXPERTS, nj, 1, tf),
      b_e_down.reshape(N_EXPERTS, 1, D_MODEL), w_e_gate, w_e_up, w_e_down)


def _combine_kernel(dest_ref, x1_ref, gate_ref, ys_hbm, o_ref, buf, sems, *, tt, row0):
    i = pl.program_id(0)
    n_steps = pl.num_programs(0)
    slot = i % 2

    def gather(step, to_slot):
        base = (row0 + step * tt) * TOP_K

        def issue(r, carry):
            for k in range(TOP_K):
                src = dest_ref[base + r * TOP_K + k]
                pltpu.make_async_copy(ys_hbm.at[pl.ds(src, 1), :], buf.at[to_slot, k, pl.ds(r, 1), :],
                                      sems.at[to_slot]).start()
            return carry

        lax.fori_loop(0, tt, issue, 0, unroll=2)

    @pl.when(i == 0)
    def _():
        gather(0, 0)

    @pl.when(i + 1 < n_steps)
    def _():
        gather(i + 1, 1 - slot)

    for k in range(TOP_K):
        pltpu.make_async_copy(ys_hbm.at[pl.ds(0, tt), :], buf.at[slot, k], sems.at[slot]).wait()
    gate = gate_ref[...]
    acc = x1_ref[...]
    for k in range(TOP_K):
        acc = acc + gate[:, k:k + 1] * buf[slot, k]
    o_ref[...] = acc


def _combine(dest_flat, x1, gate, ys, row0, rows):
    tt = _pick(math.gcd(row0, rows), (128, 64, 32))
    blk0 = row0 // tt
    grid_spec = pltpu.PrefetchScalarGridSpec(
        num_scalar_prefetch=1,
        grid=(rows // tt,),
        in_specs=[pl.BlockSpec((tt, D_MODEL), lambda i, d: (blk0 + i, 0)),
                  pl.BlockSpec((tt, LANES), lambda i, d: (blk0 + i, 0)),
                  pl.BlockSpec(memory_space=pl.ANY)],
        out_specs=pl.BlockSpec((tt, D_MODEL), lambda i, d: (i, 0)),
        scratch_shapes=[pltpu.VMEM((2, TOP_K, tt, D_MODEL), F32), pltpu.SemaphoreType.DMA((2,))],
    )
    return pl.pallas_call(
        functools.partial(_combine_kernel, tt=tt, row0=row0),
        grid_spec=grid_spec,
        out_shape=jax.ShapeDtypeStruct((rows, D_MODEL), F32),
        compiler_params=_cparams(("arbitrary",)),
        name="combine",
    )(dest_flat, x1, gate, ys)


def _moe(x1, h2, idx, gate, w_e_gate, b_e_gate, w_e_up, b_e_up, w_e_down, b_e_down, n_first):
    m = x1.shape[0]
    tm = MOE_TM
    ranks, cnt = _ranks(idx)
    counts = cnt[0, :N_EXPERTS]
    padded = (counts + tm - 1) // tm * tm
    pad_end = jnp.cumsum(padded)
    pad_start = pad_end - padded
    e_sel = idx[:, :TOP_K]
    dest = (pad_start[e_sel] + ranks[:, :TOP_K]).astype(I32).reshape(-1)
    n_blocks = -(-(m * TOP_K + N_EXPERTS * (tm - 1)) // tm)
    block_row = jnp.arange(n_blocks, dtype=I32) * tm
    block_e = jnp.minimum(jnp.sum((pad_end[None, :] <= block_row[:, None]).astype(I32), axis=1), N_EXPERTS - 1)
    n_used = (pad_end[-1] // tm).astype(I32).reshape(1)
    xs = _dispatch(dest, h2, n_blocks * tm)
    ys = _experts(block_e, n_used, xs, w_e_gate, b_e_gate, w_e_up, b_e_up, w_e_down, b_e_down)
    return _combine(dest, x1, gate, ys, 0, n_first), _combine(dest, x1, gate, ys, n_first, m - n_first)


def kernel(x_prompt, x_sample, cache_k, cache_v, state_conv, state_ssm, norm1, w_in, q_norm, k_norm, sinks, conv_w,
           conv_b, dt_bias, a_log, d_skip, ssm_norm, w_gate, b_gate, w_attn_up, w_ssm_up, w_out, norm2, w_router,
           b_router, w_e_gate, b_e_gate, w_e_up, b_e_up, w_e_down, b_e_down):
    bp, sp, _ = x_prompt.shape
    bs, ss, _ = x_sample.shape
    tp, ts = bp * sp, bs * ss
    kv_len = cache_k.shape[2]
    l = 0

    xa, xb = x_prompt.reshape(tp, D_MODEL), x_sample.reshape(ts, D_MODEL)
    pos = jnp.concatenate([jnp.tile(jnp.arange(sp, dtype=I32), bp), jnp.tile(PAST_LEN + jnp.arange(ss, dtype=I32), bs)])

    o1 = ATTN_WIDTH
    o2 = o1 + 2 * KV_WIDTH
    o3 = o2 + D_INNER
    o4 = o3 + CONV_DIM
    w = w_in[l]
    h = _rmsnorm(xa, xb, norm1[l], BF16)
    q_p = _matmul(h, w[:, :o1].astype(BF16), F32, "proj_q")
    kv_p = _matmul(h, w[:, o1:o2].astype(BF16), F32, "proj_kv")
    z = _matmul(h, w[:, o2:o3].astype(BF16), BF16, "proj_z")
    xbc = _matmul(h, w[:, o3:o4].astype(BF16), F32, "proj_xbc")
    dt_raw = _matmul(h, jnp.pad(w[:, o4:], ((0, 0), (0, LANES - SSM_HEADS))).astype(BF16), F32, "proj_dt")

    q_rot, k_rot = _qk_prep(q_p, kv_p, pos, q_norm[l], k_norm[l])
    sk = sinks[l].astype(F32)
    attn_p = _attn_prompt(q_rot, k_rot, kv_p, sk, bp, sp)
    ck = cache_k[l].reshape(bs, kv_len, KV_WIDTH)
    cv = cache_v[l].reshape(bs, kv_len, KV_WIDTH)
    attn_s = _attn_sample(q_rot, k_rot, kv_p, ck, cv, sk, tp, bs, ss)

    ssd_w = (conv_w[l], conv_b[l], dt_bias[l], a_log[l], d_skip[l], ssm_norm[l])
    ssm_p, hfin_p = _ssd(xbc, dt_raw, z, *ssd_w, 0, bp, sp, CHUNK)
    ssm_s, hfin_s = _ssd(xbc, dt_raw, z, *ssd_w, tp, bs, ss, ss, state_conv[l], state_ssm[l])

    merged = _merge(h, (attn_p, attn_s), (ssm_p, ssm_s), w_gate[l].astype(BF16), b_gate[l], w_attn_up[l].astype(BF16), w_ssm_up[l].astype(BF16))
    x1, h2, idx, gate = _outproj_router(merged, w_out[l].astype(BF16), xa, xb, norm2[l], w_router[l], b_router[l])
    y_p, y_s = _moe(x1, h2, idx, gate, w_e_gate[l], b_e_gate[l], w_e_up[l], b_e_up[l], w_e_down[l], b_e_down[l], tp)

    def prompt_tail(a, n):
        return jnp.stack([a[(b + 1) * sp - n:(b + 1) * sp] for b in range(bp)])

    heads = lambda a: a.reshape(a.shape[0], a.shape[1], N_KV_HEADS, HEAD_DIM)
    new_k_p = heads(prompt_tail(k_rot, WINDOW))
    new_v_p = heads(prompt_tail(kv_p, WINDOW)[:, :, KV_WIDTH:])
    ks4 = heads(k_rot[tp:].reshape(bs, ss, KV_WIDTH))
    vs4 = heads(kv_p[tp:, KV_WIDTH:].reshape(bs, ss, KV_WIDTH))
    new_k_s = jnp.concatenate([cache_k[l], ks4], axis=1)[:, -kv_len:]
    new_v_s = jnp.concatenate([cache_v[l], vs4], axis=1)[:, -kv_len:]
    xbc_s = jnp.concatenate([state_conv[l], xbc[tp:].reshape(bs, ss, CONV_DIM)], axis=1)
    return (y_p.reshape(bp, sp, D_MODEL), y_s.reshape(bs, ss, D_MODEL),
            new_k_p[None], new_v_p[None], prompt_tail(xbc, CONV_W - 1)[None], hfin_p[None],
            new_k_s[None], new_v_s[None], xbc_s[:, -(CONV_W - 1):][None], hfin_s[None])
```

```python
import functools
import math

import jax
import jax.numpy as jnp
from jax import lax
from jax.experimental import pallas as pl
from jax.experimental.pallas import tpu as pltpu

F32 = jnp.float32
BF16 = jnp.bfloat16
I32 = jnp.int32

D_MODEL = 2048
CHUNK = 64
N_HEADS = 32
N_KV_HEADS = 8
HEAD_DIM = 64
Q_PER_KV = N_HEADS // N_KV_HEADS
ATTN_WIDTH = N_HEADS * HEAD_DIM
KV_WIDTH = N_KV_HEADS * HEAD_DIM
WINDOW = 128
N_PREV_CHUNKS = WINDOW // CHUNK
ROPE_THETA = 500000.0
ROT_DIM = HEAD_DIM // 4
D_INNER = 2 * D_MODEL
SSM_HEAD_DIM = 64
SSM_HEADS = D_INNER // SSM_HEAD_DIM
SSM_GROUPS = 8
D_STATE = 128
CONV_W = 4
BC_WIDTH = SSM_GROUPS * D_STATE
CONV_DIM = D_INNER + 2 * BC_WIDTH
N_EXPERTS = 32
TOP_K = 4
D_FF = D_MODEL
SWIGLU_LIMIT = 7.0
SWIGLU_ALPHA = 1.702
EPS = 1e-6
NEG_INF = -1e30
PAST_LEN = 2048

LANES = 128
VMEM_LIMIT = 56 * 1024 * 1024
EXPERT_VMEM_LIMIT = 60 * 1024 * 1024
MOE_TM = 512
MOE_TF = 256


def _pick(n, cands):
    for c in cands:
        if n % c == 0:
            return c
    return n


def _cparams(sem):
    return pltpu.CompilerParams(dimension_semantics=sem, vmem_limit_bytes=VMEM_LIMIT)


def _split2(x):
    hi = x.astype(BF16)
    lo = (x - hi.astype(F32)).astype(BF16)
    return hi, lo


def _split3(x):
    hi = x.astype(BF16)
    r = x - hi.astype(F32)
    mid = r.astype(BF16)
    lo = (r - mid.astype(F32)).astype(BF16)
    return hi, mid, lo


def _dot(a, b):
    return jnp.dot(a, b, preferred_element_type=F32)


def _dot_nt(a, b):
    return lax.dot_general(a, b, (((1,), (1,)), ((), ())), preferred_element_type=F32)


def _silu(x):
    half = 0.5 * x
    return half + half * jnp.tanh(half)


def _first_rows(n_first):
    return lambda i, *_: (jnp.minimum(i, n_first - 1), 0)


def _later_rows(n_first):
    return lambda i, *_: (jnp.maximum(i - n_first, 0), 0)


def _by_part(i, n_first, fn, first_refs, later_refs):
    @pl.when(i < n_first)
    def _():
        fn(*first_refs)

    @pl.when(i >= n_first)
    def _():
        fn(*later_refs)


def _rms_kernel(xa_ref, xb_ref, w_ref, o_ref, *, n_first):
    def body(x_ref):
        x = x_ref[...]
        ms = jnp.mean(x * x, axis=-1, keepdims=True)
        o_ref[...] = (x * lax.rsqrt(ms + EPS) * w_ref[...]).astype(o_ref.dtype)

    _by_part(pl.program_id(0), n_first, body, (xa_ref,), (xb_ref,))


def _rmsnorm(xa, xb, w, out_dtype):
    (ma, d), mb = xa.shape, xb.shape[0]
    tm = _pick(math.gcd(ma, mb), (512, 256, 128, 64, 32))
    nf = ma // tm
    return pl.pallas_call(
        functools.partial(_rms_kernel, n_first=nf),
        grid=((ma + mb) // tm,),
        in_specs=[pl.BlockSpec((tm, d), _first_rows(nf)), pl.BlockSpec((tm, d), _later_rows(nf)),
                  pl.BlockSpec((1, d), lambda i: (0, 0))],
        out_specs=pl.BlockSpec((tm, d), lambda i: (i, 0)),
        out_shape=jax.ShapeDtypeStruct((ma + mb, d), out_dtype),
        compiler_params=_cparams(("parallel",)),
        name="rmsnorm",
    )(xa, xb, w.reshape(1, d))


def _mm_kernel(x_ref, w_ref, o_ref, wb_ref):
    @pl.when(pl.program_id(1) == 0)
    def _():
        wb_ref[...] = w_ref[...].astype(BF16)

    o_ref[...] = _dot(x_ref[...], wb_ref[...]).astype(o_ref.dtype)


def _matmul(x, w, col0, n, out_dtype, name):
    m, k = x.shape
    tm = _pick(m, (1024, 512, 256, 128, 64, 32))
    tn = _pick(math.gcd(n, col0), (1024, 512, 256, 128))
    cb0 = col0 // tn
    return pl.pallas_call(
        _mm_kernel,
        grid=(n // tn, m // tm),
        in_specs=[pl.BlockSpec((tm, k), lambda j, i: (i, 0)), pl.BlockSpec((k, tn), lambda j, i: (0, cb0 + j))],
        out_specs=pl.BlockSpec((tm, tn), lambda j, i: (i, j)),
        out_shape=jax.ShapeDtypeStruct((m, n), out_dtype),
        scratch_shapes=[pltpu.VMEM((k, tn), BF16)],
        compiler_params=_cparams(("parallel", "arbitrary")),
        name=name,
    )(x, w)


def _qk_prep_kernel(q_ref, k_ref, cos_ref, s1_ref, s2_ref, qn_ref, kn_ref, g_ref, gt_ref, qo_ref, ko_ref):
    cos = cos_ref[...]
    s1 = s1_ref[...]
    s2 = s2_ref[...]

    def norm_rope(x, nw, width):
        g = g_ref[0:width, :]
        gt = gt_ref[:, 0:width]
        sq_hi, sq_lo = _split2(x * x)
        ssum = _dot(sq_hi, g) + _dot(sq_lo, g)
        r = lax.rsqrt(ssum * (1.0 / HEAD_DIM) + EPS)
        r_hi, r_lo = _split2(r)
        y = x * (_dot(r_hi, gt) + _dot(r_lo, gt)) * nw
        outs = []
        for s in range(width // LANES):
            blk = y[:, s * LANES:(s + 1) * LANES]
            outs.append(blk * cos + pltpu.roll(blk, ROT_DIM // 2, 1) * s1
                        + pltpu.roll(blk, LANES - ROT_DIM // 2, 1) * s2)
        return jnp.concatenate(outs, axis=1)

    q = norm_rope(q_ref[...], qn_ref[...], ATTN_WIDTH)
    qo_ref[...] = (q * (HEAD_DIM ** -0.5)).astype(qo_ref.dtype)
    ko_ref[...] = norm_rope(k_ref[...], kn_ref[...], KV_WIDTH)


def _qk_prep(q, k, pos, q_norm, k_norm):
    m = q.shape[0]
    tm = _pick(m, (256, 128, 64, 32))
    half = ROT_DIM // 2
    inv_freq = ROPE_THETA ** (-jnp.arange(half, dtype=F32) * 2.0 / ROT_DIM)
    ang = pos.astype(F32)[:, None] * inv_freq[None, :]
    cos, sin = jnp.cos(ang), jnp.sin(ang)
    ones = jnp.ones((m, HEAD_DIM - ROT_DIM), F32)
    zeros = jnp.zeros((m, HEAD_DIM - ROT_DIM), F32)
    zh = jnp.zeros((m, half), F32)
    cos_t = jnp.tile(jnp.concatenate([cos, cos, ones], axis=1), (1, LANES // HEAD_DIM))
    s1_t = jnp.tile(jnp.concatenate([zh, sin, zeros], axis=1), (1, LANES // HEAD_DIM))
    s2_t = jnp.tile(jnp.concatenate([-sin, zh, zeros], axis=1), (1, LANES // HEAD_DIM))
    head_of_lane = jnp.arange(ATTN_WIDTH) // HEAD_DIM
    g = (head_of_lane[:, None] == jnp.arange(LANES)[None, :]).astype(BF16)
    gt = g.T
    qn = jnp.tile(q_norm.astype(F32), N_HEADS).reshape(1, ATTN_WIDTH)
    kn = jnp.tile(k_norm.astype(F32), N_KV_HEADS).reshape(1, KV_WIDTH)
    row = lambda w: pl.BlockSpec((tm, w), lambda i: (i, 0))
    full = lambda a: pl.BlockSpec(a.shape, lambda i: (0, 0))
    return pl.pallas_call(
        _qk_prep_kernel,
        grid=(m // tm,),
        in_specs=[row(ATTN_WIDTH), row(KV_WIDTH), row(LANES), row(LANES), row(LANES),
                  full(qn), full(kn), full(g), full(gt)],
        out_specs=[row(ATTN_WIDTH), row(KV_WIDTH)],
        out_shape=[jax.ShapeDtypeStruct((m, ATTN_WIDTH), BF16), jax.ShapeDtypeStruct((m, KV_WIDTH), F32)],
        compiler_params=_cparams(("parallel",)),
        name="qk_prep",
    )(q, k, cos_t, s1_t, s2_t, qn, kn, g, gt)


def _attend(q, kk, vv, sinks_ref, valid):
    tq = q.shape[0]
    outs = []
    for j in range(N_KV_HEADS):
        kj = kk[:, j * HEAD_DIM:(j + 1) * HEAD_DIM]
        vj = vv[:, j * HEAD_DIM:(j + 1) * HEAD_DIM]
        heads = [Q_PER_KV * j + g for g in range(Q_PER_KV)]
        q4 = jnp.concatenate([q[:, h * HEAD_DIM:(h + 1) * HEAD_DIM] for h in heads], axis=0)
        s = _dot_nt(q4, kj)
        if valid is not None:
            s = jnp.where(valid, s, NEG_INF)
        sink = jnp.concatenate([jnp.full((tq, 1), sinks_ref[h], F32) for h in heads], axis=0)
        m = jnp.maximum(jnp.max(s, axis=-1, keepdims=True), sink)
        p = jnp.exp(s - m)
        denom = jnp.sum(p, axis=-1, keepdims=True) + jnp.exp(sink - m)
        o = _dot(p.astype(BF16), vj) / denom
        for g in range(Q_PER_KV):
            outs.append(o[g * tq:(g + 1) * tq, :])
    return jnp.concatenate(outs, axis=1)


def _attn_prompt_kernel(sinks_ref, q_ref, ka_ref, kb_ref, va_ref, vb_ref, o_ref, st_ref, pt_ref, rl_ref):
    i = pl.program_id(1)
    tq = q_ref.shape[0]
    tk = 2 * tq
    q = q_ref[...]
    k32 = jnp.concatenate([ka_ref[...], kb_ref[...]], axis=0)
    vt = jnp.concatenate([va_ref[...], vb_ref[...]], axis=0).T.astype(BF16)

    lane_k = lax.broadcasted_iota(I32, (tk, LANES), 1)
    kc = lax.broadcasted_iota(I32, (tk, LANES), 0) // CHUNK
    k_ind = jnp.where(lane_k == 0, jnp.where(kc == 0, 1.0, 0.0),
                      jnp.where(lane_k == 1, jnp.where(kc == N_PREV_CHUNKS + 1, 1.0, 0.0),
                                jnp.where(lane_k == 2, jnp.where(kc < N_PREV_CHUNKS, 1.0, 0.0), 0.0)))
    first_step = jnp.where(i == 0, 1.0, 0.0)
    k_ind = jnp.where(lane_k == 2, k_ind * first_step, k_ind).astype(BF16)
    lane_q = lax.broadcasted_iota(I32, (tq, LANES), 1)
    qc = lax.broadcasted_iota(I32, (tq, LANES), 0) // CHUNK
    q_msk = jnp.where(lane_q == 0, jnp.where(qc == 1, NEG_INF, 0.0),
                      jnp.where(lane_q == 1, jnp.where(qc == 0, NEG_INF, 0.0),
                                jnp.where(lane_q == 2, NEG_INF, 0.0))).astype(BF16)
    low = lane_k < HEAD_DIM

    for slab in range(KV_WIDTH // LANES):
        ks = k32[:, slab * LANES:(slab + 1) * LANES]
        kr = pltpu.roll(ks, HEAD_DIM, 1)
        for jj in range(2):
            j = 2 * slab + jj
            in_low = jnp.where(low, ks if jj == 0 else kr, 0.0).astype(BF16)
            in_high = jnp.where(low, 0.0, kr if jj == 0 else ks).astype(BF16)
            k_ext = (jnp.concatenate([in_low, k_ind], axis=1), jnp.concatenate([in_high, k_ind], axis=1))
            for g in range(Q_PER_KV):
                h = Q_PER_KV * j + g
                pair = h // 2
                q_ext = jnp.concatenate([q[:, pair * LANES:(pair + 1) * LANES], q_msk], axis=1)
                st_ref[h] = _dot_nt(k_ext[h % 2], q_ext)

    for h in range(N_HEADS):
        st = st_ref[h]
        sink = sinks_ref[h]
        m = jnp.maximum(jnp.max(st, axis=0, keepdims=True), sink)
        pt = jnp.exp(st - m)
        rl_ref[h:h + 1, :] = 1.0 / (jnp.sum(pt, axis=0, keepdims=True) + jnp.exp(sink - m))
        pt_ref[h] = pt.astype(BF16)

    for pair in range(N_HEADS // 2):
        pieces = []
        for h in (2 * pair, 2 * pair + 1):
            j = h // Q_PER_KV
            pieces.append(_dot(vt[j * HEAD_DIM:(j + 1) * HEAD_DIM, :], pt_ref[h]) * rl_ref[h:h + 1, :])
        o_ref[:, pair * LANES:(pair + 1) * LANES] = jnp.concatenate(pieces, axis=0).T.astype(o_ref.dtype)


def _attn_prompt(q, k, kv, sinks, bsz, seq):
    tq = 2 * CHUNK
    nb = seq // tq

    def kv_spec(back, col):
        return pl.BlockSpec((tq, KV_WIDTH), lambda b, i, s: (b * nb + jnp.maximum(i - back, 0), col))

    grid_spec = pltpu.PrefetchScalarGridSpec(
        num_scalar_prefetch=1,
        grid=(bsz, nb),
        in_specs=[pl.BlockSpec((tq, ATTN_WIDTH), lambda b, i, s: (b * nb + i, 0)),
                  kv_spec(1, 0), kv_spec(0, 0), kv_spec(1, 1), kv_spec(0, 1)],
        out_specs=pl.BlockSpec((tq, ATTN_WIDTH), lambda b, i, s: (b * nb + i, 0)),
        scratch_shapes=[pltpu.VMEM((N_HEADS, 2 * tq, tq), F32), pltpu.VMEM((N_HEADS, 2 * tq, tq), BF16),
                        pltpu.VMEM((N_HEADS, tq), F32)],
    )
    return pl.pallas_call(
        _attn_prompt_kernel,
        grid_spec=grid_spec,
        out_shape=jax.ShapeDtypeStruct((bsz * seq, ATTN_WIDTH), BF16),
        compiler_params=_cparams(("parallel", "parallel")),
        name="attn_prompt",
    )(sinks, q, k, k, kv, kv)


def _attn_sample_kernel(sinks_ref, q_ref, kn_ref, vn_ref, kc_ref, vc_ref, o_ref):
    kk = jnp.concatenate([kc_ref[0], kn_ref[...]], axis=0).astype(BF16)
    vv = jnp.concatenate([vc_ref[0], vn_ref[...]], axis=0).astype(BF16)
    o_ref[...] = _attend(q_ref[...], kk, vv, sinks_ref, None).astype(o_ref.dtype)


def _attn_sample(q, k, kv, cache_k, cache_v, sinks, row0, bsz, seq):
    blk0 = row0 // seq
    kv_len = cache_k.shape[1]
    new = lambda w, col: pl.BlockSpec((seq, w), lambda b, s: (blk0 + b, col))
    cache = pl.BlockSpec((1, kv_len, KV_WIDTH), lambda b, s: (b, 0, 0))
    grid_spec = pltpu.PrefetchScalarGridSpec(
        num_scalar_prefetch=1,
        grid=(bsz,),
        in_specs=[new(ATTN_WIDTH, 0), new(KV_WIDTH, 0), new(KV_WIDTH, 1), cache, cache],
        out_specs=pl.BlockSpec((seq, ATTN_WIDTH), lambda b, s: (b, 0)),
    )
    return pl.pallas_call(
        _attn_sample_kernel,
        grid_spec=grid_spec,
        out_shape=jax.ShapeDtypeStruct((bsz * seq, ATTN_WIDTH), BF16),
        compiler_params=_cparams(("parallel",)),
        name="attn_sample",
    )(sinks, q, k, kv, cache_k, cache_v)


def _pad_rows(x, rows):
    if x.shape[0] == rows:
        return x
    return jnp.concatenate([x, jnp.zeros((rows - x.shape[0], x.shape[1]), x.dtype)], axis=0)


def _ssd_kernel(*refs, clen, has_past):
    if has_past:
        (xbc_ref, dt_ref, z_ref, cw_ref, cb_ref, dtb_ref, alog_ref, dskip_ref, nw_ref, cpast_ref, hpast_ref,
         y_ref, hout_ref, xpad, u_ref, g_ref, ht_ref, ca_ref, xdt_ref, yst_ref, xw_ref, cb2_ref) = refs
    else:
        (xbc_ref, dt_ref, z_ref, cw_ref, cb_ref, dtb_ref, alog_ref, dskip_ref, nw_ref,
         y_ref, hout_ref, xpad, u_ref, g_ref, ht_ref, ca_ref, xdt_ref, yst_ref, xw_ref, cb2_ref) = refs
    L = clen
    c = pl.program_id(1)
    nc = pl.num_programs(1)
    n_pairs = SSM_HEADS // 2
    pairs_per_group = n_pairs // SSM_GROUPS
    gw = D_INNER // SSM_GROUPS

    @pl.when(c == 0)
    def _init():
        if has_past:
            xpad[5:8, :] = cpast_ref[0]
            for p in range(n_pairs):
                both = jnp.concatenate([hpast_ref[0, 2 * p], hpast_ref[0, 2 * p + 1]], axis=0)
                ht_ref[:, p * LANES:(p + 1) * LANES] = both.T
        else:
            xpad[0:8, :] = jnp.zeros((8, CONV_DIM), F32)
            ht_ref[...] = jnp.zeros(ht_ref.shape, F32)

    xpad[8:8 + L, :] = xbc_ref[...]
    cblk = 512
    for j in range(CONV_DIM // cblk):
        sl = slice(j * cblk, (j + 1) * cblk)
        acc = cb_ref[:, sl] + xpad[8:8 + L, sl] * cw_ref[3:4, sl]
        acc = acc + xpad[7:7 + L, sl] * cw_ref[2:3, sl]
        acc = acc + xpad[6:6 + L, sl] * cw_ref[1:2, sl]
        acc = acc + xpad[5:5 + L, sl] * cw_ref[0:1, sl]
        u_ref[:, sl] = _silu(acc)
    xpad[5:8, :] = xpad[5 + L:8 + L, :]

    dtx = dt_ref[...] + dtb_ref[...]
    dt = jnp.maximum(dtx, 0.0) + jnp.log(1.0 + jnp.exp(-jnp.abs(dtx)))
    loga = dt * (-jnp.exp(alog_ref[...]))
    t_idx = lax.broadcasted_iota(I32, (L, L), 0)
    s_idx = lax.broadcasted_iota(I32, (L, L), 1)
    incl = (s_idx <= t_idx).astype(BF16)
    acum = sum(_dot(incl, part) for part in _split3(loga))
    acum_t = _pad_rows(acum, LANES).T[:, 0:L]

    lane_m = lax.broadcasted_iota(I32, (L, 2 * L), 1)
    row_m = lax.broadcasted_iota(I32, (L, 2 * L), 0)
    left_m = lane_m < L
    causal_m = jnp.where(left_m, lane_m, lane_m - L) <= row_m
    left = lax.broadcasted_iota(I32, (L, LANES), 1) < SSM_HEAD_DIM
    b_of = lambda grp: u_ref[:, D_INNER + grp * D_STATE:D_INNER + (grp + 1) * D_STATE]
    c_of = lambda grp: u_ref[:, D_INNER + BC_WIDTH + grp * D_STATE:D_INNER + BC_WIDTH + (grp + 1) * D_STATE]
    pair_lanes = lambda p: slice(p * LANES, (p + 1) * LANES)
    group_lanes = lambda grp: slice(grp * gw, (grp + 1) * gw)

    for p in range(n_pairs):
        h0, h1, sl = 2 * p, 2 * p + 1, pair_lanes(p)
        ca_ref[:, sl] = jnp.where(left, acum[:, h0:h0 + 1], acum[:, h1:h1 + 1])
        xdt_ref[:, sl] = u_ref[:, sl] * jnp.where(left, dt[:, h0:h0 + 1], dt[:, h1:h1 + 1])

    for grp in range(SSM_GROUPS):
        c_bf = c_of(grp).astype(BF16)
        cb = _dot_nt(c_bf, b_of(grp).astype(BF16))
        cb2_ref[grp] = jnp.concatenate([cb, cb], axis=1)
        yst_ref[:, group_lanes(grp)] = _dot(c_bf, ht_ref[:, group_lanes(grp)].astype(BF16))

    for p in range(n_pairs):
        h0, h1, sl = 2 * p, 2 * p + 1, pair_lanes(p)
        ca = ca_ref[:, sl]
        ca_m = ca if L == SSM_HEAD_DIM else jnp.where(left_m, acum[:, h0:h0 + 1], acum[:, h1:h1 + 1])
        row_a = jnp.concatenate([acum_t[h0:h0 + 1, :], acum_t[h1:h1 + 1, :]], axis=1)
        dec = jnp.exp(jnp.where(causal_m, ca_m - row_a, NEG_INF))
        m_pair = (cb2_ref[p // pairs_per_group] * dec).astype(BF16)
        xdt = xdt_ref[:, sl]
        x_bd = jnp.concatenate([jnp.where(left, xdt, 0.0), jnp.where(left, 0.0, xdt)], axis=0).astype(BF16)
        y = _dot(m_pair, x_bd) + yst_ref[:, sl] * jnp.exp(ca) + u_ref[:, sl] * dskip_ref[:, sl]
        g_ref[:, sl] = y * _silu(z_ref[:, sl].astype(F32))
        xw_ref[:, sl] = (xdt * jnp.exp(ca[L - 1:L, :] - ca)).astype(BF16)

    for grp in range(SSM_GROUPS):
        gsl = group_lanes(grp)
        gg = g_ref[:, gsl]
        ms = jnp.mean(gg * gg, axis=-1, keepdims=True)
        y_ref[:, gsl] = (gg * lax.rsqrt(ms + EPS) * nw_ref[:, gsl]).astype(y_ref.dtype)
        bt_bf = _pad_rows(b_of(grp), LANES).T[:, 0:L].astype(BF16)
        ht_ref[:, gsl] = ht_ref[:, gsl] * jnp.exp(ca_ref[L - 1:L, gsl]) + _dot(bt_bf, xw_ref[:, gsl])

    @pl.when(c == nc - 1)
    def _fin():
        for p in range(n_pairs):
            both = ht_ref[:, p * LANES:(p + 1) * LANES].T
            hout_ref[0, 2 * p] = both[0:SSM_HEAD_DIM, :]
            hout_ref[0, 2 * p + 1] = both[SSM_HEAD_DIM:2 * SSM_HEAD_DIM, :]


def _ssd(xbc, dt_raw, z, conv_w, conv_b, dt_bias, a_log, d_skip, ssm_norm, row0, bsz, seq, clen,
         conv_past=None, ssm_past=None):
    nc = seq // clen
    blk0 = row0 // clen
    has_past = conv_past is not None
    pad = lambda a: jnp.pad(a.astype(F32), (0, LANES - SSM_HEADS)).reshape(1, LANES)
    params = [conv_w.astype(F32), conv_b.astype(F32).reshape(1, CONV_DIM), pad(dt_bias), pad(a_log),
              jnp.repeat(d_skip.astype(F32), SSM_HEAD_DIM).reshape(1, D_INNER), ssm_norm.astype(F32).reshape(1, D_INNER)]
    rows = lambda w: pl.BlockSpec((clen, w), lambda b, c: (blk0 + b * nc + c, 0))
    full = lambda a: pl.BlockSpec(a.shape, lambda b, c: (0,) * a.ndim)
    in_specs = [rows(CONV_DIM), rows(LANES), rows(D_INNER)] + [full(a) for a in params]
    args = [xbc, dt_raw, z] + params
    if has_past:
        in_specs += [pl.BlockSpec((1, CONV_W - 1, CONV_DIM), lambda b, c: (b, 0, 0)),
                     pl.BlockSpec((1, SSM_HEADS, SSM_HEAD_DIM, D_STATE), lambda b, c: (b, 0, 0, 0))]
        args += [conv_past.astype(F32), ssm_past.astype(F32)]
    return pl.pallas_call(
        functools.partial(_ssd_kernel, clen=clen, has_past=has_past),
        grid=(bsz, nc),
        in_specs=in_specs,
        out_specs=[pl.BlockSpec((clen, D_INNER), lambda b, c: (b * nc + c, 0)),
                   pl.BlockSpec((1, SSM_HEADS, SSM_HEAD_DIM, D_STATE), lambda b, c: (b, 0, 0, 0))],
        out_shape=[jax.ShapeDtypeStruct((bsz * seq, D_INNER), BF16),
                   jax.ShapeDtypeStruct((bsz, SSM_HEADS, SSM_HEAD_DIM, D_STATE), F32)],
        scratch_shapes=[pltpu.VMEM((8 + clen, CONV_DIM), F32), pltpu.VMEM((clen, CONV_DIM), F32),
                        pltpu.VMEM((clen, D_INNER), F32), pltpu.VMEM((D_STATE, D_INNER), F32),
                        pltpu.VMEM((clen, D_INNER), F32), pltpu.VMEM((clen, D_INNER), F32),
                        pltpu.VMEM((clen, D_INNER), F32), pltpu.VMEM((clen, D_INNER), BF16),
                        pltpu.VMEM((SSM_GROUPS, clen, 2 * clen), F32)],
        compiler_params=_cparams(("parallel", "arbitrary")),
        name="ssd_past" if has_past else "ssd_prompt",
    )(*args)


def _merge_kernel(h_ref, a1_ref, a2_ref, s1_ref, s2_ref, wga_ref, wgs_ref, bga_ref, bgs_ref, wa_ref, ws_ref, o_ref,
                  *, n_first):
    def body(a_ref, s_ref):
        h = h_ref[...]
        g_a = jax.nn.sigmoid(_dot(h, wga_ref[...]) + bga_ref[...])
        g_s = jax.nn.sigmoid(_dot(h, wgs_ref[...]) + bgs_ref[...])
        o_ref[...] = (g_a * _dot(a_ref[...], wa_ref[...]) + g_s * _dot(s_ref[...], ws_ref[...])).astype(o_ref.dtype)

    _by_part(pl.program_id(0), n_first, body, (a1_ref, s1_ref), (a2_ref, s2_ref))


def _merge(h, attn_parts, ssm_parts, w_gate, b_gate, w_attn_up, w_ssm_up):
    m = h.shape[0]
    m1, m2 = attn_parts[0].shape[0], attn_parts[1].shape[0]
    tm = _pick(math.gcd(m1, m2), (512, 256, 128, 64, 32))
    nf = m1 // tm
    tn = 512
    nj = D_MODEL // tn
    bg = b_gate.astype(F32).reshape(1, 2 * D_MODEL)
    return pl.pallas_call(
        functools.partial(_merge_kernel, n_first=nf),
        grid=(m // tm, nj),
        in_specs=[pl.BlockSpec((tm, D_MODEL), lambda i, j: (i, 0)),
                  pl.BlockSpec((tm, ATTN_WIDTH), _first_rows(nf)), pl.BlockSpec((tm, ATTN_WIDTH), _later_rows(nf)),
                  pl.BlockSpec((tm, D_INNER), _first_rows(nf)), pl.BlockSpec((tm, D_INNER), _later_rows(nf)),
                  pl.BlockSpec((D_MODEL, tn), lambda i, j: (0, j)),
                  pl.BlockSpec((D_MODEL, tn), lambda i, j: (0, j + nj)),
                  pl.BlockSpec((1, tn), lambda i, j: (0, j)),
                  pl.BlockSpec((1, tn), lambda i, j: (0, j + nj)),
                  pl.BlockSpec((ATTN_WIDTH, tn), lambda i, j: (0, j)),
                  pl.BlockSpec((D_INNER, tn), lambda i, j: (0, j))],
        out_specs=pl.BlockSpec((tm, tn), lambda i, j: (i, j)),
        out_shape=jax.ShapeDtypeStruct((m, D_MODEL), BF16),
        compiler_params=_cparams(("parallel", "parallel")),
        name="merge",
    )(h, attn_parts[0], attn_parts[1], ssm_parts[0], ssm_parts[1], w_gate, w_gate, bg, bg, w_attn_up, w_ssm_up)


def _outproj_router_kernel(mg_ref, wo_ref, xa_ref, xb_ref, n2_ref, wr_ref, br_ref, x1_ref, h2_ref, idx_ref, gate_ref,
                           *, n_first):
    def residual(x_ref):
        x1_ref[...] = x_ref[...] + _dot(mg_ref[...], wo_ref[...])

    _by_part(pl.program_id(0), n_first, residual, (xa_ref,), (xb_ref,))
    x1 = x1_ref[...]
    ms = jnp.mean(x1 * x1, axis=-1, keepdims=True)
    h2 = x1 * lax.rsqrt(ms + EPS) * n2_ref[...]
    h2_ref[...] = _pack_bf16_pairs(h2)
    h_hi, h_lo = _split2(h2)
    w_hi, w_lo = _split2(wr_ref[...])
    logits = _dot(h_hi, w_hi) + (_dot(h_hi, w_lo) + _dot(h_lo, w_hi)) + br_ref[...]
    tm = logits.shape[0]
    lane = lax.broadcasted_iota(I32, (tm, LANES), 1)
    logits = jnp.where(lane < N_EXPERTS, logits, -jnp.inf)
    idx_out = jnp.zeros((tm, LANES), I32)
    val_out = jnp.zeros((tm, LANES), F32)
    top = None
    for k in range(TOP_K):
        v = jnp.max(logits, axis=-1, keepdims=True)
        i = jnp.min(jnp.where(logits == v, lane, LANES), axis=-1, keepdims=True)
        if k == 0:
            top = v
        idx_out = jnp.where(lane == k, i, idx_out)
        val_out = jnp.where(lane == k, jnp.exp(v - top), val_out)
        logits = jnp.where(lane == i, -jnp.inf, logits)
    idx_ref[...] = idx_out
    gate_ref[...] = val_out / jnp.sum(val_out, axis=-1, keepdims=True)


def _outproj_router(merged, w_out, xa, xb, norm2, w_router, b_router):
    m = merged.shape[0]
    tm = _pick(math.gcd(xa.shape[0], xb.shape[0]), (256, 128, 64, 32))
    nf = xa.shape[0] // tm
    wr = jnp.pad(w_router.astype(F32), ((0, 0), (0, LANES - N_EXPERTS)))
    br = jnp.pad(b_router.astype(F32), (0, LANES - N_EXPERTS)).reshape(1, LANES)
    row = lambda w: pl.BlockSpec((tm, w), lambda i: (i, 0))
    full = lambda r, c: pl.BlockSpec((r, c), lambda i: (0, 0))
    return pl.pallas_call(
        functools.partial(_outproj_router_kernel, n_first=nf),
        grid=(m // tm,),
        in_specs=[row(D_MODEL), full(D_MODEL, D_MODEL),
                  pl.BlockSpec((tm, D_MODEL), _first_rows(nf)), pl.BlockSpec((tm, D_MODEL), _later_rows(nf)),
                  full(1, D_MODEL), full(D_MODEL, LANES), full(1, LANES)],
        out_specs=[row(D_MODEL), row(D_MODEL // 2), row(LANES), row(LANES)],
        out_shape=[jax.ShapeDtypeStruct((m, D_MODEL), F32), jax.ShapeDtypeStruct((m, D_MODEL // 2), I32),
                   jax.ShapeDtypeStruct((m, LANES), I32), jax.ShapeDtypeStruct((m, LANES), F32)],
        compiler_params=_cparams(("parallel",)),
        name="outproj_router",
    )(merged, w_out, xa, xb, norm2.astype(F32).reshape(1, D_MODEL), wr, br)


def _rank_kernel(idx_ref, rank_ref, cnt_ref, base_ref):
    i = pl.program_id(0)
    tt = idx_ref.shape[0]

    @pl.when(i == 0)
    def _():
        base_ref[...] = jnp.zeros(base_ref.shape, F32)

    idx = idx_ref[...]
    lane = lax.broadcasted_iota(I32, (tt, LANES), 1)
    sel = [lane == idx[:, k:k + 1] for k in range(TOP_K)]
    onehot = jnp.zeros((tt, LANES), F32)
    for k in range(TOP_K):
        onehot = jnp.where(sel[k], 1.0, onehot)
    r_idx = lax.broadcasted_iota(I32, (tt, tt), 0)
    c_idx = lax.broadcasted_iota(I32, (tt, tt), 1)
    before = (c_idx < r_idx).astype(BF16)
    rank_all = _dot(before, onehot.astype(BF16)) + base_ref[0:1, :]
    out = jnp.zeros((tt, LANES), F32)
    for k in range(TOP_K):
        out = jnp.where(lane == k, jnp.sum(jnp.where(sel[k], rank_all, 0.0), axis=-1, keepdims=True), out)
    rank_ref[...] = out.astype(I32)
    base_ref[0:1, :] = base_ref[0:1, :] + jnp.sum(onehot, axis=0, keepdims=True)
    cnt_ref[...] = base_ref[...].astype(I32)


def _ranks(idx):
    m = idx.shape[0]
    tt = _pick(m, (512, 256, 128, 64, 32))
    return pl.pallas_call(
        _rank_kernel,
        grid=(m // tt,),
        in_specs=[pl.BlockSpec((tt, LANES), lambda i: (i, 0))],
        out_specs=[pl.BlockSpec((tt, LANES), lambda i: (i, 0)), pl.BlockSpec((8, LANES), lambda i: (0, 0))],
        out_shape=[jax.ShapeDtypeStruct((m, LANES), I32), jax.ShapeDtypeStruct((8, LANES), I32)],
        scratch_shapes=[pltpu.VMEM((8, LANES), F32)],
        compiler_params=_cparams(("arbitrary",)),
        name="expert_ranks",
    )(idx)


def _dispatch_kernel(dest_ref, fill_lo_ref, fill_n_ref, nu_ref, h_ref, xs_hbm, zero_ref, sem, zsem, *, tt, tm, n_blocks):
    @pl.when(pl.program_id(0) == 0)
    def _():
        zero_ref[...] = jnp.zeros(zero_ref.shape, zero_ref.dtype)

        def pad_rows(e, carry):
            def row(r, c):
                cp = pltpu.make_async_copy(zero_ref.at[pl.ds(0, 1), :], xs_hbm.at[pl.ds(fill_lo_ref[e] + r, 1), :], zsem)
                cp.start()
                return c

            return lax.fori_loop(0, fill_n_ref[e], row, carry)

        lax.fori_loop(0, N_EXPERTS, pad_rows, 0)

        def block(b, carry):
            pltpu.make_async_copy(zero_ref, xs_hbm.at[pl.ds(pl.multiple_of(b * tm, tm), tm), :], zsem).start()
            return carry

        lax.fori_loop(nu_ref[0], n_blocks, block, 0)

        def pad_rows_done(e, carry):
            def row(r, c):
                pltpu.make_async_copy(zero_ref.at[pl.ds(0, 1), :], xs_hbm.at[pl.ds(0, 1), :], zsem).wait()
                return c

            return lax.fori_loop(0, fill_n_ref[e], row, carry)

        lax.fori_loop(0, N_EXPERTS, pad_rows_done, 0)

        def block_done(b, carry):
            pltpu.make_async_copy(zero_ref, xs_hbm.at[pl.ds(0, tm), :], zsem).wait()
            return carry

        lax.fori_loop(nu_ref[0], n_blocks, block_done, 0)

    base = pl.program_id(0) * (tt * TOP_K)

    def issue(r, carry):
        for k in range(TOP_K):
            dst = dest_ref[base + r * TOP_K + k]
            pltpu.make_async_copy(h_ref.at[pl.ds(r, 1), :], xs_hbm.at[pl.ds(dst, 1), :], sem).start()
        return carry

    lax.fori_loop(0, tt, issue, 0, unroll=2)
    for _ in range(TOP_K):
        pltpu.make_async_copy(h_ref, xs_hbm.at[pl.ds(0, tt), :], sem).wait()


def _dispatch(dest_flat, fill_lo, fill_n, n_used, h2, n_blocks, tm):
    m = h2.shape[0]
    tt = _pick(m, (256, 128, 64, 32))
    width = h2.shape[1]
    grid_spec = pltpu.PrefetchScalarGridSpec(
        num_scalar_prefetch=4,
        grid=(m // tt,),
        in_specs=[pl.BlockSpec((tt, width), lambda i, *_: (i, 0))],
        out_specs=pl.BlockSpec(memory_space=pl.ANY),
        scratch_shapes=[pltpu.VMEM((tm, width), h2.dtype), pltpu.SemaphoreType.DMA, pltpu.SemaphoreType.DMA],
    )
    return pl.pallas_call(
        functools.partial(_dispatch_kernel, tt=tt, tm=tm, n_blocks=n_blocks),
        grid_spec=grid_spec,
        out_shape=jax.ShapeDtypeStruct((n_blocks * tm, width), h2.dtype),
        compiler_params=_cparams(("arbitrary",)),
        name="dispatch",
    )(dest_flat, fill_lo, fill_n, n_used, h2)


def _pack_bf16_pairs(x):
    c = x.shape[1] // 2
    bits = lax.bitcast_convert_type(x.astype(BF16).astype(F32), I32)
    return lax.shift_right_logical(bits[:, :c], 16) | bits[:, c:]


def _unpack_bf16_pairs(p):
    lo = lax.bitcast_convert_type(lax.shift_left(p, 16), F32).astype(BF16)
    hi = lax.bitcast_convert_type(p & jnp.int32(-65536), F32).astype(BF16)
    return lo, hi


def _expert_kernel(be_ref, nu_ref, xs_ref, bg_ref, bu_ref, bd_ref, wg_hbm, wu_hbm, wd_hbm, ys_ref,
                   cg, cu, cd, stg_g, stg_u, stg_d, xb_ref, sems, *, tf):
    i = pl.program_id(0)
    nj = D_FF // tf
    half = D_MODEL // 2
    e = be_ref[i]
    first = jnp.logical_or(i == 0, e != be_ref[jnp.maximum(i - 1, 0)])

    def tile_copies(t, slot):
        cols = pl.ds(pl.multiple_of(t * tf, tf), tf)
        return (pltpu.make_async_copy(wg_hbm.at[e, :, cols], stg_g.at[slot], sems.at[slot, 0]),
                pltpu.make_async_copy(wu_hbm.at[e, :, cols], stg_u.at[slot], sems.at[slot, 1]),
                pltpu.make_async_copy(wd_hbm.at[e, cols, :], stg_d.at[slot], sems.at[slot, 2]))

    @pl.when(i < nu_ref[0])
    def _():
        @pl.when(first)
        def _():
            for cp in tile_copies(0, 0):
                cp.start()

        lo, hi = _unpack_bf16_pairs(xs_ref[...])
        xb_ref[:, :half] = lo
        xb_ref[:, half:] = hi
        ys_ref[...] = jnp.broadcast_to(bd_ref[0], ys_ref.shape)

        def tile(t, carry):
            @pl.when(first)
            def _():
                slot = t % 2

                @pl.when(t + 1 < nj)
                def _():
                    for cp in tile_copies(t + 1, 1 - slot):
                        cp.start()

                for cp in tile_copies(t, slot):
                    cp.wait()
                cg[t] = stg_g[slot].astype(BF16)
                cu[t] = stg_u[slot].astype(BF16)
                cd[t] = stg_d[slot].astype(BF16)

            xb = xb_ref[...]
            g = jnp.minimum(_dot(xb, cg[t]) + bg_ref[0, t], SWIGLU_LIMIT)
            u = jnp.clip(_dot(xb, cu[t]) + bu_ref[0, t], -SWIGLU_LIMIT, SWIGLU_LIMIT)
            act = (u + 1.0) * (g * jax.nn.sigmoid(SWIGLU_ALPHA * g))
            ys_ref[...] += _dot(act.astype(BF16), cd[t])
            return carry

        lax.fori_loop(0, nj, tile, 0)

    @pl.when(i >= nu_ref[0])
    def _():
        ys_ref[...] = jnp.zeros(ys_ref.shape, F32)


def _experts(block_e, n_used, xs, w_e_gate, b_e_gate, w_e_up, b_e_up, w_e_down, b_e_down):
    n_rows = xs.shape[0]
    tm, tf = MOE_TM, MOE_TF
    n_blocks = n_rows // tm
    nj = D_FF // tf
    half = D_MODEL // 2

    def blk(i, nu):
        return jnp.minimum(i, nu[0] - 1)

    hbm = pl.BlockSpec(memory_space=pl.ANY)
    grid_spec = pltpu.PrefetchScalarGridSpec(
        num_scalar_prefetch=2,
        grid=(n_blocks,),
        in_specs=[pl.BlockSpec((tm, half), lambda i, be, nu: (blk(i, nu), 0)),
                  pl.BlockSpec((1, nj, 1, tf), lambda i, be, nu: (be[blk(i, nu)], 0, 0, 0)),
                  pl.BlockSpec((1, nj, 1, tf), lambda i, be, nu: (be[blk(i, nu)], 0, 0, 0)),
                  pl.BlockSpec((1, 1, D_MODEL), lambda i, be, nu: (be[blk(i, nu)], 0, 0)),
                  hbm, hbm, hbm],
        out_specs=pl.BlockSpec((tm, D_MODEL), lambda i, be, nu: (i, 0)),
        scratch_shapes=[pltpu.VMEM((nj, D_MODEL, tf), BF16), pltpu.VMEM((nj, D_MODEL, tf), BF16),
                        pltpu.VMEM((nj, tf, D_MODEL), BF16),
                        pltpu.VMEM((2, D_MODEL, tf), F32), pltpu.VMEM((2, D_MODEL, tf), F32),
                        pltpu.VMEM((2, tf, D_MODEL), F32),
                        pltpu.VMEM((tm, D_MODEL), BF16), pltpu.SemaphoreType.DMA((2, 3))],
    )
    return pl.pallas_call(
        functools.partial(_expert_kernel, tf=tf),
        grid_spec=grid_spec,
        out_shape=jax.ShapeDtypeStruct((n_rows, D_MODEL), F32),
        compiler_params=pltpu.CompilerParams(dimension_semantics=("arbitrary",), vmem_limit_bytes=EXPERT_VMEM_LIMIT),
        name="experts",
    )(block_e, n_used, xs, b_e_gate.reshape(N_EXPERTS, nj, 1, tf), b_e_up.reshape(N_EXPERTS, nj, 1, tf),
      b_e_down.reshape(N_EXPERTS, 1, D_MODEL), w_e_gate, w_e_up, w_e_down)


def _combine_kernel(dest_ref, x1_ref, gate_ref, ys_hbm, o_ref, buf, sems, *, tt, row0):
    i = pl.program_id(0)
    n_steps = pl.num_programs(0)
    slot = i % 2

    def gather(step, to_slot):
        base = (row0 + step * tt) * TOP_K

        def issue(r, carry):
            for k in range(TOP_K):
                src = dest_ref[base + r * TOP_K + k]
                pltpu.make_async_copy(ys_hbm.at[pl.ds(src, 1), :], buf.at[to_slot, k, pl.ds(r, 1), :],
                                      sems.at[to_slot]).start()
            return carry

        lax.fori_loop(0, tt, issue, 0, unroll=2)

    @pl.when(i == 0)
    def _():
        gather(0, 0)

    @pl.when(i + 1 < n_steps)
    def _():
        gather(i + 1, 1 - slot)

    for k in range(TOP_K):
        pltpu.make_async_copy(ys_hbm.at[pl.ds(0, tt), :], buf.at[slot, k], sems.at[slot]).wait()
    gate = gate_ref[...]
    acc = x1_ref[...]
    for k in range(TOP_K):
        acc = acc + gate[:, k:k + 1] * buf[slot, k]
    o_ref[...] = acc


def _combine(dest_flat, x1, gate, ys, row0, rows):
    tt = _pick(math.gcd(row0, rows), (128, 64, 32))
    blk0 = row0 // tt
    grid_spec = pltpu.PrefetchScalarGridSpec(
        num_scalar_prefetch=1,
        grid=(rows // tt,),
        in_specs=[pl.BlockSpec((tt, D_MODEL), lambda i, d: (blk0 + i, 0)),
                  pl.BlockSpec((tt, LANES), lambda i, d: (blk0 + i, 0)),
                  pl.BlockSpec(memory_space=pl.ANY)],
        out_specs=pl.BlockSpec((tt, D_MODEL), lambda i, d: (i, 0)),
        scratch_shapes=[pltpu.VMEM((2, TOP_K, tt, D_MODEL), F32), pltpu.SemaphoreType.DMA((2,))],
    )
    return pl.pallas_call(
        functools.partial(_combine_kernel, tt=tt, row0=row0),
        grid_spec=grid_spec,
        out_shape=jax.ShapeDtypeStruct((rows, D_MODEL), F32),
        compiler_params=_cparams(("arbitrary",)),
        name="combine",
    )(dest_flat, x1, gate, ys)


def _moe(x1, h2, idx, gate, w_e_gate, b_e_gate, w_e_up, b_e_up, w_e_down, b_e_down, n_first):
    m = x1.shape[0]
    tm = MOE_TM
    ranks, cnt = _ranks(idx)
    counts = cnt[0, :N_EXPERTS]
    padded = (counts + tm - 1) // tm * tm
    pad_end = jnp.cumsum(padded)
    pad_start = pad_end - padded
    e_sel = idx[:, :TOP_K]
    dest = (pad_start[e_sel] + ranks[:, :TOP_K]).astype(I32).reshape(-1)
    n_blocks = -(-(m * TOP_K + N_EXPERTS * (tm - 1)) // tm)
    block_row = jnp.arange(n_blocks, dtype=I32) * tm
    block_e = jnp.minimum(jnp.sum((pad_end[None, :] <= block_row[:, None]).astype(I32), axis=1), N_EXPERTS - 1)
    n_used = (pad_end[-1] // tm).astype(I32).reshape(1)
    fill_lo = (pad_start + counts).astype(I32)
    fill_n = (padded - counts).astype(I32)
    xs = _dispatch(dest, fill_lo, fill_n, n_used, h2, n_blocks, tm)
    ys = _experts(block_e, n_used, xs, w_e_gate, b_e_gate, w_e_up, b_e_up, w_e_down, b_e_down)
    return _combine(dest, x1, gate, ys, 0, n_first), _combine(dest, x1, gate, ys, n_first, m - n_first)


def kernel(x_prompt, x_sample, cache_k, cache_v, state_conv, state_ssm, norm1, w_in, q_norm, k_norm, sinks, conv_w,
           conv_b, dt_bias, a_log, d_skip, ssm_norm, w_gate, b_gate, w_attn_up, w_ssm_up, w_out, norm2, w_router,
           b_router, w_e_gate, b_e_gate, w_e_up, b_e_up, w_e_down, b_e_down):
    bp, sp, _ = x_prompt.shape
    bs, ss, _ = x_sample.shape
    tp, ts = bp * sp, bs * ss
    kv_len = cache_k.shape[2]
    l = 0

    xa, xb = x_prompt.reshape(tp, D_MODEL), x_sample.reshape(ts, D_MODEL)
    pos = jnp.concatenate([jnp.tile(jnp.arange(sp, dtype=I32), bp), jnp.tile(PAST_LEN + jnp.arange(ss, dtype=I32), bs)])

    o1 = ATTN_WIDTH
    o2 = o1 + 2 * KV_WIDTH
    o3 = o2 + D_INNER
    o4 = o3 + CONV_DIM
    w = w_in[l]
    h = _rmsnorm(xa, xb, norm1[l], BF16)
    q_p = _matmul(h, w, 0, o1, F32, "proj_q")
    kv_p = _matmul(h, w, o1, o2 - o1, F32, "proj_kv")
    z = _matmul(h, w, o2, o3 - o2, BF16, "proj_z")
    xbc = _matmul(h, w, o3, o4 - o3, F32, "proj_xbc")
    dt_raw = _matmul(h, jnp.pad(w[:, o4:], ((0, 0), (0, LANES - SSM_HEADS))), 0, LANES, F32, "proj_dt")

    q_rot, k_rot = _qk_prep(q_p, kv_p, pos, q_norm[l], k_norm[l])
    sk = sinks[l].astype(F32)
    attn_p = _attn_prompt(q_rot, k_rot, kv_p, sk, bp, sp)
    ck = cache_k[l].reshape(bs, kv_len, KV_WIDTH)
    cv = cache_v[l].reshape(bs, kv_len, KV_WIDTH)
    attn_s = _attn_sample(q_rot, k_rot, kv_p, ck, cv, sk, tp, bs, ss)

    ssd_w = (conv_w[l], conv_b[l], dt_bias[l], a_log[l], d_skip[l], ssm_norm[l])
    ssm_p, hfin_p = _ssd(xbc, dt_raw, z, *ssd_w, 0, bp, sp, CHUNK)
    ssm_s, hfin_s = _ssd(xbc, dt_raw, z, *ssd_w, tp, bs, ss, ss, state_conv[l], state_ssm[l])

    merged = _merge(h, (attn_p, attn_s), (ssm_p, ssm_s), w_gate[l].astype(BF16), b_gate[l], w_attn_up[l].astype(BF16), w_ssm_up[l].astype(BF16))
    x1, h2, idx, gate = _outproj_router(merged, w_out[l].astype(BF16), xa, xb, norm2[l], w_router[l], b_router[l])
    y_p, y_s = _moe(x1, h2, idx, gate, w_e_gate[l], b_e_gate[l], w_e_up[l], b_e_up[l], w_e_down[l], b_e_down[l], tp)

    def prompt_tail(a, n):
        return jnp.stack([a[(b + 1) * sp - n:(b + 1) * sp] for b in range(bp)])

    heads = lambda a: a.reshape(a.shape[0], a.shape[1], N_KV_HEADS, HEAD_DIM)
    new_k_p = heads(prompt_tail(k_rot, WINDOW))
    new_v_p = heads(prompt_tail(kv_p, WINDOW)[:, :, KV_WIDTH:])
    ks4 = heads(k_rot[tp:].reshape(bs, ss, KV_WIDTH))
    vs4 = heads(kv_p[tp:, KV_WIDTH:].reshape(bs, ss, KV_WIDTH))
    new_k_s = jnp.concatenate([cache_k[l], ks4], axis=1)[:, -kv_len:]
    new_v_s = jnp.concatenate([cache_v[l], vs4], axis=1)[:, -kv_len:]
    xbc_s = jnp.concatenate([state_conv[l], xbc[tp:].reshape(bs, ss, CONV_DIM)], axis=1)
    return (y_p.reshape(bp, sp, D_MODEL), y_s.reshape(bs, ss, D_MODEL),
            new_k_p[None], new_v_p[None], prompt_tail(xbc, CONV_W - 1)[None], hfin_p[None],
            new_k_s[None], new_v_s[None], xbc_s[:, -(CONV_W - 1):][None], hfin_s[None])
```

```python
import functools
import math

import jax
import jax.numpy as jnp
from jax import lax
from jax.experimental import pallas as pl
from jax.experimental.pallas import tpu as pltpu

F32 = jnp.float32
BF16 = jnp.bfloat16
I32 = jnp.int32

D_MODEL = 2048
CHUNK = 64
N_HEADS = 32
N_KV_HEADS = 8
HEAD_DIM = 64
Q_PER_KV = N_HEADS // N_KV_HEADS
ATTN_WIDTH = N_HEADS * HEAD_DIM
KV_WIDTH = N_KV_HEADS * HEAD_DIM
WINDOW = 128
N_PREV_CHUNKS = WINDOW // CHUNK
ROPE_THETA = 500000.0
ROT_DIM = HEAD_DIM // 4
D_INNER = 2 * D_MODEL
SSM_HEAD_DIM = 64
SSM_HEADS = D_INNER // SSM_HEAD_DIM
SSM_GROUPS = 8
D_STATE = 128
CONV_W = 4
BC_WIDTH = SSM_GROUPS * D_STATE
CONV_DIM = D_INNER + 2 * BC_WIDTH
N_EXPERTS = 32
TOP_K = 4
D_FF = D_MODEL
SWIGLU_LIMIT = 7.0
SWIGLU_ALPHA = 1.702
EPS = 1e-6
NEG_INF = -1e30
PAST_LEN = 2048

LANES = 128
VMEM_LIMIT = 56 * 1024 * 1024
EXPERT_VMEM_LIMIT = 60 * 1024 * 1024
MOE_TM = 512
MOE_TF = 256


def _pick(n, cands):
    for c in cands:
        if n % c == 0:
            return c
    return n


def _cparams(sem):
    return pltpu.CompilerParams(dimension_semantics=sem, vmem_limit_bytes=VMEM_LIMIT)


def _split2(x):
    hi = x.astype(BF16)
    lo = (x - hi.astype(F32)).astype(BF16)
    return hi, lo


def _split3(x):
    hi = x.astype(BF16)
    r = x - hi.astype(F32)
    mid = r.astype(BF16)
    lo = (r - mid.astype(F32)).astype(BF16)
    return hi, mid, lo


def _dot(a, b):
    return jnp.dot(a, b, preferred_element_type=F32)


def _dot_nt(a, b):
    return lax.dot_general(a, b, (((1,), (1,)), ((), ())), preferred_element_type=F32)


def _silu(x):
    half = 0.5 * x
    return half + half * jnp.tanh(half)


def _first_rows(n_first):
    return lambda i, *_: (jnp.minimum(i, n_first - 1), 0)


def _later_rows(n_first):
    return lambda i, *_: (jnp.maximum(i - n_first, 0), 0)


def _by_part(i, n_first, fn, first_refs, later_refs):
    @pl.when(i < n_first)
    def _():
        fn(*first_refs)

    @pl.when(i >= n_first)
    def _():
        fn(*later_refs)


def _rms_kernel(xa_ref, xb_ref, w_ref, o_ref, *, n_first):
    def body(x_ref):
        x = x_ref[...]
        ms = jnp.mean(x * x, axis=-1, keepdims=True)
        o_ref[...] = (x * lax.rsqrt(ms + EPS) * w_ref[...]).astype(o_ref.dtype)

    _by_part(pl.program_id(0), n_first, body, (xa_ref,), (xb_ref,))


def _rmsnorm(xa, xb, w, out_dtype):
    (ma, d), mb = xa.shape, xb.shape[0]
    tm = _pick(math.gcd(ma, mb), (512, 256, 128, 64, 32))
    nf = ma // tm
    return pl.pallas_call(
        functools.partial(_rms_kernel, n_first=nf),
        grid=((ma + mb) // tm,),
        in_specs=[pl.BlockSpec((tm, d), _first_rows(nf)), pl.BlockSpec((tm, d), _later_rows(nf)),
                  pl.BlockSpec((1, d), lambda i: (0, 0))],
        out_specs=pl.BlockSpec((tm, d), lambda i: (i, 0)),
        out_shape=jax.ShapeDtypeStruct((ma + mb, d), out_dtype),
        compiler_params=_cparams(("parallel",)),
        name="rmsnorm",
    )(xa, xb, w.reshape(1, d))


def _mm_kernel(x_ref, w_ref, o_ref, wb_ref):
    @pl.when(pl.program_id(1) == 0)
    def _():
        wb_ref[...] = w_ref[0].astype(BF16)

    o_ref[...] = _dot(x_ref[...], wb_ref[...]).astype(o_ref.dtype)


def _matmul(x, w, layer, col0, n, out_dtype, name):
    m, k = x.shape
    tm = _pick(m, (1024, 512, 256, 128, 64, 32))
    tn = _pick(math.gcd(n, col0), (1024, 512, 256, 128))
    cb0 = col0 // tn
    return pl.pallas_call(
        _mm_kernel,
        grid=(n // tn, m // tm),
        in_specs=[pl.BlockSpec((tm, k), lambda j, i: (i, 0)),
                  pl.BlockSpec((1, k, tn), lambda j, i: (layer, 0, cb0 + j))],
        out_specs=pl.BlockSpec((tm, tn), lambda j, i: (i, j)),
        out_shape=jax.ShapeDtypeStruct((m, n), out_dtype),
        scratch_shapes=[pltpu.VMEM((k, tn), BF16)],
        compiler_params=_cparams(("parallel", "arbitrary")),
        name=name,
    )(x, w)


def _qk_prep_kernel(q_ref, k_ref, cos_ref, s1_ref, s2_ref, qn_ref, kn_ref, g_ref, gt_ref, qo_ref, ko_ref):
    cos = cos_ref[...]
    s1 = s1_ref[...]
    s2 = s2_ref[...]

    def norm_rope(x, nw, width):
        g = g_ref[0:width, :]
        gt = gt_ref[:, 0:width]
        sq_hi, sq_lo = _split2(x * x)
        ssum = _dot(sq_hi, g) + _dot(sq_lo, g)
        r = lax.rsqrt(ssum * (1.0 / HEAD_DIM) + EPS)
        r_hi, r_lo = _split2(r)
        y = x * (_dot(r_hi, gt) + _dot(r_lo, gt)) * nw
        outs = []
        for s in range(width // LANES):
            blk = y[:, s * LANES:(s + 1) * LANES]
            outs.append(blk * cos + pltpu.roll(blk, ROT_DIM // 2, 1) * s1
                        + pltpu.roll(blk, LANES - ROT_DIM // 2, 1) * s2)
        return jnp.concatenate(outs, axis=1)

    q = norm_rope(q_ref[...], qn_ref[...], ATTN_WIDTH)
    qo_ref[...] = (q * (HEAD_DIM ** -0.5)).astype(qo_ref.dtype)
    ko_ref[...] = norm_rope(k_ref[...], kn_ref[...], KV_WIDTH)


def _qk_prep(q, k, pos, q_norm, k_norm):
    m = q.shape[0]
    tm = _pick(m, (256, 128, 64, 32))
    half = ROT_DIM // 2
    inv_freq = ROPE_THETA ** (-jnp.arange(half, dtype=F32) * 2.0 / ROT_DIM)
    ang = pos.astype(F32)[:, None] * inv_freq[None, :]
    cos, sin = jnp.cos(ang), jnp.sin(ang)
    ones = jnp.ones((m, HEAD_DIM - ROT_DIM), F32)
    zeros = jnp.zeros((m, HEAD_DIM - ROT_DIM), F32)
    zh = jnp.zeros((m, half), F32)
    cos_t = jnp.tile(jnp.concatenate([cos, cos, ones], axis=1), (1, LANES // HEAD_DIM))
    s1_t = jnp.tile(jnp.concatenate([zh, sin, zeros], axis=1), (1, LANES // HEAD_DIM))
    s2_t = jnp.tile(jnp.concatenate([-sin, zh, zeros], axis=1), (1, LANES // HEAD_DIM))
    head_of_lane = jnp.arange(ATTN_WIDTH) // HEAD_DIM
    g = (head_of_lane[:, None] == jnp.arange(LANES)[None, :]).astype(BF16)
    gt = g.T
    qn = jnp.tile(q_norm.astype(F32), N_HEADS).reshape(1, ATTN_WIDTH)
    kn = jnp.tile(k_norm.astype(F32), N_KV_HEADS).reshape(1, KV_WIDTH)
    row = lambda w: pl.BlockSpec((tm, w), lambda i: (i, 0))
    full = lambda a: pl.BlockSpec(a.shape, lambda i: (0, 0))
    return pl.pallas_call(
        _qk_prep_kernel,
        grid=(m // tm,),
        in_specs=[row(ATTN_WIDTH), row(KV_WIDTH), row(LANES), row(LANES), row(LANES),
                  full(qn), full(kn), full(g), full(gt)],
        out_specs=[row(ATTN_WIDTH), row(KV_WIDTH)],
        out_shape=[jax.ShapeDtypeStruct((m, ATTN_WIDTH), BF16), jax.ShapeDtypeStruct((m, KV_WIDTH), F32)],
        compiler_params=_cparams(("parallel",)),
        name="qk_prep",
    )(q, k, cos_t, s1_t, s2_t, qn, kn, g, gt)


def _attend(q, kk, vv, sinks_ref, valid):
    tq = q.shape[0]
    outs = []
    for j in range(N_KV_HEADS):
        kj = kk[:, j * HEAD_DIM:(j + 1) * HEAD_DIM]
        vj = vv[:, j * HEAD_DIM:(j + 1) * HEAD_DIM]
        heads = [Q_PER_KV * j + g for g in range(Q_PER_KV)]
        q4 = jnp.concatenate([q[:, h * HEAD_DIM:(h + 1) * HEAD_DIM] for h in heads], axis=0)
        s = _dot_nt(q4, kj)
        if valid is not None:
            s = jnp.where(valid, s, NEG_INF)
        sink = jnp.concatenate([jnp.full((tq, 1), sinks_ref[h], F32) for h in heads], axis=0)
        m = jnp.maximum(jnp.max(s, axis=-1, keepdims=True), sink)
        p = jnp.exp(s - m)
        denom = jnp.sum(p, axis=-1, keepdims=True) + jnp.exp(sink - m)
        o = _dot(p.astype(BF16), vj) / denom
        for g in range(Q_PER_KV):
            outs.append(o[g * tq:(g + 1) * tq, :])
    return jnp.concatenate(outs, axis=1)


def _attn_prompt_kernel(sinks_ref, q_ref, ka_ref, kb_ref, va_ref, vb_ref, o_ref, st_ref, pt_ref, rl_ref):
    i = pl.program_id(1)
    tq = q_ref.shape[0]
    tk = 2 * tq
    q = q_ref[...]
    k32 = jnp.concatenate([ka_ref[...], kb_ref[...]], axis=0)
    vt = jnp.concatenate([va_ref[...], vb_ref[...]], axis=0).T.astype(BF16)

    lane_k = lax.broadcasted_iota(I32, (tk, LANES), 1)
    kc = lax.broadcasted_iota(I32, (tk, LANES), 0) // CHUNK
    k_ind = jnp.where(lane_k == 0, jnp.where(kc == 0, 1.0, 0.0),
                      jnp.where(lane_k == 1, jnp.where(kc == N_PREV_CHUNKS + 1, 1.0, 0.0),
                                jnp.where(lane_k == 2, jnp.where(kc < N_PREV_CHUNKS, 1.0, 0.0), 0.0)))
    first_step = jnp.where(i == 0, 1.0, 0.0)
    k_ind = jnp.where(lane_k == 2, k_ind * first_step, k_ind).astype(BF16)
    lane_q = lax.broadcasted_iota(I32, (tq, LANES), 1)
    qc = lax.broadcasted_iota(I32, (tq, LANES), 0) // CHUNK
    q_msk = jnp.where(lane_q == 0, jnp.where(qc == 1, NEG_INF, 0.0),
                      jnp.where(lane_q == 1, jnp.where(qc == 0, NEG_INF, 0.0),
                                jnp.where(lane_q == 2, NEG_INF, 0.0))).astype(BF16)
    low = lane_k < HEAD_DIM

    for slab in range(KV_WIDTH // LANES):
        ks = k32[:, slab * LANES:(slab + 1) * LANES]
        kr = pltpu.roll(ks, HEAD_DIM, 1)
        for jj in range(2):
            j = 2 * slab + jj
            in_low = jnp.where(low, ks if jj == 0 else kr, 0.0).astype(BF16)
            in_high = jnp.where(low, 0.0, kr if jj == 0 else ks).astype(BF16)
            k_ext = (jnp.concatenate([in_low, k_ind], axis=1), jnp.concatenate([in_high, k_ind], axis=1))
            for g in range(Q_PER_KV):
                h = Q_PER_KV * j + g
                pair = h // 2
                q_ext = jnp.concatenate([q[:, pair * LANES:(pair + 1) * LANES], q_msk], axis=1)
                st_ref[h] = _dot_nt(k_ext[h % 2], q_ext)

    for h in range(N_HEADS):
        st = st_ref[h]
        sink = sinks_ref[h]
        m = jnp.maximum(jnp.max(st, axis=0, keepdims=True), sink)
        pt = jnp.exp(st - m)
        rl_ref[h:h + 1, :] = 1.0 / (jnp.sum(pt, axis=0, keepdims=True) + jnp.exp(sink - m))
        pt_ref[h] = pt.astype(BF16)

    for pair in range(N_HEADS // 2):
        pieces = []
        for h in (2 * pair, 2 * pair + 1):
            j = h // Q_PER_KV
            pieces.append(_dot(vt[j * HEAD_DIM:(j + 1) * HEAD_DIM, :], pt_ref[h]) * rl_ref[h:h + 1, :])
        o_ref[:, pair * LANES:(pair + 1) * LANES] = jnp.concatenate(pieces, axis=0).T.astype(o_ref.dtype)


def _attn_prompt(q, k, kv, sinks, bsz, seq):
    tq = 2 * CHUNK
    nb = seq // tq

    def kv_spec(back, col):
        return pl.BlockSpec((tq, KV_WIDTH), lambda b, i, s: (b * nb + jnp.maximum(i - back, 0), col))

    grid_spec = pltpu.PrefetchScalarGridSpec(
        num_scalar_prefetch=1,
        grid=(bsz, nb),
        in_specs=[pl.BlockSpec((tq, ATTN_WIDTH), lambda b, i, s: (b * nb + i, 0)),
                  kv_spec(1, 0), kv_spec(0, 0), kv_spec(1, 1), kv_spec(0, 1)],
        out_specs=pl.BlockSpec((tq, ATTN_WIDTH), lambda b, i, s: (b * nb + i, 0)),
        scratch_shapes=[pltpu.VMEM((N_HEADS, 2 * tq, tq), F32), pltpu.VMEM((N_HEADS, 2 * tq, tq), BF16),
                        pltpu.VMEM((N_HEADS, tq), F32)],
    )
    return pl.pallas_call(
        _attn_prompt_kernel,
        grid_spec=grid_spec,
        out_shape=jax.ShapeDtypeStruct((bsz * seq, ATTN_WIDTH), BF16),
        compiler_params=_cparams(("parallel", "parallel")),
        name="attn_prompt",
    )(sinks, q, k, k, kv, kv)


def _attn_sample_kernel(sinks_ref, q_ref, kn_ref, vn_ref, kc_ref, vc_ref, o_ref):
    kk = jnp.concatenate([kc_ref[0], kn_ref[...]], axis=0).astype(BF16)
    vv = jnp.concatenate([vc_ref[0], vn_ref[...]], axis=0).astype(BF16)
    o_ref[...] = _attend(q_ref[...], kk, vv, sinks_ref, None).astype(o_ref.dtype)


def _attn_sample(q, k, kv, cache_k, cache_v, sinks, row0, bsz, seq):
    blk0 = row0 // seq
    kv_len = cache_k.shape[1]
    new = lambda w, col: pl.BlockSpec((seq, w), lambda b, s: (blk0 + b, col))
    cache = pl.BlockSpec((1, kv_len, KV_WIDTH), lambda b, s: (b, 0, 0))
    grid_spec = pltpu.PrefetchScalarGridSpec(
        num_scalar_prefetch=1,
        grid=(bsz,),
        in_specs=[new(ATTN_WIDTH, 0), new(KV_WIDTH, 0), new(KV_WIDTH, 1), cache, cache],
        out_specs=pl.BlockSpec((seq, ATTN_WIDTH), lambda b, s: (b, 0)),
    )
    return pl.pallas_call(
        _attn_sample_kernel,
        grid_spec=grid_spec,
        out_shape=jax.ShapeDtypeStruct((bsz * seq, ATTN_WIDTH), BF16),
        compiler_params=_cparams(("parallel",)),
        name="attn_sample",
    )(sinks, q, k, kv, cache_k, cache_v)


def _pad_rows(x, rows):
    if x.shape[0] == rows:
        return x
    return jnp.concatenate([x, jnp.zeros((rows - x.shape[0], x.shape[1]), x.dtype)], axis=0)


def _ssd_kernel(*refs, clen, has_past):
    if has_past:
        (xbc_ref, dt_ref, z_ref, cw_ref, cb_ref, dtb_ref, alog_ref, dskip_ref, nw_ref, cpast_ref, hpast_ref,
         y_ref, hout_ref, xpad, u_ref, g_ref, ht_ref, ca_ref, xdt_ref, yst_ref, xw_ref, cb2_ref) = refs
    else:
        (xbc_ref, dt_ref, z_ref, cw_ref, cb_ref, dtb_ref, alog_ref, dskip_ref, nw_ref,
         y_ref, hout_ref, xpad, u_ref, g_ref, ht_ref, ca_ref, xdt_ref, yst_ref, xw_ref, cb2_ref) = refs
    L = clen
    c = pl.program_id(1)
    nc = pl.num_programs(1)
    n_pairs = SSM_HEADS // 2
    pairs_per_group = n_pairs // SSM_GROUPS
    gw = D_INNER // SSM_GROUPS

    @pl.when(c == 0)
    def _init():
        if has_past:
            xpad[5:8, :] = cpast_ref[0]
            for p in range(n_pairs):
                both = jnp.concatenate([hpast_ref[0, 2 * p], hpast_ref[0, 2 * p + 1]], axis=0)
                ht_ref[:, p * LANES:(p + 1) * LANES] = both.T
        else:
            xpad[0:8, :] = jnp.zeros((8, CONV_DIM), F32)
            ht_ref[...] = jnp.zeros(ht_ref.shape, F32)

    xpad[8:8 + L, :] = xbc_ref[...]
    cblk = 512
    for j in range(CONV_DIM // cblk):
        sl = slice(j * cblk, (j + 1) * cblk)
        acc = cb_ref[:, sl] + xpad[8:8 + L, sl] * cw_ref[3:4, sl]
        acc = acc + xpad[7:7 + L, sl] * cw_ref[2:3, sl]
        acc = acc + xpad[6:6 + L, sl] * cw_ref[1:2, sl]
        acc = acc + xpad[5:5 + L, sl] * cw_ref[0:1, sl]
        u_ref[:, sl] = _silu(acc)
    xpad[5:8, :] = xpad[5 + L:8 + L, :]

    dtx = dt_ref[...] + dtb_ref[...]
    dt = jnp.maximum(dtx, 0.0) + jnp.log(1.0 + jnp.exp(-jnp.abs(dtx)))
    loga = dt * (-jnp.exp(alog_ref[...]))
    t_idx = lax.broadcasted_iota(I32, (L, L), 0)
    s_idx = lax.broadcasted_iota(I32, (L, L), 1)
    incl = (s_idx <= t_idx).astype(BF16)
    acum = sum(_dot(incl, part) for part in _split3(loga))
    acum_t = _pad_rows(acum, LANES).T[:, 0:L]

    lane_m = lax.broadcasted_iota(I32, (L, 2 * L), 1)
    row_m = lax.broadcasted_iota(I32, (L, 2 * L), 0)
    left_m = lane_m < L
    causal_m = jnp.where(left_m, lane_m, lane_m - L) <= row_m
    left = lax.broadcasted_iota(I32, (L, LANES), 1) < SSM_HEAD_DIM
    b_of = lambda grp: u_ref[:, D_INNER + grp * D_STATE:D_INNER + (grp + 1) * D_STATE]
    c_of = lambda grp: u_ref[:, D_INNER + BC_WIDTH + grp * D_STATE:D_INNER + BC_WIDTH + (grp + 1) * D_STATE]
    pair_lanes = lambda p: slice(p * LANES, (p + 1) * LANES)
    group_lanes = lambda grp: slice(grp * gw, (grp + 1) * gw)

    for p in range(n_pairs):
        h0, h1, sl = 2 * p, 2 * p + 1, pair_lanes(p)
        ca_ref[:, sl] = jnp.where(left, acum[:, h0:h0 + 1], acum[:, h1:h1 + 1])
        xdt_ref[:, sl] = u_ref[:, sl] * jnp.where(left, dt[:, h0:h0 + 1], dt[:, h1:h1 + 1])

    for grp in range(SSM_GROUPS):
        c_bf = c_of(grp).astype(BF16)
        cb = _dot_nt(c_bf, b_of(grp).astype(BF16))
        cb2_ref[grp] = jnp.concatenate([cb, cb], axis=1)
        yst_ref[:, group_lanes(grp)] = _dot(c_bf, ht_ref[:, group_lanes(grp)].astype(BF16))

    for p in range(n_pairs):
        h0, h1, sl = 2 * p, 2 * p + 1, pair_lanes(p)
        ca = ca_ref[:, sl]
        ca_m = ca if L == SSM_HEAD_DIM else jnp.where(left_m, acum[:, h0:h0 + 1], acum[:, h1:h1 + 1])
        row_a = jnp.concatenate([acum_t[h0:h0 + 1, :], acum_t[h1:h1 + 1, :]], axis=1)
        dec = jnp.exp(jnp.where(causal_m, ca_m - row_a, NEG_INF))
        m_pair = (cb2_ref[p // pairs_per_group] * dec).astype(BF16)
        xdt = xdt_ref[:, sl]
        x_bd = jnp.concatenate([jnp.where(left, xdt, 0.0), jnp.where(left, 0.0, xdt)], axis=0).astype(BF16)
        y = _dot(m_pair, x_bd) + yst_ref[:, sl] * jnp.exp(ca) + u_ref[:, sl] * dskip_ref[:, sl]
        g_ref[:, sl] = y * _silu(z_ref[:, sl].astype(F32))
        xw_ref[:, sl] = (xdt * jnp.exp(ca[L - 1:L, :] - ca)).astype(BF16)

    for grp in range(SSM_GROUPS):
        gsl = group_lanes(grp)
        gg = g_ref[:, gsl]
        ms = jnp.mean(gg * gg, axis=-1, keepdims=True)
        y_ref[:, gsl] = (gg * lax.rsqrt(ms + EPS) * nw_ref[:, gsl]).astype(y_ref.dtype)
        bt_bf = _pad_rows(b_of(grp), LANES).T[:, 0:L].astype(BF16)
        ht_ref[:, gsl] = ht_ref[:, gsl] * jnp.exp(ca_ref[L - 1:L, gsl]) + _dot(bt_bf, xw_ref[:, gsl])

    @pl.when(c == nc - 1)
    def _fin():
        for p in range(n_pairs):
            both = ht_ref[:, p * LANES:(p + 1) * LANES].T
            hout_ref[0, 2 * p] = both[0:SSM_HEAD_DIM, :]
            hout_ref[0, 2 * p + 1] = both[SSM_HEAD_DIM:2 * SSM_HEAD_DIM, :]


def _ssd(xbc, dt_raw, z, conv_w, conv_b, dt_bias, a_log, d_skip, ssm_norm, row0, bsz, seq, clen,
         conv_past=None, ssm_past=None):
    nc = seq // clen
    blk0 = row0 // clen
    has_past = conv_past is not None
    pad = lambda a: jnp.pad(a.astype(F32), (0, LANES - SSM_HEADS)).reshape(1, LANES)
    params = [conv_w.astype(F32), conv_b.astype(F32).reshape(1, CONV_DIM), pad(dt_bias), pad(a_log),
              jnp.repeat(d_skip.astype(F32), SSM_HEAD_DIM).reshape(1, D_INNER), ssm_norm.astype(F32).reshape(1, D_INNER)]
    rows = lambda w: pl.BlockSpec((clen, w), lambda b, c: (blk0 + b * nc + c, 0))
    full = lambda a: pl.BlockSpec(a.shape, lambda b, c: (0,) * a.ndim)
    in_specs = [rows(CONV_DIM), rows(LANES), rows(D_INNER)] + [full(a) for a in params]
    args = [xbc, dt_raw, z] + params
    if has_past:
        in_specs += [pl.BlockSpec((1, CONV_W - 1, CONV_DIM), lambda b, c: (b, 0, 0)),
                     pl.BlockSpec((1, SSM_HEADS, SSM_HEAD_DIM, D_STATE), lambda b, c: (b, 0, 0, 0))]
        args += [conv_past.astype(F32), ssm_past.astype(F32)]
    return pl.pallas_call(
        functools.partial(_ssd_kernel, clen=clen, has_past=has_past),
        grid=(bsz, nc),
        in_specs=in_specs,
        out_specs=[pl.BlockSpec((clen, D_INNER), lambda b, c: (b * nc + c, 0)),
                   pl.BlockSpec((1, SSM_HEADS, SSM_HEAD_DIM, D_STATE), lambda b, c: (b, 0, 0, 0))],
        out_shape=[jax.ShapeDtypeStruct((bsz * seq, D_INNER), BF16),
                   jax.ShapeDtypeStruct((bsz, SSM_HEADS, SSM_HEAD_DIM, D_STATE), F32)],
        scratch_shapes=[pltpu.VMEM((8 + clen, CONV_DIM), F32), pltpu.VMEM((clen, CONV_DIM), F32),
                        pltpu.VMEM((clen, D_INNER), F32), pltpu.VMEM((D_STATE, D_INNER), F32),
                        pltpu.VMEM((clen, D_INNER), F32), pltpu.VMEM((clen, D_INNER), F32),
                        pltpu.VMEM((clen, D_INNER), F32), pltpu.VMEM((clen, D_INNER), BF16),
                        pltpu.VMEM((SSM_GROUPS, clen, 2 * clen), F32)],
        compiler_params=_cparams(("parallel", "arbitrary")),
        name="ssd_past" if has_past else "ssd_prompt",
    )(*args)


def _merge_kernel(h_ref, a1_ref, a2_ref, s1_ref, s2_ref, wga_ref, wgs_ref, bga_ref, bgs_ref, wa_ref, ws_ref, o_ref,
                  *, n_first):
    def body(a_ref, s_ref):
        h = h_ref[...]
        g_a = jax.nn.sigmoid(_dot(h, wga_ref[...]) + bga_ref[...])
        g_s = jax.nn.sigmoid(_dot(h, wgs_ref[...]) + bgs_ref[...])
        o_ref[...] = (g_a * _dot(a_ref[...], wa_ref[...]) + g_s * _dot(s_ref[...], ws_ref[...])).astype(o_ref.dtype)

    _by_part(pl.program_id(0), n_first, body, (a1_ref, s1_ref), (a2_ref, s2_ref))


def _merge(h, attn_parts, ssm_parts, w_gate, b_gate, w_attn_up, w_ssm_up):
    m = h.shape[0]
    m1, m2 = attn_parts[0].shape[0], attn_parts[1].shape[0]
    tm = _pick(math.gcd(m1, m2), (512, 256, 128, 64, 32))
    nf = m1 // tm
    tn = 512
    nj = D_MODEL // tn
    bg = b_gate.astype(F32).reshape(1, 2 * D_MODEL)
    return pl.pallas_call(
        functools.partial(_merge_kernel, n_first=nf),
        grid=(m // tm, nj),
        in_specs=[pl.BlockSpec((tm, D_MODEL), lambda i, j: (i, 0)),
                  pl.BlockSpec((tm, ATTN_WIDTH), _first_rows(nf)), pl.BlockSpec((tm, ATTN_WIDTH), _later_rows(nf)),
                  pl.BlockSpec((tm, D_INNER), _first_rows(nf)), pl.BlockSpec((tm, D_INNER), _later_rows(nf)),
                  pl.BlockSpec((D_MODEL, tn), lambda i, j: (0, j)),
                  pl.BlockSpec((D_MODEL, tn), lambda i, j: (0, j + nj)),
                  pl.BlockSpec((1, tn), lambda i, j: (0, j)),
                  pl.BlockSpec((1, tn), lambda i, j: (0, j + nj)),
                  pl.BlockSpec((ATTN_WIDTH, tn), lambda i, j: (0, j)),
                  pl.BlockSpec((D_INNER, tn), lambda i, j: (0, j))],
        out_specs=pl.BlockSpec((tm, tn), lambda i, j: (i, j)),
        out_shape=jax.ShapeDtypeStruct((m, D_MODEL), BF16),
        compiler_params=_cparams(("parallel", "parallel")),
        name="merge",
    )(h, attn_parts[0], attn_parts[1], ssm_parts[0], ssm_parts[1], w_gate, w_gate, bg, bg, w_attn_up, w_ssm_up)


def _outproj_router_kernel(mg_ref, wo_ref, xa_ref, xb_ref, n2_ref, wr_ref, br_ref, x1_ref, h2_ref, idx_ref, gate_ref,
                           *, n_first):
    def residual(x_ref):
        x1_ref[...] = x_ref[...] + _dot(mg_ref[...], wo_ref[...])

    _by_part(pl.program_id(0), n_first, residual, (xa_ref,), (xb_ref,))
    x1 = x1_ref[...]
    ms = jnp.mean(x1 * x1, axis=-1, keepdims=True)
    h2 = x1 * lax.rsqrt(ms + EPS) * n2_ref[...]
    h2_ref[...] = _pack_bf16_pairs(h2)
    h_hi, h_lo = _split2(h2)
    w_hi, w_lo = _split2(wr_ref[...])
    logits = _dot(h_hi, w_hi) + (_dot(h_hi, w_lo) + _dot(h_lo, w_hi)) + br_ref[...]
    tm = logits.shape[0]
    lane = lax.broadcasted_iota(I32, (tm, LANES), 1)
    logits = jnp.where(lane < N_EXPERTS, logits, -jnp.inf)
    idx_out = jnp.zeros((tm, LANES), I32)
    val_out = jnp.zeros((tm, LANES), F32)
    top = None
    for k in range(TOP_K):
        v = jnp.max(logits, axis=-1, keepdims=True)
        i = jnp.min(jnp.where(logits == v, lane, LANES), axis=-1, keepdims=True)
        if k == 0:
            top = v
        idx_out = jnp.where(lane == k, i, idx_out)
        val_out = jnp.where(lane == k, jnp.exp(v - top), val_out)
        logits = jnp.where(lane == i, -jnp.inf, logits)
    idx_ref[...] = idx_out
    gate_ref[...] = val_out / jnp.sum(val_out, axis=-1, keepdims=True)


def _outproj_router(merged, w_out, xa, xb, norm2, w_router, b_router):
    m = merged.shape[0]
    tm = _pick(math.gcd(xa.shape[0], xb.shape[0]), (256, 128, 64, 32))
    nf = xa.shape[0] // tm
    wr = jnp.pad(w_router.astype(F32), ((0, 0), (0, LANES - N_EXPERTS)))
    br = jnp.pad(b_router.astype(F32), (0, LANES - N_EXPERTS)).reshape(1, LANES)
    row = lambda w: pl.BlockSpec((tm, w), lambda i: (i, 0))
    full = lambda r, c: pl.BlockSpec((r, c), lambda i: (0, 0))
    return pl.pallas_call(
        functools.partial(_outproj_router_kernel, n_first=nf),
        grid=(m // tm,),
        in_specs=[row(D_MODEL), full(D_MODEL, D_MODEL),
                  pl.BlockSpec((tm, D_MODEL), _first_rows(nf)), pl.BlockSpec((tm, D_MODEL), _later_rows(nf)),
                  full(1, D_MODEL), full(D_MODEL, LANES), full(1, LANES)],
        out_specs=[row(D_MODEL), row(D_MODEL // 2), row(LANES), row(LANES)],
        out_shape=[jax.ShapeDtypeStruct((m, D_MODEL), F32), jax.ShapeDtypeStruct((m, D_MODEL // 2), I32),
                   jax.ShapeDtypeStruct((m, LANES), I32), jax.ShapeDtypeStruct((m, LANES), F32)],
        compiler_params=_cparams(("parallel",)),
        name="outproj_router",
    )(merged, w_out, xa, xb, norm2.astype(F32).reshape(1, D_MODEL), wr, br)


def _rank_kernel(idx_ref, rank_ref, cnt_ref, base_ref):
    i = pl.program_id(0)
    tt = idx_ref.shape[0]

    @pl.when(i == 0)
    def _():
        base_ref[...] = jnp.zeros(base_ref.shape, F32)

    idx = idx_ref[...]
    lane = lax.broadcasted_iota(I32, (tt, LANES), 1)
    sel = [lane == idx[:, k:k + 1] for k in range(TOP_K)]
    onehot = jnp.zeros((tt, LANES), F32)
    for k in range(TOP_K):
        onehot = jnp.where(sel[k], 1.0, onehot)
    r_idx = lax.broadcasted_iota(I32, (tt, tt), 0)
    c_idx = lax.broadcasted_iota(I32, (tt, tt), 1)
    before = (c_idx < r_idx).astype(BF16)
    rank_all = _dot(before, onehot.astype(BF16)) + base_ref[0:1, :]
    out = jnp.zeros((tt, LANES), F32)
    for k in range(TOP_K):
        out = jnp.where(lane == k, jnp.sum(jnp.where(sel[k], rank_all, 0.0), axis=-1, keepdims=True), out)
    rank_ref[...] = out.astype(I32)
    base_ref[0:1, :] = base_ref[0:1, :] + jnp.sum(onehot, axis=0, keepdims=True)
    cnt_ref[...] = base_ref[...].astype(I32)


def _ranks(idx):
    m = idx.shape[0]
    tt = _pick(m, (512, 256, 128, 64, 32))
    return pl.pallas_call(
        _rank_kernel,
        grid=(m // tt,),
        in_specs=[pl.BlockSpec((tt, LANES), lambda i: (i, 0))],
        out_specs=[pl.BlockSpec((tt, LANES), lambda i: (i, 0)), pl.BlockSpec((8, LANES), lambda i: (0, 0))],
        out_shape=[jax.ShapeDtypeStruct((m, LANES), I32), jax.ShapeDtypeStruct((8, LANES), I32)],
        scratch_shapes=[pltpu.VMEM((8, LANES), F32)],
        compiler_params=_cparams(("arbitrary",)),
        name="expert_ranks",
    )(idx)


def _dispatch_kernel(dest_ref, h_ref, xs_in_hbm, xs_hbm, sem, *, tt):
    del xs_in_hbm
    base = pl.program_id(0) * (tt * TOP_K)

    def issue(r, carry):
        for k in range(TOP_K):
            dst = dest_ref[base + r * TOP_K + k]
            pltpu.make_async_copy(h_ref.at[pl.ds(r, 1), :], xs_hbm.at[pl.ds(dst, 1), :], sem).start()
        return carry

    lax.fori_loop(0, tt, issue, 0, unroll=2)
    for _ in range(TOP_K):
        pltpu.make_async_copy(h_ref, xs_hbm.at[pl.ds(0, tt), :], sem).wait()


def _dispatch(dest_flat, h2, n_rows):
    m = h2.shape[0]
    tt = _pick(m, (256, 128, 64, 32))
    width = h2.shape[1]
    xs0 = jnp.zeros((n_rows, width), h2.dtype)
    grid_spec = pltpu.PrefetchScalarGridSpec(
        num_scalar_prefetch=1,
        grid=(m // tt,),
        in_specs=[pl.BlockSpec((tt, width), lambda i, d: (i, 0)), pl.BlockSpec(memory_space=pl.ANY)],
        out_specs=pl.BlockSpec(memory_space=pl.ANY),
        scratch_shapes=[pltpu.SemaphoreType.DMA],
    )
    return pl.pallas_call(
        functools.partial(_dispatch_kernel, tt=tt),
        grid_spec=grid_spec,
        out_shape=jax.ShapeDtypeStruct((n_rows, width), h2.dtype),
        input_output_aliases={2: 0},
        compiler_params=_cparams(("arbitrary",)),
        name="dispatch",
    )(dest_flat, h2, xs0)


def _pack_bf16_pairs(x):
    c = x.shape[1] // 2
    bits = lax.bitcast_convert_type(x.astype(BF16).astype(F32), I32)
    return lax.shift_right_logical(bits[:, :c], 16) | bits[:, c:]


def _unpack_bf16_pairs(p):
    lo = lax.bitcast_convert_type(lax.shift_left(p, 16), F32).astype(BF16)
    hi = lax.bitcast_convert_type(p & jnp.int32(-65536), F32).astype(BF16)
    return lo, hi


def _expert_kernel(be_ref, nu_ref, xs_ref, bg_ref, bu_ref, bd_ref, wg_hbm, wu_hbm, wd_hbm, ys_ref,
                   cg, cu, cd, stg_g, stg_u, stg_d, xb_ref, sems, *, tf):
    i = pl.program_id(0)
    nj = D_FF // tf
    half = D_MODEL // 2
    e = be_ref[i]
    first = jnp.logical_or(i == 0, e != be_ref[jnp.maximum(i - 1, 0)])

    def tile_copies(t, slot):
        cols = pl.ds(pl.multiple_of(t * tf, tf), tf)
        return (pltpu.make_async_copy(wg_hbm.at[e, :, cols], stg_g.at[slot], sems.at[slot, 0]),
                pltpu.make_async_copy(wu_hbm.at[e, :, cols], stg_u.at[slot], sems.at[slot, 1]),
                pltpu.make_async_copy(wd_hbm.at[e, cols, :], stg_d.at[slot], sems.at[slot, 2]))

    @pl.when(i < nu_ref[0])
    def _():
        @pl.when(first)
        def _():
            for cp in tile_copies(0, 0):
                cp.start()

        lo, hi = _unpack_bf16_pairs(xs_ref[...])
        xb_ref[:, :half] = lo
        xb_ref[:, half:] = hi

        def refill(t):
            slot = t % 2

            @pl.when(t + 1 < nj)
            def _():
                for cp in tile_copies(t + 1, 1 - slot):
                    cp.start()

            for cp in tile_copies(t, slot):
                cp.wait()
            cg[t] = stg_g[slot].astype(BF16)
            cu[t] = stg_u[slot].astype(BF16)
            cd[t] = stg_d[slot].astype(BF16)

        def compute(t, opening):
            xb = xb_ref[...]
            g = jnp.minimum(_dot(xb, cg[t]) + bg_ref[0, t], SWIGLU_LIMIT)
            u = jnp.clip(_dot(xb, cu[t]) + bu_ref[0, t], -SWIGLU_LIMIT, SWIGLU_LIMIT)
            act = (u + 1.0) * (g * jax.nn.sigmoid(SWIGLU_ALPHA * g))
            part = _dot(act.astype(BF16), cd[t])
            if opening:
                ys_ref[...] = part + bd_ref[0]
            else:
                ys_ref[...] += part

        @pl.when(first)
        def _():
            refill(jnp.int32(0))
            compute(0, True)

            def step(t, carry):
                refill(t)
                compute(t, False)
                return carry

            lax.fori_loop(1, nj, step, 0)

        @pl.when(jnp.logical_not(first))
        def _():
            for t in range(nj):
                compute(t, t == 0)

    @pl.when(i >= nu_ref[0])
    def _():
        ys_ref[...] = jnp.zeros(ys_ref.shape, F32)


def _experts(block_e, n_used, xs, w_e_gate, b_e_gate, w_e_up, b_e_up, w_e_down, b_e_down):
    n_rows = xs.shape[0]
    tm, tf = MOE_TM, MOE_TF
    n_blocks = n_rows // tm
    nj = D_FF // tf
    half = D_MODEL // 2

    def blk(i, nu):
        return jnp.minimum(i, nu[0] - 1)

    hbm = pl.BlockSpec(memory_space=pl.ANY)
    grid_spec = pltpu.PrefetchScalarGridSpec(
        num_scalar_prefetch=2,
        grid=(n_blocks,),
        in_specs=[pl.BlockSpec((tm, half), lambda i, be, nu: (blk(i, nu), 0)),
                  pl.BlockSpec((1, nj, 1, tf), lambda i, be, nu: (be[blk(i, nu)], 0, 0, 0)),
                  pl.BlockSpec((1, nj, 1, tf), lambda i, be, nu: (be[blk(i, nu)], 0, 0, 0)),
                  pl.BlockSpec((1, 1, D_MODEL), lambda i, be, nu: (be[blk(i, nu)], 0, 0)),
                  hbm, hbm, hbm],
        out_specs=pl.BlockSpec((tm, D_MODEL), lambda i, be, nu: (i, 0)),
        scratch_shapes=[pltpu.VMEM((nj, D_MODEL, tf), BF16), pltpu.VMEM((nj, D_MODEL, tf), BF16),
                        pltpu.VMEM((nj, tf, D_MODEL), BF16),
                        pltpu.VMEM((2, D_MODEL, tf), F32), pltpu.VMEM((2, D_MODEL, tf), F32),
                        pltpu.VMEM((2, tf, D_MODEL), F32),
                        pltpu.VMEM((tm, D_MODEL), BF16), pltpu.SemaphoreType.DMA((2, 3))],
    )
    return pl.pallas_call(
        functools.partial(_expert_kernel, tf=tf),
        grid_spec=grid_spec,
        out_shape=jax.ShapeDtypeStruct((n_rows, D_MODEL), F32),
        compiler_params=pltpu.CompilerParams(dimension_semantics=("arbitrary",), vmem_limit_bytes=EXPERT_VMEM_LIMIT),
        name="experts",
    )(block_e, n_used, xs, b_e_gate.reshape(N_EXPERTS, nj, 1, tf), b_e_up.reshape(N_EXPERTS, nj, 1, tf),
      b_e_down.reshape(N_EXPERTS, 1, D_MODEL), w_e_gate, w_e_up, w_e_down)


def _combine_kernel(dest_ref, x1_ref, gate_ref, ys_hbm, o_ref, buf, sems, *, tt, row0):
    i = pl.program_id(0)
    n_steps = pl.num_programs(0)
    slot = i % 2

    def gather(step, to_slot):
        base = (row0 + step * tt) * TOP_K

        def issue(r, carry):
            for k in range(TOP_K):
                src = dest_ref[base + r * TOP_K + k]
                pltpu.make_async_copy(ys_hbm.at[pl.ds(src, 1), :], buf.at[to_slot, k, pl.ds(r, 1), :],
                                      sems.at[to_slot]).start()
            return carry

        lax.fori_loop(0, tt, issue, 0, unroll=2)

    @pl.when(i == 0)
    def _():
        gather(0, 0)

    @pl.when(i + 1 < n_steps)
    def _():
        gather(i + 1, 1 - slot)

    for k in range(TOP_K):
        pltpu.make_async_copy(ys_hbm.at[pl.ds(0, tt), :], buf.at[slot, k], sems.at[slot]).wait()
    gate = gate_ref[...]
    acc = x1_ref[...]
    for k in range(TOP_K):
        acc = acc + gate[:, k:k + 1] * buf[slot, k]
    o_ref[...] = acc


def _combine(dest_flat, x1, gate, ys, row0, rows):
    tt = _pick(math.gcd(row0, rows), (128, 64, 32))
    blk0 = row0 // tt
    grid_spec = pltpu.PrefetchScalarGridSpec(
        num_scalar_prefetch=1,
        grid=(rows // tt,),
        in_specs=[pl.BlockSpec((tt, D_MODEL), lambda i, d: (blk0 + i, 0)),
                  pl.BlockSpec((tt, LANES), lambda i, d: (blk0 + i, 0)),
                  pl.BlockSpec(memory_space=pl.ANY)],
        out_specs=pl.BlockSpec((tt, D_MODEL), lambda i, d: (i, 0)),
        scratch_shapes=[pltpu.VMEM((2, TOP_K, tt, D_MODEL), F32), pltpu.SemaphoreType.DMA((2,))],
    )
    return pl.pallas_call(
        functools.partial(_combine_kernel, tt=tt, row0=row0),
        grid_spec=grid_spec,
        out_shape=jax.ShapeDtypeStruct((rows, D_MODEL), F32),
        compiler_params=_cparams(("arbitrary",)),
        name="combine",
    )(dest_flat, x1, gate, ys)


def _moe(x1, h2, idx, gate, w_e_gate, b_e_gate, w_e_up, b_e_up, w_e_down, b_e_down, n_first):
    m = x1.shape[0]
    tm = MOE_TM
    ranks, cnt = _ranks(idx)
    counts = cnt[0, :N_EXPERTS]
    padded = (counts + tm - 1) // tm * tm
    pad_end = jnp.cumsum(padded)
    pad_start = pad_end - padded
    e_sel = idx[:, :TOP_K]
    dest = (pad_start[e_sel] + ranks[:, :TOP_K]).astype(I32).reshape(-1)
    n_blocks = -(-(m * TOP_K + N_EXPERTS * (tm - 1)) // tm)
    block_row = jnp.arange(n_blocks, dtype=I32) * tm
    block_e = jnp.minimum(jnp.sum((pad_end[None, :] <= block_row[:, None]).astype(I32), axis=1), N_EXPERTS - 1)
    n_used = (pad_end[-1] // tm).astype(I32).reshape(1)
    xs = _dispatch(dest, h2, n_blocks * tm)
    ys = _experts(block_e, n_used, xs, w_e_gate, b_e_gate, w_e_up, b_e_up, w_e_down, b_e_down)
    return _combine(dest, x1, gate, ys, 0, n_first), _combine(dest, x1, gate, ys, n_first, m - n_first)


def kernel(x_prompt, x_sample, cache_k, cache_v, state_conv, state_ssm, norm1, w_in, q_norm, k_norm, sinks, conv_w,
           conv_b, dt_bias, a_log, d_skip, ssm_norm, w_gate, b_gate, w_attn_up, w_ssm_up, w_out, norm2, w_router,
           b_router, w_e_gate, b_e_gate, w_e_up, b_e_up, w_e_down, b_e_down):
    bp, sp, _ = x_prompt.shape
    bs, ss, _ = x_sample.shape
    tp, ts = bp * sp, bs * ss
    kv_len = cache_k.shape[2]
    l = 0

    xa, xb = x_prompt.reshape(tp, D_MODEL), x_sample.reshape(ts, D_MODEL)
    pos = jnp.concatenate([jnp.tile(jnp.arange(sp, dtype=I32), bp), jnp.tile(PAST_LEN + jnp.arange(ss, dtype=I32), bs)])

    o1 = ATTN_WIDTH
    o2 = o1 + 2 * KV_WIDTH
    o3 = o2 + D_INNER
    o4 = o3 + CONV_DIM
    h = _rmsnorm(xa, xb, norm1[l], BF16)
    q_p = _matmul(h, w_in, l, 0, o1, F32, "proj_q")
    kv_p = _matmul(h, w_in, l, o1, o2 - o1, F32, "proj_kv")
    z = _matmul(h, w_in, l, o2, o3 - o2, BF16, "proj_z")
    xbc = _matmul(h, w_in, l, o3, o4 - o3, F32, "proj_xbc")
    w_dt = jnp.pad(w_in[l:l + 1, :, o4:], ((0, 0), (0, 0), (0, LANES - SSM_HEADS)))
    dt_raw = _matmul(h, w_dt, 0, 0, LANES, F32, "proj_dt")

    q_rot, k_rot = _qk_prep(q_p, kv_p, pos, q_norm[l], k_norm[l])
    sk = sinks[l].astype(F32)
    attn_p = _attn_prompt(q_rot, k_rot, kv_p, sk, bp, sp)
    ck = cache_k[l].reshape(bs, kv_len, KV_WIDTH)
    cv = cache_v[l].reshape(bs, kv_len, KV_WIDTH)
    attn_s = _attn_sample(q_rot, k_rot, kv_p, ck, cv, sk, tp, bs, ss)

    ssd_w = (conv_w[l], conv_b[l], dt_bias[l], a_log[l], d_skip[l], ssm_norm[l])
    ssm_p, hfin_p = _ssd(xbc, dt_raw, z, *ssd_w, 0, bp, sp, CHUNK)
    ssm_s, hfin_s = _ssd(xbc, dt_raw, z, *ssd_w, tp, bs, ss, ss, state_conv[l], state_ssm[l])

    merged = _merge(h, (attn_p, attn_s), (ssm_p, ssm_s), w_gate[l].astype(BF16), b_gate[l], w_attn_up[l].astype(BF16), w_ssm_up[l].astype(BF16))
    x1, h2, idx, gate = _outproj_router(merged, w_out[l].astype(BF16), xa, xb, norm2[l], w_router[l], b_router[l])
    y_p, y_s = _moe(x1, h2, idx, gate, w_e_gate[l], b_e_gate[l], w_e_up[l], b_e_up[l], w_e_down[l], b_e_down[l], tp)

    def prompt_tail(a, n):
        return jnp.stack([a[(b + 1) * sp - n:(b + 1) * sp] for b in range(bp)])

    heads = lambda a: a.reshape(a.shape[0], a.shape[1], N_KV_HEADS, HEAD_DIM)
    new_k_p = heads(prompt_tail(k_rot, WINDOW))
    new_v_p = heads(prompt_tail(kv_p, WINDOW)[:, :, KV_WIDTH:])
    ks4 = heads(k_rot[tp:].reshape(bs, ss, KV_WIDTH))
    vs4 = heads(kv_p[tp:, KV_WIDTH:].reshape(bs, ss, KV_WIDTH))
    new_k_s = jnp.concatenate([cache_k[l], ks4], axis=1)[:, -kv_len:]
    new_v_s = jnp.concatenate([cache_v[l], vs4], axis=1)[:, -kv_len:]
    xbc_s = jnp.concatenate([state_conv[l], xbc[tp:].reshape(bs, ss, CONV_DIM)], axis=1)
    return (y_p.reshape(bp, sp, D_MODEL), y_s.reshape(bs, ss, D_MODEL),
            new_k_p[None], new_v_p[None], prompt_tail(xbc, CONV_W - 1)[None], hfin_p[None],
            new_k_s[None], new_v_s[None], xbc_s[:, -(CONV_W - 1):][None], hfin_s[None])
```

```python
import functools
import math

import jax
import jax.numpy as jnp
from jax import lax
from jax.experimental import pallas as pl
from jax.experimental.pallas import tpu as pltpu

F32 = jnp.float32
BF16 = jnp.bfloat16
I32 = jnp.int32

D_MODEL = 2048
CHUNK = 64
N_HEADS = 32
N_KV_HEADS = 8
HEAD_DIM = 64
Q_PER_KV = N_HEADS // N_KV_HEADS
ATTN_WIDTH = N_HEADS * HEAD_DIM
KV_WIDTH = N_KV_HEADS * HEAD_DIM
WINDOW = 128
N_PREV_CHUNKS = WINDOW // CHUNK
ROPE_THETA = 500000.0
ROT_DIM = HEAD_DIM // 4
D_INNER = 2 * D_MODEL
SSM_HEAD_DIM = 64
SSM_HEADS = D_INNER // SSM_HEAD_DIM
SSM_GROUPS = 8
D_STATE = 128
CONV_W = 4
BC_WIDTH = SSM_GROUPS * D_STATE
CONV_DIM = D_INNER + 2 * BC_WIDTH
N_EXPERTS = 32
TOP_K = 4
D_FF = D_MODEL
SWIGLU_LIMIT = 7.0
SWIGLU_ALPHA = 1.702
EPS = 1e-6
NEG_INF = -1e30
PAST_LEN = 2048

LANES = 128
VMEM_LIMIT = 56 * 1024 * 1024
EXPERT_VMEM_LIMIT = 60 * 1024 * 1024
MOE_TM = 512
MOE_TF = 256


def _pick(n, cands):
    for c in cands:
        if n % c == 0:
            return c
    return n


def _cparams(sem):
    return pltpu.CompilerParams(dimension_semantics=sem, vmem_limit_bytes=VMEM_LIMIT)


def _split2(x):
    hi = x.astype(BF16)
    lo = (x - hi.astype(F32)).astype(BF16)
    return hi, lo


def _split3(x):
    hi = x.astype(BF16)
    r = x - hi.astype(F32)
    mid = r.astype(BF16)
    lo = (r - mid.astype(F32)).astype(BF16)
    return hi, mid, lo


def _dot(a, b):
    return jnp.dot(a, b, preferred_element_type=F32)


def _dot_nt(a, b):
    return lax.dot_general(a, b, (((1,), (1,)), ((), ())), preferred_element_type=F32)


def _silu(x):
    half = 0.5 * x
    return half + half * jnp.tanh(half)


def _first_rows(n_first):
    return lambda i, *_: (jnp.minimum(i, n_first - 1), 0)


def _later_rows(n_first):
    return lambda i, *_: (jnp.maximum(i - n_first, 0), 0)


def _by_part(i, n_first, fn, first_refs, later_refs):
    @pl.when(i < n_first)
    def _():
        fn(*first_refs)

    @pl.when(i >= n_first)
    def _():
        fn(*later_refs)


def _rms_kernel(xa_ref, xb_ref, w_ref, o_ref, *, n_first):
    def body(x_ref):
        x = x_ref[...]
        ms = jnp.mean(x * x, axis=-1, keepdims=True)
        o_ref[...] = (x * lax.rsqrt(ms + EPS) * w_ref[...]).astype(o_ref.dtype)

    _by_part(pl.program_id(0), n_first, body, (xa_ref,), (xb_ref,))


def _rmsnorm(xa, xb, w, out_dtype):
    (ma, d), mb = xa.shape, xb.shape[0]
    tm = _pick(math.gcd(ma, mb), (512, 256, 128, 64, 32))
    nf = ma // tm
    return pl.pallas_call(
        functools.partial(_rms_kernel, n_first=nf),
        grid=((ma + mb) // tm,),
        in_specs=[pl.BlockSpec((tm, d), _first_rows(nf)), pl.BlockSpec((tm, d), _later_rows(nf)),
                  pl.BlockSpec((1, d), lambda i: (0, 0))],
        out_specs=pl.BlockSpec((tm, d), lambda i: (i, 0)),
        out_shape=jax.ShapeDtypeStruct((ma + mb, d), out_dtype),
        compiler_params=_cparams(("parallel",)),
        name="rmsnorm",
    )(xa, xb, w.reshape(1, d))


def _mm_kernel(x_ref, w_ref, o_ref, wb_ref):
    @pl.when(pl.program_id(1) == 0)
    def _():
        wb_ref[...] = w_ref[0].astype(BF16)

    o_ref[...] = _dot_nt(x_ref[...], wb_ref[...]).astype(o_ref.dtype)


def _matmul(x, wt, layer, row0, n, out_dtype, name):
    m, k = x.shape
    tm = _pick(m, (1024, 512, 256, 128, 64, 32))
    tn = _pick(math.gcd(n, row0), (1024, 512, 256, 128))
    rb0 = row0 // tn
    return pl.pallas_call(
        _mm_kernel,
        grid=(n // tn, m // tm),
        in_specs=[pl.BlockSpec((tm, k), lambda j, i: (i, 0)),
                  pl.BlockSpec((1, tn, k), lambda j, i: (layer, rb0 + j, 0))],
        out_specs=pl.BlockSpec((tm, tn), lambda j, i: (i, j)),
        out_shape=jax.ShapeDtypeStruct((m, n), out_dtype),
        scratch_shapes=[pltpu.VMEM((tn, k), BF16)],
        compiler_params=_cparams(("parallel", "arbitrary")),
        name=name,
    )(x, wt)


def _qk_prep_kernel(q_ref, k_ref, cos_ref, s1_ref, s2_ref, qn_ref, kn_ref, g_ref, gt_ref, qo_ref, ko_ref):
    cos = cos_ref[...]
    s1 = s1_ref[...]
    s2 = s2_ref[...]

    def norm_rope(x, nw, width):
        g = g_ref[0:width, :]
        gt = gt_ref[:, 0:width]
        sq_hi, sq_lo = _split2(x * x)
        ssum = _dot(sq_hi, g) + _dot(sq_lo, g)
        r = lax.rsqrt(ssum * (1.0 / HEAD_DIM) + EPS)
        r_hi, r_lo = _split2(r)
        y = x * (_dot(r_hi, gt) + _dot(r_lo, gt)) * nw
        outs = []
        for s in range(width // LANES):
            blk = y[:, s * LANES:(s + 1) * LANES]
            outs.append(blk * cos + pltpu.roll(blk, ROT_DIM // 2, 1) * s1
                        + pltpu.roll(blk, LANES - ROT_DIM // 2, 1) * s2)
        return jnp.concatenate(outs, axis=1)

    q = norm_rope(q_ref[...], qn_ref[...], ATTN_WIDTH)
    qo_ref[...] = (q * (HEAD_DIM ** -0.5)).astype(qo_ref.dtype)
    ko_ref[...] = norm_rope(k_ref[...], kn_ref[...], KV_WIDTH)


def _qk_prep(q, k, pos, q_norm, k_norm):
    m = q.shape[0]
    tm = _pick(m, (256, 128, 64, 32))
    half = ROT_DIM // 2
    inv_freq = ROPE_THETA ** (-jnp.arange(half, dtype=F32) * 2.0 / ROT_DIM)
    ang = pos.astype(F32)[:, None] * inv_freq[None, :]
    cos, sin = jnp.cos(ang), jnp.sin(ang)
    ones = jnp.ones((m, HEAD_DIM - ROT_DIM), F32)
    zeros = jnp.zeros((m, HEAD_DIM - ROT_DIM), F32)
    zh = jnp.zeros((m, half), F32)
    cos_t = jnp.tile(jnp.concatenate([cos, cos, ones], axis=1), (1, LANES // HEAD_DIM))
    s1_t = jnp.tile(jnp.concatenate([zh, sin, zeros], axis=1), (1, LANES // HEAD_DIM))
    s2_t = jnp.tile(jnp.concatenate([-sin, zh, zeros], axis=1), (1, LANES // HEAD_DIM))
    head_of_lane = jnp.arange(ATTN_WIDTH) // HEAD_DIM
    g = (head_of_lane[:, None] == jnp.arange(LANES)[None, :]).astype(BF16)
    gt = g.T
    qn = jnp.tile(q_norm.astype(F32), N_HEADS).reshape(1, ATTN_WIDTH)
    kn = jnp.tile(k_norm.astype(F32), N_KV_HEADS).reshape(1, KV_WIDTH)
    row = lambda w: pl.BlockSpec((tm, w), lambda i: (i, 0))
    full = lambda a: pl.BlockSpec(a.shape, lambda i: (0, 0))
    return pl.pallas_call(
        _qk_prep_kernel,
        grid=(m // tm,),
        in_specs=[row(ATTN_WIDTH), row(KV_WIDTH), row(LANES), row(LANES), row(LANES),
                  full(qn), full(kn), full(g), full(gt)],
        out_specs=[row(ATTN_WIDTH), row(KV_WIDTH)],
        out_shape=[jax.ShapeDtypeStruct((m, ATTN_WIDTH), BF16), jax.ShapeDtypeStruct((m, KV_WIDTH), F32)],
        compiler_params=_cparams(("parallel",)),
        name="qk_prep",
    )(q, k, cos_t, s1_t, s2_t, qn, kn, g, gt)


def _attend(q, kk, vv, sinks_ref, valid):
    tq = q.shape[0]
    outs = []
    for j in range(N_KV_HEADS):
        kj = kk[:, j * HEAD_DIM:(j + 1) * HEAD_DIM]
        vj = vv[:, j * HEAD_DIM:(j + 1) * HEAD_DIM]
        heads = [Q_PER_KV * j + g for g in range(Q_PER_KV)]
        q4 = jnp.concatenate([q[:, h * HEAD_DIM:(h + 1) * HEAD_DIM] for h in heads], axis=0)
        s = _dot_nt(q4, kj)
        if valid is not None:
            s = jnp.where(valid, s, NEG_INF)
        sink = jnp.concatenate([jnp.full((tq, 1), sinks_ref[h], F32) for h in heads], axis=0)
        m = jnp.maximum(jnp.max(s, axis=-1, keepdims=True), sink)
        p = jnp.exp(s - m)
        denom = jnp.sum(p, axis=-1, keepdims=True) + jnp.exp(sink - m)
        o = _dot(p.astype(BF16), vj) / denom
        for g in range(Q_PER_KV):
            outs.append(o[g * tq:(g + 1) * tq, :])
    return jnp.concatenate(outs, axis=1)


def _attn_prompt_kernel(sinks_ref, q_ref, ka_ref, kb_ref, va_ref, vb_ref, o_ref, st_ref, pt_ref, rl_ref):
    i = pl.program_id(1)
    tq = q_ref.shape[0]
    tk = 2 * tq
    q = q_ref[...]
    k32 = jnp.concatenate([ka_ref[...], kb_ref[...]], axis=0)
    vt = jnp.concatenate([va_ref[...], vb_ref[...]], axis=0).T.astype(BF16)

    lane_k = lax.broadcasted_iota(I32, (tk, LANES), 1)
    kc = lax.broadcasted_iota(I32, (tk, LANES), 0) // CHUNK
    k_ind = jnp.where(lane_k == 0, jnp.where(kc == 0, 1.0, 0.0),
                      jnp.where(lane_k == 1, jnp.where(kc == N_PREV_CHUNKS + 1, 1.0, 0.0),
                                jnp.where(lane_k == 2, jnp.where(kc < N_PREV_CHUNKS, 1.0, 0.0), 0.0)))
    first_step = jnp.where(i == 0, 1.0, 0.0)
    k_ind = jnp.where(lane_k == 2, k_ind * first_step, k_ind).astype(BF16)
    lane_q = lax.broadcasted_iota(I32, (tq, LANES), 1)
    qc = lax.broadcasted_iota(I32, (tq, LANES), 0) // CHUNK
    q_msk = jnp.where(lane_q == 0, jnp.where(qc == 1, NEG_INF, 0.0),
                      jnp.where(lane_q == 1, jnp.where(qc == 0, NEG_INF, 0.0),
                                jnp.where(lane_q == 2, NEG_INF, 0.0))).astype(BF16)
    low = lane_k < HEAD_DIM

    for slab in range(KV_WIDTH // LANES):
        ks = k32[:, slab * LANES:(slab + 1) * LANES]
        kr = pltpu.roll(ks, HEAD_DIM, 1)
        for jj in range(2):
            j = 2 * slab + jj
            in_low = jnp.where(low, ks if jj == 0 else kr, 0.0).astype(BF16)
            in_high = jnp.where(low, 0.0, kr if jj == 0 else ks).astype(BF16)
            k_ext = (jnp.concatenate([in_low, k_ind], axis=1), jnp.concatenate([in_high, k_ind], axis=1))
            for g in range(Q_PER_KV):
                h = Q_PER_KV * j + g
                pair = h // 2
                q_ext = jnp.concatenate([q[:, pair * LANES:(pair + 1) * LANES], q_msk], axis=1)
                st_ref[h] = _dot_nt(k_ext[h % 2], q_ext)

    for h in range(N_HEADS):
        st = st_ref[h]
        sink = sinks_ref[h]
        m = jnp.maximum(jnp.max(st, axis=0, keepdims=True), sink)
        pt = jnp.exp(st - m)
        rl_ref[h:h + 1, :] = 1.0 / (jnp.sum(pt, axis=0, keepdims=True) + jnp.exp(sink - m))
        pt_ref[h] = pt.astype(BF16)

    for pair in range(N_HEADS // 2):
        pieces = []
        for h in (2 * pair, 2 * pair + 1):
            j = h // Q_PER_KV
            pieces.append(_dot(vt[j * HEAD_DIM:(j + 1) * HEAD_DIM, :], pt_ref[h]) * rl_ref[h:h + 1, :])
        o_ref[:, pair * LANES:(pair + 1) * LANES] = jnp.concatenate(pieces, axis=0).T.astype(o_ref.dtype)


def _attn_prompt(q, k, kv, sinks, bsz, seq):
    tq = 2 * CHUNK
    nb = seq // tq

    def kv_spec(back, col):
        return pl.BlockSpec((tq, KV_WIDTH), lambda b, i, s: (b * nb + jnp.maximum(i - back, 0), col))

    grid_spec = pltpu.PrefetchScalarGridSpec(
        num_scalar_prefetch=1,
        grid=(bsz, nb),
        in_specs=[pl.BlockSpec((tq, ATTN_WIDTH), lambda b, i, s: (b * nb + i, 0)),
                  kv_spec(1, 0), kv_spec(0, 0), kv_spec(1, 1), kv_spec(0, 1)],
        out_specs=pl.BlockSpec((tq, ATTN_WIDTH), lambda b, i, s: (b * nb + i, 0)),
        scratch_shapes=[pltpu.VMEM((N_HEADS, 2 * tq, tq), F32), pltpu.VMEM((N_HEADS, 2 * tq, tq), BF16),
                        pltpu.VMEM((N_HEADS, tq), F32)],
    )
    return pl.pallas_call(
        _attn_prompt_kernel,
        grid_spec=grid_spec,
        out_shape=jax.ShapeDtypeStruct((bsz * seq, ATTN_WIDTH), BF16),
        compiler_params=_cparams(("parallel", "parallel")),
        name="attn_prompt",
    )(sinks, q, k, k, kv, kv)


def _attn_sample_kernel(sinks_ref, q_ref, kn_ref, vn_ref, kc_ref, vc_ref, o_ref):
    kk = jnp.concatenate([kc_ref[0], kn_ref[...]], axis=0).astype(BF16)
    vv = jnp.concatenate([vc_ref[0], vn_ref[...]], axis=0).astype(BF16)
    o_ref[...] = _attend(q_ref[...], kk, vv, sinks_ref, None).astype(o_ref.dtype)


def _attn_sample(q, k, kv, cache_k, cache_v, sinks, row0, bsz, seq):
    blk0 = row0 // seq
    kv_len = cache_k.shape[1]
    new = lambda w, col: pl.BlockSpec((seq, w), lambda b, s: (blk0 + b, col))
    cache = pl.BlockSpec((1, kv_len, KV_WIDTH), lambda b, s: (b, 0, 0))
    grid_spec = pltpu.PrefetchScalarGridSpec(
        num_scalar_prefetch=1,
        grid=(bsz,),
        in_specs=[new(ATTN_WIDTH, 0), new(KV_WIDTH, 0), new(KV_WIDTH, 1), cache, cache],
        out_specs=pl.BlockSpec((seq, ATTN_WIDTH), lambda b, s: (b, 0)),
    )
    return pl.pallas_call(
        _attn_sample_kernel,
        grid_spec=grid_spec,
        out_shape=jax.ShapeDtypeStruct((bsz * seq, ATTN_WIDTH), BF16),
        compiler_params=_cparams(("parallel",)),
        name="attn_sample",
    )(sinks, q, k, kv, cache_k, cache_v)


def _pad_rows(x, rows):
    if x.shape[0] == rows:
        return x
    return jnp.concatenate([x, jnp.zeros((rows - x.shape[0], x.shape[1]), x.dtype)], axis=0)


def _ssd_kernel(*refs, clen, has_past):
    if has_past:
        (xbc_ref, dt_ref, z_ref, cw_ref, cb_ref, dtb_ref, alog_ref, dskip_ref, nw_ref, cpast_ref, hpast_ref,
         y_ref, hout_ref, xpad, u_ref, g_ref, ht_ref, ca_ref, xdt_ref, yst_ref, xw_ref, cb2_ref) = refs
    else:
        (xbc_ref, dt_ref, z_ref, cw_ref, cb_ref, dtb_ref, alog_ref, dskip_ref, nw_ref,
         y_ref, hout_ref, xpad, u_ref, g_ref, ht_ref, ca_ref, xdt_ref, yst_ref, xw_ref, cb2_ref) = refs
    L = clen
    c = pl.program_id(1)
    nc = pl.num_programs(1)
    n_pairs = SSM_HEADS // 2
    pairs_per_group = n_pairs // SSM_GROUPS
    gw = D_INNER // SSM_GROUPS

    @pl.when(c == 0)
    def _init():
        if has_past:
            for cidx in range(CONV_DIM // LANES):
                xpad[cidx, 5:8, :] = cpast_ref[0, :, cidx * LANES:(cidx + 1) * LANES]
            for p in range(n_pairs):
                both = jnp.concatenate([hpast_ref[0, 2 * p], hpast_ref[0, 2 * p + 1]], axis=0)
                ht_ref[:, p * LANES:(p + 1) * LANES] = both.T
        else:
            xpad[:, 0:8, :] = jnp.zeros((CONV_DIM // LANES, 8, LANES), F32)
            ht_ref[...] = jnp.zeros(ht_ref.shape, F32)

    n_col = CONV_DIM // LANES
    for cidx in range(n_col):
        xpad[cidx, 8:8 + L, :] = xbc_ref[:, cidx * LANES:(cidx + 1) * LANES]
    for cidx in range(n_col):
        sl = slice(cidx * LANES, (cidx + 1) * LANES)
        acc = cb_ref[:, sl] + xpad[cidx, 8:8 + L, :] * cw_ref[3:4, sl]
        acc = acc + xpad[cidx, 7:7 + L, :] * cw_ref[2:3, sl]
        acc = acc + xpad[cidx, 6:6 + L, :] * cw_ref[1:2, sl]
        acc = acc + xpad[cidx, 5:5 + L, :] * cw_ref[0:1, sl]
        u_ref[:, sl] = _silu(acc)
    for cidx in range(n_col):
        xpad[cidx, 5:8, :] = xpad[cidx, 5 + L:8 + L, :]

    dtx = dt_ref[...] + dtb_ref[...]
    dt = jnp.maximum(dtx, 0.0) + jnp.log(1.0 + jnp.exp(-jnp.abs(dtx)))
    loga = dt * (-jnp.exp(alog_ref[...]))
    t_idx = lax.broadcasted_iota(I32, (L, L), 0)
    s_idx = lax.broadcasted_iota(I32, (L, L), 1)
    incl = (s_idx <= t_idx).astype(BF16)
    acum = sum(_dot(incl, part) for part in _split3(loga))
    acum_t = _pad_rows(acum, LANES).T[:, 0:L]

    lane_m = lax.broadcasted_iota(I32, (L, 2 * L), 1)
    row_m = lax.broadcasted_iota(I32, (L, 2 * L), 0)
    left_m = lane_m < L
    causal_m = jnp.where(left_m, lane_m, lane_m - L) <= row_m
    left = lax.broadcasted_iota(I32, (L, LANES), 1) < SSM_HEAD_DIM
    b_of = lambda grp: u_ref[:, D_INNER + grp * D_STATE:D_INNER + (grp + 1) * D_STATE]
    c_of = lambda grp: u_ref[:, D_INNER + BC_WIDTH + grp * D_STATE:D_INNER + BC_WIDTH + (grp + 1) * D_STATE]
    pair_lanes = lambda p: slice(p * LANES, (p + 1) * LANES)
    group_lanes = lambda grp: slice(grp * gw, (grp + 1) * gw)

    for p in range(n_pairs):
        h0, h1, sl = 2 * p, 2 * p + 1, pair_lanes(p)
        ca_ref[:, sl] = jnp.where(left, acum[:, h0:h0 + 1], acum[:, h1:h1 + 1])
        xdt_ref[:, sl] = u_ref[:, sl] * jnp.where(left, dt[:, h0:h0 + 1], dt[:, h1:h1 + 1])

    for grp in range(SSM_GROUPS):
        c_bf = c_of(grp).astype(BF16)
        cb = _dot_nt(c_bf, b_of(grp).astype(BF16))
        cb2_ref[grp] = jnp.concatenate([cb, cb], axis=1)
        yst_ref[:, group_lanes(grp)] = _dot(c_bf, ht_ref[:, group_lanes(grp)].astype(BF16))

    for p in range(n_pairs):
        h0, h1, sl = 2 * p, 2 * p + 1, pair_lanes(p)
        ca = ca_ref[:, sl]
        ca_m = ca if L == SSM_HEAD_DIM else jnp.where(left_m, acum[:, h0:h0 + 1], acum[:, h1:h1 + 1])
        row_a = jnp.concatenate([acum_t[h0:h0 + 1, :], acum_t[h1:h1 + 1, :]], axis=1)
        dec = jnp.exp(jnp.where(causal_m, ca_m - row_a, NEG_INF))
        m_pair = (cb2_ref[p // pairs_per_group] * dec).astype(BF16)
        xdt = xdt_ref[:, sl]
        x_bd = jnp.concatenate([jnp.where(left, xdt, 0.0), jnp.where(left, 0.0, xdt)], axis=0).astype(BF16)
        y = _dot(m_pair, x_bd) + yst_ref[:, sl] * jnp.exp(ca) + u_ref[:, sl] * dskip_ref[:, sl]
        g_ref[:, sl] = y * _silu(z_ref[:, sl].astype(F32))
        xw_ref[:, sl] = (xdt * jnp.exp(ca[L - 1:L, :] - ca)).astype(BF16)

    for grp in range(SSM_GROUPS):
        gsl = group_lanes(grp)
        gg = g_ref[:, gsl]
        ms = jnp.mean(gg * gg, axis=-1, keepdims=True)
        y_ref[:, gsl] = (gg * lax.rsqrt(ms + EPS) * nw_ref[:, gsl]).astype(y_ref.dtype)
        bt_bf = _pad_rows(b_of(grp), LANES).T[:, 0:L].astype(BF16)
        ht_ref[:, gsl] = ht_ref[:, gsl] * jnp.exp(ca_ref[L - 1:L, gsl]) + _dot(bt_bf, xw_ref[:, gsl])

    @pl.when(c == nc - 1)
    def _fin():
        for p in range(n_pairs):
            both = ht_ref[:, p * LANES:(p + 1) * LANES].T
            hout_ref[0, 2 * p] = both[0:SSM_HEAD_DIM, :]
            hout_ref[0, 2 * p + 1] = both[SSM_HEAD_DIM:2 * SSM_HEAD_DIM, :]


def _ssd(xbc, dt_raw, z, conv_w, conv_b, dt_bias, a_log, d_skip, ssm_norm, row0, bsz, seq, clen,
         conv_past=None, ssm_past=None):
    nc = seq // clen
    blk0 = row0 // clen
    has_past = conv_past is not None
    pad = lambda a: jnp.pad(a.astype(F32), (0, LANES - SSM_HEADS)).reshape(1, LANES)
    params = [conv_w.astype(F32), conv_b.astype(F32).reshape(1, CONV_DIM), pad(dt_bias), pad(a_log),
              jnp.repeat(d_skip.astype(F32), SSM_HEAD_DIM).reshape(1, D_INNER), ssm_norm.astype(F32).reshape(1, D_INNER)]
    rows = lambda w: pl.BlockSpec((clen, w), lambda b, c: (blk0 + b * nc + c, 0))
    full = lambda a: pl.BlockSpec(a.shape, lambda b, c: (0,) * a.ndim)
    in_specs = [rows(CONV_DIM), rows(LANES), rows(D_INNER)] + [full(a) for a in params]
    args = [xbc, dt_raw, z] + params
    if has_past:
        in_specs += [pl.BlockSpec((1, CONV_W - 1, CONV_DIM), lambda b, c: (b, 0, 0)),
                     pl.BlockSpec((1, SSM_HEADS, SSM_HEAD_DIM, D_STATE), lambda b, c: (b, 0, 0, 0))]
        args += [conv_past.astype(F32), ssm_past.astype(F32)]
    return pl.pallas_call(
        functools.partial(_ssd_kernel, clen=clen, has_past=has_past),
        grid=(bsz, nc),
        in_specs=in_specs,
        out_specs=[pl.BlockSpec((clen, D_INNER), lambda b, c: (b * nc + c, 0)),
                   pl.BlockSpec((1, SSM_HEADS, SSM_HEAD_DIM, D_STATE), lambda b, c: (b, 0, 0, 0))],
        out_shape=[jax.ShapeDtypeStruct((bsz * seq, D_INNER), BF16),
                   jax.ShapeDtypeStruct((bsz, SSM_HEADS, SSM_HEAD_DIM, D_STATE), F32)],
        scratch_shapes=[pltpu.VMEM((CONV_DIM // LANES, 8 + clen, LANES), F32), pltpu.VMEM((clen, CONV_DIM), F32),
                        pltpu.VMEM((clen, D_INNER), F32), pltpu.VMEM((D_STATE, D_INNER), F32),
                        pltpu.VMEM((clen, D_INNER), F32), pltpu.VMEM((clen, D_INNER), F32),
                        pltpu.VMEM((clen, D_INNER), F32), pltpu.VMEM((clen, D_INNER), BF16),
                        pltpu.VMEM((SSM_GROUPS, clen, 2 * clen), F32)],
        compiler_params=_cparams(("parallel", "arbitrary")),
        name="ssd_past" if has_past else "ssd_prompt",
    )(*args)


def _merge_kernel(h_ref, a1_ref, a2_ref, s1_ref, s2_ref, wga_ref, wgs_ref, bga_ref, bgs_ref, wa_ref, ws_ref, o_ref,
                  *, n_first):
    def body(a_ref, s_ref):
        h = h_ref[...]
        g_a = jax.nn.sigmoid(_dot(h, wga_ref[...]) + bga_ref[...])
        g_s = jax.nn.sigmoid(_dot(h, wgs_ref[...]) + bgs_ref[...])
        o_ref[...] = (g_a * _dot(a_ref[...], wa_ref[...]) + g_s * _dot(s_ref[...], ws_ref[...])).astype(o_ref.dtype)

    _by_part(pl.program_id(0), n_first, body, (a1_ref, s1_ref), (a2_ref, s2_ref))


def _merge(h, attn_parts, ssm_parts, w_gate, b_gate, w_attn_up, w_ssm_up):
    m = h.shape[0]
    m1, m2 = attn_parts[0].shape[0], attn_parts[1].shape[0]
    tm = _pick(math.gcd(m1, m2), (512, 256, 128, 64, 32))
    nf = m1 // tm
    tn = 512
    nj = D_MODEL // tn
    bg = b_gate.astype(F32).reshape(1, 2 * D_MODEL)
    return pl.pallas_call(
        functools.partial(_merge_kernel, n_first=nf),
        grid=(m // tm, nj),
        in_specs=[pl.BlockSpec((tm, D_MODEL), lambda i, j: (i, 0)),
                  pl.BlockSpec((tm, ATTN_WIDTH), _first_rows(nf)), pl.BlockSpec((tm, ATTN_WIDTH), _later_rows(nf)),
                  pl.BlockSpec((tm, D_INNER), _first_rows(nf)), pl.BlockSpec((tm, D_INNER), _later_rows(nf)),
                  pl.BlockSpec((D_MODEL, tn), lambda i, j: (0, j)),
                  pl.BlockSpec((D_MODEL, tn), lambda i, j: (0, j + nj)),
                  pl.BlockSpec((1, tn), lambda i, j: (0, j)),
                  pl.BlockSpec((1, tn), lambda i, j: (0, j + nj)),
                  pl.BlockSpec((ATTN_WIDTH, tn), lambda i, j: (0, j)),
                  pl.BlockSpec((D_INNER, tn), lambda i, j: (0, j))],
        out_specs=pl.BlockSpec((tm, tn), lambda i, j: (i, j)),
        out_shape=jax.ShapeDtypeStruct((m, D_MODEL), BF16),
        compiler_params=_cparams(("parallel", "parallel")),
        name="merge",
    )(h, attn_parts[0], attn_parts[1], ssm_parts[0], ssm_parts[1], w_gate, w_gate, bg, bg, w_attn_up, w_ssm_up)


def _outproj_router_kernel(mg_ref, wo_ref, xa_ref, xb_ref, n2_ref, wr_ref, br_ref, x1_ref, h2_ref, idx_ref, gate_ref,
                           *, n_first):
    def residual(x_ref):
        x1_ref[...] = x_ref[...] + _dot(mg_ref[...], wo_ref[...])

    _by_part(pl.program_id(0), n_first, residual, (xa_ref,), (xb_ref,))
    x1 = x1_ref[...]
    ms = jnp.mean(x1 * x1, axis=-1, keepdims=True)
    h2 = x1 * lax.rsqrt(ms + EPS) * n2_ref[...]
    h2_ref[...] = _pack_bf16_pairs(h2)
    h_hi, h_lo = _split2(h2)
    w_hi, w_lo = _split2(wr_ref[...])
    logits = _dot(h_hi, w_hi) + (_dot(h_hi, w_lo) + _dot(h_lo, w_hi)) + br_ref[...]
    tm = logits.shape[0]
    lane = lax.broadcasted_iota(I32, (tm, LANES), 1)
    logits = jnp.where(lane < N_EXPERTS, logits, -jnp.inf)
    idx_out = jnp.zeros((tm, LANES), I32)
    val_out = jnp.zeros((tm, LANES), F32)
    top = None
    for k in range(TOP_K):
        v = jnp.max(logits, axis=-1, keepdims=True)
        i = jnp.min(jnp.where(logits == v, lane, LANES), axis=-1, keepdims=True)
        if k == 0:
            top = v
        idx_out = jnp.where(lane == k, i, idx_out)
        val_out = jnp.where(lane == k, jnp.exp(v - top), val_out)
        logits = jnp.where(lane == i, -jnp.inf, logits)
    idx_ref[...] = idx_out
    gate_ref[...] = val_out / jnp.sum(val_out, axis=-1, keepdims=True)


def _outproj_router(merged, w_out, xa, xb, norm2, w_router, b_router):
    m = merged.shape[0]
    tm = _pick(math.gcd(xa.shape[0], xb.shape[0]), (256, 128, 64, 32))
    nf = xa.shape[0] // tm
    wr = jnp.pad(w_router.astype(F32), ((0, 0), (0, LANES - N_EXPERTS)))
    br = jnp.pad(b_router.astype(F32), (0, LANES - N_EXPERTS)).reshape(1, LANES)
    row = lambda w: pl.BlockSpec((tm, w), lambda i: (i, 0))
    full = lambda r, c: pl.BlockSpec((r, c), lambda i: (0, 0))
    return pl.pallas_call(
        functools.partial(_outproj_router_kernel, n_first=nf),
        grid=(m // tm,),
        in_specs=[row(D_MODEL), full(D_MODEL, D_MODEL),
                  pl.BlockSpec((tm, D_MODEL), _first_rows(nf)), pl.BlockSpec((tm, D_MODEL), _later_rows(nf)),
                  full(1, D_MODEL), full(D_MODEL, LANES), full(1, LANES)],
        out_specs=[row(D_MODEL), row(D_MODEL // 2), row(LANES), row(LANES)],
        out_shape=[jax.ShapeDtypeStruct((m, D_MODEL), F32), jax.ShapeDtypeStruct((m, D_MODEL // 2), I32),
                   jax.ShapeDtypeStruct((m, LANES), I32), jax.ShapeDtypeStruct((m, LANES), F32)],
        compiler_params=_cparams(("parallel",)),
        name="outproj_router",
    )(merged, w_out, xa, xb, norm2.astype(F32).reshape(1, D_MODEL), wr, br)


def _rank_kernel(idx_ref, rank_ref, cnt_ref, base_ref):
    i = pl.program_id(0)
    tt = idx_ref.shape[0]

    @pl.when(i == 0)
    def _():
        base_ref[...] = jnp.zeros(base_ref.shape, F32)

    idx = idx_ref[...]
    lane = lax.broadcasted_iota(I32, (tt, LANES), 1)
    sel = [lane == idx[:, k:k + 1] for k in range(TOP_K)]
    onehot = jnp.zeros((tt, LANES), F32)
    for k in range(TOP_K):
        onehot = jnp.where(sel[k], 1.0, onehot)
    r_idx = lax.broadcasted_iota(I32, (tt, tt), 0)
    c_idx = lax.broadcasted_iota(I32, (tt, tt), 1)
    before = (c_idx < r_idx).astype(BF16)
    rank_all = _dot(before, onehot.astype(BF16)) + base_ref[0:1, :]
    out = jnp.zeros((tt, LANES), F32)
    for k in range(TOP_K):
        out = jnp.where(lane == k, jnp.sum(jnp.where(sel[k], rank_all, 0.0), axis=-1, keepdims=True), out)
    rank_ref[...] = out.astype(I32)
    base_ref[0:1, :] = base_ref[0:1, :] + jnp.sum(onehot, axis=0, keepdims=True)
    cnt_ref[...] = base_ref[...].astype(I32)


def _ranks(idx):
    m = idx.shape[0]
    tt = _pick(m, (512, 256, 128, 64, 32))
    return pl.pallas_call(
        _rank_kernel,
        grid=(m // tt,),
        in_specs=[pl.BlockSpec((tt, LANES), lambda i: (i, 0))],
        out_specs=[pl.BlockSpec((tt, LANES), lambda i: (i, 0)), pl.BlockSpec((8, LANES), lambda i: (0, 0))],
        out_shape=[jax.ShapeDtypeStruct((m, LANES), I32), jax.ShapeDtypeStruct((8, LANES), I32)],
        scratch_shapes=[pltpu.VMEM((8, LANES), F32)],
        compiler_params=_cparams(("arbitrary",)),
        name="expert_ranks",
    )(idx)


def _dispatch_kernel(dest_ref, fill_ref, h_ref, xs_hbm, zero_ref, sem, zsem, *, tt, tm, n_blocks):
    @pl.when(pl.program_id(0) == 0)
    def _():
        zero_ref[...] = jnp.zeros(zero_ref.shape, zero_ref.dtype)

        def block_copy(b):
            return pltpu.make_async_copy(zero_ref, xs_hbm.at[pl.ds(pl.multiple_of(b * tm, tm), tm), :], zsem)

        def start(b, carry):
            @pl.when(fill_ref[b] != 0)
            def _():
                block_copy(b).start()

            return carry

        def done(b, carry):
            @pl.when(fill_ref[b] != 0)
            def _():
                block_copy(b).wait()

            return carry

        lax.fori_loop(0, n_blocks, start, 0)
        lax.fori_loop(0, n_blocks, done, 0)

    base = pl.program_id(0) * (tt * TOP_K)

    def issue(r, carry):
        for k in range(TOP_K):
            dst = dest_ref[base + r * TOP_K + k]
            pltpu.make_async_copy(h_ref.at[pl.ds(r, 1), :], xs_hbm.at[pl.ds(dst, 1), :], sem).start()
        return carry

    lax.fori_loop(0, tt, issue, 0, unroll=2)
    for _ in range(TOP_K):
        pltpu.make_async_copy(h_ref, xs_hbm.at[pl.ds(0, tt), :], sem).wait()


def _dispatch(dest_flat, fill_block, h2, tm):
    m = h2.shape[0]
    n_blocks = fill_block.shape[0]
    tt = _pick(m, (256, 128, 64, 32))
    width = h2.shape[1]
    grid_spec = pltpu.PrefetchScalarGridSpec(
        num_scalar_prefetch=2,
        grid=(m // tt,),
        in_specs=[pl.BlockSpec((tt, width), lambda i, d, f: (i, 0))],
        out_specs=pl.BlockSpec(memory_space=pl.ANY),
        scratch_shapes=[pltpu.VMEM((tm, width), h2.dtype), pltpu.SemaphoreType.DMA, pltpu.SemaphoreType.DMA],
    )
    return pl.pallas_call(
        functools.partial(_dispatch_kernel, tt=tt, tm=tm, n_blocks=n_blocks),
        grid_spec=grid_spec,
        out_shape=jax.ShapeDtypeStruct((n_blocks * tm, width), h2.dtype),
        compiler_params=_cparams(("arbitrary",)),
        name="dispatch",
    )(dest_flat, fill_block, h2)


def _pack_bf16_pairs(x):
    c = x.shape[1] // 2
    bits = lax.bitcast_convert_type(x.astype(BF16).astype(F32), I32)
    return lax.shift_right_logical(bits[:, :c], 16) | bits[:, c:]


def _unpack_bf16_pairs(p):
    lo = lax.bitcast_convert_type(lax.shift_left(p, 16), F32).astype(BF16)
    hi = lax.bitcast_convert_type(p & jnp.int32(-65536), F32).astype(BF16)
    return lo, hi


def _expert_kernel(be_ref, nu_ref, xs_ref, bg_ref, bu_ref, bd_ref, wg_hbm, wu_hbm, wd_hbm, ys_ref,
                   cg, cu, cd, stg_g, stg_u, stg_d, xb_ref, sems, *, tf):
    i = pl.program_id(0)
    nj = D_FF // tf
    half = D_MODEL // 2
    e = be_ref[i]
    first = jnp.logical_or(i == 0, e != be_ref[jnp.maximum(i - 1, 0)])

    def tile_copies(t, slot):
        cols = pl.ds(pl.multiple_of(t * tf, tf), tf)
        return (pltpu.make_async_copy(wg_hbm.at[e, :, cols], stg_g.at[slot], sems.at[slot, 0]),
                pltpu.make_async_copy(wu_hbm.at[e, :, cols], stg_u.at[slot], sems.at[slot, 1]),
                pltpu.make_async_copy(wd_hbm.at[e, cols, :], stg_d.at[slot], sems.at[slot, 2]))

    @pl.when(i < nu_ref[0])
    def _():
        @pl.when(first)
        def _():
            for cp in tile_copies(0, 0):
                cp.start()

        lo, hi = _unpack_bf16_pairs(xs_ref[...])
        xb_ref[:, :half] = lo
        xb_ref[:, half:] = hi

        def refill(t):
            slot = t % 2

            @pl.when(t + 1 < nj)
            def _():
                for cp in tile_copies(t + 1, 1 - slot):
                    cp.start()

            for cp in tile_copies(t, slot):
                cp.wait()
            cg[t] = stg_g[slot].astype(BF16)
            cu[t] = stg_u[slot].astype(BF16)
            cd[t] = stg_d[slot].astype(BF16)

        def compute(t, opening):
            xb = xb_ref[...]
            g = jnp.minimum(_dot(xb, cg[t]) + bg_ref[0, t], SWIGLU_LIMIT)
            u = jnp.clip(_dot(xb, cu[t]) + bu_ref[0, t], -SWIGLU_LIMIT, SWIGLU_LIMIT)
            act = (u + 1.0) * (g * jax.nn.sigmoid(SWIGLU_ALPHA * g))
            part = _dot(act.astype(BF16), cd[t])
            if opening:
                ys_ref[...] = part + bd_ref[0]
            else:
                ys_ref[...] += part

        @pl.when(first)
        def _():
            refill(jnp.int32(0))
            compute(0, True)

            def step(t, carry):
                refill(t)
                compute(t, False)
                return carry

            lax.fori_loop(1, nj, step, 0)

        @pl.when(jnp.logical_not(first))
        def _():
            for t in range(nj):
                compute(t, t == 0)

    @pl.when(i >= nu_ref[0])
    def _():
        ys_ref[...] = jnp.zeros(ys_ref.shape, F32)


def _experts(block_e, n_used, xs, w_e_gate, b_e_gate, w_e_up, b_e_up, w_e_down, b_e_down):
    n_rows = xs.shape[0]
    tm, tf = MOE_TM, MOE_TF
    n_blocks = n_rows // tm
    nj = D_FF // tf
    half = D_MODEL // 2

    def blk(i, nu):
        return jnp.minimum(i, nu[0] - 1)

    hbm = pl.BlockSpec(memory_space=pl.ANY)
    grid_spec = pltpu.PrefetchScalarGridSpec(
        num_scalar_prefetch=2,
        grid=(n_blocks,),
        in_specs=[pl.BlockSpec((tm, half), lambda i, be, nu: (blk(i, nu), 0)),
                  pl.BlockSpec((1, nj, 1, tf), lambda i, be, nu: (be[blk(i, nu)], 0, 0, 0)),
                  pl.BlockSpec((1, nj, 1, tf), lambda i, be, nu: (be[blk(i, nu)], 0, 0, 0)),
                  pl.BlockSpec((1, 1, D_MODEL), lambda i, be, nu: (be[blk(i, nu)], 0, 0)),
                  hbm, hbm, hbm],
        out_specs=pl.BlockSpec((tm, D_MODEL), lambda i, be, nu: (i, 0)),
        scratch_shapes=[pltpu.VMEM((nj, D_MODEL, tf), BF16), pltpu.VMEM((nj, D_MODEL, tf), BF16),
                        pltpu.VMEM((nj, tf, D_MODEL), BF16),
                        pltpu.VMEM((2, D_MODEL, tf), F32), pltpu.VMEM((2, D_MODEL, tf), F32),
                        pltpu.VMEM((2, tf, D_MODEL), F32),
                        pltpu.VMEM((tm, D_MODEL), BF16), pltpu.SemaphoreType.DMA((2, 3))],
    )
    return pl.pallas_call(
        functools.partial(_expert_kernel, tf=tf),
        grid_spec=grid_spec,
        out_shape=jax.ShapeDtypeStruct((n_rows, D_MODEL), F32),
        compiler_params=pltpu.CompilerParams(dimension_semantics=("arbitrary",), vmem_limit_bytes=EXPERT_VMEM_LIMIT),
        name="experts",
    )(block_e, n_used, xs, b_e_gate.reshape(N_EXPERTS, nj, 1, tf), b_e_up.reshape(N_EXPERTS, nj, 1, tf),
      b_e_down.reshape(N_EXPERTS, 1, D_MODEL), w_e_gate, w_e_up, w_e_down)


def _combine_kernel(dest_ref, x1_ref, gate_ref, ys_hbm, o_ref, buf, sems, *, tt, row0):
    i = pl.program_id(0)
    n_steps = pl.num_programs(0)
    slot = i % 2

    def gather(step, to_slot):
        base = (row0 + step * tt) * TOP_K

        def issue(r, carry):
            for k in range(TOP_K):
                src = dest_ref[base + r * TOP_K + k]
                pltpu.make_async_copy(ys_hbm.at[pl.ds(src, 1), :], buf.at[to_slot, k, pl.ds(r, 1), :],
                                      sems.at[to_slot]).start()
            return carry

        lax.fori_loop(0, tt, issue, 0, unroll=2)

    @pl.when(i == 0)
    def _():
        gather(0, 0)

    @pl.when(i + 1 < n_steps)
    def _():
        gather(i + 1, 1 - slot)

    for k in range(TOP_K):
        pltpu.make_async_copy(ys_hbm.at[pl.ds(0, tt), :], buf.at[slot, k], sems.at[slot]).wait()
    gate = gate_ref[...]
    acc = x1_ref[...]
    for k in range(TOP_K):
        acc = acc + gate[:, k:k + 1] * buf[slot, k]
    o_ref[...] = acc


def _combine(dest_flat, x1, gate, ys, row0, rows):
    tt = _pick(math.gcd(row0, rows), (128, 64, 32))
    blk0 = row0 // tt
    grid_spec = pltpu.PrefetchScalarGridSpec(
        num_scalar_prefetch=1,
        grid=(rows // tt,),
        in_specs=[pl.BlockSpec((tt, D_MODEL), lambda i, d: (blk0 + i, 0)),
                  pl.BlockSpec((tt, LANES), lambda i, d: (blk0 + i, 0)),
                  pl.BlockSpec(memory_space=pl.ANY)],
        out_specs=pl.BlockSpec((tt, D_MODEL), lambda i, d: (i, 0)),
        scratch_shapes=[pltpu.VMEM((2, TOP_K, tt, D_MODEL), F32), pltpu.SemaphoreType.DMA((2,))],
    )
    return pl.pallas_call(
        functools.partial(_combine_kernel, tt=tt, row0=row0),
        grid_spec=grid_spec,
        out_shape=jax.ShapeDtypeStruct((rows, D_MODEL), F32),
        compiler_params=_cparams(("arbitrary",)),
        name="combine",
    )(dest_flat, x1, gate, ys)


def _moe(x1, h2, idx, gate, w_e_gate, b_e_gate, w_e_up, b_e_up, w_e_down, b_e_down, n_first):
    m = x1.shape[0]
    tm = MOE_TM
    ranks, cnt = _ranks(idx)
    counts = cnt[0, :N_EXPERTS]
    padded = (counts + tm - 1) // tm * tm
    pad_end = jnp.cumsum(padded)
    pad_start = pad_end - padded
    e_sel = idx[:, :TOP_K]
    dest = (pad_start[e_sel] + ranks[:, :TOP_K]).astype(I32).reshape(-1)
    n_blocks = -(-(m * TOP_K + N_EXPERTS * (tm - 1)) // tm)
    block_row = jnp.arange(n_blocks, dtype=I32) * tm
    block_e = jnp.minimum(jnp.sum((pad_end[None, :] <= block_row[:, None]).astype(I32), axis=1), N_EXPERTS - 1)
    n_used = (pad_end[-1] // tm).astype(I32).reshape(1)
    is_last = jnp.sum((pad_end[None, :] == block_row[:, None] + tm).astype(I32), axis=1) > 0
    fill_block = jnp.logical_or(is_last, block_row >= pad_end[-1]).astype(I32)
    xs = _dispatch(dest, fill_block, h2, tm)
    ys = _experts(block_e, n_used, xs, w_e_gate, b_e_gate, w_e_up, b_e_up, w_e_down, b_e_down)
    return _combine(dest, x1, gate, ys, 0, n_first), _combine(dest, x1, gate, ys, n_first, m - n_first)


def kernel(x_prompt, x_sample, cache_k, cache_v, state_conv, state_ssm, norm1, w_in, q_norm, k_norm, sinks, conv_w,
           conv_b, dt_bias, a_log, d_skip, ssm_norm, w_gate, b_gate, w_attn_up, w_ssm_up, w_out, norm2, w_router,
           b_router, w_e_gate, b_e_gate, w_e_up, b_e_up, w_e_down, b_e_down):
    bp, sp, _ = x_prompt.shape
    bs, ss, _ = x_sample.shape
    tp, ts = bp * sp, bs * ss
    kv_len = cache_k.shape[2]
    l = 0

    xa, xb = x_prompt.reshape(tp, D_MODEL), x_sample.reshape(ts, D_MODEL)
    pos = jnp.concatenate([jnp.tile(jnp.arange(sp, dtype=I32), bp), jnp.tile(PAST_LEN + jnp.arange(ss, dtype=I32), bs)])

    o1 = ATTN_WIDTH
    o2 = o1 + 2 * KV_WIDTH
    o3 = o2 + D_INNER
    o4 = o3 + CONV_DIM
    h = _rmsnorm(xa, xb, norm1[l], BF16)
    w_in_t = jnp.swapaxes(w_in, 1, 2)
    q_p = _matmul(h, w_in_t, l, 0, o1, F32, "proj_q")
    kv_p = _matmul(h, w_in_t, l, o1, o2 - o1, F32, "proj_kv")
    z = _matmul(h, w_in_t, l, o2, o3 - o2, BF16, "proj_z")
    xbc = _matmul(h, w_in_t, l, o3, o4 - o3, F32, "proj_xbc")
    w_dt = jnp.pad(w_in_t[l:l + 1, o4:, :], ((0, 0), (0, LANES - SSM_HEADS), (0, 0)))
    dt_raw = _matmul(h, w_dt, 0, 0, LANES, F32, "proj_dt")

    q_rot, k_rot = _qk_prep(q_p, kv_p, pos, q_norm[l], k_norm[l])
    sk = sinks[l].astype(F32)
    attn_p = _attn_prompt(q_rot, k_rot, kv_p, sk, bp, sp)
    ck = cache_k[l].reshape(bs, kv_len, KV_WIDTH)
    cv = cache_v[l].reshape(bs, kv_len, KV_WIDTH)
    attn_s = _attn_sample(q_rot, k_rot, kv_p, ck, cv, sk, tp, bs, ss)

    ssd_w = (conv_w[l], conv_b[l], dt_bias[l], a_log[l], d_skip[l], ssm_norm[l])
    ssm_p, hfin_p = _ssd(xbc, dt_raw, z, *ssd_w, 0, bp, sp, CHUNK)
    ssm_s, hfin_s = _ssd(xbc, dt_raw, z, *ssd_w, tp, bs, ss, ss, state_conv[l], state_ssm[l])

    merged = _merge(h, (attn_p, attn_s), (ssm_p, ssm_s), w_gate[l].astype(BF16), b_gate[l], w_attn_up[l].astype(BF16), w_ssm_up[l].astype(BF16))
    x1, h2, idx, gate = _outproj_router(merged, w_out[l].astype(BF16), xa, xb, norm2[l], w_router[l], b_router[l])
    y_p, y_s = _moe(x1, h2, idx, gate, w_e_gate[l], b_e_gate[l], w_e_up[l], b_e_up[l], w_e_down[l], b_e_down[l], tp)

    def prompt_tail(a, n):
        return jnp.stack([a[(b + 1) * sp - n:(b + 1) * sp] for b in range(bp)])

    heads = lambda a: a.reshape(a.shape[0], a.shape[1], N_KV_HEADS, HEAD_DIM)
    new_k_p = heads(prompt_tail(k_rot, WINDOW))
    new_v_p = heads(prompt_tail(kv_p, WINDOW)[:, :, KV_WIDTH:])
    ks4 = heads(k_rot[tp:].reshape(bs, ss, KV_WIDTH))
    vs4 = heads(kv_p[tp:, KV_WIDTH:].reshape(bs, ss, KV_WIDTH))
    new_k_s = jnp.concatenate([cache_k[l], ks4], axis=1)[:, -kv_len:]
    new_v_s = jnp.concatenate([cache_v[l], vs4], axis=1)[:, -kv_len:]
    xbc_s = jnp.concatenate([state_conv[l], xbc[tp:].reshape(bs, ss, CONV_DIM)], axis=1)
    return (y_p.reshape(bp, sp, D_MODEL), y_s.reshape(bs, ss, D_MODEL),
            new_k_p[None], new_v_p[None], prompt_tail(xbc, CONV_W - 1)[None], hfin_p[None],
            new_k_s[None], new_v_s[None], xbc_s[:, -(CONV_W - 1):][None], hfin_s[None])
```

```python
import functools
import math

import jax
import jax.numpy as jnp
from jax import lax
from jax.experimental import pallas as pl
from jax.experimental.pallas import tpu as pltpu

F32 = jnp.float32
BF16 = jnp.bfloat16
I32 = jnp.int32

D_MODEL = 2048
CHUNK = 64
N_HEADS = 32
N_KV_HEADS = 8
HEAD_DIM = 64
Q_PER_KV = N_HEADS // N_KV_HEADS
ATTN_WIDTH = N_HEADS * HEAD_DIM
KV_WIDTH = N_KV_HEADS * HEAD_DIM
WINDOW = 128
N_PREV_CHUNKS = WINDOW // CHUNK
ROPE_THETA = 500000.0
ROT_DIM = HEAD_DIM // 4
D_INNER = 2 * D_MODEL
SSM_HEAD_DIM = 64
SSM_HEADS = D_INNER // SSM_HEAD_DIM
SSM_GROUPS = 8
D_STATE = 128
CONV_W = 4
BC_WIDTH = SSM_GROUPS * D_STATE
CONV_DIM = D_INNER + 2 * BC_WIDTH
N_EXPERTS = 32
TOP_K = 4
D_FF = D_MODEL
SWIGLU_LIMIT = 7.0
SWIGLU_ALPHA = 1.702
EPS = 1e-6
NEG_INF = -1e30
PAST_LEN = 2048

LANES = 128
VMEM_LIMIT = 56 * 1024 * 1024
EXPERT_VMEM_LIMIT = 60 * 1024 * 1024
MOE_TM = 512
MOE_TF = 256


def _pick(n, cands):
    for c in cands:
        if n % c == 0:
            return c
    return n


def _cparams(sem):
    return pltpu.CompilerParams(dimension_semantics=sem, vmem_limit_bytes=VMEM_LIMIT)


def _split2(x):
    hi = x.astype(BF16)
    lo = (x - hi.astype(F32)).astype(BF16)
    return hi, lo


def _split3(x):
    hi = x.astype(BF16)
    r = x - hi.astype(F32)
    mid = r.astype(BF16)
    lo = (r - mid.astype(F32)).astype(BF16)
    return hi, mid, lo


def _dot(a, b):
    return jnp.dot(a, b, preferred_element_type=F32)


def _dot_nt(a, b):
    return lax.dot_general(a, b, (((1,), (1,)), ((), ())), preferred_element_type=F32)


def _silu(x):
    half = 0.5 * x
    return half + half * jnp.tanh(half)


def _first_rows(n_first):
    return lambda i, *_: (jnp.minimum(i, n_first - 1), 0)


def _later_rows(n_first):
    return lambda i, *_: (jnp.maximum(i - n_first, 0), 0)


def _by_part(i, n_first, fn, first_refs, later_refs):
    @pl.when(i < n_first)
    def _():
        fn(*first_refs)

    @pl.when(i >= n_first)
    def _():
        fn(*later_refs)


def _rms_kernel(xa_ref, xb_ref, w_ref, o_ref, *, n_first):
    def body(x_ref):
        x = x_ref[...]
        ms = jnp.mean(x * x, axis=-1, keepdims=True)
        o_ref[...] = (x * lax.rsqrt(ms + EPS) * w_ref[...]).astype(o_ref.dtype)

    _by_part(pl.program_id(0), n_first, body, (xa_ref,), (xb_ref,))


def _rmsnorm(xa, xb, w, out_dtype):
    (ma, d), mb = xa.shape, xb.shape[0]
    tm = _pick(math.gcd(ma, mb), (512, 256, 128, 64, 32))
    nf = ma // tm
    return pl.pallas_call(
        functools.partial(_rms_kernel, n_first=nf),
        grid=((ma + mb) // tm,),
        in_specs=[pl.BlockSpec((tm, d), _first_rows(nf)), pl.BlockSpec((tm, d), _later_rows(nf)),
                  pl.BlockSpec((1, d), lambda i: (0, 0))],
        out_specs=pl.BlockSpec((tm, d), lambda i: (i, 0)),
        out_shape=jax.ShapeDtypeStruct((ma + mb, d), out_dtype),
        compiler_params=_cparams(("parallel",)),
        name="rmsnorm",
    )(xa, xb, w.reshape(1, d))


def _mm_kernel(x_ref, w_ref, o_ref, wb_ref):
    @pl.when(pl.program_id(1) == 0)
    def _():
        wb_ref[...] = w_ref[0].astype(BF16)

    o_ref[...] = _dot_nt(x_ref[...], wb_ref[...]).astype(o_ref.dtype)


def _matmul(x, wt, layer, row0, n, out_dtype, name):
    m, k = x.shape
    tm = _pick(m, (1024, 512, 256, 128, 64, 32))
    tn = _pick(math.gcd(n, row0), (1024, 512, 256, 128))
    rb0 = row0 // tn
    return pl.pallas_call(
        _mm_kernel,
        grid=(n // tn, m // tm),
        in_specs=[pl.BlockSpec((tm, k), lambda j, i: (i, 0)),
                  pl.BlockSpec((1, tn, k), lambda j, i: (layer, rb0 + j, 0))],
        out_specs=pl.BlockSpec((tm, tn), lambda j, i: (i, j)),
        out_shape=jax.ShapeDtypeStruct((m, n), out_dtype),
        scratch_shapes=[pltpu.VMEM((tn, k), BF16)],
        compiler_params=_cparams(("parallel", "arbitrary")),
        name=name,
    )(x, wt)


def _qk_prep_kernel(q_ref, k_ref, cos_ref, s1_ref, s2_ref, qn_ref, kn_ref, g_ref, gt_ref, qo_ref, ko_ref):
    cos = cos_ref[...]
    s1 = s1_ref[...]
    s2 = s2_ref[...]

    def norm_rope(x, nw, width):
        g = g_ref[0:width, :]
        gt = gt_ref[:, 0:width]
        sq_hi, sq_lo = _split2(x * x)
        ssum = _dot(sq_hi, g) + _dot(sq_lo, g)
        r = lax.rsqrt(ssum * (1.0 / HEAD_DIM) + EPS)
        r_hi, r_lo = _split2(r)
        y = x * (_dot(r_hi, gt) + _dot(r_lo, gt)) * nw
        outs = []
        for s in range(width // LANES):
            blk = y[:, s * LANES:(s + 1) * LANES]
            outs.append(blk * cos + pltpu.roll(blk, ROT_DIM // 2, 1) * s1
                        + pltpu.roll(blk, LANES - ROT_DIM // 2, 1) * s2)
        return jnp.concatenate(outs, axis=1)

    q = norm_rope(q_ref[...], qn_ref[...], ATTN_WIDTH)
    qo_ref[...] = (q * (HEAD_DIM ** -0.5)).astype(qo_ref.dtype)
    ko_ref[...] = norm_rope(k_ref[...], kn_ref[...], KV_WIDTH)


def _qk_prep(q, k, pos, q_norm, k_norm):
    m = q.shape[0]
    tm = _pick(m, (256, 128, 64, 32))
    half = ROT_DIM // 2
    inv_freq = ROPE_THETA ** (-jnp.arange(half, dtype=F32) * 2.0 / ROT_DIM)
    ang = pos.astype(F32)[:, None] * inv_freq[None, :]
    cos, sin = jnp.cos(ang), jnp.sin(ang)
    ones = jnp.ones((m, HEAD_DIM - ROT_DIM), F32)
    zeros = jnp.zeros((m, HEAD_DIM - ROT_DIM), F32)
    zh = jnp.zeros((m, half), F32)
    cos_t = jnp.tile(jnp.concatenate([cos, cos, ones], axis=1), (1, LANES // HEAD_DIM))
    s1_t = jnp.tile(jnp.concatenate([zh, sin, zeros], axis=1), (1, LANES // HEAD_DIM))
    s2_t = jnp.tile(jnp.concatenate([-sin, zh, zeros], axis=1), (1, LANES // HEAD_DIM))
    head_of_lane = jnp.arange(ATTN_WIDTH) // HEAD_DIM
    g = (head_of_lane[:, None] == jnp.arange(LANES)[None, :]).astype(BF16)
    gt = g.T
    qn = jnp.tile(q_norm.astype(F32), N_HEADS).reshape(1, ATTN_WIDTH)
    kn = jnp.tile(k_norm.astype(F32), N_KV_HEADS).reshape(1, KV_WIDTH)
    row = lambda w: pl.BlockSpec((tm, w), lambda i: (i, 0))
    full = lambda a: pl.BlockSpec(a.shape, lambda i: (0, 0))
    return pl.pallas_call(
        _qk_prep_kernel,
        grid=(m // tm,),
        in_specs=[row(ATTN_WIDTH), row(KV_WIDTH), row(LANES), row(LANES), row(LANES),
                  full(qn), full(kn), full(g), full(gt)],
        out_specs=[row(ATTN_WIDTH), row(KV_WIDTH)],
        out_shape=[jax.ShapeDtypeStruct((m, ATTN_WIDTH), BF16), jax.ShapeDtypeStruct((m, KV_WIDTH), F32)],
        compiler_params=_cparams(("parallel",)),
        name="qk_prep",
    )(q, k, cos_t, s1_t, s2_t, qn, kn, g, gt)


def _attend(q, kk, vv, sinks_ref, valid):
    tq = q.shape[0]
    outs = []
    for j in range(N_KV_HEADS):
        kj = kk[:, j * HEAD_DIM:(j + 1) * HEAD_DIM]
        vj = vv[:, j * HEAD_DIM:(j + 1) * HEAD_DIM]
        heads = [Q_PER_KV * j + g for g in range(Q_PER_KV)]
        q4 = jnp.concatenate([q[:, h * HEAD_DIM:(h + 1) * HEAD_DIM] for h in heads], axis=0)
        s = _dot_nt(q4, kj)
        if valid is not None:
            s = jnp.where(valid, s, NEG_INF)
        sink = jnp.concatenate([jnp.full((tq, 1), sinks_ref[h], F32) for h in heads], axis=0)
        m = jnp.maximum(jnp.max(s, axis=-1, keepdims=True), sink)
        p = jnp.exp(s - m)
        denom = jnp.sum(p, axis=-1, keepdims=True) + jnp.exp(sink - m)
        o = _dot(p.astype(BF16), vj) / denom
        for g in range(Q_PER_KV):
            outs.append(o[g * tq:(g + 1) * tq, :])
    return jnp.concatenate(outs, axis=1)


def _attn_prompt_kernel(sinks_ref, q_ref, ka_ref, kb_ref, va_ref, vb_ref, o_ref, st_ref, pt_ref, rl_ref):
    i = pl.program_id(1)
    tq = q_ref.shape[0]
    tk = 2 * tq
    q = q_ref[...]
    k32 = jnp.concatenate([ka_ref[...], kb_ref[...]], axis=0)
    vt = jnp.concatenate([va_ref[...], vb_ref[...]], axis=0).T.astype(BF16)

    lane_k = lax.broadcasted_iota(I32, (tk, LANES), 1)
    kc = lax.broadcasted_iota(I32, (tk, LANES), 0) // CHUNK
    k_ind = jnp.where(lane_k == 0, jnp.where(kc == 0, 1.0, 0.0),
                      jnp.where(lane_k == 1, jnp.where(kc == N_PREV_CHUNKS + 1, 1.0, 0.0),
                                jnp.where(lane_k == 2, jnp.where(kc < N_PREV_CHUNKS, 1.0, 0.0), 0.0)))
    first_step = jnp.where(i == 0, 1.0, 0.0)
    k_ind = jnp.where(lane_k == 2, k_ind * first_step, k_ind).astype(BF16)
    lane_q = lax.broadcasted_iota(I32, (tq, LANES), 1)
    qc = lax.broadcasted_iota(I32, (tq, LANES), 0) // CHUNK
    q_msk = jnp.where(lane_q == 0, jnp.where(qc == 1, NEG_INF, 0.0),
                      jnp.where(lane_q == 1, jnp.where(qc == 0, NEG_INF, 0.0),
                                jnp.where(lane_q == 2, NEG_INF, 0.0))).astype(BF16)
    low = lane_k < HEAD_DIM

    for slab in range(KV_WIDTH // LANES):
        ks = k32[:, slab * LANES:(slab + 1) * LANES]
        kr = pltpu.roll(ks, HEAD_DIM, 1)
        for jj in range(2):
            j = 2 * slab + jj
            in_low = jnp.where(low, ks if jj == 0 else kr, 0.0).astype(BF16)
            in_high = jnp.where(low, 0.0, kr if jj == 0 else ks).astype(BF16)
            k_ext = (jnp.concatenate([in_low, k_ind], axis=1), jnp.concatenate([in_high, k_ind], axis=1))
            for g in range(Q_PER_KV):
                h = Q_PER_KV * j + g
                pair = h // 2
                q_ext = jnp.concatenate([q[:, pair * LANES:(pair + 1) * LANES], q_msk], axis=1)
                st_ref[h] = _dot_nt(k_ext[h % 2], q_ext)

    for h in range(N_HEADS):
        st = st_ref[h]
        sink = sinks_ref[h]
        m = jnp.maximum(jnp.max(st, axis=0, keepdims=True), sink)
        pt = jnp.exp(st - m)
        rl_ref[h:h + 1, :] = 1.0 / (jnp.sum(pt, axis=0, keepdims=True) + jnp.exp(sink - m))
        pt_ref[h] = pt.astype(BF16)

    for pair in range(N_HEADS // 2):
        pieces = []
        for h in (2 * pair, 2 * pair + 1):
            j = h // Q_PER_KV
            pieces.append(_dot(vt[j * HEAD_DIM:(j + 1) * HEAD_DIM, :], pt_ref[h]) * rl_ref[h:h + 1, :])
        o_ref[:, pair * LANES:(pair + 1) * LANES] = jnp.concatenate(pieces, axis=0).T.astype(o_ref.dtype)


def _attn_prompt(q, k, kv, sinks, bsz, seq):
    tq = 2 * CHUNK
    nb = seq // tq

    def kv_spec(back, col):
        return pl.BlockSpec((tq, KV_WIDTH), lambda b, i, s: (b * nb + jnp.maximum(i - back, 0), col))

    grid_spec = pltpu.PrefetchScalarGridSpec(
        num_scalar_prefetch=1,
        grid=(bsz, nb),
        in_specs=[pl.BlockSpec((tq, ATTN_WIDTH), lambda b, i, s: (b * nb + i, 0)),
                  kv_spec(1, 0), kv_spec(0, 0), kv_spec(1, 1), kv_spec(0, 1)],
        out_specs=pl.BlockSpec((tq, ATTN_WIDTH), lambda b, i, s: (b * nb + i, 0)),
        scratch_shapes=[pltpu.VMEM((N_HEADS, 2 * tq, tq), F32), pltpu.VMEM((N_HEADS, 2 * tq, tq), BF16),
                        pltpu.VMEM((N_HEADS, tq), F32)],
    )
    return pl.pallas_call(
        _attn_prompt_kernel,
        grid_spec=grid_spec,
        out_shape=jax.ShapeDtypeStruct((bsz * seq, ATTN_WIDTH), BF16),
        compiler_params=_cparams(("parallel", "parallel")),
        name="attn_prompt",
    )(sinks, q, k, k, kv, kv)


def _attn_sample_kernel(sinks_ref, q_ref, kn_ref, vn_ref, kc_ref, vc_ref, o_ref):
    kk = jnp.concatenate([kc_ref[0], kn_ref[...]], axis=0).astype(BF16)
    vv = jnp.concatenate([vc_ref[0], vn_ref[...]], axis=0).astype(BF16)
    o_ref[...] = _attend(q_ref[...], kk, vv, sinks_ref, None).astype(o_ref.dtype)


def _attn_sample(q, k, kv, cache_k, cache_v, sinks, row0, bsz, seq):
    blk0 = row0 // seq
    kv_len = cache_k.shape[1]
    new = lambda w, col: pl.BlockSpec((seq, w), lambda b, s: (blk0 + b, col))
    cache = pl.BlockSpec((1, kv_len, KV_WIDTH), lambda b, s: (b, 0, 0))
    grid_spec = pltpu.PrefetchScalarGridSpec(
        num_scalar_prefetch=1,
        grid=(bsz,),
        in_specs=[new(ATTN_WIDTH, 0), new(KV_WIDTH, 0), new(KV_WIDTH, 1), cache, cache],
        out_specs=pl.BlockSpec((seq, ATTN_WIDTH), lambda b, s: (b, 0)),
    )
    return pl.pallas_call(
        _attn_sample_kernel,
        grid_spec=grid_spec,
        out_shape=jax.ShapeDtypeStruct((bsz * seq, ATTN_WIDTH), BF16),
        compiler_params=_cparams(("parallel",)),
        name="attn_sample",
    )(sinks, q, k, kv, cache_k, cache_v)


def _pad_rows(x, rows):
    if x.shape[0] == rows:
        return x
    return jnp.concatenate([x, jnp.zeros((rows - x.shape[0], x.shape[1]), x.dtype)], axis=0)


def _ssd_kernel(*refs, clen, has_past):
    if has_past:
        (xbc_ref, dt_ref, z_ref, cw_ref, cb_ref, dtb_ref, alog_ref, dskip_ref, nw_ref, cpast_ref, hpast_ref,
         y_ref, hout_ref, xpad, u_ref, g_ref, ht_ref, ca_ref, xdt_ref, yst_ref, xw_ref, cb2_ref) = refs
    else:
        (xbc_ref, dt_ref, z_ref, cw_ref, cb_ref, dtb_ref, alog_ref, dskip_ref, nw_ref,
         y_ref, hout_ref, xpad, u_ref, g_ref, ht_ref, ca_ref, xdt_ref, yst_ref, xw_ref, cb2_ref) = refs
    L = clen
    c = pl.program_id(1)
    nc = pl.num_programs(1)
    n_pairs = SSM_HEADS // 2
    pairs_per_group = n_pairs // SSM_GROUPS
    gw = D_INNER // SSM_GROUPS

    @pl.when(c == 0)
    def _init():
        if has_past:
            for cidx in range(CONV_DIM // LANES):
                xpad[cidx, 5:8, :] = cpast_ref[0, :, cidx * LANES:(cidx + 1) * LANES]
            for p in range(n_pairs):
                both = jnp.concatenate([hpast_ref[0, 2 * p], hpast_ref[0, 2 * p + 1]], axis=0)
                ht_ref[:, p * LANES:(p + 1) * LANES] = both.T
        else:
            xpad[:, 0:8, :] = jnp.zeros((CONV_DIM // LANES, 8, LANES), F32)
            ht_ref[...] = jnp.zeros(ht_ref.shape, F32)

    n_col = CONV_DIM // LANES
    for cidx in range(n_col):
        xpad[cidx, 8:8 + L, :] = xbc_ref[:, cidx * LANES:(cidx + 1) * LANES]
    for cidx in range(n_col):
        sl = slice(cidx * LANES, (cidx + 1) * LANES)
        acc = cb_ref[:, sl] + xpad[cidx, 8:8 + L, :] * cw_ref[3:4, sl]
        acc = acc + xpad[cidx, 7:7 + L, :] * cw_ref[2:3, sl]
        acc = acc + xpad[cidx, 6:6 + L, :] * cw_ref[1:2, sl]
        acc = acc + xpad[cidx, 5:5 + L, :] * cw_ref[0:1, sl]
        u_ref[:, sl] = _silu(acc)
    for cidx in range(n_col):
        xpad[cidx, 5:8, :] = xpad[cidx, 5 + L:8 + L, :]

    dtx = dt_ref[...] + dtb_ref[...]
    dt = jnp.maximum(dtx, 0.0) + jnp.log(1.0 + jnp.exp(-jnp.abs(dtx)))
    loga = dt * (-jnp.exp(alog_ref[...]))
    t_idx = lax.broadcasted_iota(I32, (L, L), 0)
    s_idx = lax.broadcasted_iota(I32, (L, L), 1)
    incl = (s_idx <= t_idx).astype(BF16)
    acum = sum(_dot(incl, part) for part in _split3(loga))
    acum_t = _pad_rows(acum, LANES).T[:, 0:L]

    lane_m = lax.broadcasted_iota(I32, (L, 2 * L), 1)
    row_m = lax.broadcasted_iota(I32, (L, 2 * L), 0)
    left_m = lane_m < L
    causal_m = jnp.where(left_m, lane_m, lane_m - L) <= row_m
    left = lax.broadcasted_iota(I32, (L, LANES), 1) < SSM_HEAD_DIM
    b_of = lambda grp: u_ref[:, D_INNER + grp * D_STATE:D_INNER + (grp + 1) * D_STATE]
    c_of = lambda grp: u_ref[:, D_INNER + BC_WIDTH + grp * D_STATE:D_INNER + BC_WIDTH + (grp + 1) * D_STATE]
    pair_lanes = lambda p: slice(p * LANES, (p + 1) * LANES)
    group_lanes = lambda grp: slice(grp * gw, (grp + 1) * gw)

    for p in range(n_pairs):
        h0, h1, sl = 2 * p, 2 * p + 1, pair_lanes(p)
        ca_ref[:, sl] = jnp.where(left, acum[:, h0:h0 + 1], acum[:, h1:h1 + 1])
        xdt_ref[:, sl] = u_ref[:, sl] * jnp.where(left, dt[:, h0:h0 + 1], dt[:, h1:h1 + 1])

    for grp in range(SSM_GROUPS):
        c_bf = c_of(grp).astype(BF16)
        cb = _dot_nt(c_bf, b_of(grp).astype(BF16))
        cb2_ref[grp] = jnp.concatenate([cb, cb], axis=1)
        yst_ref[:, group_lanes(grp)] = _dot(c_bf, ht_ref[:, group_lanes(grp)].astype(BF16))

    for p in range(n_pairs):
        h0, h1, sl = 2 * p, 2 * p + 1, pair_lanes(p)
        ca = ca_ref[:, sl]
        ca_m = ca if L == SSM_HEAD_DIM else jnp.where(left_m, acum[:, h0:h0 + 1], acum[:, h1:h1 + 1])
        row_a = jnp.concatenate([acum_t[h0:h0 + 1, :], acum_t[h1:h1 + 1, :]], axis=1)
        dec = jnp.exp(jnp.where(causal_m, ca_m - row_a, NEG_INF))
        m_pair = (cb2_ref[p // pairs_per_group] * dec).astype(BF16)
        xdt = xdt_ref[:, sl]
        x_bd = jnp.concatenate([jnp.where(left, xdt, 0.0), jnp.where(left, 0.0, xdt)], axis=0).astype(BF16)
        y = _dot(m_pair, x_bd) + yst_ref[:, sl] * jnp.exp(ca) + u_ref[:, sl] * dskip_ref[:, sl]
        g_ref[:, sl] = y * _silu(z_ref[:, sl].astype(F32))
        xw_ref[:, sl] = (xdt * jnp.exp(ca[L - 1:L, :] - ca)).astype(BF16)

    for grp in range(SSM_GROUPS):
        gsl = group_lanes(grp)
        gg = g_ref[:, gsl]
        ms = jnp.mean(gg * gg, axis=-1, keepdims=True)
        y_ref[:, gsl] = (gg * lax.rsqrt(ms + EPS) * nw_ref[:, gsl]).astype(y_ref.dtype)
        bt_bf = _pad_rows(b_of(grp), LANES).T[:, 0:L].astype(BF16)
        ht_ref[:, gsl] = ht_ref[:, gsl] * jnp.exp(ca_ref[L - 1:L, gsl]) + _dot(bt_bf, xw_ref[:, gsl])

    @pl.when(c == nc - 1)
    def _fin():
        for p in range(n_pairs):
            both = ht_ref[:, p * LANES:(p + 1) * LANES].T
            hout_ref[0, 2 * p] = both[0:SSM_HEAD_DIM, :]
            hout_ref[0, 2 * p + 1] = both[SSM_HEAD_DIM:2 * SSM_HEAD_DIM, :]


def _ssd(xbc, dt_raw, z, conv_w, conv_b, dt_bias, a_log, d_skip, ssm_norm, row0, bsz, seq, clen,
         conv_past=None, ssm_past=None):
    nc = seq // clen
    blk0 = row0 // clen
    has_past = conv_past is not None
    pad = lambda a: jnp.pad(a.astype(F32), (0, LANES - SSM_HEADS)).reshape(1, LANES)
    params = [conv_w.astype(F32), conv_b.astype(F32).reshape(1, CONV_DIM), pad(dt_bias), pad(a_log),
              jnp.repeat(d_skip.astype(F32), SSM_HEAD_DIM).reshape(1, D_INNER), ssm_norm.astype(F32).reshape(1, D_INNER)]
    rows = lambda w: pl.BlockSpec((clen, w), lambda b, c: (blk0 + b * nc + c, 0))
    full = lambda a: pl.BlockSpec(a.shape, lambda b, c: (0,) * a.ndim)
    in_specs = [rows(CONV_DIM), rows(LANES), rows(D_INNER)] + [full(a) for a in params]
    args = [xbc, dt_raw, z] + params
    if has_past:
        in_specs += [pl.BlockSpec((1, CONV_W - 1, CONV_DIM), lambda b, c: (b, 0, 0)),
                     pl.BlockSpec((1, SSM_HEADS, SSM_HEAD_DIM, D_STATE), lambda b, c: (b, 0, 0, 0))]
        args += [conv_past.astype(F32), ssm_past.astype(F32)]
    return pl.pallas_call(
        functools.partial(_ssd_kernel, clen=clen, has_past=has_past),
        grid=(bsz, nc),
        in_specs=in_specs,
        out_specs=[pl.BlockSpec((clen, D_INNER), lambda b, c: (b * nc + c, 0)),
                   pl.BlockSpec((1, SSM_HEADS, SSM_HEAD_DIM, D_STATE), lambda b, c: (b, 0, 0, 0))],
        out_shape=[jax.ShapeDtypeStruct((bsz * seq, D_INNER), BF16),
                   jax.ShapeDtypeStruct((bsz, SSM_HEADS, SSM_HEAD_DIM, D_STATE), F32)],
        scratch_shapes=[pltpu.VMEM((CONV_DIM // LANES, 8 + clen, LANES), F32), pltpu.VMEM((clen, CONV_DIM), F32),
                        pltpu.VMEM((clen, D_INNER), F32), pltpu.VMEM((D_STATE, D_INNER), F32),
                        pltpu.VMEM((clen, D_INNER), F32), pltpu.VMEM((clen, D_INNER), F32),
                        pltpu.VMEM((clen, D_INNER), F32), pltpu.VMEM((clen, D_INNER), BF16),
                        pltpu.VMEM((SSM_GROUPS, clen, 2 * clen), F32)],
        compiler_params=_cparams(("parallel", "arbitrary")),
        name="ssd_past" if has_past else "ssd_prompt",
    )(*args)


def _merge_kernel(h_ref, a1_ref, a2_ref, s1_ref, s2_ref, wga_ref, wgs_ref, bga_ref, bgs_ref, wa_ref, ws_ref, o_ref,
                  *, n_first):
    def body(a_ref, s_ref):
        h = h_ref[...]
        g_a = jax.nn.sigmoid(_dot(h, wga_ref[...]) + bga_ref[...])
        g_s = jax.nn.sigmoid(_dot(h, wgs_ref[...]) + bgs_ref[...])
        o_ref[...] = (g_a * _dot(a_ref[...], wa_ref[...]) + g_s * _dot(s_ref[...], ws_ref[...])).astype(o_ref.dtype)

    _by_part(pl.program_id(0), n_first, body, (a1_ref, s1_ref), (a2_ref, s2_ref))


def _merge(h, attn_parts, ssm_parts, w_gate, b_gate, w_attn_up, w_ssm_up):
    m = h.shape[0]
    m1, m2 = attn_parts[0].shape[0], attn_parts[1].shape[0]
    tm = _pick(math.gcd(m1, m2), (512, 256, 128, 64, 32))
    nf = m1 // tm
    tn = 512
    nj = D_MODEL // tn
    bg = b_gate.astype(F32).reshape(1, 2 * D_MODEL)
    return pl.pallas_call(
        functools.partial(_merge_kernel, n_first=nf),
        grid=(m // tm, nj),
        in_specs=[pl.BlockSpec((tm, D_MODEL), lambda i, j: (i, 0)),
                  pl.BlockSpec((tm, ATTN_WIDTH), _first_rows(nf)), pl.BlockSpec((tm, ATTN_WIDTH), _later_rows(nf)),
                  pl.BlockSpec((tm, D_INNER), _first_rows(nf)), pl.BlockSpec((tm, D_INNER), _later_rows(nf)),
                  pl.BlockSpec((D_MODEL, tn), lambda i, j: (0, j)),
                  pl.BlockSpec((D_MODEL, tn), lambda i, j: (0, j + nj)),
                  pl.BlockSpec((1, tn), lambda i, j: (0, j)),
                  pl.BlockSpec((1, tn), lambda i, j: (0, j + nj)),
                  pl.BlockSpec((ATTN_WIDTH, tn), lambda i, j: (0, j)),
                  pl.BlockSpec((D_INNER, tn), lambda i, j: (0, j))],
        out_specs=pl.BlockSpec((tm, tn), lambda i, j: (i, j)),
        out_shape=jax.ShapeDtypeStruct((m, D_MODEL), BF16),
        compiler_params=_cparams(("parallel", "parallel")),
        name="merge",
    )(h, attn_parts[0], attn_parts[1], ssm_parts[0], ssm_parts[1], w_gate, w_gate, bg, bg, w_attn_up, w_ssm_up)


def _outproj_router_kernel(mg_ref, wo_ref, xa_ref, xb_ref, n2_ref, wr_ref, br_ref, x1_ref, h2_ref, idx_ref, gate_ref,
                           *, n_first):
    def residual(x_ref):
        x1_ref[...] = x_ref[...] + _dot(mg_ref[...], wo_ref[...])

    _by_part(pl.program_id(0), n_first, residual, (xa_ref,), (xb_ref,))
    x1 = x1_ref[...]
    ms = jnp.mean(x1 * x1, axis=-1, keepdims=True)
    h2 = x1 * lax.rsqrt(ms + EPS) * n2_ref[...]
    h2_ref[...] = _pack_bf16_pairs(h2)
    h_hi, h_lo = _split2(h2)
    w_hi, w_lo = _split2(wr_ref[...])
    logits = _dot(h_hi, w_hi) + (_dot(h_hi, w_lo) + _dot(h_lo, w_hi)) + br_ref[...]
    tm = logits.shape[0]
    lane = lax.broadcasted_iota(I32, (tm, LANES), 1)
    logits = jnp.where(lane < N_EXPERTS, logits, -jnp.inf)
    idx_out = jnp.zeros((tm, LANES), I32)
    val_out = jnp.zeros((tm, LANES), F32)
    top = None
    for k in range(TOP_K):
        v = jnp.max(logits, axis=-1, keepdims=True)
        i = jnp.min(jnp.where(logits == v, lane, LANES), axis=-1, keepdims=True)
        if k == 0:
            top = v
        idx_out = jnp.where(lane == k, i, idx_out)
        val_out = jnp.where(lane == k, jnp.exp(v - top), val_out)
        logits = jnp.where(lane == i, -jnp.inf, logits)
    idx_ref[...] = idx_out
    gate_ref[...] = val_out / jnp.sum(val_out, axis=-1, keepdims=True)


def _outproj_router(merged, w_out, xa, xb, norm2, w_router, b_router):
    m = merged.shape[0]
    tm = _pick(math.gcd(xa.shape[0], xb.shape[0]), (256, 128, 64, 32))
    nf = xa.shape[0] // tm
    wr = jnp.pad(w_router.astype(F32), ((0, 0), (0, LANES - N_EXPERTS)))
    br = jnp.pad(b_router.astype(F32), (0, LANES - N_EXPERTS)).reshape(1, LANES)
    row = lambda w: pl.BlockSpec((tm, w), lambda i: (i, 0))
    full = lambda r, c: pl.BlockSpec((r, c), lambda i: (0, 0))
    return pl.pallas_call(
        functools.partial(_outproj_router_kernel, n_first=nf),
        grid=(m // tm,),
        in_specs=[row(D_MODEL), full(D_MODEL, D_MODEL),
                  pl.BlockSpec((tm, D_MODEL), _first_rows(nf)), pl.BlockSpec((tm, D_MODEL), _later_rows(nf)),
                  full(1, D_MODEL), full(D_MODEL, LANES), full(1, LANES)],
        out_specs=[row(D_MODEL), row(D_MODEL // 2), row(LANES), row(LANES)],
        out_shape=[jax.ShapeDtypeStruct((m, D_MODEL), F32), jax.ShapeDtypeStruct((m, D_MODEL // 2), I32),
                   jax.ShapeDtypeStruct((m, LANES), I32), jax.ShapeDtypeStruct((m, LANES), F32)],
        compiler_params=_cparams(("parallel",)),
        name="outproj_router",
    )(merged, w_out, xa, xb, norm2.astype(F32).reshape(1, D_MODEL), wr, br)


def _rank_kernel(idx_ref, rank_ref, cnt_ref, base_ref):
    i = pl.program_id(0)
    tt = idx_ref.shape[0]

    @pl.when(i == 0)
    def _():
        base_ref[...] = jnp.zeros(base_ref.shape, F32)

    idx = idx_ref[...]
    lane = lax.broadcasted_iota(I32, (tt, LANES), 1)
    sel = [lane == idx[:, k:k + 1] for k in range(TOP_K)]
    onehot = jnp.zeros((tt, LANES), F32)
    for k in range(TOP_K):
        onehot = jnp.where(sel[k], 1.0, onehot)
    r_idx = lax.broadcasted_iota(I32, (tt, tt), 0)
    c_idx = lax.broadcasted_iota(I32, (tt, tt), 1)
    before = (c_idx < r_idx).astype(BF16)
    rank_all = _dot(before, onehot.astype(BF16)) + base_ref[0:1, :]
    out = jnp.zeros((tt, LANES), F32)
    for k in range(TOP_K):
        out = jnp.where(lane == k, jnp.sum(jnp.where(sel[k], rank_all, 0.0), axis=-1, keepdims=True), out)
    rank_ref[...] = out.astype(I32)
    base_ref[0:1, :] = base_ref[0:1, :] + jnp.sum(onehot, axis=0, keepdims=True)
    cnt_ref[...] = base_ref[...].astype(I32)


def _ranks(idx):
    m = idx.shape[0]
    tt = _pick(m, (512, 256, 128, 64, 32))
    return pl.pallas_call(
        _rank_kernel,
        grid=(m // tt,),
        in_specs=[pl.BlockSpec((tt, LANES), lambda i: (i, 0))],
        out_specs=[pl.BlockSpec((tt, LANES), lambda i: (i, 0)), pl.BlockSpec((8, LANES), lambda i: (0, 0))],
        out_shape=[jax.ShapeDtypeStruct((m, LANES), I32), jax.ShapeDtypeStruct((8, LANES), I32)],
        scratch_shapes=[pltpu.VMEM((8, LANES), F32)],
        compiler_params=_cparams(("arbitrary",)),
        name="expert_ranks",
    )(idx)


def _dispatch_kernel(dest_ref, fill_ref, h_ref, xs_hbm, zero_ref, sem, zsem, *, tt, tm, n_blocks):
    @pl.when(pl.program_id(0) == 0)
    def _():
        zero_ref[...] = jnp.zeros(zero_ref.shape, zero_ref.dtype)

        def block_copy(b):
            return pltpu.make_async_copy(zero_ref, xs_hbm.at[pl.ds(pl.multiple_of(b * tm, tm), tm), :], zsem)

        def start(b, carry):
            @pl.when(fill_ref[b] != 0)
            def _():
                block_copy(b).start()

            return carry

        def done(b, carry):
            @pl.when(fill_ref[b] != 0)
            def _():
                block_copy(b).wait()

            return carry

        lax.fori_loop(0, n_blocks, start, 0)
        lax.fori_loop(0, n_blocks, done, 0)

    base = pl.program_id(0) * (tt * TOP_K)

    def issue(r, carry):
        for k in range(TOP_K):
            dst = dest_ref[base + r * TOP_K + k]
            pltpu.make_async_copy(h_ref.at[pl.ds(r, 1), :], xs_hbm.at[pl.ds(dst, 1), :], sem).start()
        return carry

    lax.fori_loop(0, tt, issue, 0, unroll=True)
    for _ in range(TOP_K):
        pltpu.make_async_copy(h_ref, xs_hbm.at[pl.ds(0, tt), :], sem).wait()


def _dispatch(dest_flat, fill_block, h2, tm):
    m = h2.shape[0]
    n_blocks = fill_block.shape[0]
    tt = _pick(m, (256, 128, 64, 32))
    width = h2.shape[1]
    grid_spec = pltpu.PrefetchScalarGridSpec(
        num_scalar_prefetch=2,
        grid=(m // tt,),
        in_specs=[pl.BlockSpec((tt, width), lambda i, d, f: (i, 0))],
        out_specs=pl.BlockSpec(memory_space=pl.ANY),
        scratch_shapes=[pltpu.VMEM((tm, width), h2.dtype), pltpu.SemaphoreType.DMA, pltpu.SemaphoreType.DMA],
    )
    return pl.pallas_call(
        functools.partial(_dispatch_kernel, tt=tt, tm=tm, n_blocks=n_blocks),
        grid_spec=grid_spec,
        out_shape=jax.ShapeDtypeStruct((n_blocks * tm, width), h2.dtype),
        compiler_params=_cparams(("arbitrary",)),
        name="dispatch",
    )(dest_flat, fill_block, h2)


def _pack_bf16_pairs(x):
    c = x.shape[1] // 2
    bits = lax.bitcast_convert_type(x.astype(BF16).astype(F32), I32)
    return lax.shift_right_logical(bits[:, :c], 16) | bits[:, c:]


def _unpack_bf16_pairs(p):
    lo = lax.bitcast_convert_type(lax.shift_left(p, 16), F32).astype(BF16)
    hi = lax.bitcast_convert_type(p & jnp.int32(-65536), F32).astype(BF16)
    return lo, hi


def _expert_kernel(be_ref, nu_ref, xs_ref, bg_ref, bu_ref, bd_ref, wg_hbm, wu_hbm, wd_hbm, ys_ref,
                   cg, cu, cd, stg_g, stg_u, stg_d, xb_ref, sems, *, tf):
    i = pl.program_id(0)
    nj = D_FF // tf
    half = D_MODEL // 2
    e = be_ref[i]
    first = jnp.logical_or(i == 0, e != be_ref[jnp.maximum(i - 1, 0)])
    e_next = be_ref[jnp.minimum(i + 1, pl.num_programs(0) - 1)]
    next_is_new = jnp.logical_and(i + 1 < nu_ref[0], e_next != e)

    def tile_copies(t, slot, expert=e):
        cols = pl.ds(pl.multiple_of(t * tf, tf), tf)
        return (pltpu.make_async_copy(wg_hbm.at[expert, :, cols], stg_g.at[slot], sems.at[slot, 0]),
                pltpu.make_async_copy(wu_hbm.at[expert, :, cols], stg_u.at[slot], sems.at[slot, 1]),
                pltpu.make_async_copy(wd_hbm.at[expert, cols, :], stg_d.at[slot], sems.at[slot, 2]))

    @pl.when(i < nu_ref[0])
    def _():
        @pl.when(i == 0)
        def _():
            for cp in tile_copies(0, 0):
                cp.start()

        lo, hi = _unpack_bf16_pairs(xs_ref[...])
        xb_ref[:, :half] = lo
        xb_ref[:, half:] = hi

        def refill(t):
            slot = t % 2

            @pl.when(t + 1 < nj)
            def _():
                for cp in tile_copies(t + 1, 1 - slot):
                    cp.start()

            for cp in tile_copies(t, slot):
                cp.wait()
            cg[t] = stg_g[slot].astype(BF16)
            cu[t] = stg_u[slot].astype(BF16)
            cd[t] = stg_d[slot].astype(BF16)

        def compute(t, opening):
            xb = xb_ref[...]
            g = jnp.minimum(_dot(xb, cg[t]) + bg_ref[0, t], SWIGLU_LIMIT)
            u = jnp.clip(_dot(xb, cu[t]) + bu_ref[0, t], -SWIGLU_LIMIT, SWIGLU_LIMIT)
            act = (u + 1.0) * (g * jax.nn.sigmoid(SWIGLU_ALPHA * g))
            part = _dot(act.astype(BF16), cd[t])
            if opening:
                ys_ref[...] = part + bd_ref[0]
            else:
                ys_ref[...] += part

        @pl.when(first)
        def _():
            refill(jnp.int32(0))
            compute(0, True)

            def step(t, carry):
                refill(t)
                compute(t, False)
                return carry

            lax.fori_loop(1, nj, step, 0)

        @pl.when(jnp.logical_not(first))
        def _():
            for t in range(nj):
                compute(t, t == 0)

        @pl.when(next_is_new)
        def _():
            for cp in tile_copies(0, 0, e_next):
                cp.start()

    @pl.when(i >= nu_ref[0])
    def _():
        ys_ref[...] = jnp.zeros(ys_ref.shape, F32)


def _experts(block_e, n_used, xs, w_e_gate, b_e_gate, w_e_up, b_e_up, w_e_down, b_e_down):
    n_rows = xs.shape[0]
    tm, tf = MOE_TM, MOE_TF
    n_blocks = n_rows // tm
    nj = D_FF // tf
    half = D_MODEL // 2

    def blk(i, nu):
        return jnp.minimum(i, nu[0] - 1)

    hbm = pl.BlockSpec(memory_space=pl.ANY)
    grid_spec = pltpu.PrefetchScalarGridSpec(
        num_scalar_prefetch=2,
        grid=(n_blocks,),
        in_specs=[pl.BlockSpec((tm, half), lambda i, be, nu: (blk(i, nu), 0)),
                  pl.BlockSpec((1, nj, 1, tf), lambda i, be, nu: (be[blk(i, nu)], 0, 0, 0)),
                  pl.BlockSpec((1, nj, 1, tf), lambda i, be, nu: (be[blk(i, nu)], 0, 0, 0)),
                  pl.BlockSpec((1, 1, D_MODEL), lambda i, be, nu: (be[blk(i, nu)], 0, 0)),
                  hbm, hbm, hbm],
        out_specs=pl.BlockSpec((tm, D_MODEL), lambda i, be, nu: (i, 0)),
        scratch_shapes=[pltpu.VMEM((nj, D_MODEL, tf), BF16), pltpu.VMEM((nj, D_MODEL, tf), BF16),
                        pltpu.VMEM((nj, tf, D_MODEL), BF16),
                        pltpu.VMEM((2, D_MODEL, tf), F32), pltpu.VMEM((2, D_MODEL, tf), F32),
                        pltpu.VMEM((2, tf, D_MODEL), F32),
                        pltpu.VMEM((tm, D_MODEL), BF16), pltpu.SemaphoreType.DMA((2, 3))],
    )
    return pl.pallas_call(
        functools.partial(_expert_kernel, tf=tf),
        grid_spec=grid_spec,
        out_shape=jax.ShapeDtypeStruct((n_rows, D_MODEL), F32),
        compiler_params=pltpu.CompilerParams(dimension_semantics=("arbitrary",), vmem_limit_bytes=EXPERT_VMEM_LIMIT),
        name="experts",
    )(block_e, n_used, xs, b_e_gate.reshape(N_EXPERTS, nj, 1, tf), b_e_up.reshape(N_EXPERTS, nj, 1, tf),
      b_e_down.reshape(N_EXPERTS, 1, D_MODEL), w_e_gate, w_e_up, w_e_down)


def _combine_kernel(dest_ref, x1_ref, gate_ref, ys_hbm, o_ref, buf, sems, *, tt, row0):
    i = pl.program_id(0)
    n_steps = pl.num_programs(0)
    slot = i % 2

    def gather(step, to_slot):
        base = (row0 + step * tt) * TOP_K

        def issue(r, carry):
            for k in range(TOP_K):
                src = dest_ref[base + r * TOP_K + k]
                pltpu.make_async_copy(ys_hbm.at[pl.ds(src, 1), :], buf.at[to_slot, k, pl.ds(r, 1), :],
                                      sems.at[to_slot]).start()
            return carry

        lax.fori_loop(0, tt, issue, 0, unroll=True)

    @pl.when(i == 0)
    def _():
        gather(0, 0)

    @pl.when(i + 1 < n_steps)
    def _():
        gather(i + 1, 1 - slot)

    for k in range(TOP_K):
        pltpu.make_async_copy(ys_hbm.at[pl.ds(0, tt), :], buf.at[slot, k], sems.at[slot]).wait()
    gate = gate_ref[...]
    acc = x1_ref[...]
    for k in range(TOP_K):
        acc = acc + gate[:, k:k + 1] * buf[slot, k]
    o_ref[...] = acc


def _combine(dest_flat, x1, gate, ys, row0, rows):
    tt = _pick(math.gcd(row0, rows), (128, 64, 32))
    blk0 = row0 // tt
    grid_spec = pltpu.PrefetchScalarGridSpec(
        num_scalar_prefetch=1,
        grid=(rows // tt,),
        in_specs=[pl.BlockSpec((tt, D_MODEL), lambda i, d: (blk0 + i, 0)),
                  pl.BlockSpec((tt, LANES), lambda i, d: (blk0 + i, 0)),
                  pl.BlockSpec(memory_space=pl.ANY)],
        out_specs=pl.BlockSpec((tt, D_MODEL), lambda i, d: (i, 0)),
        scratch_shapes=[pltpu.VMEM((2, TOP_K, tt, D_MODEL), F32), pltpu.SemaphoreType.DMA((2,))],
    )
    return pl.pallas_call(
        functools.partial(_combine_kernel, tt=tt, row0=row0),
        grid_spec=grid_spec,
        out_shape=jax.ShapeDtypeStruct((rows, D_MODEL), F32),
        compiler_params=_cparams(("arbitrary",)),
        name="combine",
    )(dest_flat, x1, gate, ys)


def _moe(x1, h2, idx, gate, w_e_gate, b_e_gate, w_e_up, b_e_up, w_e_down, b_e_down, n_first):
    m = x1.shape[0]
    tm = MOE_TM
    ranks, cnt = _ranks(idx)
    counts = cnt[0, :N_EXPERTS]
    padded = (counts + tm - 1) // tm * tm
    pad_end = jnp.cumsum(padded)
    pad_start = pad_end - padded
    e_sel = idx[:, :TOP_K]
    dest = (pad_start[e_sel] + ranks[:, :TOP_K]).astype(I32).reshape(-1)
    n_blocks = -(-(m * TOP_K + N_EXPERTS * (tm - 1)) // tm)
    block_row = jnp.arange(n_blocks, dtype=I32) * tm
    block_e = jnp.minimum(jnp.sum((pad_end[None, :] <= block_row[:, None]).astype(I32), axis=1), N_EXPERTS - 1)
    n_used = (pad_end[-1] // tm).astype(I32).reshape(1)
    is_last = jnp.sum((pad_end[None, :] == block_row[:, None] + tm).astype(I32), axis=1) > 0
    fill_block = jnp.logical_or(is_last, block_row >= pad_end[-1]).astype(I32)
    xs = _dispatch(dest, fill_block, h2, tm)
    ys = _experts(block_e, n_used, xs, w_e_gate, b_e_gate, w_e_up, b_e_up, w_e_down, b_e_down)
    return _combine(dest, x1, gate, ys, 0, n_first), _combine(dest, x1, gate, ys, n_first, m - n_first)


def kernel(x_prompt, x_sample, cache_k, cache_v, state_conv, state_ssm, norm1, w_in, q_norm, k_norm, sinks, conv_w,
           conv_b, dt_bias, a_log, d_skip, ssm_norm, w_gate, b_gate, w_attn_up, w_ssm_up, w_out, norm2, w_router,
           b_router, w_e_gate, b_e_gate, w_e_up, b_e_up, w_e_down, b_e_down):
    bp, sp, _ = x_prompt.shape
    bs, ss, _ = x_sample.shape
    tp, ts = bp * sp, bs * ss
    kv_len = cache_k.shape[2]
    l = 0

    xa, xb = x_prompt.reshape(tp, D_MODEL), x_sample.reshape(ts, D_MODEL)
    pos = jnp.concatenate([jnp.tile(jnp.arange(sp, dtype=I32), bp), jnp.tile(PAST_LEN + jnp.arange(ss, dtype=I32), bs)])

    o1 = ATTN_WIDTH
    o2 = o1 + 2 * KV_WIDTH
    o3 = o2 + D_INNER
    o4 = o3 + CONV_DIM
    h = _rmsnorm(xa, xb, norm1[l], BF16)
    w_in_t = jnp.swapaxes(w_in, 1, 2)
    q_p = _matmul(h, w_in_t, l, 0, o1, F32, "proj_q")
    kv_p = _matmul(h, w_in_t, l, o1, o2 - o1, F32, "proj_kv")
    z = _matmul(h, w_in_t, l, o2, o3 - o2, BF16, "proj_z")
    xbc = _matmul(h, w_in_t, l, o3, o4 - o3, F32, "proj_xbc")
    w_dt = jnp.pad(w_in_t[l:l + 1, o4:, :], ((0, 0), (0, LANES - SSM_HEADS), (0, 0)))
    dt_raw = _matmul(h, w_dt, 0, 0, LANES, F32, "proj_dt")

    q_rot, k_rot = _qk_prep(q_p, kv_p, pos, q_norm[l], k_norm[l])
    sk = sinks[l].astype(F32)
    attn_p = _attn_prompt(q_rot, k_rot, kv_p, sk, bp, sp)
    ck = cache_k[l].reshape(bs, kv_len, KV_WIDTH)
    cv = cache_v[l].reshape(bs, kv_len, KV_WIDTH)
    attn_s = _attn_sample(q_rot, k_rot, kv_p, ck, cv, sk, tp, bs, ss)

    ssd_w = (conv_w[l], conv_b[l], dt_bias[l], a_log[l], d_skip[l], ssm_norm[l])
    ssm_p, hfin_p = _ssd(xbc, dt_raw, z, *ssd_w, 0, bp, sp, CHUNK)
    ssm_s, hfin_s = _ssd(xbc, dt_raw, z, *ssd_w, tp, bs, ss, ss, state_conv[l], state_ssm[l])

    merged = _merge(h, (attn_p, attn_s), (ssm_p, ssm_s), w_gate[l].astype(BF16), b_gate[l], w_attn_up[l].astype(BF16), w_ssm_up[l].astype(BF16))
    x1, h2, idx, gate = _outproj_router(merged, w_out[l].astype(BF16), xa, xb, norm2[l], w_router[l], b_router[l])
    y_p, y_s = _moe(x1, h2, idx, gate, w_e_gate[l], b_e_gate[l], w_e_up[l], b_e_up[l], w_e_down[l], b_e_down[l], tp)

    def prompt_tail(a, n):
        return jnp.stack([a[(b + 1) * sp - n:(b + 1) * sp] for b in range(bp)])

    heads = lambda a: a.reshape(a.shape[0], a.shape[1], N_KV_HEADS, HEAD_DIM)
    new_k_p = heads(prompt_tail(k_rot, WINDOW))
    new_v_p = heads(prompt_tail(kv_p, WINDOW)[:, :, KV_WIDTH:])
    ks4 = heads(k_rot[tp:].reshape(bs, ss, KV_WIDTH))
    vs4 = heads(kv_p[tp:, KV_WIDTH:].reshape(bs, ss, KV_WIDTH))
    new_k_s = jnp.concatenate([cache_k[l], ks4], axis=1)[:, -kv_len:]
    new_v_s = jnp.concatenate([cache_v[l], vs4], axis=1)[:, -kv_len:]
    xbc_s = jnp.concatenate([state_conv[l], xbc[tp:].reshape(bs, ss, CONV_DIM)], axis=1)
    return (y_p.reshape(bp, sp, D_MODEL), y_s.reshape(bs, ss, D_MODEL),
            new_k_p[None], new_v_p[None], prompt_tail(xbc, CONV_W - 1)[None], hfin_p[None],
            new_k_s[None], new_v_s[None], xbc_s[:, -(CONV_W - 1):][None], hfin_s[None])
```

```python
import functools
import math

import jax
import jax.numpy as jnp
import numpy as np
from jax import lax
from jax.experimental import pallas as pl
from jax.experimental.pallas import tpu as pltpu

F32 = jnp.float32
BF16 = jnp.bfloat16
I32 = jnp.int32

D_MODEL = 2048
CHUNK = 64
N_HEADS = 32
N_KV_HEADS = 8
HEAD_DIM = 64
Q_PER_KV = N_HEADS // N_KV_HEADS
ATTN_WIDTH = N_HEADS * HEAD_DIM
KV_WIDTH = N_KV_HEADS * HEAD_DIM
WINDOW = 128
N_PREV_CHUNKS = WINDOW // CHUNK
ROPE_THETA = 500000.0
ROT_DIM = HEAD_DIM // 4
D_INNER = 2 * D_MODEL
SSM_HEAD_DIM = 64
SSM_HEADS = D_INNER // SSM_HEAD_DIM
SSM_GROUPS = 8
D_STATE = 128
CONV_W = 4
BC_WIDTH = SSM_GROUPS * D_STATE
CONV_DIM = D_INNER + 2 * BC_WIDTH
N_EXPERTS = 32
TOP_K = 4
D_FF = D_MODEL
SWIGLU_LIMIT = 7.0
SWIGLU_ALPHA = 1.702
EPS = 1e-6
NEG_INF = -1e30
PAST_LEN = 2048

LANES = 128
VMEM_LIMIT = 56 * 1024 * 1024
EXPERT_VMEM_LIMIT = 60 * 1024 * 1024
MOE_TM = 512
MOE_TF = 256
CACHED_TILES = 2


def _pick(n, cands):
    for c in cands:
        if n % c == 0:
            return c
    return n


def _cparams(sem):
    return pltpu.CompilerParams(dimension_semantics=sem, vmem_limit_bytes=VMEM_LIMIT)


def _split2(x):
    hi = x.astype(BF16)
    lo = (x - hi.astype(F32)).astype(BF16)
    return hi, lo


def _split3(x):
    hi = x.astype(BF16)
    r = x - hi.astype(F32)
    mid = r.astype(BF16)
    lo = (r - mid.astype(F32)).astype(BF16)
    return hi, mid, lo


def _dot(a, b):
    return jnp.dot(a, b, preferred_element_type=F32)


def _dot_nt(a, b):
    return lax.dot_general(a, b, (((1,), (1,)), ((), ())), preferred_element_type=F32)


def _silu(x):
    half = 0.5 * x
    return half + half * jnp.tanh(half)


def _first_rows(n_first):
    return lambda i, *_: (jnp.minimum(i, n_first - 1), 0)


def _later_rows(n_first):
    return lambda i, *_: (jnp.maximum(i - n_first, 0), 0)


def _by_part(i, n_first, fn, first_refs, later_refs):
    @pl.when(i < n_first)
    def _():
        fn(*first_refs)

    @pl.when(i >= n_first)
    def _():
        fn(*later_refs)


def _rms_kernel(xa_ref, xb_ref, w_ref, o_ref, *, n_first):
    def body(x_ref):
        x = x_ref[...]
        ms = jnp.mean(x * x, axis=-1, keepdims=True)
        o_ref[...] = (x * lax.rsqrt(ms + EPS) * w_ref[...]).astype(o_ref.dtype)

    _by_part(pl.program_id(0), n_first, body, (xa_ref,), (xb_ref,))


def _rmsnorm(xa, xb, w, out_dtype):
    (ma, d), mb = xa.shape, xb.shape[0]
    tm = _pick(math.gcd(ma, mb), (512, 256, 128, 64, 32))
    nf = ma // tm
    return pl.pallas_call(
        functools.partial(_rms_kernel, n_first=nf),
        grid=((ma + mb) // tm,),
        in_specs=[pl.BlockSpec((tm, d), _first_rows(nf)), pl.BlockSpec((tm, d), _later_rows(nf)),
                  pl.BlockSpec((1, d), lambda i: (0, 0))],
        out_specs=pl.BlockSpec((tm, d), lambda i: (i, 0)),
        out_shape=jax.ShapeDtypeStruct((ma + mb, d), out_dtype),
        compiler_params=_cparams(("parallel",)),
        name="rmsnorm",
    )(xa, xb, w.reshape(1, d))


def _mm_kernel(x_ref, w_ref, o_ref, wb_ref):
    @pl.when(pl.program_id(1) == 0)
    def _():
        wb_ref[...] = w_ref[0].astype(BF16)

    o_ref[...] = _dot_nt(x_ref[...], wb_ref[...]).astype(o_ref.dtype)


def _matmul(x, wt, layer, row0, n, out_dtype, name):
    m, k = x.shape
    tm = _pick(m, (1024, 512, 256, 128, 64, 32))
    tn = _pick(math.gcd(n, row0), (1024, 512, 256, 128))
    rb0 = row0 // tn
    return pl.pallas_call(
        _mm_kernel,
        grid=(n // tn, m // tm),
        in_specs=[pl.BlockSpec((tm, k), lambda j, i: (i, 0)),
                  pl.BlockSpec((1, tn, k), lambda j, i: (layer, rb0 + j, 0))],
        out_specs=pl.BlockSpec((tm, tn), lambda j, i: (i, j)),
        out_shape=jax.ShapeDtypeStruct((m, n), out_dtype),
        scratch_shapes=[pltpu.VMEM((tn, k), BF16)],
        compiler_params=_cparams(("parallel", "arbitrary")),
        name=name,
    )(x, wt)


def _qk_prep_kernel(q_ref, k_ref, cos_ref, s1_ref, s2_ref, qn_ref, kn_ref, g_ref, gt_ref, qo_ref, ko_ref):
    cos = cos_ref[...]
    s1 = s1_ref[...]
    s2 = s2_ref[...]

    def norm_rope(x, nw, width):
        g = g_ref[0:width, :]
        gt = gt_ref[:, 0:width]
        sq_hi, sq_lo = _split2(x * x)
        ssum = _dot(sq_hi, g) + _dot(sq_lo, g)
        r = lax.rsqrt(ssum * (1.0 / HEAD_DIM) + EPS)
        r_hi, r_lo = _split2(r)
        y = x * (_dot(r_hi, gt) + _dot(r_lo, gt)) * nw
        outs = []
        for s in range(width // LANES):
            blk = y[:, s * LANES:(s + 1) * LANES]
            outs.append(blk * cos + pltpu.roll(blk, ROT_DIM // 2, 1) * s1
                        + pltpu.roll(blk, LANES - ROT_DIM // 2, 1) * s2)
        return jnp.concatenate(outs, axis=1)

    q = norm_rope(q_ref[...], qn_ref[...], ATTN_WIDTH)
    qo_ref[...] = (q * (HEAD_DIM ** -0.5)).astype(qo_ref.dtype)
    ko_ref[...] = norm_rope(k_ref[...], kn_ref[...], KV_WIDTH)


def _rope_tables(seq, dec_seq, tm):
    half = ROT_DIM // 2
    inv_freq = np.float32(ROPE_THETA) ** (-np.arange(half, dtype=np.float32) * np.float32(2.0) / np.float32(ROT_DIM))
    pos = np.concatenate([np.arange(seq), np.tile(PAST_LEN + np.arange(dec_seq), tm // dec_seq)]).astype(np.float32)
    ang = pos[:, None] * inv_freq.astype(np.float32)[None, :]
    cos, sin = np.cos(ang).astype(np.float32), np.sin(ang).astype(np.float32)
    rows = pos.shape[0]
    ones = np.ones((rows, HEAD_DIM - ROT_DIM), np.float32)
    zeros = np.zeros((rows, HEAD_DIM - ROT_DIM), np.float32)
    zh = np.zeros((rows, half), np.float32)
    reps = (1, LANES // HEAD_DIM)
    return (np.tile(np.concatenate([cos, cos, ones], axis=1), reps),
            np.tile(np.concatenate([zh, sin, zeros], axis=1), reps),
            np.tile(np.concatenate([-sin, zh, zeros], axis=1), reps))


def _qk_prep(q, k, q_norm, k_norm, n_prompt, seq, dec_seq):
    m = q.shape[0]
    tm = _pick(math.gcd(math.gcd(seq, m - n_prompt), 256), (256, 128, 64, 32))
    cos_t, s1_t, s2_t = (jnp.asarray(t) for t in _rope_tables(seq, dec_seq, tm))
    prompt_tiles, seq_tiles = n_prompt // tm, seq // tm
    pos_rows = pl.BlockSpec((tm, LANES), lambda i: (jnp.where(i < prompt_tiles, i % seq_tiles, seq_tiles), 0))
    head_of_lane = jnp.arange(ATTN_WIDTH) // HEAD_DIM
    g = (head_of_lane[:, None] == jnp.arange(LANES)[None, :]).astype(BF16)
    gt = g.T
    qn = jnp.tile(q_norm.astype(F32), N_HEADS).reshape(1, ATTN_WIDTH)
    kn = jnp.tile(k_norm.astype(F32), N_KV_HEADS).reshape(1, KV_WIDTH)
    row = lambda w: pl.BlockSpec((tm, w), lambda i: (i, 0))
    full = lambda a: pl.BlockSpec(a.shape, lambda i: (0, 0))
    return pl.pallas_call(
        _qk_prep_kernel,
        grid=(m // tm,),
        in_specs=[row(ATTN_WIDTH), row(KV_WIDTH), pos_rows, pos_rows, pos_rows,
                  full(qn), full(kn), full(g), full(gt)],
        out_specs=[row(ATTN_WIDTH), row(KV_WIDTH)],
        out_shape=[jax.ShapeDtypeStruct((m, ATTN_WIDTH), BF16), jax.ShapeDtypeStruct((m, KV_WIDTH), F32)],
        compiler_params=_cparams(("parallel",)),
        name="qk_prep",
    )(q, k, cos_t, s1_t, s2_t, qn, kn, g, gt)


def _attend(q, kk, vv, sinks_ref, valid):
    tq = q.shape[0]
    outs = []
    for j in range(N_KV_HEADS):
        kj = kk[:, j * HEAD_DIM:(j + 1) * HEAD_DIM]
        vj = vv[:, j * HEAD_DIM:(j + 1) * HEAD_DIM]
        heads = [Q_PER_KV * j + g for g in range(Q_PER_KV)]
        q4 = jnp.concatenate([q[:, h * HEAD_DIM:(h + 1) * HEAD_DIM] for h in heads], axis=0)
        s = _dot_nt(q4, kj)
        if valid is not None:
            s = jnp.where(valid, s, NEG_INF)
        sink = jnp.concatenate([jnp.full((tq, 1), sinks_ref[h], F32) for h in heads], axis=0)
        m = jnp.maximum(jnp.max(s, axis=-1, keepdims=True), sink)
        p = jnp.exp(s - m)
        denom = jnp.sum(p, axis=-1, keepdims=True) + jnp.exp(sink - m)
        o = _dot(p.astype(BF16), vj) / denom
        for g in range(Q_PER_KV):
            outs.append(o[g * tq:(g + 1) * tq, :])
    return jnp.concatenate(outs, axis=1)


def _attn_prompt_kernel(sinks_ref, q_ref, ka_ref, kb_ref, va_ref, vb_ref, o_ref, st_ref, pt_ref, rl_ref):
    i = pl.program_id(1)
    tq = q_ref.shape[0]
    tk = 2 * tq
    q = q_ref[...]
    k32 = jnp.concatenate([ka_ref[...], kb_ref[...]], axis=0)
    vt = jnp.concatenate([va_ref[...], vb_ref[...]], axis=0).T.astype(BF16)

    lane_k = lax.broadcasted_iota(I32, (tk, LANES), 1)
    kc = lax.broadcasted_iota(I32, (tk, LANES), 0) // CHUNK
    k_ind = jnp.where(lane_k == 0, jnp.where(kc == 0, 1.0, 0.0),
                      jnp.where(lane_k == 1, jnp.where(kc == N_PREV_CHUNKS + 1, 1.0, 0.0),
                                jnp.where(lane_k == 2, jnp.where(kc < N_PREV_CHUNKS, 1.0, 0.0), 0.0)))
    first_step = jnp.where(i == 0, 1.0, 0.0)
    k_ind = jnp.where(lane_k == 2, k_ind * first_step, k_ind).astype(BF16)
    lane_q = lax.broadcasted_iota(I32, (tq, LANES), 1)
    qc = lax.broadcasted_iota(I32, (tq, LANES), 0) // CHUNK
    q_msk = jnp.where(lane_q == 0, jnp.where(qc == 1, NEG_INF, 0.0),
                      jnp.where(lane_q == 1, jnp.where(qc == 0, NEG_INF, 0.0),
                                jnp.where(lane_q == 2, NEG_INF, 0.0))).astype(BF16)
    low = lane_k < HEAD_DIM

    for slab in range(KV_WIDTH // LANES):
        ks = k32[:, slab * LANES:(slab + 1) * LANES]
        kr = pltpu.roll(ks, HEAD_DIM, 1)
        for jj in range(2):
            j = 2 * slab + jj
            in_low = jnp.where(low, ks if jj == 0 else kr, 0.0).astype(BF16)
            in_high = jnp.where(low, 0.0, kr if jj == 0 else ks).astype(BF16)
            k_ext = (jnp.concatenate([in_low, k_ind], axis=1), jnp.concatenate([in_high, k_ind], axis=1))
            for g in range(Q_PER_KV):
                h = Q_PER_KV * j + g
                pair = h // 2
                q_ext = jnp.concatenate([q[:, pair * LANES:(pair + 1) * LANES], q_msk], axis=1)
                st_ref[h] = _dot_nt(k_ext[h % 2], q_ext)

    for h in range(N_HEADS):
        st = st_ref[h]
        sink = sinks_ref[h]
        m = jnp.maximum(jnp.max(st, axis=0, keepdims=True), sink)
        pt = jnp.exp(st - m)
        rl_ref[h:h + 1, :] = 1.0 / (jnp.sum(pt, axis=0, keepdims=True) + jnp.exp(sink - m))
        pt_ref[h] = pt.astype(BF16)

    for pair in range(N_HEADS // 2):
        pieces = []
        for h in (2 * pair, 2 * pair + 1):
            j = h // Q_PER_KV
            pieces.append(_dot(vt[j * HEAD_DIM:(j + 1) * HEAD_DIM, :], pt_ref[h]) * rl_ref[h:h + 1, :])
        o_ref[:, pair * LANES:(pair + 1) * LANES] = jnp.concatenate(pieces, axis=0).T.astype(o_ref.dtype)


def _attn_prompt(q, k, kv, sinks, bsz, seq):
    tq = 2 * CHUNK
    nb = seq // tq

    def kv_spec(back, col):
        return pl.BlockSpec((tq, KV_WIDTH), lambda b, i, s: (b * nb + jnp.maximum(i - back, 0), col))

    grid_spec = pltpu.PrefetchScalarGridSpec(
        num_scalar_prefetch=1,
        grid=(bsz, nb),
        in_specs=[pl.BlockSpec((tq, ATTN_WIDTH), lambda b, i, s: (b * nb + i, 0)),
                  kv_spec(1, 0), kv_spec(0, 0), kv_spec(1, 1), kv_spec(0, 1)],
        out_specs=pl.BlockSpec((tq, ATTN_WIDTH), lambda b, i, s: (b * nb + i, 0)),
        scratch_shapes=[pltpu.VMEM((N_HEADS, 2 * tq, tq), F32), pltpu.VMEM((N_HEADS, 2 * tq, tq), BF16),
                        pltpu.VMEM((N_HEADS, tq), F32)],
    )
    return pl.pallas_call(
        _attn_prompt_kernel,
        grid_spec=grid_spec,
        out_shape=jax.ShapeDtypeStruct((bsz * seq, ATTN_WIDTH), BF16),
        compiler_params=_cparams(("parallel", "parallel")),
        name="attn_prompt",
    )(sinks, q, k, k, kv, kv)


def _attn_sample_kernel(sinks_ref, q_ref, kn_ref, vn_ref, kc_ref, vc_ref, o_ref):
    kk = jnp.concatenate([kc_ref[0], kn_ref[...]], axis=0).astype(BF16)
    vv = jnp.concatenate([vc_ref[0], vn_ref[...]], axis=0).astype(BF16)
    o_ref[...] = _attend(q_ref[...], kk, vv, sinks_ref, None).astype(o_ref.dtype)


def _attn_sample(q, k, kv, cache_k, cache_v, sinks, row0, bsz, seq):
    blk0 = row0 // seq
    kv_len = cache_k.shape[1]
    new = lambda w, col: pl.BlockSpec((seq, w), lambda b, s: (blk0 + b, col))
    cache = pl.BlockSpec((1, kv_len, KV_WIDTH), lambda b, s: (b, 0, 0))
    grid_spec = pltpu.PrefetchScalarGridSpec(
        num_scalar_prefetch=1,
        grid=(bsz,),
        in_specs=[new(ATTN_WIDTH, 0), new(KV_WIDTH, 0), new(KV_WIDTH, 1), cache, cache],
        out_specs=pl.BlockSpec((seq, ATTN_WIDTH), lambda b, s: (b, 0)),
    )
    return pl.pallas_call(
        _attn_sample_kernel,
        grid_spec=grid_spec,
        out_shape=jax.ShapeDtypeStruct((bsz * seq, ATTN_WIDTH), BF16),
        compiler_params=_cparams(("parallel",)),
        name="attn_sample",
    )(sinks, q, k, kv, cache_k, cache_v)


def _pad_rows(x, rows):
    if x.shape[0] == rows:
        return x
    return jnp.concatenate([x, jnp.zeros((rows - x.shape[0], x.shape[1]), x.dtype)], axis=0)


def _ssd_kernel(*refs, clen, has_past):
    if has_past:
        (xbc_ref, dt_ref, z_ref, cw_ref, cb_ref, dtb_ref, alog_ref, dskip_ref, nw_ref, cpast_ref, hpast_ref,
         y_ref, hout_ref, xpad, u_ref, g_ref, ht_ref, ca_ref, xdt_ref, yst_ref, xw_ref, cb2_ref) = refs
    else:
        (xbc_ref, dt_ref, z_ref, cw_ref, cb_ref, dtb_ref, alog_ref, dskip_ref, nw_ref,
         y_ref, hout_ref, xpad, u_ref, g_ref, ht_ref, ca_ref, xdt_ref, yst_ref, xw_ref, cb2_ref) = refs
    L = clen
    c = pl.program_id(1)
    nc = pl.num_programs(1)
    n_pairs = SSM_HEADS // 2
    pairs_per_group = n_pairs // SSM_GROUPS
    gw = D_INNER // SSM_GROUPS

    @pl.when(c == 0)
    def _init():
        if has_past:
            for cidx in range(CONV_DIM // LANES):
                xpad[cidx, 5:8, :] = cpast_ref[0, :, cidx * LANES:(cidx + 1) * LANES]
            for p in range(n_pairs):
                both = jnp.concatenate([hpast_ref[0, 2 * p], hpast_ref[0, 2 * p + 1]], axis=0)
                ht_ref[:, p * LANES:(p + 1) * LANES] = both.T
        else:
            xpad[:, 0:8, :] = jnp.zeros((CONV_DIM // LANES, 8, LANES), F32)
            ht_ref[...] = jnp.zeros(ht_ref.shape, F32)

    n_col = CONV_DIM // LANES
    for cidx in range(n_col):
        xpad[cidx, 8:8 + L, :] = xbc_ref[:, cidx * LANES:(cidx + 1) * LANES]
    for cidx in range(n_col):
        sl = slice(cidx * LANES, (cidx + 1) * LANES)
        acc = cb_ref[:, sl] + xpad[cidx, 8:8 + L, :] * cw_ref[3:4, sl]
        acc = acc + xpad[cidx, 7:7 + L, :] * cw_ref[2:3, sl]
        acc = acc + xpad[cidx, 6:6 + L, :] * cw_ref[1:2, sl]
        acc = acc + xpad[cidx, 5:5 + L, :] * cw_ref[0:1, sl]
        u_ref[:, sl] = _silu(acc)
    for cidx in range(n_col):
        xpad[cidx, 5:8, :] = xpad[cidx, 5 + L:8 + L, :]

    dtx = dt_ref[...] + dtb_ref[...]
    dt = jnp.maximum(dtx, 0.0) + jnp.log(1.0 + jnp.exp(-jnp.abs(dtx)))
    loga = dt * (-jnp.exp(alog_ref[...]))
    t_idx = lax.broadcasted_iota(I32, (L, L), 0)
    s_idx = lax.broadcasted_iota(I32, (L, L), 1)
    incl = (s_idx <= t_idx).astype(BF16)
    acum = sum(_dot(incl, part) for part in _split3(loga))
    acum_t = _pad_rows(acum, LANES).T[:, 0:L]

    lane_m = lax.broadcasted_iota(I32, (L, 2 * L), 1)
    row_m = lax.broadcasted_iota(I32, (L, 2 * L), 0)
    left_m = lane_m < L
    causal_m = jnp.where(left_m, lane_m, lane_m - L) <= row_m
    left = lax.broadcasted_iota(I32, (L, LANES), 1) < SSM_HEAD_DIM
    b_of = lambda grp: u_ref[:, D_INNER + grp * D_STATE:D_INNER + (grp + 1) * D_STATE]
    c_of = lambda grp: u_ref[:, D_INNER + BC_WIDTH + grp * D_STATE:D_INNER + BC_WIDTH + (grp + 1) * D_STATE]
    pair_lanes = lambda p: slice(p * LANES, (p + 1) * LANES)
    group_lanes = lambda grp: slice(grp * gw, (grp + 1) * gw)

    for p in range(n_pairs):
        h0, h1, sl = 2 * p, 2 * p + 1, pair_lanes(p)
        ca_ref[:, sl] = jnp.where(left, acum[:, h0:h0 + 1], acum[:, h1:h1 + 1])
        xdt_ref[:, sl] = u_ref[:, sl] * jnp.where(left, dt[:, h0:h0 + 1], dt[:, h1:h1 + 1])

    for grp in range(SSM_GROUPS):
        c_bf = c_of(grp).astype(BF16)
        cb = _dot_nt(c_bf, b_of(grp).astype(BF16))
        cb2_ref[grp] = jnp.concatenate([cb, cb], axis=1)
        yst_ref[:, group_lanes(grp)] = _dot(c_bf, ht_ref[:, group_lanes(grp)].astype(BF16))

    for p in range(n_pairs):
        h0, h1, sl = 2 * p, 2 * p + 1, pair_lanes(p)
        ca = ca_ref[:, sl]
        ca_m = ca if L == SSM_HEAD_DIM else jnp.where(left_m, acum[:, h0:h0 + 1], acum[:, h1:h1 + 1])
        row_a = jnp.concatenate([acum_t[h0:h0 + 1, :], acum_t[h1:h1 + 1, :]], axis=1)
        dec = jnp.exp(jnp.where(causal_m, ca_m - row_a, NEG_INF))
        m_pair = (cb2_ref[p // pairs_per_group] * dec).astype(BF16)
        xdt = xdt_ref[:, sl]
        x_bd = jnp.concatenate([jnp.where(left, xdt, 0.0), jnp.where(left, 0.0, xdt)], axis=0).astype(BF16)
        y = _dot(m_pair, x_bd) + yst_ref[:, sl] * jnp.exp(ca) + u_ref[:, sl] * dskip_ref[:, sl]
        g_ref[:, sl] = y * _silu(z_ref[:, sl].astype(F32))
        xw_ref[:, sl] = (xdt * jnp.exp(ca[L - 1:L, :] - ca)).astype(BF16)

    for grp in range(SSM_GROUPS):
        gsl = group_lanes(grp)
        gg = g_ref[:, gsl]
        ms = jnp.mean(gg * gg, axis=-1, keepdims=True)
        y_ref[:, gsl] = (gg * lax.rsqrt(ms + EPS) * nw_ref[:, gsl]).astype(y_ref.dtype)
        bt_bf = _pad_rows(b_of(grp), LANES).T[:, 0:L].astype(BF16)
        ht_ref[:, gsl] = ht_ref[:, gsl] * jnp.exp(ca_ref[L - 1:L, gsl]) + _dot(bt_bf, xw_ref[:, gsl])

    @pl.when(c == nc - 1)
    def _fin():
        for p in range(n_pairs):
            both = ht_ref[:, p * LANES:(p + 1) * LANES].T
            hout_ref[0, 2 * p] = both[0:SSM_HEAD_DIM, :]
            hout_ref[0, 2 * p + 1] = both[SSM_HEAD_DIM:2 * SSM_HEAD_DIM, :]


def _ssd(xbc, dt_raw, z, conv_w, conv_b, dt_bias, a_log, d_skip, ssm_norm, row0, bsz, seq, clen,
         conv_past=None, ssm_past=None):
    nc = seq // clen
    blk0 = row0 // clen
    has_past = conv_past is not None
    pad = lambda a: jnp.pad(a.astype(F32), (0, LANES - SSM_HEADS)).reshape(1, LANES)
    params = [conv_w.astype(F32), conv_b.astype(F32).reshape(1, CONV_DIM), pad(dt_bias), pad(a_log),
              jnp.repeat(d_skip.astype(F32), SSM_HEAD_DIM).reshape(1, D_INNER), ssm_norm.astype(F32).reshape(1, D_INNER)]
    rows = lambda w: pl.BlockSpec((clen, w), lambda b, c: (blk0 + b * nc + c, 0))
    full = lambda a: pl.BlockSpec(a.shape, lambda b, c: (0,) * a.ndim)
    in_specs = [rows(CONV_DIM), rows(LANES), rows(D_INNER)] + [full(a) for a in params]
    args = [xbc, dt_raw, z] + params
    if has_past:
        in_specs += [pl.BlockSpec((1, CONV_W - 1, CONV_DIM), lambda b, c: (b, 0, 0)),
                     pl.BlockSpec((1, SSM_HEADS, SSM_HEAD_DIM, D_STATE), lambda b, c: (b, 0, 0, 0))]
        args += [conv_past.astype(F32), ssm_past.astype(F32)]
    return pl.pallas_call(
        functools.partial(_ssd_kernel, clen=clen, has_past=has_past),
        grid=(bsz, nc),
        in_specs=in_specs,
        out_specs=[pl.BlockSpec((clen, D_INNER), lambda b, c: (b * nc + c, 0)),
                   pl.BlockSpec((1, SSM_HEADS, SSM_HEAD_DIM, D_STATE), lambda b, c: (b, 0, 0, 0))],
        out_shape=[jax.ShapeDtypeStruct((bsz * seq, D_INNER), BF16),
                   jax.ShapeDtypeStruct((bsz, SSM_HEADS, SSM_HEAD_DIM, D_STATE), F32)],
        scratch_shapes=[pltpu.VMEM((CONV_DIM // LANES, 8 + clen, LANES), F32), pltpu.VMEM((clen, CONV_DIM), F32),
                        pltpu.VMEM((clen, D_INNER), F32), pltpu.VMEM((D_STATE, D_INNER), F32),
                        pltpu.VMEM((clen, D_INNER), F32), pltpu.VMEM((clen, D_INNER), F32),
                        pltpu.VMEM((clen, D_INNER), F32), pltpu.VMEM((clen, D_INNER), BF16),
                        pltpu.VMEM((SSM_GROUPS, clen, 2 * clen), F32)],
        compiler_params=_cparams(("parallel", "arbitrary")),
        name="ssd_past" if has_past else "ssd_prompt",
    )(*args)


def _merge_kernel(h_ref, a1_ref, a2_ref, s1_ref, s2_ref, wga_ref, wgs_ref, bga_ref, bgs_ref, wa_ref, ws_ref, o_ref,
                  *, n_first):
    def body(a_ref, s_ref):
        h = h_ref[...]
        g_a = jax.nn.sigmoid(_dot(h, wga_ref[...]) + bga_ref[...])
        g_s = jax.nn.sigmoid(_dot(h, wgs_ref[...]) + bgs_ref[...])
        o_ref[...] = (g_a * _dot(a_ref[...], wa_ref[...]) + g_s * _dot(s_ref[...], ws_ref[...])).astype(o_ref.dtype)

    _by_part(pl.program_id(0), n_first, body, (a1_ref, s1_ref), (a2_ref, s2_ref))


def _merge(h, attn_parts, ssm_parts, w_gate, b_gate, w_attn_up, w_ssm_up):
    m = h.shape[0]
    m1, m2 = attn_parts[0].shape[0], attn_parts[1].shape[0]
    tm = _pick(math.gcd(m1, m2), (512, 256, 128, 64, 32))
    nf = m1 // tm
    tn = 512
    nj = D_MODEL // tn
    bg = b_gate.astype(F32).reshape(1, 2 * D_MODEL)
    return pl.pallas_call(
        functools.partial(_merge_kernel, n_first=nf),
        grid=(m // tm, nj),
        in_specs=[pl.BlockSpec((tm, D_MODEL), lambda i, j: (i, 0)),
                  pl.BlockSpec((tm, ATTN_WIDTH), _first_rows(nf)), pl.BlockSpec((tm, ATTN_WIDTH), _later_rows(nf)),
                  pl.BlockSpec((tm, D_INNER), _first_rows(nf)), pl.BlockSpec((tm, D_INNER), _later_rows(nf)),
                  pl.BlockSpec((D_MODEL, tn), lambda i, j: (0, j)),
                  pl.BlockSpec((D_MODEL, tn), lambda i, j: (0, j + nj)),
                  pl.BlockSpec((1, tn), lambda i, j: (0, j)),
                  pl.BlockSpec((1, tn), lambda i, j: (0, j + nj)),
                  pl.BlockSpec((ATTN_WIDTH, tn), lambda i, j: (0, j)),
                  pl.BlockSpec((D_INNER, tn), lambda i, j: (0, j))],
        out_specs=pl.BlockSpec((tm, tn), lambda i, j: (i, j)),
        out_shape=jax.ShapeDtypeStruct((m, D_MODEL), BF16),
        compiler_params=_cparams(("parallel", "parallel")),
        name="merge",
    )(h, attn_parts[0], attn_parts[1], ssm_parts[0], ssm_parts[1], w_gate, w_gate, bg, bg, w_attn_up, w_ssm_up)


def _outproj_router_kernel(mg_ref, wo_ref, xa_ref, xb_ref, n2_ref, wr_ref, br_ref, x1_ref, h2_ref, idx_ref, gate_ref,
                           *, n_first):
    def residual(x_ref):
        x1_ref[...] = x_ref[...] + _dot(mg_ref[...], wo_ref[...])

    _by_part(pl.program_id(0), n_first, residual, (xa_ref,), (xb_ref,))
    x1 = x1_ref[...]
    ms = jnp.mean(x1 * x1, axis=-1, keepdims=True)
    h2 = x1 * lax.rsqrt(ms + EPS) * n2_ref[...]
    h2_ref[...] = _pack_bf16_pairs(h2)
    h_hi, h_lo = _split2(h2)
    w_hi, w_lo = _split2(wr_ref[...])
    logits = _dot(h_hi, w_hi) + (_dot(h_hi, w_lo) + _dot(h_lo, w_hi)) + br_ref[...]
    tm = logits.shape[0]
    lane = lax.broadcasted_iota(I32, (tm, LANES), 1)
    logits = jnp.where(lane < N_EXPERTS, logits, -jnp.inf)
    idx_out = jnp.zeros((tm, LANES), I32)
    val_out = jnp.zeros((tm, LANES), F32)
    top = None
    for k in range(TOP_K):
        v = jnp.max(logits, axis=-1, keepdims=True)
        i = jnp.min(jnp.where(logits == v, lane, LANES), axis=-1, keepdims=True)
        if k == 0:
            top = v
        idx_out = jnp.where(lane == k, i, idx_out)
        val_out = jnp.where(lane == k, jnp.exp(v - top), val_out)
        logits = jnp.where(lane == i, -jnp.inf, logits)
    idx_ref[...] = idx_out
    gate_ref[...] = val_out / jnp.sum(val_out, axis=-1, keepdims=True)


def _outproj_router(merged, w_out, xa, xb, norm2, w_router, b_router):
    m = merged.shape[0]
    tm = _pick(math.gcd(xa.shape[0], xb.shape[0]), (256, 128, 64, 32))
    nf = xa.shape[0] // tm
    wr = jnp.pad(w_router.astype(F32), ((0, 0), (0, LANES - N_EXPERTS)))
    br = jnp.pad(b_router.astype(F32), (0, LANES - N_EXPERTS)).reshape(1, LANES)
    row = lambda w: pl.BlockSpec((tm, w), lambda i: (i, 0))
    full = lambda r, c: pl.BlockSpec((r, c), lambda i: (0, 0))
    return pl.pallas_call(
        functools.partial(_outproj_router_kernel, n_first=nf),
        grid=(m // tm,),
        in_specs=[row(D_MODEL), full(D_MODEL, D_MODEL),
                  pl.BlockSpec((tm, D_MODEL), _first_rows(nf)), pl.BlockSpec((tm, D_MODEL), _later_rows(nf)),
                  full(1, D_MODEL), full(D_MODEL, LANES), full(1, LANES)],
        out_specs=[row(D_MODEL), row(D_MODEL // 2), row(LANES), row(LANES)],
        out_shape=[jax.ShapeDtypeStruct((m, D_MODEL), F32), jax.ShapeDtypeStruct((m, D_MODEL // 2), I32),
                   jax.ShapeDtypeStruct((m, LANES), I32), jax.ShapeDtypeStruct((m, LANES), F32)],
        compiler_params=_cparams(("parallel",)),
        name="outproj_router",
    )(merged, w_out, xa, xb, norm2.astype(F32).reshape(1, D_MODEL), wr, br)


def _rank_kernel(idx_ref, rank_ref, cnt_ref, base_ref):
    i = pl.program_id(0)
    tt = idx_ref.shape[0]

    @pl.when(i == 0)
    def _():
        base_ref[...] = jnp.zeros(base_ref.shape, F32)

    idx = idx_ref[...]
    lane = lax.broadcasted_iota(I32, (tt, LANES), 1)
    sel = [lane == idx[:, k:k + 1] for k in range(TOP_K)]
    onehot = jnp.zeros((tt, LANES), F32)
    for k in range(TOP_K):
        onehot = jnp.where(sel[k], 1.0, onehot)
    r_idx = lax.broadcasted_iota(I32, (tt, tt), 0)
    c_idx = lax.broadcasted_iota(I32, (tt, tt), 1)
    before = (c_idx < r_idx).astype(BF16)
    rank_all = _dot(before, onehot.astype(BF16)) + base_ref[0:1, :]
    out = jnp.zeros((tt, LANES), F32)
    for k in range(TOP_K):
        out = jnp.where(lane == k, jnp.sum(jnp.where(sel[k], rank_all, 0.0), axis=-1, keepdims=True), out)
    rank_ref[...] = out.astype(I32)
    base_ref[0:1, :] = base_ref[0:1, :] + jnp.sum(onehot, axis=0, keepdims=True)
    cnt_ref[...] = base_ref[...].astype(I32)


def _ranks(idx):
    m = idx.shape[0]
    tt = _pick(m, (512, 256, 128, 64, 32))
    return pl.pallas_call(
        _rank_kernel,
        grid=(m // tt,),
        in_specs=[pl.BlockSpec((tt, LANES), lambda i: (i, 0))],
        out_specs=[pl.BlockSpec((tt, LANES), lambda i: (i, 0)), pl.BlockSpec((8, LANES), lambda i: (0, 0))],
        out_shape=[jax.ShapeDtypeStruct((m, LANES), I32), jax.ShapeDtypeStruct((8, LANES), I32)],
        scratch_shapes=[pltpu.VMEM((8, LANES), F32)],
        compiler_params=_cparams(("arbitrary",)),
        name="expert_ranks",
    )(idx)


def _dispatch_kernel(dest_ref, fill_ref, h_ref, xs_hbm, zero_ref, sem, zsem, *, tt, tm, n_blocks):
    @pl.when(pl.program_id(0) == 0)
    def _():
        zero_ref[...] = jnp.zeros(zero_ref.shape, zero_ref.dtype)

        def block_copy(b):
            return pltpu.make_async_copy(zero_ref, xs_hbm.at[pl.ds(pl.multiple_of(b * tm, tm), tm), :], zsem)

        def start(b, carry):
            @pl.when(fill_ref[b] != 0)
            def _():
                block_copy(b).start()

            return carry

        def done(b, carry):
            @pl.when(fill_ref[b] != 0)
            def _():
                block_copy(b).wait()

            return carry

        lax.fori_loop(0, n_blocks, start, 0)
        lax.fori_loop(0, n_blocks, done, 0)

    base = pl.program_id(0) * (tt * TOP_K)

    def issue(r, carry):
        for k in range(TOP_K):
            dst = dest_ref[base + r * TOP_K + k]
            pltpu.make_async_copy(h_ref.at[pl.ds(r, 1), :], xs_hbm.at[pl.ds(dst, 1), :], sem).start()
        return carry

    lax.fori_loop(0, tt, issue, 0, unroll=True)
    for _ in range(TOP_K):
        pltpu.make_async_copy(h_ref, xs_hbm.at[pl.ds(0, tt), :], sem).wait()


def _dispatch(dest_flat, fill_block, h2, tm):
    m = h2.shape[0]
    n_blocks = fill_block.shape[0]
    tt = _pick(m, (256, 128, 64, 32))
    width = h2.shape[1]
    grid_spec = pltpu.PrefetchScalarGridSpec(
        num_scalar_prefetch=2,
        grid=(m // tt,),
        in_specs=[pl.BlockSpec((tt, width), lambda i, d, f: (i, 0))],
        out_specs=pl.BlockSpec(memory_space=pl.ANY),
        scratch_shapes=[pltpu.VMEM((tm, width), h2.dtype), pltpu.SemaphoreType.DMA, pltpu.SemaphoreType.DMA],
    )
    return pl.pallas_call(
        functools.partial(_dispatch_kernel, tt=tt, tm=tm, n_blocks=n_blocks),
        grid_spec=grid_spec,
        out_shape=jax.ShapeDtypeStruct((n_blocks * tm, width), h2.dtype),
        compiler_params=_cparams(("arbitrary",)),
        name="dispatch",
    )(dest_flat, fill_block, h2)


def _pack_bf16_pairs(x):
    c = x.shape[1] // 2
    bits = lax.bitcast_convert_type(x.astype(BF16).astype(F32), I32)
    return lax.shift_right_logical(bits[:, :c], 16) | bits[:, c:]


def _unpack_bf16_pairs(p):
    lo = lax.bitcast_convert_type(lax.shift_left(p, 16), F32).astype(BF16)
    hi = lax.bitcast_convert_type(p & jnp.int32(-65536), F32).astype(BF16)
    return lo, hi


def _expert_kernel(be_ref, nu_ref, xs_ref, bg_ref, bu_ref, bd_ref, wg_hbm, wu_hbm, wd_hbm, ys_ref,
                   cg, cu, cd, stg_g, stg_u, stg_d, xb_ref, sems, *, tf):
    i = pl.program_id(0)
    nj = D_FF // tf
    half = D_MODEL // 2
    e = be_ref[i]
    first = jnp.logical_or(i == 0, e != be_ref[jnp.maximum(i - 1, 0)])
    e_next = be_ref[jnp.minimum(i + 1, pl.num_programs(0) - 1)]
    next_is_new = jnp.logical_and(i + 1 < nu_ref[0], e_next != e)

    def tile_copies(t, slot, expert=e):
        cols = pl.ds(pl.multiple_of(t * tf, tf), tf)
        return (pltpu.make_async_copy(wg_hbm.at[expert, :, cols], stg_g.at[slot], sems.at[slot, 0]),
                pltpu.make_async_copy(wu_hbm.at[expert, :, cols], stg_u.at[slot], sems.at[slot, 1]),
                pltpu.make_async_copy(wd_hbm.at[expert, cols, :], stg_d.at[slot], sems.at[slot, 2]))

    @pl.when(i < nu_ref[0])
    def _():
        @pl.when(i == 0)
        def _():
            for cp in tile_copies(0, 0):
                cp.start()

        lo, hi = _unpack_bf16_pairs(xs_ref[...])
        xb_ref[:, :half] = lo
        xb_ref[:, half:] = hi

        def refill(t):
            slot = t % 2

            @pl.when(t + 1 < nj)
            def _():
                for cp in tile_copies(t + 1, 1 - slot):
                    cp.start()

            for cp in tile_copies(t, slot):
                cp.wait()
            cg[t] = stg_g[slot].astype(BF16)
            cu[t] = stg_u[slot].astype(BF16)
            cd[t] = stg_d[slot].astype(BF16)

        def compute(t, opening):
            xb = xb_ref[...]
            g = jnp.minimum(_dot(xb, cg[t]) + bg_ref[0, t], SWIGLU_LIMIT)
            u = jnp.clip(_dot(xb, cu[t]) + bu_ref[0, t], -SWIGLU_LIMIT, SWIGLU_LIMIT)
            act = (u + 1.0) * (g * jax.nn.sigmoid(SWIGLU_ALPHA * g))
            part = _dot(act.astype(BF16), cd[t])
            if opening:
                ys_ref[...] = part + bd_ref[0]
            else:
                ys_ref[...] += part

        @pl.when(first)
        def _():
            refill(jnp.int32(0))
            compute(0, True)

            def step(t, carry):
                refill(t)
                compute(t, False)
                return carry

            lax.fori_loop(1, nj, step, 0)

        def compute_wide(t, opening):
            xb = xb_ref[...]
            span = range(t, t + CACHED_TILES)
            wide = lambda ref: jnp.concatenate([ref[k] for k in span], axis=1)
            bias = lambda ref: jnp.concatenate([ref[0, k] for k in span], axis=1)
            g = jnp.minimum(_dot(xb, wide(cg)) + bias(bg_ref), SWIGLU_LIMIT)
            u = jnp.clip(_dot(xb, wide(cu)) + bias(bu_ref), -SWIGLU_LIMIT, SWIGLU_LIMIT)
            act = (u + 1.0) * (g * jax.nn.sigmoid(SWIGLU_ALPHA * g))
            part = _dot(act.astype(BF16), jnp.concatenate([cd[k] for k in span], axis=0))
            if opening:
                ys_ref[...] = part + bd_ref[0]
            else:
                ys_ref[...] += part

        @pl.when(jnp.logical_not(first))
        def _():
            for t in range(0, nj, CACHED_TILES):
                compute_wide(t, t == 0)

        @pl.when(next_is_new)
        def _():
            for cp in tile_copies(0, 0, e_next):
                cp.start()

    @pl.when(i >= nu_ref[0])
    def _():
        ys_ref[...] = jnp.zeros(ys_ref.shape, F32)


def _experts(block_e, n_used, xs, w_e_gate, b_e_gate, w_e_up, b_e_up, w_e_down, b_e_down):
    n_rows = xs.shape[0]
    tm, tf = MOE_TM, MOE_TF
    n_blocks = n_rows // tm
    nj = D_FF // tf
    half = D_MODEL // 2

    def blk(i, nu):
        return jnp.minimum(i, nu[0] - 1)

    hbm = pl.BlockSpec(memory_space=pl.ANY)
    grid_spec = pltpu.PrefetchScalarGridSpec(
        num_scalar_prefetch=2,
        grid=(n_blocks,),
        in_specs=[pl.BlockSpec((tm, half), lambda i, be, nu: (blk(i, nu), 0)),
                  pl.BlockSpec((1, nj, 1, tf), lambda i, be, nu: (be[blk(i, nu)], 0, 0, 0)),
                  pl.BlockSpec((1, nj, 1, tf), lambda i, be, nu: (be[blk(i, nu)], 0, 0, 0)),
                  pl.BlockSpec((1, 1, D_MODEL), lambda i, be, nu: (be[blk(i, nu)], 0, 0)),
                  hbm, hbm, hbm],
        out_specs=pl.BlockSpec((tm, D_MODEL), lambda i, be, nu: (i, 0)),
        scratch_shapes=[pltpu.VMEM((nj, D_MODEL, tf), BF16), pltpu.VMEM((nj, D_MODEL, tf), BF16),
                        pltpu.VMEM((nj, tf, D_MODEL), BF16),
                        pltpu.VMEM((2, D_MODEL, tf), F32), pltpu.VMEM((2, D_MODEL, tf), F32),
                        pltpu.VMEM((2, tf, D_MODEL), F32),
                        pltpu.VMEM((tm, D_MODEL), BF16), pltpu.SemaphoreType.DMA((2, 3))],
    )
    return pl.pallas_call(
        functools.partial(_expert_kernel, tf=tf),
        grid_spec=grid_spec,
        out_shape=jax.ShapeDtypeStruct((n_rows, D_MODEL), F32),
        compiler_params=pltpu.CompilerParams(dimension_semantics=("arbitrary",), vmem_limit_bytes=EXPERT_VMEM_LIMIT),
        name="experts",
    )(block_e, n_used, xs, b_e_gate.reshape(N_EXPERTS, nj, 1, tf), b_e_up.reshape(N_EXPERTS, nj, 1, tf),
      b_e_down.reshape(N_EXPERTS, 1, D_MODEL), w_e_gate, w_e_up, w_e_down)


def _combine_kernel(dest_ref, x1_ref, gate_ref, ys_hbm, o_ref, buf, sems, *, tt, row0):
    i = pl.program_id(0)
    n_steps = pl.num_programs(0)
    slot = i % 2

    def gather(step, to_slot):
        base = (row0 + step * tt) * TOP_K

        def issue(r, carry):
            for k in range(TOP_K):
                src = dest_ref[base + r * TOP_K + k]
                pltpu.make_async_copy(ys_hbm.at[pl.ds(src, 1), :], buf.at[to_slot, k, pl.ds(r, 1), :],
                                      sems.at[to_slot]).start()
            return carry

        lax.fori_loop(0, tt, issue, 0, unroll=True)

    @pl.when(i == 0)
    def _():
        gather(0, 0)

    @pl.when(i + 1 < n_steps)
    def _():
        gather(i + 1, 1 - slot)

    for k in range(TOP_K):
        pltpu.make_async_copy(ys_hbm.at[pl.ds(0, tt), :], buf.at[slot, k], sems.at[slot]).wait()
    gate = gate_ref[...]
    acc = x1_ref[...]
    for k in range(TOP_K):
        acc = acc + gate[:, k:k + 1] * buf[slot, k]
    o_ref[...] = acc


def _combine(dest_flat, x1, gate, ys, row0, rows):
    tt = _pick(math.gcd(row0, rows), (128, 64, 32))
    blk0 = row0 // tt
    grid_spec = pltpu.PrefetchScalarGridSpec(
        num_scalar_prefetch=1,
        grid=(rows // tt,),
        in_specs=[pl.BlockSpec((tt, D_MODEL), lambda i, d: (blk0 + i, 0)),
                  pl.BlockSpec((tt, LANES), lambda i, d: (blk0 + i, 0)),
                  pl.BlockSpec(memory_space=pl.ANY)],
        out_specs=pl.BlockSpec((tt, D_MODEL), lambda i, d: (i, 0)),
        scratch_shapes=[pltpu.VMEM((2, TOP_K, tt, D_MODEL), F32), pltpu.SemaphoreType.DMA((2,))],
    )
    return pl.pallas_call(
        functools.partial(_combine_kernel, tt=tt, row0=row0),
        grid_spec=grid_spec,
        out_shape=jax.ShapeDtypeStruct((rows, D_MODEL), F32),
        compiler_params=_cparams(("arbitrary",)),
        name="combine",
    )(dest_flat, x1, gate, ys)


def _moe(x1, h2, idx, gate, w_e_gate, b_e_gate, w_e_up, b_e_up, w_e_down, b_e_down, n_first):
    m = x1.shape[0]
    tm = MOE_TM
    ranks, cnt = _ranks(idx)
    counts = cnt[0, :N_EXPERTS]
    padded = (counts + tm - 1) // tm * tm
    pad_end = jnp.cumsum(padded)
    pad_start = pad_end - padded
    e_sel = idx[:, :TOP_K]
    dest = (pad_start[e_sel] + ranks[:, :TOP_K]).astype(I32).reshape(-1)
    n_blocks = -(-(m * TOP_K + N_EXPERTS * (tm - 1)) // tm)
    block_row = jnp.arange(n_blocks, dtype=I32) * tm
    block_e = jnp.minimum(jnp.sum((pad_end[None, :] <= block_row[:, None]).astype(I32), axis=1), N_EXPERTS - 1)
    n_used = (pad_end[-1] // tm).astype(I32).reshape(1)
    is_last = jnp.sum((pad_end[None, :] == block_row[:, None] + tm).astype(I32), axis=1) > 0
    fill_block = jnp.logical_or(is_last, block_row >= pad_end[-1]).astype(I32)
    xs = _dispatch(dest, fill_block, h2, tm)
    ys = _experts(block_e, n_used, xs, w_e_gate, b_e_gate, w_e_up, b_e_up, w_e_down, b_e_down)
    return _combine(dest, x1, gate, ys, 0, n_first), _combine(dest, x1, gate, ys, n_first, m - n_first)


def kernel(x_prompt, x_sample, cache_k, cache_v, state_conv, state_ssm, norm1, w_in, q_norm, k_norm, sinks, conv_w,
           conv_b, dt_bias, a_log, d_skip, ssm_norm, w_gate, b_gate, w_attn_up, w_ssm_up, w_out, norm2, w_router,
           b_router, w_e_gate, b_e_gate, w_e_up, b_e_up, w_e_down, b_e_down):
    bp, sp, _ = x_prompt.shape
    bs, ss, _ = x_sample.shape
    tp, ts = bp * sp, bs * ss
    kv_len = cache_k.shape[2]
    l = 0

    xa, xb = x_prompt.reshape(tp, D_MODEL), x_sample.reshape(ts, D_MODEL)

    o1 = ATTN_WIDTH
    o2 = o1 + 2 * KV_WIDTH
    o3 = o2 + D_INNER
    o4 = o3 + CONV_DIM
    h = _rmsnorm(xa, xb, norm1[l], BF16)
    w_in_t = jnp.swapaxes(w_in, 1, 2)
    q_p = _matmul(h, w_in_t, l, 0, o1, F32, "proj_q")
    kv_p = _matmul(h, w_in_t, l, o1, o2 - o1, F32, "proj_kv")
    z = _matmul(h, w_in_t, l, o2, o3 - o2, BF16, "proj_z")
    xbc = _matmul(h, w_in_t, l, o3, o4 - o3, F32, "proj_xbc")
    w_dt = jnp.pad(w_in_t[l:l + 1, o4:, :], ((0, 0), (0, LANES - SSM_HEADS), (0, 0)))
    dt_raw = _matmul(h, w_dt, 0, 0, LANES, F32, "proj_dt")

    q_rot, k_rot = _qk_prep(q_p, kv_p, q_norm[l], k_norm[l], tp, sp, ss)
    sk = sinks[l].astype(F32)
    attn_p = _attn_prompt(q_rot, k_rot, kv_p, sk, bp, sp)
    ck = cache_k[l].reshape(bs, kv_len, KV_WIDTH)
    cv = cache_v[l].reshape(bs, kv_len, KV_WIDTH)
    attn_s = _attn_sample(q_rot, k_rot, kv_p, ck, cv, sk, tp, bs, ss)

    ssd_w = (conv_w[l], conv_b[l], dt_bias[l], a_log[l], d_skip[l], ssm_norm[l])
    ssm_p, hfin_p = _ssd(xbc, dt_raw, z, *ssd_w, 0, bp, sp, CHUNK)
    ssm_s, hfin_s = _ssd(xbc, dt_raw, z, *ssd_w, tp, bs, ss, ss, state_conv[l], state_ssm[l])

    merged = _merge(h, (attn_p, attn_s), (ssm_p, ssm_s), w_gate[l].astype(BF16), b_gate[l], w_attn_up[l].astype(BF16), w_ssm_up[l].astype(BF16))
    x1, h2, idx, gate = _outproj_router(merged, w_out[l].astype(BF16), xa, xb, norm2[l], w_router[l], b_router[l])
    y_p, y_s = _moe(x1, h2, idx, gate, w_e_gate[l], b_e_gate[l], w_e_up[l], b_e_up[l], w_e_down[l], b_e_down[l], tp)

    def prompt_tail(a, n):
        return jnp.stack([a[(b + 1) * sp - n:(b + 1) * sp] for b in range(bp)])

    heads = lambda a: a.reshape(a.shape[0], a.shape[1], N_KV_HEADS, HEAD_DIM)
    new_k_p = heads(prompt_tail(k_rot, WINDOW))
    new_v_p = heads(prompt_tail(kv_p, WINDOW)[:, :, KV_WIDTH:])
    ks4 = heads(k_rot[tp:].reshape(bs, ss, KV_WIDTH))
    vs4 = heads(kv_p[tp:, KV_WIDTH:].reshape(bs, ss, KV_WIDTH))
    new_k_s = jnp.concatenate([cache_k[l], ks4], axis=1)[:, -kv_len:]
    new_v_s = jnp.concatenate([cache_v[l], vs4], axis=1)[:, -kv_len:]
    xbc_s = jnp.concatenate([state_conv[l], xbc[tp:].reshape(bs, ss, CONV_DIM)], axis=1)
    return (y_p.reshape(bp, sp, D_MODEL), y_s.reshape(bs, ss, D_MODEL),
            new_k_p[None], new_v_p[None], prompt_tail(xbc, CONV_W - 1)[None], hfin_p[None],
            new_k_s[None], new_v_s[None], xbc_s[:, -(CONV_W - 1):][None], hfin_s[None])
```

```python
import functools
import math

import jax
import jax.numpy as jnp
import numpy as np
from jax import lax
from jax.experimental import pallas as pl
from jax.experimental.pallas import tpu as pltpu

F32 = jnp.float32
BF16 = jnp.bfloat16
I32 = jnp.int32

D_MODEL = 2048
CHUNK = 64
N_HEADS = 32
N_KV_HEADS = 8
HEAD_DIM = 64
Q_PER_KV = N_HEADS // N_KV_HEADS
ATTN_WIDTH = N_HEADS * HEAD_DIM
KV_WIDTH = N_KV_HEADS * HEAD_DIM
WINDOW = 128
N_PREV_CHUNKS = WINDOW // CHUNK
ROPE_THETA = 500000.0
ROT_DIM = HEAD_DIM // 4
D_INNER = 2 * D_MODEL
SSM_HEAD_DIM = 64
SSM_HEADS = D_INNER // SSM_HEAD_DIM
SSM_GROUPS = 8
D_STATE = 128
CONV_W = 4
BC_WIDTH = SSM_GROUPS * D_STATE
CONV_DIM = D_INNER + 2 * BC_WIDTH
N_EXPERTS = 32
TOP_K = 4
D_FF = D_MODEL
SWIGLU_LIMIT = 7.0
SWIGLU_ALPHA = 1.702
EPS = 1e-6
NEG_INF = -1e30
PAST_LEN = 2048

LANES = 128
VMEM_LIMIT = 56 * 1024 * 1024
EXPERT_VMEM_LIMIT = 60 * 1024 * 1024
MOE_TM = 512
MOE_TF = 256
CACHED_TILES = 2


def _pick(n, cands):
    for c in cands:
        if n % c == 0:
            return c
    return n


def _cparams(sem):
    return pltpu.CompilerParams(dimension_semantics=sem, vmem_limit_bytes=VMEM_LIMIT)


def _split2(x):
    hi = x.astype(BF16)
    lo = (x - hi.astype(F32)).astype(BF16)
    return hi, lo


def _split3(x):
    hi = x.astype(BF16)
    r = x - hi.astype(F32)
    mid = r.astype(BF16)
    lo = (r - mid.astype(F32)).astype(BF16)
    return hi, mid, lo


def _dot(a, b):
    return jnp.dot(a, b, preferred_element_type=F32)


def _dot_nt(a, b):
    return lax.dot_general(a, b, (((1,), (1,)), ((), ())), preferred_element_type=F32)


def _silu(x):
    half = 0.5 * x
    return half + half * jnp.tanh(half)


def _first_rows(n_first):
    return lambda i, *_: (jnp.minimum(i, n_first - 1), 0)


def _later_rows(n_first):
    return lambda i, *_: (jnp.maximum(i - n_first, 0), 0)


def _by_part(i, n_first, fn, first_refs, later_refs):
    @pl.when(i < n_first)
    def _():
        fn(*first_refs)

    @pl.when(i >= n_first)
    def _():
        fn(*later_refs)


def _rms_kernel(xa_ref, xb_ref, w_ref, o_ref, *, n_first):
    def body(x_ref):
        x = x_ref[...]
        ms = jnp.mean(x * x, axis=-1, keepdims=True)
        o_ref[...] = (x * lax.rsqrt(ms + EPS) * w_ref[...]).astype(o_ref.dtype)

    _by_part(pl.program_id(0), n_first, body, (xa_ref,), (xb_ref,))


def _rmsnorm(xa, xb, w, out_dtype):
    (ma, d), mb = xa.shape, xb.shape[0]
    tm = _pick(math.gcd(ma, mb), (512, 256, 128, 64, 32))
    nf = ma // tm
    return pl.pallas_call(
        functools.partial(_rms_kernel, n_first=nf),
        grid=((ma + mb) // tm,),
        in_specs=[pl.BlockSpec((tm, d), _first_rows(nf)), pl.BlockSpec((tm, d), _later_rows(nf)),
                  pl.BlockSpec((1, d), lambda i: (0, 0))],
        out_specs=pl.BlockSpec((tm, d), lambda i: (i, 0)),
        out_shape=jax.ShapeDtypeStruct((ma + mb, d), out_dtype),
        compiler_params=_cparams(("parallel",)),
        name="rmsnorm",
    )(xa, xb, w.reshape(1, d))


def _mm_kernel(x_ref, w_ref, o_ref, wb_ref):
    @pl.when(pl.program_id(1) == 0)
    def _():
        wb_ref[...] = w_ref[0].astype(BF16)

    o_ref[...] = _dot_nt(x_ref[...], wb_ref[...]).astype(o_ref.dtype)


def _matmul(x, wt, layer, row0, n, out_dtype, name):
    m, k = x.shape
    tm = _pick(m, (1024, 512, 256, 128, 64, 32))
    tn = _pick(math.gcd(n, row0), (1024, 512, 256, 128))
    rb0 = row0 // tn
    return pl.pallas_call(
        _mm_kernel,
        grid=(n // tn, m // tm),
        in_specs=[pl.BlockSpec((tm, k), lambda j, i: (i, 0)),
                  pl.BlockSpec((1, tn, k), lambda j, i: (layer, rb0 + j, 0))],
        out_specs=pl.BlockSpec((tm, tn), lambda j, i: (i, j)),
        out_shape=jax.ShapeDtypeStruct((m, n), out_dtype),
        scratch_shapes=[pltpu.VMEM((tn, k), BF16)],
        compiler_params=_cparams(("parallel", "arbitrary")),
        name=name,
    )(x, wt)


def _qk_prep_kernel(q_ref, k_ref, cos_ref, s1_ref, s2_ref, qn_ref, kn_ref, g_ref, gt_ref, qo_ref, ko_ref):
    cos = cos_ref[...]
    s1 = s1_ref[...]
    s2 = s2_ref[...]

    def norm_rope(x, nw, width):
        g = g_ref[0:width, :]
        gt = gt_ref[:, 0:width]
        sq_hi, sq_lo = _split2(x * x)
        ssum = _dot(sq_hi, g) + _dot(sq_lo, g)
        r = lax.rsqrt(ssum * (1.0 / HEAD_DIM) + EPS)
        r_hi, r_lo = _split2(r)
        y = x * (_dot(r_hi, gt) + _dot(r_lo, gt)) * nw
        outs = []
        for s in range(width // LANES):
            blk = y[:, s * LANES:(s + 1) * LANES]
            outs.append(blk * cos + pltpu.roll(blk, ROT_DIM // 2, 1) * s1
                        + pltpu.roll(blk, LANES - ROT_DIM // 2, 1) * s2)
        return jnp.concatenate(outs, axis=1)

    q = norm_rope(q_ref[...], qn_ref[...], ATTN_WIDTH)
    qo_ref[...] = (q * (HEAD_DIM ** -0.5)).astype(qo_ref.dtype)
    ko_ref[...] = norm_rope(k_ref[...], kn_ref[...], KV_WIDTH)


def _rope_tables(seq, dec_seq, tm):
    half = ROT_DIM // 2
    inv_freq = np.float32(ROPE_THETA) ** (-np.arange(half, dtype=np.float32) * np.float32(2.0) / np.float32(ROT_DIM))
    pos = np.concatenate([np.arange(seq), np.tile(PAST_LEN + np.arange(dec_seq), tm // dec_seq)]).astype(np.float32)
    ang = pos[:, None] * inv_freq.astype(np.float32)[None, :]
    cos, sin = np.cos(ang).astype(np.float32), np.sin(ang).astype(np.float32)
    rows = pos.shape[0]
    ones = np.ones((rows, HEAD_DIM - ROT_DIM), np.float32)
    zeros = np.zeros((rows, HEAD_DIM - ROT_DIM), np.float32)
    zh = np.zeros((rows, half), np.float32)
    reps = (1, LANES // HEAD_DIM)
    return (np.tile(np.concatenate([cos, cos, ones], axis=1), reps),
            np.tile(np.concatenate([zh, sin, zeros], axis=1), reps),
            np.tile(np.concatenate([-sin, zh, zeros], axis=1), reps))


def _qk_prep(q, k, q_norm, k_norm, n_prompt, seq, dec_seq):
    m = q.shape[0]
    tm = _pick(math.gcd(math.gcd(seq, m - n_prompt), 256), (256, 128, 64, 32))
    cos_t, s1_t, s2_t = (jnp.asarray(t) for t in _rope_tables(seq, dec_seq, tm))
    prompt_tiles, seq_tiles = n_prompt // tm, seq // tm
    pos_rows = pl.BlockSpec((tm, LANES), lambda i: (jnp.where(i < prompt_tiles, i % seq_tiles, seq_tiles), 0))
    head_of_lane = jnp.arange(ATTN_WIDTH) // HEAD_DIM
    g = (head_of_lane[:, None] == jnp.arange(LANES)[None, :]).astype(BF16)
    gt = g.T
    qn = jnp.tile(q_norm.astype(F32), N_HEADS).reshape(1, ATTN_WIDTH)
    kn = jnp.tile(k_norm.astype(F32), N_KV_HEADS).reshape(1, KV_WIDTH)
    row = lambda w: pl.BlockSpec((tm, w), lambda i: (i, 0))
    full = lambda a: pl.BlockSpec(a.shape, lambda i: (0, 0))
    return pl.pallas_call(
        _qk_prep_kernel,
        grid=(m // tm,),
        in_specs=[row(ATTN_WIDTH), row(KV_WIDTH), pos_rows, pos_rows, pos_rows,
                  full(qn), full(kn), full(g), full(gt)],
        out_specs=[row(ATTN_WIDTH), row(KV_WIDTH)],
        out_shape=[jax.ShapeDtypeStruct((m, ATTN_WIDTH), BF16), jax.ShapeDtypeStruct((m, KV_WIDTH), F32)],
        compiler_params=_cparams(("parallel",)),
        name="qk_prep",
    )(q, k, cos_t, s1_t, s2_t, qn, kn, g, gt)


def _attend(q, kk, vv, sinks_ref, s_ref, p_ref, l_ref):
    tq = q.shape[0]
    for j in range(N_KV_HEADS):
        heads = [Q_PER_KV * j + g for g in range(Q_PER_KV)]
        q4 = jnp.concatenate([q[:, h * HEAD_DIM:(h + 1) * HEAD_DIM] for h in heads], axis=0)
        s_ref[j] = _dot_nt(q4, kk[:, j * HEAD_DIM:(j + 1) * HEAD_DIM])
    for j in range(N_KV_HEADS):
        heads = [Q_PER_KV * j + g for g in range(Q_PER_KV)]
        s = s_ref[j]
        sink = jnp.concatenate([jnp.full((tq, 1), sinks_ref[h], F32) for h in heads], axis=0)
        m = jnp.maximum(jnp.max(s, axis=-1, keepdims=True), sink)
        p = jnp.exp(s - m)
        l_ref[j] = jnp.broadcast_to(jnp.sum(p, axis=-1, keepdims=True) + jnp.exp(sink - m), l_ref.shape[1:])
        p_ref[j] = p.astype(BF16)
    outs = []
    for j in range(N_KV_HEADS):
        o = _dot(p_ref[j], vv[:, j * HEAD_DIM:(j + 1) * HEAD_DIM]) / l_ref[j][:, 0:HEAD_DIM]
        for g in range(Q_PER_KV):
            outs.append(o[g * tq:(g + 1) * tq, :])
    return jnp.concatenate(outs, axis=1)


def _attn_prompt_kernel(sinks_ref, q_ref, ka_ref, kb_ref, va_ref, vb_ref, o_ref, st_ref, pt_ref, rl_ref):
    i = pl.program_id(1)
    tq = q_ref.shape[0]
    tk = 2 * tq
    q = q_ref[...]
    k32 = jnp.concatenate([ka_ref[...], kb_ref[...]], axis=0)
    vt = jnp.concatenate([va_ref[...], vb_ref[...]], axis=0).T.astype(BF16)

    lane_k = lax.broadcasted_iota(I32, (tk, LANES), 1)
    kc = lax.broadcasted_iota(I32, (tk, LANES), 0) // CHUNK
    k_ind = jnp.where(lane_k == 0, jnp.where(kc == 0, 1.0, 0.0),
                      jnp.where(lane_k == 1, jnp.where(kc == N_PREV_CHUNKS + 1, 1.0, 0.0),
                                jnp.where(lane_k == 2, jnp.where(kc < N_PREV_CHUNKS, 1.0, 0.0), 0.0)))
    first_step = jnp.where(i == 0, 1.0, 0.0)
    k_ind = jnp.where(lane_k == 2, k_ind * first_step, k_ind).astype(BF16)
    lane_q = lax.broadcasted_iota(I32, (tq, LANES), 1)
    qc = lax.broadcasted_iota(I32, (tq, LANES), 0) // CHUNK
    q_msk = jnp.where(lane_q == 0, jnp.where(qc == 1, NEG_INF, 0.0),
                      jnp.where(lane_q == 1, jnp.where(qc == 0, NEG_INF, 0.0),
                                jnp.where(lane_q == 2, NEG_INF, 0.0))).astype(BF16)
    low = lane_k < HEAD_DIM

    for slab in range(KV_WIDTH // LANES):
        ks = k32[:, slab * LANES:(slab + 1) * LANES]
        kr = pltpu.roll(ks, HEAD_DIM, 1)
        for jj in range(2):
            j = 2 * slab + jj
            in_low = jnp.where(low, ks if jj == 0 else kr, 0.0).astype(BF16)
            in_high = jnp.where(low, 0.0, kr if jj == 0 else ks).astype(BF16)
            k_ext = (jnp.concatenate([in_low, k_ind], axis=1), jnp.concatenate([in_high, k_ind], axis=1))
            for g in range(Q_PER_KV):
                h = Q_PER_KV * j + g
                pair = h // 2
                q_ext = jnp.concatenate([q[:, pair * LANES:(pair + 1) * LANES], q_msk], axis=1)
                st_ref[h] = _dot_nt(k_ext[h % 2], q_ext)

    for h in range(N_HEADS):
        st = st_ref[h]
        sink = sinks_ref[h]
        m = jnp.maximum(jnp.max(st, axis=0, keepdims=True), sink)
        pt = jnp.exp(st - m)
        rl_ref[h:h + 1, :] = 1.0 / (jnp.sum(pt, axis=0, keepdims=True) + jnp.exp(sink - m))
        pt_ref[h] = pt.astype(BF16)

    for pair in range(N_HEADS // 2):
        pieces = []
        for h in (2 * pair, 2 * pair + 1):
            j = h // Q_PER_KV
            pieces.append(_dot(vt[j * HEAD_DIM:(j + 1) * HEAD_DIM, :], pt_ref[h]) * rl_ref[h:h + 1, :])
        o_ref[:, pair * LANES:(pair + 1) * LANES] = jnp.concatenate(pieces, axis=0).T.astype(o_ref.dtype)


def _attn_prompt(q, k, kv, sinks, bsz, seq):
    tq = 2 * CHUNK
    nb = seq // tq

    def kv_spec(back, col):
        return pl.BlockSpec((tq, KV_WIDTH), lambda b, i, s: (b * nb + jnp.maximum(i - back, 0), col))

    grid_spec = pltpu.PrefetchScalarGridSpec(
        num_scalar_prefetch=1,
        grid=(bsz, nb),
        in_specs=[pl.BlockSpec((tq, ATTN_WIDTH), lambda b, i, s: (b * nb + i, 0)),
                  kv_spec(1, 0), kv_spec(0, 0), kv_spec(1, 1), kv_spec(0, 1)],
        out_specs=pl.BlockSpec((tq, ATTN_WIDTH), lambda b, i, s: (b * nb + i, 0)),
        scratch_shapes=[pltpu.VMEM((N_HEADS, 2 * tq, tq), F32), pltpu.VMEM((N_HEADS, 2 * tq, tq), BF16),
                        pltpu.VMEM((N_HEADS, tq), F32)],
    )
    return pl.pallas_call(
        _attn_prompt_kernel,
        grid_spec=grid_spec,
        out_shape=jax.ShapeDtypeStruct((bsz * seq, ATTN_WIDTH), BF16),
        compiler_params=_cparams(("parallel", "parallel")),
        name="attn_prompt",
    )(sinks, q, k, k, kv, kv)


def _attn_sample_kernel(sinks_ref, q_ref, kn_ref, vn_ref, kc_ref, vc_ref, o_ref, s_ref, p_ref, l_ref):
    kk = jnp.concatenate([kc_ref[0], kn_ref[...]], axis=0).astype(BF16)
    vv = jnp.concatenate([vc_ref[0], vn_ref[...]], axis=0).astype(BF16)
    o_ref[...] = _attend(q_ref[...], kk, vv, sinks_ref, s_ref, p_ref, l_ref).astype(o_ref.dtype)


def _attn_sample(q, k, kv, cache_k, cache_v, sinks, row0, bsz, seq):
    blk0 = row0 // seq
    kv_len = cache_k.shape[1]
    new = lambda w, col: pl.BlockSpec((seq, w), lambda b, s: (blk0 + b, col))
    cache = pl.BlockSpec((1, kv_len, KV_WIDTH), lambda b, s: (b, 0, 0))
    grid_spec = pltpu.PrefetchScalarGridSpec(
        num_scalar_prefetch=1,
        grid=(bsz,),
        in_specs=[new(ATTN_WIDTH, 0), new(KV_WIDTH, 0), new(KV_WIDTH, 1), cache, cache],
        out_specs=pl.BlockSpec((seq, ATTN_WIDTH), lambda b, s: (b, 0)),
        scratch_shapes=[pltpu.VMEM((N_KV_HEADS, Q_PER_KV * seq, kv_len + seq), F32),
                        pltpu.VMEM((N_KV_HEADS, Q_PER_KV * seq, kv_len + seq), BF16),
                        pltpu.VMEM((N_KV_HEADS, Q_PER_KV * seq, LANES), F32)],
    )
    return pl.pallas_call(
        _attn_sample_kernel,
        grid_spec=grid_spec,
        out_shape=jax.ShapeDtypeStruct((bsz * seq, ATTN_WIDTH), BF16),
        compiler_params=_cparams(("parallel",)),
        name="attn_sample",
    )(sinks, q, k, kv, cache_k, cache_v)


def _pad_rows(x, rows):
    if x.shape[0] == rows:
        return x
    return jnp.concatenate([x, jnp.zeros((rows - x.shape[0], x.shape[1]), x.dtype)], axis=0)


def _ssd_kernel(*refs, clen, has_past):
    if has_past:
        (xbc_ref, dt_ref, z_ref, cw_ref, cb_ref, dtb_ref, alog_ref, dskip_ref, nw_ref, cpast_ref, hpast_ref,
         y_ref, hout_ref, xpad, u_ref, g_ref, ht_ref, ca_ref, xdt_ref, yst_ref, xw_ref, cb2_ref) = refs
    else:
        (xbc_ref, dt_ref, z_ref, cw_ref, cb_ref, dtb_ref, alog_ref, dskip_ref, nw_ref,
         y_ref, hout_ref, xpad, u_ref, g_ref, ht_ref, ca_ref, xdt_ref, yst_ref, xw_ref, cb2_ref) = refs
    L = clen
    c = pl.program_id(1)
    nc = pl.num_programs(1)
    n_pairs = SSM_HEADS // 2
    pairs_per_group = n_pairs // SSM_GROUPS
    gw = D_INNER // SSM_GROUPS

    @pl.when(c == 0)
    def _init():
        if has_past:
            for cidx in range(CONV_DIM // LANES):
                xpad[cidx, 5:8, :] = cpast_ref[0, :, cidx * LANES:(cidx + 1) * LANES]
            for p in range(n_pairs):
                both = jnp.concatenate([hpast_ref[0, 2 * p], hpast_ref[0, 2 * p + 1]], axis=0)
                ht_ref[:, p * LANES:(p + 1) * LANES] = both.T
        else:
            xpad[:, 0:8, :] = jnp.zeros((CONV_DIM // LANES, 8, LANES), F32)
            ht_ref[...] = jnp.zeros(ht_ref.shape, F32)

    n_col = CONV_DIM // LANES
    for cidx in range(n_col):
        xpad[cidx, 8:8 + L, :] = xbc_ref[:, cidx * LANES:(cidx + 1) * LANES]
    for cidx in range(n_col):
        sl = slice(cidx * LANES, (cidx + 1) * LANES)
        acc = cb_ref[:, sl] + xpad[cidx, 8:8 + L, :] * cw_ref[3:4, sl]
        acc = acc + xpad[cidx, 7:7 + L, :] * cw_ref[2:3, sl]
        acc = acc + xpad[cidx, 6:6 + L, :] * cw_ref[1:2, sl]
        acc = acc + xpad[cidx, 5:5 + L, :] * cw_ref[0:1, sl]
        u_ref[:, sl] = _silu(acc)
    for cidx in range(n_col):
        xpad[cidx, 5:8, :] = xpad[cidx, 5 + L:8 + L, :]

    dtx = dt_ref[...] + dtb_ref[...]
    dt = jnp.maximum(dtx, 0.0) + jnp.log(1.0 + jnp.exp(-jnp.abs(dtx)))
    loga = dt * (-jnp.exp(alog_ref[...]))
    t_idx = lax.broadcasted_iota(I32, (L, L), 0)
    s_idx = lax.broadcasted_iota(I32, (L, L), 1)
    incl = (s_idx <= t_idx).astype(BF16)
    acum = sum(_dot(incl, part) for part in _split3(loga))
    acum_t = _pad_rows(acum, LANES).T[:, 0:L]

    lane_m = lax.broadcasted_iota(I32, (L, 2 * L), 1)
    row_m = lax.broadcasted_iota(I32, (L, 2 * L), 0)
    left_m = lane_m < L
    causal_m = jnp.where(left_m, lane_m, lane_m - L) <= row_m
    left = lax.broadcasted_iota(I32, (L, LANES), 1) < SSM_HEAD_DIM
    b_of = lambda grp: u_ref[:, D_INNER + grp * D_STATE:D_INNER + (grp + 1) * D_STATE]
    c_of = lambda grp: u_ref[:, D_INNER + BC_WIDTH + grp * D_STATE:D_INNER + BC_WIDTH + (grp + 1) * D_STATE]
    pair_lanes = lambda p: slice(p * LANES, (p + 1) * LANES)
    group_lanes = lambda grp: slice(grp * gw, (grp + 1) * gw)

    for p in range(n_pairs):
        h0, h1, sl = 2 * p, 2 * p + 1, pair_lanes(p)
        ca_ref[:, sl] = jnp.where(left, acum[:, h0:h0 + 1], acum[:, h1:h1 + 1])
        xdt_ref[:, sl] = u_ref[:, sl] * jnp.where(left, dt[:, h0:h0 + 1], dt[:, h1:h1 + 1])

    for grp in range(SSM_GROUPS):
        c_bf = c_of(grp).astype(BF16)
        cb = _dot_nt(c_bf, b_of(grp).astype(BF16))
        cb2_ref[grp] = jnp.concatenate([cb, cb], axis=1)
        yst_ref[:, group_lanes(grp)] = _dot(c_bf, ht_ref[:, group_lanes(grp)].astype(BF16))

    for p in range(n_pairs):
        h0, h1, sl = 2 * p, 2 * p + 1, pair_lanes(p)
        ca = ca_ref[:, sl]
        ca_m = ca if L == SSM_HEAD_DIM else jnp.where(left_m, acum[:, h0:h0 + 1], acum[:, h1:h1 + 1])
        row_a = jnp.concatenate([acum_t[h0:h0 + 1, :], acum_t[h1:h1 + 1, :]], axis=1)
        dec = jnp.exp(jnp.where(causal_m, ca_m - row_a, NEG_INF))
        m_pair = (cb2_ref[p // pairs_per_group] * dec).astype(BF16)
        xdt = xdt_ref[:, sl]
        x_bd = jnp.concatenate([jnp.where(left, xdt, 0.0), jnp.where(left, 0.0, xdt)], axis=0).astype(BF16)
        y = _dot(m_pair, x_bd) + yst_ref[:, sl] * jnp.exp(ca) + u_ref[:, sl] * dskip_ref[:, sl]
        g_ref[:, sl] = y * _silu(z_ref[:, sl].astype(F32))
        xw_ref[:, sl] = (xdt * jnp.exp(ca[L - 1:L, :] - ca)).astype(BF16)

    for grp in range(SSM_GROUPS):
        gsl = group_lanes(grp)
        gg = g_ref[:, gsl]
        ms = jnp.mean(gg * gg, axis=-1, keepdims=True)
        y_ref[:, gsl] = (gg * lax.rsqrt(ms + EPS) * nw_ref[:, gsl]).astype(y_ref.dtype)
        bt_bf = _pad_rows(b_of(grp), LANES).T[:, 0:L].astype(BF16)
        ht_ref[:, gsl] = ht_ref[:, gsl] * jnp.exp(ca_ref[L - 1:L, gsl]) + _dot(bt_bf, xw_ref[:, gsl])

    @pl.when(c == nc - 1)
    def _fin():
        for p in range(n_pairs):
            both = ht_ref[:, p * LANES:(p + 1) * LANES].T
            hout_ref[0, 2 * p] = both[0:SSM_HEAD_DIM, :]
            hout_ref[0, 2 * p + 1] = both[SSM_HEAD_DIM:2 * SSM_HEAD_DIM, :]


def _ssd(xbc, dt_raw, z, conv_w, conv_b, dt_bias, a_log, d_skip, ssm_norm, row0, bsz, seq, clen,
         conv_past=None, ssm_past=None):
    nc = seq // clen
    blk0 = row0 // clen
    has_past = conv_past is not None
    pad = lambda a: jnp.pad(a.astype(F32), (0, LANES - SSM_HEADS)).reshape(1, LANES)
    params = [conv_w.astype(F32), conv_b.astype(F32).reshape(1, CONV_DIM), pad(dt_bias), pad(a_log),
              jnp.repeat(d_skip.astype(F32), SSM_HEAD_DIM).reshape(1, D_INNER), ssm_norm.astype(F32).reshape(1, D_INNER)]
    rows = lambda w: pl.BlockSpec((clen, w), lambda b, c: (blk0 + b * nc + c, 0))
    full = lambda a: pl.BlockSpec(a.shape, lambda b, c: (0,) * a.ndim)
    in_specs = [rows(CONV_DIM), rows(LANES), rows(D_INNER)] + [full(a) for a in params]
    args = [xbc, dt_raw, z] + params
    if has_past:
        in_specs += [pl.BlockSpec((1, CONV_W - 1, CONV_DIM), lambda b, c: (b, 0, 0)),
                     pl.BlockSpec((1, SSM_HEADS, SSM_HEAD_DIM, D_STATE), lambda b, c: (b, 0, 0, 0))]
        args += [conv_past.astype(F32), ssm_past.astype(F32)]
    return pl.pallas_call(
        functools.partial(_ssd_kernel, clen=clen, has_past=has_past),
        grid=(bsz, nc),
        in_specs=in_specs,
        out_specs=[pl.BlockSpec((clen, D_INNER), lambda b, c: (b * nc + c, 0)),
                   pl.BlockSpec((1, SSM_HEADS, SSM_HEAD_DIM, D_STATE), lambda b, c: (b, 0, 0, 0))],
        out_shape=[jax.ShapeDtypeStruct((bsz * seq, D_INNER), BF16),
                   jax.ShapeDtypeStruct((bsz, SSM_HEADS, SSM_HEAD_DIM, D_STATE), F32)],
        scratch_shapes=[pltpu.VMEM((CONV_DIM // LANES, 8 + clen, LANES), F32), pltpu.VMEM((clen, CONV_DIM), F32),
                        pltpu.VMEM((clen, D_INNER), F32), pltpu.VMEM((D_STATE, D_INNER), F32),
                        pltpu.VMEM((clen, D_INNER), F32), pltpu.VMEM((clen, D_INNER), F32),
                        pltpu.VMEM((clen, D_INNER), F32), pltpu.VMEM((clen, D_INNER), BF16),
                        pltpu.VMEM((SSM_GROUPS, clen, 2 * clen), F32)],
        compiler_params=_cparams(("parallel", "arbitrary")),
        name="ssd_past" if has_past else "ssd_prompt",
    )(*args)


def _merge_kernel(h_ref, a1_ref, a2_ref, s1_ref, s2_ref, wga_ref, wgs_ref, bga_ref, bgs_ref, wa_ref, ws_ref, o_ref,
                  *, n_first):
    def body(a_ref, s_ref):
        h = h_ref[...]
        g_a = jax.nn.sigmoid(_dot(h, wga_ref[...]) + bga_ref[...])
        g_s = jax.nn.sigmoid(_dot(h, wgs_ref[...]) + bgs_ref[...])
        o_ref[...] = (g_a * _dot(a_ref[...], wa_ref[...]) + g_s * _dot(s_ref[...], ws_ref[...])).astype(o_ref.dtype)

    _by_part(pl.program_id(0), n_first, body, (a1_ref, s1_ref), (a2_ref, s2_ref))


def _merge(h, attn_parts, ssm_parts, w_gate, b_gate, w_attn_up, w_ssm_up):
    m = h.shape[0]
    m1, m2 = attn_parts[0].shape[0], attn_parts[1].shape[0]
    tm = _pick(math.gcd(m1, m2), (512, 256, 128, 64, 32))
    nf = m1 // tm
    tn = 512
    nj = D_MODEL // tn
    bg = b_gate.astype(F32).reshape(1, 2 * D_MODEL)
    return pl.pallas_call(
        functools.partial(_merge_kernel, n_first=nf),
        grid=(m // tm, nj),
        in_specs=[pl.BlockSpec((tm, D_MODEL), lambda i, j: (i, 0)),
                  pl.BlockSpec((tm, ATTN_WIDTH), _first_rows(nf)), pl.BlockSpec((tm, ATTN_WIDTH), _later_rows(nf)),
                  pl.BlockSpec((tm, D_INNER), _first_rows(nf)), pl.BlockSpec((tm, D_INNER), _later_rows(nf)),
                  pl.BlockSpec((D_MODEL, tn), lambda i, j: (0, j)),
                  pl.BlockSpec((D_MODEL, tn), lambda i, j: (0, j + nj)),
                  pl.BlockSpec((1, tn), lambda i, j: (0, j)),
                  pl.BlockSpec((1, tn), lambda i, j: (0, j + nj)),
                  pl.BlockSpec((ATTN_WIDTH, tn), lambda i, j: (0, j)),
                  pl.BlockSpec((D_INNER, tn), lambda i, j: (0, j))],
        out_specs=pl.BlockSpec((tm, tn), lambda i, j: (i, j)),
        out_shape=jax.ShapeDtypeStruct((m, D_MODEL), BF16),
        compiler_params=_cparams(("parallel", "parallel")),
        name="merge",
    )(h, attn_parts[0], attn_parts[1], ssm_parts[0], ssm_parts[1], w_gate, w_gate, bg, bg, w_attn_up, w_ssm_up)


def _outproj_router_kernel(mg_ref, wo_ref, xa_ref, xb_ref, n2_ref, wr_ref, br_ref, x1_ref, h2_ref, idx_ref, gate_ref,
                           *, n_first):
    def residual(x_ref):
        x1_ref[...] = x_ref[...] + _dot(mg_ref[...], wo_ref[...])

    _by_part(pl.program_id(0), n_first, residual, (xa_ref,), (xb_ref,))
    x1 = x1_ref[...]
    ms = jnp.mean(x1 * x1, axis=-1, keepdims=True)
    h2 = x1 * lax.rsqrt(ms + EPS) * n2_ref[...]
    h2_ref[...] = _pack_bf16_pairs(h2)
    h_hi, h_lo = _split2(h2)
    w_hi, w_lo = _split2(wr_ref[...])
    logits = _dot(h_hi, w_hi) + (_dot(h_hi, w_lo) + _dot(h_lo, w_hi)) + br_ref[...]
    tm = logits.shape[0]
    lane = lax.broadcasted_iota(I32, (tm, LANES), 1)
    logits = jnp.where(lane < N_EXPERTS, logits, -jnp.inf)
    idx_out = jnp.zeros((tm, LANES), I32)
    val_out = jnp.zeros((tm, LANES), F32)
    top = None
    for k in range(TOP_K):
        v = jnp.max(logits, axis=-1, keepdims=True)
        i = jnp.min(jnp.where(logits == v, lane, LANES), axis=-1, keepdims=True)
        if k == 0:
            top = v
        idx_out = jnp.where(lane == k, i, idx_out)
        val_out = jnp.where(lane == k, jnp.exp(v - top), val_out)
        logits = jnp.where(lane == i, -jnp.inf, logits)
    idx_ref[...] = idx_out
    gate_ref[...] = val_out / jnp.sum(val_out, axis=-1, keepdims=True)


def _outproj_router(merged, w_out, xa, xb, norm2, w_router, b_router):
    m = merged.shape[0]
    tm = _pick(math.gcd(xa.shape[0], xb.shape[0]), (512, 256, 128, 64, 32))
    nf = xa.shape[0] // tm
    wr = jnp.pad(w_router.astype(F32), ((0, 0), (0, LANES - N_EXPERTS)))
    br = jnp.pad(b_router.astype(F32), (0, LANES - N_EXPERTS)).reshape(1, LANES)
    row = lambda w: pl.BlockSpec((tm, w), lambda i: (i, 0))
    full = lambda r, c: pl.BlockSpec((r, c), lambda i: (0, 0))
    return pl.pallas_call(
        functools.partial(_outproj_router_kernel, n_first=nf),
        grid=(m // tm,),
        in_specs=[row(D_MODEL), full(D_MODEL, D_MODEL),
                  pl.BlockSpec((tm, D_MODEL), _first_rows(nf)), pl.BlockSpec((tm, D_MODEL), _later_rows(nf)),
                  full(1, D_MODEL), full(D_MODEL, LANES), full(1, LANES)],
        out_specs=[row(D_MODEL), row(D_MODEL // 2), row(LANES), row(LANES)],
        out_shape=[jax.ShapeDtypeStruct((m, D_MODEL), F32), jax.ShapeDtypeStruct((m, D_MODEL // 2), I32),
                   jax.ShapeDtypeStruct((m, LANES), I32), jax.ShapeDtypeStruct((m, LANES), F32)],
        compiler_params=_cparams(("parallel",)),
        name="outproj_router",
    )(merged, w_out, xa, xb, norm2.astype(F32).reshape(1, D_MODEL), wr, br)


def _rank_kernel(idx_ref, rank_ref, cnt_ref, base_ref):
    i = pl.program_id(0)
    tt = idx_ref.shape[0]

    @pl.when(i == 0)
    def _():
        base_ref[...] = jnp.zeros(base_ref.shape, F32)

    idx = idx_ref[...]
    lane = lax.broadcasted_iota(I32, (tt, LANES), 1)
    sel = [lane == idx[:, k:k + 1] for k in range(TOP_K)]
    onehot = jnp.zeros((tt, LANES), F32)
    for k in range(TOP_K):
        onehot = jnp.where(sel[k], 1.0, onehot)
    r_idx = lax.broadcasted_iota(I32, (tt, tt), 0)
    c_idx = lax.broadcasted_iota(I32, (tt, tt), 1)
    before = (c_idx < r_idx).astype(BF16)
    rank_all = _dot(before, onehot.astype(BF16)) + base_ref[0:1, :]
    out = jnp.zeros((tt, LANES), F32)
    for k in range(TOP_K):
        out = jnp.where(lane == k, jnp.sum(jnp.where(sel[k], rank_all, 0.0), axis=-1, keepdims=True), out)
    rank_ref[...] = out.astype(I32)
    base_ref[0:1, :] = base_ref[0:1, :] + jnp.sum(onehot, axis=0, keepdims=True)
    cnt_ref[...] = base_ref[...].astype(I32)


def _ranks(idx):
    m = idx.shape[0]
    tt = _pick(m, (512, 256, 128, 64, 32))
    return pl.pallas_call(
        _rank_kernel,
        grid=(m // tt,),
        in_specs=[pl.BlockSpec((tt, LANES), lambda i: (i, 0))],
        out_specs=[pl.BlockSpec((tt, LANES), lambda i: (i, 0)), pl.BlockSpec((8, LANES), lambda i: (0, 0))],
        out_shape=[jax.ShapeDtypeStruct((m, LANES), I32), jax.ShapeDtypeStruct((8, LANES), I32)],
        scratch_shapes=[pltpu.VMEM((8, LANES), F32)],
        compiler_params=_cparams(("arbitrary",)),
        name="expert_ranks",
    )(idx)


def _dispatch_kernel(dest_ref, fill_ref, h_ref, xs_hbm, zero_ref, sem, zsem, *, tt, tm, n_blocks):
    @pl.when(pl.program_id(0) == 0)
    def _():
        zero_ref[...] = jnp.zeros(zero_ref.shape, zero_ref.dtype)

        def block_copy(b):
            return pltpu.make_async_copy(zero_ref, xs_hbm.at[pl.ds(pl.multiple_of(b * tm, tm), tm), :], zsem)

        def start(b, carry):
            @pl.when(fill_ref[b] != 0)
            def _():
                block_copy(b).start()

            return carry

        def done(b, carry):
            @pl.when(fill_ref[b] != 0)
            def _():
                block_copy(b).wait()

            return carry

        lax.fori_loop(0, n_blocks, start, 0)
        lax.fori_loop(0, n_blocks, done, 0)

    base = pl.program_id(0) * (tt * TOP_K)

    def issue(r, carry):
        for k in range(TOP_K):
            dst = dest_ref[base + r * TOP_K + k]
            pltpu.make_async_copy(h_ref.at[pl.ds(r, 1), :], xs_hbm.at[pl.ds(dst, 1), :], sem).start()
        return carry

    lax.fori_loop(0, tt, issue, 0, unroll=True)
    for _ in range(TOP_K):
        pltpu.make_async_copy(h_ref, xs_hbm.at[pl.ds(0, tt), :], sem).wait()


def _dispatch(dest_flat, fill_block, h2, tm):
    m = h2.shape[0]
    n_blocks = fill_block.shape[0]
    tt = _pick(m, (256, 128, 64, 32))
    width = h2.shape[1]
    grid_spec = pltpu.PrefetchScalarGridSpec(
        num_scalar_prefetch=2,
        grid=(m // tt,),
        in_specs=[pl.BlockSpec((tt, width), lambda i, d, f: (i, 0))],
        out_specs=pl.BlockSpec(memory_space=pl.ANY),
        scratch_shapes=[pltpu.VMEM((tm, width), h2.dtype), pltpu.SemaphoreType.DMA, pltpu.SemaphoreType.DMA],
    )
    return pl.pallas_call(
        functools.partial(_dispatch_kernel, tt=tt, tm=tm, n_blocks=n_blocks),
        grid_spec=grid_spec,
        out_shape=jax.ShapeDtypeStruct((n_blocks * tm, width), h2.dtype),
        compiler_params=_cparams(("arbitrary",)),
        name="dispatch",
    )(dest_flat, fill_block, h2)


def _pack_bf16_pairs(x):
    c = x.shape[1] // 2
    bits = lax.bitcast_convert_type(x.astype(BF16).astype(F32), I32)
    return lax.shift_right_logical(bits[:, :c], 16) | bits[:, c:]


def _unpack_bf16_pairs(p):
    lo = lax.bitcast_convert_type(lax.shift_left(p, 16), F32).astype(BF16)
    hi = lax.bitcast_convert_type(p & jnp.int32(-65536), F32).astype(BF16)
    return lo, hi


def _expert_kernel(be_ref, nu_ref, xs_ref, bg_ref, bu_ref, bd_ref, wg_hbm, wu_hbm, wd_hbm, ys_ref,
                   cg, cu, cd, stg_g, stg_u, stg_d, xb_ref, sems, *, tf):
    i = pl.program_id(0)
    nj = D_FF // tf
    half = D_MODEL // 2
    e = be_ref[i]
    first = jnp.logical_or(i == 0, e != be_ref[jnp.maximum(i - 1, 0)])
    e_next = be_ref[jnp.minimum(i + 1, pl.num_programs(0) - 1)]
    next_is_new = jnp.logical_and(i + 1 < nu_ref[0], e_next != e)

    def tile_copies(t, slot, expert=e):
        cols = pl.ds(pl.multiple_of(t * tf, tf), tf)
        return (pltpu.make_async_copy(wg_hbm.at[expert, :, cols], stg_g.at[slot], sems.at[slot, 0]),
                pltpu.make_async_copy(wu_hbm.at[expert, :, cols], stg_u.at[slot], sems.at[slot, 1]),
                pltpu.make_async_copy(wd_hbm.at[expert, cols, :], stg_d.at[slot], sems.at[slot, 2]))

    @pl.when(i < nu_ref[0])
    def _():
        @pl.when(i == 0)
        def _():
            for cp in tile_copies(0, 0):
                cp.start()

        lo, hi = _unpack_bf16_pairs(xs_ref[...])
        xb_ref[:, :half] = lo
        xb_ref[:, half:] = hi

        def refill(t):
            slot = t % 2

            @pl.when(t + 1 < nj)
            def _():
                for cp in tile_copies(t + 1, 1 - slot):
                    cp.start()

            for cp in tile_copies(t, slot):
                cp.wait()
            cg[t] = stg_g[slot].astype(BF16)
            cu[t] = stg_u[slot].astype(BF16)
            cd[t] = stg_d[slot].astype(BF16)

        def compute(t, opening):
            xb = xb_ref[...]
            g = jnp.minimum(_dot(xb, cg[t]) + bg_ref[0, t], SWIGLU_LIMIT)
            u = jnp.clip(_dot(xb, cu[t]) + bu_ref[0, t], -SWIGLU_LIMIT, SWIGLU_LIMIT)
            act = (u + 1.0) * (g * jax.nn.sigmoid(SWIGLU_ALPHA * g))
            part = _dot(act.astype(BF16), cd[t])
            if opening:
                ys_ref[...] = part + bd_ref[0]
            else:
                ys_ref[...] += part

        @pl.when(first)
        def _():
            refill(jnp.int32(0))
            compute(0, True)

            def step(t, carry):
                refill(t)
                compute(t, False)
                return carry

            lax.fori_loop(1, nj, step, 0)

        def compute_wide(t, opening):
            xb = xb_ref[...]
            span = range(t, t + CACHED_TILES)
            wide = lambda ref: jnp.concatenate([ref[k] for k in span], axis=1)
            bias = lambda ref: jnp.concatenate([ref[0, k] for k in span], axis=1)
            g = jnp.minimum(_dot(xb, wide(cg)) + bias(bg_ref), SWIGLU_LIMIT)
            u = jnp.clip(_dot(xb, wide(cu)) + bias(bu_ref), -SWIGLU_LIMIT, SWIGLU_LIMIT)
            act = (u + 1.0) * (g * jax.nn.sigmoid(SWIGLU_ALPHA * g))
            part = _dot(act.astype(BF16), jnp.concatenate([cd[k] for k in span], axis=0))
            if opening:
                ys_ref[...] = part + bd_ref[0]
            else:
                ys_ref[...] += part

        @pl.when(jnp.logical_not(first))
        def _():
            for t in range(0, nj, CACHED_TILES):
                compute_wide(t, t == 0)

        @pl.when(next_is_new)
        def _():
            for cp in tile_copies(0, 0, e_next):
                cp.start()

    @pl.when(i >= nu_ref[0])
    def _():
        ys_ref[...] = jnp.zeros(ys_ref.shape, F32)


def _experts(block_e, n_used, xs, w_e_gate, b_e_gate, w_e_up, b_e_up, w_e_down, b_e_down):
    n_rows = xs.shape[0]
    tm, tf = MOE_TM, MOE_TF
    n_blocks = n_rows // tm
    nj = D_FF // tf
    half = D_MODEL // 2

    def blk(i, nu):
        return jnp.minimum(i, nu[0] - 1)

    hbm = pl.BlockSpec(memory_space=pl.ANY)
    grid_spec = pltpu.PrefetchScalarGridSpec(
        num_scalar_prefetch=2,
        grid=(n_blocks,),
        in_specs=[pl.BlockSpec((tm, half), lambda i, be, nu: (blk(i, nu), 0)),
                  pl.BlockSpec((1, nj, 1, tf), lambda i, be, nu: (be[blk(i, nu)], 0, 0, 0)),
                  pl.BlockSpec((1, nj, 1, tf), lambda i, be, nu: (be[blk(i, nu)], 0, 0, 0)),
                  pl.BlockSpec((1, 1, D_MODEL), lambda i, be, nu: (be[blk(i, nu)], 0, 0)),
                  hbm, hbm, hbm],
        out_specs=pl.BlockSpec((tm, D_MODEL), lambda i, be, nu: (i, 0)),
        scratch_shapes=[pltpu.VMEM((nj, D_MODEL, tf), BF16), pltpu.VMEM((nj, D_MODEL, tf), BF16),
                        pltpu.VMEM((nj, tf, D_MODEL), BF16),
                        pltpu.VMEM((2, D_MODEL, tf), F32), pltpu.VMEM((2, D_MODEL, tf), F32),
                        pltpu.VMEM((2, tf, D_MODEL), F32),
                        pltpu.VMEM((tm, D_MODEL), BF16), pltpu.SemaphoreType.DMA((2, 3))],
    )
    return pl.pallas_call(
        functools.partial(_expert_kernel, tf=tf),
        grid_spec=grid_spec,
        out_shape=jax.ShapeDtypeStruct((n_rows, D_MODEL), F32),
        compiler_params=pltpu.CompilerParams(dimension_semantics=("arbitrary",), vmem_limit_bytes=EXPERT_VMEM_LIMIT),
        name="experts",
    )(block_e, n_used, xs, b_e_gate.reshape(N_EXPERTS, nj, 1, tf), b_e_up.reshape(N_EXPERTS, nj, 1, tf),
      b_e_down.reshape(N_EXPERTS, 1, D_MODEL), w_e_gate, w_e_up, w_e_down)


def _combine_kernel(dest_ref, x1_ref, gate_ref, ys_hbm, o_ref, buf, sems, *, tt, row0):
    i = pl.program_id(0)
    n_steps = pl.num_programs(0)
    slot = i % 2

    def gather(step, to_slot):
        base = (row0 + step * tt) * TOP_K

        def issue(r, carry):
            for k in range(TOP_K):
                src = dest_ref[base + r * TOP_K + k]
                pltpu.make_async_copy(ys_hbm.at[pl.ds(src, 1), :], buf.at[to_slot, k, pl.ds(r, 1), :],
                                      sems.at[to_slot]).start()
            return carry

        lax.fori_loop(0, tt, issue, 0, unroll=True)

    @pl.when(i == 0)
    def _():
        gather(0, 0)

    @pl.when(i + 1 < n_steps)
    def _():
        gather(i + 1, 1 - slot)

    for k in range(TOP_K):
        pltpu.make_async_copy(ys_hbm.at[pl.ds(0, tt), :], buf.at[slot, k], sems.at[slot]).wait()
    gate = gate_ref[...]
    acc = x1_ref[...]
    for k in range(TOP_K):
        acc = acc + gate[:, k:k + 1] * buf[slot, k]
    o_ref[...] = acc


def _combine(dest_flat, x1, gate, ys, row0, rows):
    tt = _pick(math.gcd(row0, rows), (256, 128, 64, 32))
    blk0 = row0 // tt
    grid_spec = pltpu.PrefetchScalarGridSpec(
        num_scalar_prefetch=1,
        grid=(rows // tt,),
        in_specs=[pl.BlockSpec((tt, D_MODEL), lambda i, d: (blk0 + i, 0)),
                  pl.BlockSpec((tt, LANES), lambda i, d: (blk0 + i, 0)),
                  pl.BlockSpec(memory_space=pl.ANY)],
        out_specs=pl.BlockSpec((tt, D_MODEL), lambda i, d: (i, 0)),
        scratch_shapes=[pltpu.VMEM((2, TOP_K, tt, D_MODEL), F32), pltpu.SemaphoreType.DMA((2,))],
    )
    return pl.pallas_call(
        functools.partial(_combine_kernel, tt=tt, row0=row0),
        grid_spec=grid_spec,
        out_shape=jax.ShapeDtypeStruct((rows, D_MODEL), F32),
        compiler_params=_cparams(("arbitrary",)),
        name="combine",
    )(dest_flat, x1, gate, ys)


def _moe(x1, h2, idx, gate, w_e_gate, b_e_gate, w_e_up, b_e_up, w_e_down, b_e_down, n_first):
    m = x1.shape[0]
    tm = MOE_TM
    ranks, cnt = _ranks(idx)
    counts = cnt[0, :N_EXPERTS]
    padded = (counts + tm - 1) // tm * tm
    pad_end = jnp.cumsum(padded)
    pad_start = pad_end - padded
    e_sel = idx[:, :TOP_K]
    dest = (pad_start[e_sel] + ranks[:, :TOP_K]).astype(I32).reshape(-1)
    n_blocks = -(-(m * TOP_K + N_EXPERTS * (tm - 1)) // tm)
    block_row = jnp.arange(n_blocks, dtype=I32) * tm
    block_e = jnp.minimum(jnp.sum((pad_end[None, :] <= block_row[:, None]).astype(I32), axis=1), N_EXPERTS - 1)
    n_used = (pad_end[-1] // tm).astype(I32).reshape(1)
    is_last = jnp.sum((pad_end[None, :] == block_row[:, None] + tm).astype(I32), axis=1) > 0
    fill_block = jnp.logical_or(is_last, block_row >= pad_end[-1]).astype(I32)
    xs = _dispatch(dest, fill_block, h2, tm)
    ys = _experts(block_e, n_used, xs, w_e_gate, b_e_gate, w_e_up, b_e_up, w_e_down, b_e_down)
    return _combine(dest, x1, gate, ys, 0, n_first), _combine(dest, x1, gate, ys, n_first, m - n_first)


def kernel(x_prompt, x_sample, cache_k, cache_v, state_conv, state_ssm, norm1, w_in, q_norm, k_norm, sinks, conv_w,
           conv_b, dt_bias, a_log, d_skip, ssm_norm, w_gate, b_gate, w_attn_up, w_ssm_up, w_out, norm2, w_router,
           b_router, w_e_gate, b_e_gate, w_e_up, b_e_up, w_e_down, b_e_down):
    bp, sp, _ = x_prompt.shape
    bs, ss, _ = x_sample.shape
    tp, ts = bp * sp, bs * ss
    kv_len = cache_k.shape[2]
    l = 0

    xa, xb = x_prompt.reshape(tp, D_MODEL), x_sample.reshape(ts, D_MODEL)

    o1 = ATTN_WIDTH
    o2 = o1 + 2 * KV_WIDTH
    o3 = o2 + D_INNER
    o4 = o3 + CONV_DIM
    h = _rmsnorm(xa, xb, norm1[l], BF16)
    w_in_t = jnp.swapaxes(w_in, 1, 2)
    q_p = _matmul(h, w_in_t, l, 0, o1, F32, "proj_q")
    kv_p = _matmul(h, w_in_t, l, o1, o2 - o1, F32, "proj_kv")
    z = _matmul(h, w_in_t, l, o2, o3 - o2, BF16, "proj_z")
    xbc = _matmul(h, w_in_t, l, o3, o4 - o3, F32, "proj_xbc")
    w_dt = jnp.pad(w_in_t[l:l + 1, o4:, :], ((0, 0), (0, LANES - SSM_HEADS), (0, 0)))
    dt_raw = _matmul(h, w_dt, 0, 0, LANES, F32, "proj_dt")

    q_rot, k_rot = _qk_prep(q_p, kv_p, q_norm[l], k_norm[l], tp, sp, ss)
    sk = sinks[l].astype(F32)
    attn_p = _attn_prompt(q_rot, k_rot, kv_p, sk, bp, sp)
    ck = cache_k[l].reshape(bs, kv_len, KV_WIDTH)
    cv = cache_v[l].reshape(bs, kv_len, KV_WIDTH)
    attn_s = _attn_sample(q_rot, k_rot, kv_p, ck, cv, sk, tp, bs, ss)

    ssd_w = (conv_w[l], conv_b[l], dt_bias[l], a_log[l], d_skip[l], ssm_norm[l])
    ssm_p, hfin_p = _ssd(xbc, dt_raw, z, *ssd_w, 0, bp, sp, CHUNK)
    ssm_s, hfin_s = _ssd(xbc, dt_raw, z, *ssd_w, tp, bs, ss, ss, state_conv[l], state_ssm[l])

    merged = _merge(h, (attn_p, attn_s), (ssm_p, ssm_s), w_gate[l].astype(BF16), b_gate[l], w_attn_up[l].astype(BF16), w_ssm_up[l].astype(BF16))
    x1, h2, idx, gate = _outproj_router(merged, w_out[l].astype(BF16), xa, xb, norm2[l], w_router[l], b_router[l])
    y_p, y_s = _moe(x1, h2, idx, gate, w_e_gate[l], b_e_gate[l], w_e_up[l], b_e_up[l], w_e_down[l], b_e_down[l], tp)

    def prompt_tail(a, n):
        return jnp.stack([a[(b + 1) * sp - n:(b + 1) * sp] for b in range(bp)])

    heads = lambda a: a.reshape(a.shape[0], a.shape[1], N_KV_HEADS, HEAD_DIM)
    new_k_p = heads(prompt_tail(k_rot, WINDOW))
    new_v_p = heads(prompt_tail(kv_p, WINDOW)[:, :, KV_WIDTH:])
    ks4 = heads(k_rot[tp:].reshape(bs, ss, KV_WIDTH))
    vs4 = heads(kv_p[tp:, KV_WIDTH:].reshape(bs, ss, KV_WIDTH))
    new_k_s = jnp.concatenate([cache_k[l], ks4], axis=1)[:, -kv_len:]
    new_v_s = jnp.concatenate([cache_v[l], vs4], axis=1)[:, -kv_len:]
    xbc_s = jnp.concatenate([state_conv[l], xbc[tp:].reshape(bs, ss, CONV_DIM)], axis=1)
    return (y_p.reshape(bp, sp, D_MODEL), y_s.reshape(bs, ss, D_MODEL),
            new_k_p[None], new_v_p[None], prompt_tail(xbc, CONV_W - 1)[None], hfin_p[None],
            new_k_s[None], new_v_s[None], xbc_s[:, -(CONV_W - 1):][None], hfin_s[None])
```

```python
import functools
import math

import jax
import jax.numpy as jnp
import numpy as np
from jax import lax
from jax.experimental import pallas as pl
from jax.experimental.pallas import tpu as pltpu

F32 = jnp.float32
BF16 = jnp.bfloat16
I32 = jnp.int32

D_MODEL = 2048
CHUNK = 64
N_HEADS = 32
N_KV_HEADS = 8
HEAD_DIM = 64
Q_PER_KV = N_HEADS // N_KV_HEADS
ATTN_WIDTH = N_HEADS * HEAD_DIM
KV_WIDTH = N_KV_HEADS * HEAD_DIM
WINDOW = 128
N_PREV_CHUNKS = WINDOW // CHUNK
ROPE_THETA = 500000.0
ROT_DIM = HEAD_DIM // 4
D_INNER = 2 * D_MODEL
SSM_HEAD_DIM = 64
SSM_HEADS = D_INNER // SSM_HEAD_DIM
SSM_GROUPS = 8
D_STATE = 128
CONV_W = 4
BC_WIDTH = SSM_GROUPS * D_STATE
CONV_DIM = D_INNER + 2 * BC_WIDTH
N_EXPERTS = 32
TOP_K = 4
D_FF = D_MODEL
SWIGLU_LIMIT = 7.0
SWIGLU_ALPHA = 1.702
EPS = 1e-6
NEG_INF = -1e30
PAST_LEN = 2048

LANES = 128
VMEM_LIMIT = 56 * 1024 * 1024
EXPERT_VMEM_LIMIT = 60 * 1024 * 1024
MOE_TM = 512
MOE_TF = 256
CACHED_TILES = 2


def _pick(n, cands):
    for c in cands:
        if n % c == 0:
            return c
    return n


def _cparams(sem):
    return pltpu.CompilerParams(dimension_semantics=sem, vmem_limit_bytes=VMEM_LIMIT)


def _split2(x):
    hi = x.astype(BF16)
    lo = (x - hi.astype(F32)).astype(BF16)
    return hi, lo


def _split3(x):
    hi = x.astype(BF16)
    r = x - hi.astype(F32)
    mid = r.astype(BF16)
    lo = (r - mid.astype(F32)).astype(BF16)
    return hi, mid, lo


def _dot(a, b):
    return jnp.dot(a, b, preferred_element_type=F32)


def _dot_nt(a, b):
    return lax.dot_general(a, b, (((1,), (1,)), ((), ())), preferred_element_type=F32)


def _silu(x):
    half = 0.5 * x
    return half + half * jnp.tanh(half)


def _first_rows(n_first):
    return lambda i, *_: (jnp.minimum(i, n_first - 1), 0)


def _later_rows(n_first):
    return lambda i, *_: (jnp.maximum(i - n_first, 0), 0)


def _by_part(i, n_first, fn, first_refs, later_refs):
    @pl.when(i < n_first)
    def _():
        fn(*first_refs)

    @pl.when(i >= n_first)
    def _():
        fn(*later_refs)


def _rms_kernel(xa_ref, xb_ref, w_ref, o_ref, *, n_first):
    def body(x_ref):
        x = x_ref[...]
        ms = jnp.mean(x * x, axis=-1, keepdims=True)
        o_ref[...] = (x * lax.rsqrt(ms + EPS) * w_ref[...]).astype(o_ref.dtype)

    _by_part(pl.program_id(0), n_first, body, (xa_ref,), (xb_ref,))


def _rmsnorm(xa, xb, w, out_dtype):
    (ma, d), mb = xa.shape, xb.shape[0]
    tm = _pick(math.gcd(ma, mb), (512, 256, 128, 64, 32))
    nf = ma // tm
    return pl.pallas_call(
        functools.partial(_rms_kernel, n_first=nf),
        grid=((ma + mb) // tm,),
        in_specs=[pl.BlockSpec((tm, d), _first_rows(nf)), pl.BlockSpec((tm, d), _later_rows(nf)),
                  pl.BlockSpec((1, d), lambda i: (0, 0))],
        out_specs=pl.BlockSpec((tm, d), lambda i: (i, 0)),
        out_shape=jax.ShapeDtypeStruct((ma + mb, d), out_dtype),
        compiler_params=_cparams(("parallel",)),
        name="rmsnorm",
    )(xa, xb, w.reshape(1, d))


def _mm_kernel(x_ref, w_ref, o_ref, wb_ref):
    @pl.when(pl.program_id(1) == 0)
    def _():
        wb_ref[...] = w_ref[0].astype(BF16)

    o_ref[...] = _dot_nt(x_ref[...], wb_ref[...]).astype(o_ref.dtype)


def _matmul(x, wt, layer, row0, n, out_dtype, name):
    m, k = x.shape
    tm = _pick(m, (1024, 512, 256, 128, 64, 32))
    tn = _pick(math.gcd(n, row0), (1024, 512, 256, 128))
    rb0 = row0 // tn
    return pl.pallas_call(
        _mm_kernel,
        grid=(n // tn, m // tm),
        in_specs=[pl.BlockSpec((tm, k), lambda j, i: (i, 0)),
                  pl.BlockSpec((1, tn, k), lambda j, i: (layer, rb0 + j, 0))],
        out_specs=pl.BlockSpec((tm, tn), lambda j, i: (i, j)),
        out_shape=jax.ShapeDtypeStruct((m, n), out_dtype),
        scratch_shapes=[pltpu.VMEM((tn, k), BF16)],
        compiler_params=_cparams(("parallel", "arbitrary")),
        name=name,
    )(x, wt)


def _qk_prep_kernel(q_ref, k_ref, cos_ref, s1_ref, s2_ref, qn_ref, kn_ref, g_ref, gt_ref, qo_ref, ko_ref):
    cos = cos_ref[...]
    s1 = s1_ref[...]
    s2 = s2_ref[...]

    def norm_rope(x, nw, width):
        g = g_ref[0:width, :]
        gt = gt_ref[:, 0:width]
        sq_hi, sq_lo = _split2(x * x)
        ssum = _dot(sq_hi, g) + _dot(sq_lo, g)
        r = lax.rsqrt(ssum * (1.0 / HEAD_DIM) + EPS)
        r_hi, r_lo = _split2(r)
        y = x * (_dot(r_hi, gt) + _dot(r_lo, gt)) * nw
        outs = []
        for s in range(width // LANES):
            blk = y[:, s * LANES:(s + 1) * LANES]
            outs.append(blk * cos + pltpu.roll(blk, ROT_DIM // 2, 1) * s1
                        + pltpu.roll(blk, LANES - ROT_DIM // 2, 1) * s2)
        return jnp.concatenate(outs, axis=1)

    q = norm_rope(q_ref[...], qn_ref[...], ATTN_WIDTH)
    qo_ref[...] = (q * (HEAD_DIM ** -0.5)).astype(qo_ref.dtype)
    ko_ref[...] = norm_rope(k_ref[...], kn_ref[...], KV_WIDTH)


def _rope_tables(seq, dec_seq, tm):
    half = ROT_DIM // 2
    inv_freq = np.float32(ROPE_THETA) ** (-np.arange(half, dtype=np.float32) * np.float32(2.0) / np.float32(ROT_DIM))
    pos = np.concatenate([np.arange(seq), np.tile(PAST_LEN + np.arange(dec_seq), tm // dec_seq)]).astype(np.float32)
    ang = pos[:, None] * inv_freq.astype(np.float32)[None, :]
    cos, sin = np.cos(ang).astype(np.float32), np.sin(ang).astype(np.float32)
    rows = pos.shape[0]
    ones = np.ones((rows, HEAD_DIM - ROT_DIM), np.float32)
    zeros = np.zeros((rows, HEAD_DIM - ROT_DIM), np.float32)
    zh = np.zeros((rows, half), np.float32)
    reps = (1, LANES // HEAD_DIM)
    return (np.tile(np.concatenate([cos, cos, ones], axis=1), reps),
            np.tile(np.concatenate([zh, sin, zeros], axis=1), reps),
            np.tile(np.concatenate([-sin, zh, zeros], axis=1), reps))


def _qk_prep(q, k, q_norm, k_norm, n_prompt, seq, dec_seq):
    m = q.shape[0]
    tm = _pick(math.gcd(math.gcd(seq, m - n_prompt), 256), (256, 128, 64, 32))
    cos_t, s1_t, s2_t = (jnp.asarray(t) for t in _rope_tables(seq, dec_seq, tm))
    prompt_tiles, seq_tiles = n_prompt // tm, seq // tm
    pos_rows = pl.BlockSpec((tm, LANES), lambda i: (jnp.where(i < prompt_tiles, i % seq_tiles, seq_tiles), 0))
    head_of_lane = jnp.arange(ATTN_WIDTH) // HEAD_DIM
    g = (head_of_lane[:, None] == jnp.arange(LANES)[None, :]).astype(BF16)
    gt = g.T
    qn = jnp.tile(q_norm.astype(F32), N_HEADS).reshape(1, ATTN_WIDTH)
    kn = jnp.tile(k_norm.astype(F32), N_KV_HEADS).reshape(1, KV_WIDTH)
    row = lambda w: pl.BlockSpec((tm, w), lambda i: (i, 0))
    full = lambda a: pl.BlockSpec(a.shape, lambda i: (0, 0))
    return pl.pallas_call(
        _qk_prep_kernel,
        grid=(m // tm,),
        in_specs=[row(ATTN_WIDTH), row(KV_WIDTH), pos_rows, pos_rows, pos_rows,
                  full(qn), full(kn), full(g), full(gt)],
        out_specs=[row(ATTN_WIDTH), row(KV_WIDTH)],
        out_shape=[jax.ShapeDtypeStruct((m, ATTN_WIDTH), BF16), jax.ShapeDtypeStruct((m, KV_WIDTH), F32)],
        compiler_params=_cparams(("parallel",)),
        name="qk_prep",
    )(q, k, cos_t, s1_t, s2_t, qn, kn, g, gt)


def _attend(q, kk, vv, sinks_ref, s_ref, p_ref, l_ref):
    tq = q.shape[0]
    for j in range(N_KV_HEADS):
        heads = [Q_PER_KV * j + g for g in range(Q_PER_KV)]
        q4 = jnp.concatenate([q[:, h * HEAD_DIM:(h + 1) * HEAD_DIM] for h in heads], axis=0)
        s_ref[j] = _dot_nt(q4, kk[:, j * HEAD_DIM:(j + 1) * HEAD_DIM])
    for j in range(N_KV_HEADS):
        heads = [Q_PER_KV * j + g for g in range(Q_PER_KV)]
        s = s_ref[j]
        sink = jnp.concatenate([jnp.full((tq, 1), sinks_ref[h], F32) for h in heads], axis=0)
        m = jnp.maximum(jnp.max(s, axis=-1, keepdims=True), sink)
        p = jnp.exp(s - m)
        l_ref[j] = jnp.broadcast_to(jnp.sum(p, axis=-1, keepdims=True) + jnp.exp(sink - m), l_ref.shape[1:])
        p_ref[j] = p.astype(BF16)
    outs = []
    for j in range(N_KV_HEADS):
        o = _dot(p_ref[j], vv[:, j * HEAD_DIM:(j + 1) * HEAD_DIM]) / l_ref[j][:, 0:HEAD_DIM]
        for g in range(Q_PER_KV):
            outs.append(o[g * tq:(g + 1) * tq, :])
    return jnp.concatenate(outs, axis=1)


def _attn_prompt_kernel(sinks_ref, q_ref, ka_ref, kb_ref, va_ref, vb_ref, o_ref, st_ref, pt_ref, rl_ref):
    i = pl.program_id(1)
    tq = q_ref.shape[0]
    tk = 2 * tq
    q = q_ref[...]
    k32 = jnp.concatenate([ka_ref[...], kb_ref[...]], axis=0)
    vt = jnp.concatenate([va_ref[...], vb_ref[...]], axis=0).T.astype(BF16)

    lane_k = lax.broadcasted_iota(I32, (tk, LANES), 1)
    kc = lax.broadcasted_iota(I32, (tk, LANES), 0) // CHUNK
    k_ind = jnp.where(lane_k == 0, jnp.where(kc == 0, 1.0, 0.0),
                      jnp.where(lane_k == 1, jnp.where(kc == N_PREV_CHUNKS + 1, 1.0, 0.0),
                                jnp.where(lane_k == 2, jnp.where(kc < N_PREV_CHUNKS, 1.0, 0.0), 0.0)))
    first_step = jnp.where(i == 0, 1.0, 0.0)
    k_ind = jnp.where(lane_k == 2, k_ind * first_step, k_ind).astype(BF16)
    lane_q = lax.broadcasted_iota(I32, (tq, LANES), 1)
    qc = lax.broadcasted_iota(I32, (tq, LANES), 0) // CHUNK
    q_msk = jnp.where(lane_q == 0, jnp.where(qc == 1, NEG_INF, 0.0),
                      jnp.where(lane_q == 1, jnp.where(qc == 0, NEG_INF, 0.0),
                                jnp.where(lane_q == 2, NEG_INF, 0.0))).astype(BF16)
    low = lane_k < HEAD_DIM

    for slab in range(KV_WIDTH // LANES):
        ks = k32[:, slab * LANES:(slab + 1) * LANES]
        kr = pltpu.roll(ks, HEAD_DIM, 1)
        for jj in range(2):
            j = 2 * slab + jj
            in_low = jnp.where(low, ks if jj == 0 else kr, 0.0).astype(BF16)
            in_high = jnp.where(low, 0.0, kr if jj == 0 else ks).astype(BF16)
            k_ext = (jnp.concatenate([in_low, k_ind], axis=1), jnp.concatenate([in_high, k_ind], axis=1))
            for g in range(Q_PER_KV):
                h = Q_PER_KV * j + g
                pair = h // 2
                q_ext = jnp.concatenate([q[:, pair * LANES:(pair + 1) * LANES], q_msk], axis=1)
                st_ref[h] = _dot_nt(k_ext[h % 2], q_ext)

    for h in range(N_HEADS):
        st = st_ref[h]
        sink = sinks_ref[h]
        m = jnp.maximum(jnp.max(st, axis=0, keepdims=True), sink)
        pt = jnp.exp(st - m)
        rl_ref[h:h + 1, :] = 1.0 / (jnp.sum(pt, axis=0, keepdims=True) + jnp.exp(sink - m))
        pt_ref[h] = pt.astype(BF16)

    for pair in range(N_HEADS // 2):
        pieces = []
        for h in (2 * pair, 2 * pair + 1):
            j = h // Q_PER_KV
            pieces.append(_dot(vt[j * HEAD_DIM:(j + 1) * HEAD_DIM, :], pt_ref[h]) * rl_ref[h:h + 1, :])
        o_ref[:, pair * LANES:(pair + 1) * LANES] = jnp.concatenate(pieces, axis=0).T.astype(o_ref.dtype)


def _attn_prompt(q, k, kv, sinks, bsz, seq):
    tq = 2 * CHUNK
    nb = seq // tq

    def kv_spec(back, col):
        return pl.BlockSpec((tq, KV_WIDTH), lambda b, i, s: (b * nb + jnp.maximum(i - back, 0), col))

    grid_spec = pltpu.PrefetchScalarGridSpec(
        num_scalar_prefetch=1,
        grid=(bsz, nb),
        in_specs=[pl.BlockSpec((tq, ATTN_WIDTH), lambda b, i, s: (b * nb + i, 0)),
                  kv_spec(1, 0), kv_spec(0, 0), kv_spec(1, 1), kv_spec(0, 1)],
        out_specs=pl.BlockSpec((tq, ATTN_WIDTH), lambda b, i, s: (b * nb + i, 0)),
        scratch_shapes=[pltpu.VMEM((N_HEADS, 2 * tq, tq), F32), pltpu.VMEM((N_HEADS, 2 * tq, tq), BF16),
                        pltpu.VMEM((N_HEADS, tq), F32)],
    )
    return pl.pallas_call(
        _attn_prompt_kernel,
        grid_spec=grid_spec,
        out_shape=jax.ShapeDtypeStruct((bsz * seq, ATTN_WIDTH), BF16),
        compiler_params=_cparams(("parallel", "parallel")),
        name="attn_prompt",
    )(sinks, q, k, k, kv, kv)


def _attn_sample_kernel(sinks_ref, q_ref, kn_ref, vn_ref, kc_ref, vc_ref, o_ref, s_ref, p_ref, l_ref):
    kk = jnp.concatenate([kc_ref[0], kn_ref[...]], axis=0).astype(BF16)
    vv = jnp.concatenate([vc_ref[0], vn_ref[...]], axis=0).astype(BF16)
    o_ref[...] = _attend(q_ref[...], kk, vv, sinks_ref, s_ref, p_ref, l_ref).astype(o_ref.dtype)


def _attn_sample(q, k, kv, cache_k, cache_v, sinks, row0, bsz, seq):
    blk0 = row0 // seq
    kv_len = cache_k.shape[1]
    new = lambda w, col: pl.BlockSpec((seq, w), lambda b, s: (blk0 + b, col))
    cache = pl.BlockSpec((1, kv_len, KV_WIDTH), lambda b, s: (b, 0, 0))
    grid_spec = pltpu.PrefetchScalarGridSpec(
        num_scalar_prefetch=1,
        grid=(bsz,),
        in_specs=[new(ATTN_WIDTH, 0), new(KV_WIDTH, 0), new(KV_WIDTH, 1), cache, cache],
        out_specs=pl.BlockSpec((seq, ATTN_WIDTH), lambda b, s: (b, 0)),
        scratch_shapes=[pltpu.VMEM((N_KV_HEADS, Q_PER_KV * seq, kv_len + seq), F32),
                        pltpu.VMEM((N_KV_HEADS, Q_PER_KV * seq, kv_len + seq), BF16),
                        pltpu.VMEM((N_KV_HEADS, Q_PER_KV * seq, LANES), F32)],
    )
    return pl.pallas_call(
        _attn_sample_kernel,
        grid_spec=grid_spec,
        out_shape=jax.ShapeDtypeStruct((bsz * seq, ATTN_WIDTH), BF16),
        compiler_params=_cparams(("parallel",)),
        name="attn_sample",
    )(sinks, q, k, kv, cache_k, cache_v)


def _pad_rows(x, rows):
    if x.shape[0] == rows:
        return x
    return jnp.concatenate([x, jnp.zeros((rows - x.shape[0], x.shape[1]), x.dtype)], axis=0)


def _ssd_kernel(*refs, clen, has_past):
    if has_past:
        (xbc_ref, dt_ref, z_ref, cw_ref, cb_ref, dtb_ref, alog_ref, dskip_ref, nw_ref, cpast_ref, hpast_ref,
         y_ref, hout_ref, xpad, u_ref, g_ref, ht_ref, ca_ref, xdt_ref, yst_ref, xw_ref, cb2_ref) = refs
    else:
        (xbc_ref, dt_ref, z_ref, cw_ref, cb_ref, dtb_ref, alog_ref, dskip_ref, nw_ref,
         y_ref, hout_ref, xpad, u_ref, g_ref, ht_ref, ca_ref, xdt_ref, yst_ref, xw_ref, cb2_ref) = refs
    L = clen
    c = pl.program_id(1)
    nc = pl.num_programs(1)
    n_pairs = SSM_HEADS // 2
    pairs_per_group = n_pairs // SSM_GROUPS
    gw = D_INNER // SSM_GROUPS

    @pl.when(c == 0)
    def _init():
        if has_past:
            for cidx in range(CONV_DIM // LANES):
                xpad[cidx, 5:8, :] = cpast_ref[0, :, cidx * LANES:(cidx + 1) * LANES]
            for p in range(n_pairs):
                both = jnp.concatenate([hpast_ref[0, 2 * p], hpast_ref[0, 2 * p + 1]], axis=0)
                ht_ref[:, p * LANES:(p + 1) * LANES] = both.T
        else:
            xpad[:, 0:8, :] = jnp.zeros((CONV_DIM // LANES, 8, LANES), F32)
            ht_ref[...] = jnp.zeros(ht_ref.shape, F32)

    n_col = CONV_DIM // LANES
    for cidx in range(n_col):
        xpad[cidx, 8:8 + L, :] = xbc_ref[:, cidx * LANES:(cidx + 1) * LANES]
    for cidx in range(n_col):
        sl = slice(cidx * LANES, (cidx + 1) * LANES)
        acc = cb_ref[:, sl] + xpad[cidx, 8:8 + L, :] * cw_ref[3:4, sl]
        acc = acc + xpad[cidx, 7:7 + L, :] * cw_ref[2:3, sl]
        acc = acc + xpad[cidx, 6:6 + L, :] * cw_ref[1:2, sl]
        acc = acc + xpad[cidx, 5:5 + L, :] * cw_ref[0:1, sl]
        u_ref[:, sl] = _silu(acc)
    for cidx in range(n_col):
        xpad[cidx, 5:8, :] = xpad[cidx, 5 + L:8 + L, :]

    dtx = dt_ref[...] + dtb_ref[...]
    dt = jnp.maximum(dtx, 0.0) + jnp.log(1.0 + jnp.exp(-jnp.abs(dtx)))
    loga = dt * (-jnp.exp(alog_ref[...]))
    t_idx = lax.broadcasted_iota(I32, (L, L), 0)
    s_idx = lax.broadcasted_iota(I32, (L, L), 1)
    incl = (s_idx <= t_idx).astype(BF16)
    acum = sum(_dot(incl, part) for part in _split3(loga))
    acum_t = _pad_rows(acum, LANES).T[:, 0:L]

    lane_m = lax.broadcasted_iota(I32, (L, 2 * L), 1)
    row_m = lax.broadcasted_iota(I32, (L, 2 * L), 0)
    left_m = lane_m < L
    causal_m = jnp.where(left_m, lane_m, lane_m - L) <= row_m
    left = lax.broadcasted_iota(I32, (L, LANES), 1) < SSM_HEAD_DIM
    b_of = lambda grp: u_ref[:, D_INNER + grp * D_STATE:D_INNER + (grp + 1) * D_STATE]
    c_of = lambda grp: u_ref[:, D_INNER + BC_WIDTH + grp * D_STATE:D_INNER + BC_WIDTH + (grp + 1) * D_STATE]
    pair_lanes = lambda p: slice(p * LANES, (p + 1) * LANES)
    group_lanes = lambda grp: slice(grp * gw, (grp + 1) * gw)

    for p in range(n_pairs):
        h0, h1, sl = 2 * p, 2 * p + 1, pair_lanes(p)
        ca_ref[:, sl] = jnp.where(left, acum[:, h0:h0 + 1], acum[:, h1:h1 + 1])
        xdt_ref[:, sl] = u_ref[:, sl] * jnp.where(left, dt[:, h0:h0 + 1], dt[:, h1:h1 + 1])

    for grp in range(SSM_GROUPS):
        c_bf = c_of(grp).astype(BF16)
        cb = _dot_nt(c_bf, b_of(grp).astype(BF16))
        cb2_ref[grp] = jnp.concatenate([cb, cb], axis=1)
        yst_ref[:, group_lanes(grp)] = _dot(c_bf, ht_ref[:, group_lanes(grp)].astype(BF16))

    for p in range(n_pairs):
        h0, h1, sl = 2 * p, 2 * p + 1, pair_lanes(p)
        ca = ca_ref[:, sl]
        ca_m = ca if L == SSM_HEAD_DIM else jnp.where(left_m, acum[:, h0:h0 + 1], acum[:, h1:h1 + 1])
        row_a = jnp.concatenate([acum_t[h0:h0 + 1, :], acum_t[h1:h1 + 1, :]], axis=1)
        dec = jnp.exp(jnp.where(causal_m, ca_m - row_a, NEG_INF))
        m_pair = (cb2_ref[p // pairs_per_group] * dec).astype(BF16)
        xdt = xdt_ref[:, sl]
        x_bd = jnp.concatenate([jnp.where(left, xdt, 0.0), jnp.where(left, 0.0, xdt)], axis=0).astype(BF16)
        y = _dot(m_pair, x_bd) + yst_ref[:, sl] * jnp.exp(ca) + u_ref[:, sl] * dskip_ref[:, sl]
        g_ref[:, sl] = y * _silu(z_ref[:, sl].astype(F32))
        xw_ref[:, sl] = (xdt * jnp.exp(ca[L - 1:L, :] - ca)).astype(BF16)

    for grp in range(SSM_GROUPS):
        gsl = group_lanes(grp)
        gg = g_ref[:, gsl]
        ms = jnp.mean(gg * gg, axis=-1, keepdims=True)
        y_ref[:, gsl] = (gg * lax.rsqrt(ms + EPS) * nw_ref[:, gsl]).astype(y_ref.dtype)
        bt_bf = _pad_rows(b_of(grp), LANES).T[:, 0:L].astype(BF16)
        ht_ref[:, gsl] = ht_ref[:, gsl] * jnp.exp(ca_ref[L - 1:L, gsl]) + _dot(bt_bf, xw_ref[:, gsl])

    @pl.when(c == nc - 1)
    def _fin():
        for p in range(n_pairs):
            both = ht_ref[:, p * LANES:(p + 1) * LANES].T
            hout_ref[0, 2 * p] = both[0:SSM_HEAD_DIM, :]
            hout_ref[0, 2 * p + 1] = both[SSM_HEAD_DIM:2 * SSM_HEAD_DIM, :]


def _ssd(xbc, dt_raw, z, conv_w, conv_b, dt_bias, a_log, d_skip, ssm_norm, row0, bsz, seq, clen,
         conv_past=None, ssm_past=None):
    nc = seq // clen
    blk0 = row0 // clen
    has_past = conv_past is not None
    pad = lambda a: jnp.pad(a.astype(F32), (0, LANES - SSM_HEADS)).reshape(1, LANES)
    params = [conv_w.astype(F32), conv_b.astype(F32).reshape(1, CONV_DIM), pad(dt_bias), pad(a_log),
              jnp.repeat(d_skip.astype(F32), SSM_HEAD_DIM).reshape(1, D_INNER), ssm_norm.astype(F32).reshape(1, D_INNER)]
    rows = lambda w: pl.BlockSpec((clen, w), lambda b, c: (blk0 + b * nc + c, 0))
    full = lambda a: pl.BlockSpec(a.shape, lambda b, c: (0,) * a.ndim)
    in_specs = [rows(CONV_DIM), rows(LANES), rows(D_INNER)] + [full(a) for a in params]
    args = [xbc, dt_raw, z] + params
    if has_past:
        in_specs += [pl.BlockSpec((1, CONV_W - 1, CONV_DIM), lambda b, c: (b, 0, 0)),
                     pl.BlockSpec((1, SSM_HEADS, SSM_HEAD_DIM, D_STATE), lambda b, c: (b, 0, 0, 0))]
        args += [conv_past.astype(F32), ssm_past.astype(F32)]
    return pl.pallas_call(
        functools.partial(_ssd_kernel, clen=clen, has_past=has_past),
        grid=(bsz, nc),
        in_specs=in_specs,
        out_specs=[pl.BlockSpec((clen, D_INNER), lambda b, c: (b * nc + c, 0)),
                   pl.BlockSpec((1, SSM_HEADS, SSM_HEAD_DIM, D_STATE), lambda b, c: (b, 0, 0, 0))],
        out_shape=[jax.ShapeDtypeStruct((bsz * seq, D_INNER), BF16),
                   jax.ShapeDtypeStruct((bsz, SSM_HEADS, SSM_HEAD_DIM, D_STATE), F32)],
        scratch_shapes=[pltpu.VMEM((CONV_DIM // LANES, 8 + clen, LANES), F32), pltpu.VMEM((clen, CONV_DIM), F32),
                        pltpu.VMEM((clen, D_INNER), F32), pltpu.VMEM((D_STATE, D_INNER), F32),
                        pltpu.VMEM((clen, D_INNER), F32), pltpu.VMEM((clen, D_INNER), F32),
                        pltpu.VMEM((clen, D_INNER), F32), pltpu.VMEM((clen, D_INNER), BF16),
                        pltpu.VMEM((SSM_GROUPS, clen, 2 * clen), F32)],
        compiler_params=_cparams(("parallel", "arbitrary")),
        name="ssd_past" if has_past else "ssd_prompt",
    )(*args)


def _merge_kernel(h_ref, a1_ref, a2_ref, s1_ref, s2_ref, wga_ref, wgs_ref, bga_ref, bgs_ref, wa_ref, ws_ref, o_ref,
                  *, n_first):
    def body(a_ref, s_ref):
        h = h_ref[...]
        g_a = jax.nn.sigmoid(_dot(h, wga_ref[...]) + bga_ref[...])
        g_s = jax.nn.sigmoid(_dot(h, wgs_ref[...]) + bgs_ref[...])
        o_ref[...] = (g_a * _dot(a_ref[...], wa_ref[...]) + g_s * _dot(s_ref[...], ws_ref[...])).astype(o_ref.dtype)

    _by_part(pl.program_id(0), n_first, body, (a1_ref, s1_ref), (a2_ref, s2_ref))


def _merge(h, attn_parts, ssm_parts, w_gate, b_gate, w_attn_up, w_ssm_up):
    m = h.shape[0]
    m1, m2 = attn_parts[0].shape[0], attn_parts[1].shape[0]
    tm = _pick(math.gcd(m1, m2), (512, 256, 128, 64, 32))
    nf = m1 // tm
    tn = 512
    nj = D_MODEL // tn
    bg = b_gate.astype(F32).reshape(1, 2 * D_MODEL)
    return pl.pallas_call(
        functools.partial(_merge_kernel, n_first=nf),
        grid=(m // tm, nj),
        in_specs=[pl.BlockSpec((tm, D_MODEL), lambda i, j: (i, 0)),
                  pl.BlockSpec((tm, ATTN_WIDTH), _first_rows(nf)), pl.BlockSpec((tm, ATTN_WIDTH), _later_rows(nf)),
                  pl.BlockSpec((tm, D_INNER), _first_rows(nf)), pl.BlockSpec((tm, D_INNER), _later_rows(nf)),
                  pl.BlockSpec((D_MODEL, tn), lambda i, j: (0, j)),
                  pl.BlockSpec((D_MODEL, tn), lambda i, j: (0, j + nj)),
                  pl.BlockSpec((1, tn), lambda i, j: (0, j)),
                  pl.BlockSpec((1, tn), lambda i, j: (0, j + nj)),
                  pl.BlockSpec((ATTN_WIDTH, tn), lambda i, j: (0, j)),
                  pl.BlockSpec((D_INNER, tn), lambda i, j: (0, j))],
        out_specs=pl.BlockSpec((tm, tn), lambda i, j: (i, j)),
        out_shape=jax.ShapeDtypeStruct((m, D_MODEL), BF16),
        compiler_params=_cparams(("parallel", "parallel")),
        name="merge",
    )(h, attn_parts[0], attn_parts[1], ssm_parts[0], ssm_parts[1], w_gate, w_gate, bg, bg, w_attn_up, w_ssm_up)


def _outproj_router_kernel(mg_ref, wo_ref, xa_ref, xb_ref, n2_ref, wr_ref, br_ref, x1_ref, h2_ref, idx_ref, gate_ref,
                           *, n_first):
    def residual(x_ref):
        x1_ref[...] = x_ref[...] + _dot(mg_ref[...], wo_ref[...])

    _by_part(pl.program_id(0), n_first, residual, (xa_ref,), (xb_ref,))
    x1 = x1_ref[...]
    ms = jnp.mean(x1 * x1, axis=-1, keepdims=True)
    h2 = x1 * lax.rsqrt(ms + EPS) * n2_ref[...]
    h2_ref[...] = _pack_bf16_pairs(h2)
    h_hi, h_lo = _split2(h2)
    w_hi, w_lo = _split2(wr_ref[...])
    logits = _dot(h_hi, w_hi) + (_dot(h_hi, w_lo) + _dot(h_lo, w_hi)) + br_ref[...]
    tm = logits.shape[0]
    lane = lax.broadcasted_iota(I32, (tm, LANES), 1)
    logits = jnp.where(lane < N_EXPERTS, logits, -jnp.inf)
    idx_out = jnp.zeros((tm, LANES), I32)
    val_out = jnp.zeros((tm, LANES), F32)
    top = None
    for k in range(TOP_K):
        v = jnp.max(logits, axis=-1, keepdims=True)
        i = jnp.min(jnp.where(logits == v, lane, LANES), axis=-1, keepdims=True)
        if k == 0:
            top = v
        idx_out = jnp.where(lane == k, i, idx_out)
        val_out = jnp.where(lane == k, jnp.exp(v - top), val_out)
        logits = jnp.where(lane == i, -jnp.inf, logits)
    idx_ref[...] = idx_out
    gate_ref[...] = val_out / jnp.sum(val_out, axis=-1, keepdims=True)


def _outproj_router(merged, w_out, xa, xb, norm2, w_router, b_router):
    m = merged.shape[0]
    tm = _pick(math.gcd(xa.shape[0], xb.shape[0]), (512, 256, 128, 64, 32))
    nf = xa.shape[0] // tm
    wr = jnp.pad(w_router.astype(F32), ((0, 0), (0, LANES - N_EXPERTS)))
    br = jnp.pad(b_router.astype(F32), (0, LANES - N_EXPERTS)).reshape(1, LANES)
    row = lambda w: pl.BlockSpec((tm, w), lambda i: (i, 0))
    full = lambda r, c: pl.BlockSpec((r, c), lambda i: (0, 0))
    return pl.pallas_call(
        functools.partial(_outproj_router_kernel, n_first=nf),
        grid=(m // tm,),
        in_specs=[row(D_MODEL), full(D_MODEL, D_MODEL),
                  pl.BlockSpec((tm, D_MODEL), _first_rows(nf)), pl.BlockSpec((tm, D_MODEL), _later_rows(nf)),
                  full(1, D_MODEL), full(D_MODEL, LANES), full(1, LANES)],
        out_specs=[row(D_MODEL), row(D_MODEL // 2), row(LANES), row(LANES)],
        out_shape=[jax.ShapeDtypeStruct((m, D_MODEL), F32), jax.ShapeDtypeStruct((m, D_MODEL // 2), I32),
                   jax.ShapeDtypeStruct((m, LANES), I32), jax.ShapeDtypeStruct((m, LANES), F32)],
        compiler_params=_cparams(("parallel",)),
        name="outproj_router",
    )(merged, w_out, xa, xb, norm2.astype(F32).reshape(1, D_MODEL), wr, br)


def _rank_kernel(idx_ref, rank_ref, cnt_ref, base_ref):
    i = pl.program_id(0)
    tt = idx_ref.shape[0]

    @pl.when(i == 0)
    def _():
        base_ref[...] = jnp.zeros(base_ref.shape, F32)

    idx = idx_ref[...]
    lane = lax.broadcasted_iota(I32, (tt, LANES), 1)
    sel = [lane == idx[:, k:k + 1] for k in range(TOP_K)]
    onehot = jnp.zeros((tt, LANES), F32)
    for k in range(TOP_K):
        onehot = jnp.where(sel[k], 1.0, onehot)
    r_idx = lax.broadcasted_iota(I32, (tt, tt), 0)
    c_idx = lax.broadcasted_iota(I32, (tt, tt), 1)
    before = (c_idx < r_idx).astype(BF16)
    rank_all = _dot(before, onehot.astype(BF16)) + base_ref[0:1, :]
    out = jnp.zeros((tt, LANES), F32)
    for k in range(TOP_K):
        out = jnp.where(lane == k, jnp.sum(jnp.where(sel[k], rank_all, 0.0), axis=-1, keepdims=True), out)
    rank_ref[...] = out.astype(I32)
    base_ref[0:1, :] = base_ref[0:1, :] + jnp.sum(onehot, axis=0, keepdims=True)
    cnt_ref[...] = base_ref[...].astype(I32)


def _ranks(idx):
    m = idx.shape[0]
    tt = _pick(m, (512, 256, 128, 64, 32))
    return pl.pallas_call(
        _rank_kernel,
        grid=(m // tt,),
        in_specs=[pl.BlockSpec((tt, LANES), lambda i: (i, 0))],
        out_specs=[pl.BlockSpec((tt, LANES), lambda i: (i, 0)), pl.BlockSpec((8, LANES), lambda i: (0, 0))],
        out_shape=[jax.ShapeDtypeStruct((m, LANES), I32), jax.ShapeDtypeStruct((8, LANES), I32)],
        scratch_shapes=[pltpu.VMEM((8, LANES), F32)],
        compiler_params=_cparams(("arbitrary",)),
        name="expert_ranks",
    )(idx)


def _dispatch_kernel(dest_ref, fill_ref, h_ref, xs_hbm, zero_ref, sem, zsem, *, tt, tm, n_blocks):
    @pl.when(pl.program_id(0) == 0)
    def _():
        zero_ref[...] = jnp.zeros(zero_ref.shape, zero_ref.dtype)

        def block_copy(b):
            return pltpu.make_async_copy(zero_ref, xs_hbm.at[pl.ds(pl.multiple_of(b * tm, tm), tm), :], zsem)

        def start(b, carry):
            @pl.when(fill_ref[b] != 0)
            def _():
                block_copy(b).start()

            return carry

        def done(b, carry):
            @pl.when(fill_ref[b] != 0)
            def _():
                block_copy(b).wait()

            return carry

        lax.fori_loop(0, n_blocks, start, 0)
        lax.fori_loop(0, n_blocks, done, 0)

    base = pl.program_id(0) * (tt * TOP_K)

    def issue(r, carry):
        for k in range(TOP_K):
            dst = dest_ref[base + r * TOP_K + k]
            pltpu.make_async_copy(h_ref.at[pl.ds(r, 1), :], xs_hbm.at[pl.ds(dst, 1), :], sem).start()
        return carry

    lax.fori_loop(0, tt, issue, 0, unroll=True)
    for _ in range(TOP_K):
        pltpu.make_async_copy(h_ref, xs_hbm.at[pl.ds(0, tt), :], sem).wait()


def _dispatch(dest_flat, fill_block, h2, tm):
    m = h2.shape[0]
    n_blocks = fill_block.shape[0]
    tt = _pick(m, (256, 128, 64, 32))
    width = h2.shape[1]
    grid_spec = pltpu.PrefetchScalarGridSpec(
        num_scalar_prefetch=2,
        grid=(m // tt,),
        in_specs=[pl.BlockSpec((tt, width), lambda i, d, f: (i, 0))],
        out_specs=pl.BlockSpec(memory_space=pl.ANY),
        scratch_shapes=[pltpu.VMEM((tm, width), h2.dtype), pltpu.SemaphoreType.DMA, pltpu.SemaphoreType.DMA],
    )
    return pl.pallas_call(
        functools.partial(_dispatch_kernel, tt=tt, tm=tm, n_blocks=n_blocks),
        grid_spec=grid_spec,
        out_shape=jax.ShapeDtypeStruct((n_blocks * tm, width), h2.dtype),
        compiler_params=_cparams(("arbitrary",)),
        name="dispatch",
    )(dest_flat, fill_block, h2)


def _pack_bf16_pairs(x):
    c = x.shape[1] // 2
    bits = lax.bitcast_convert_type(x.astype(BF16).astype(F32), I32)
    return lax.shift_right_logical(bits[:, :c], 16) | bits[:, c:]


def _unpack_bf16_pairs(p):
    lo = lax.bitcast_convert_type(lax.shift_left(p, 16), F32).astype(BF16)
    hi = lax.bitcast_convert_type(p & jnp.int32(-65536), F32).astype(BF16)
    return lo, hi


def _expert_kernel(be_ref, nu_ref, rows_ref, xs_ref, bg_ref, bu_ref, bd_ref, wg_hbm, wu_hbm, wd_hbm, ys_ref,
                   cg, cu, cd, stg_g, stg_u, stg_d, xb_ref, sems, *, tf):
    i = pl.program_id(0)
    nj = D_FF // tf
    half = D_MODEL // 2
    e = be_ref[i]
    first = jnp.logical_or(i == 0, e != be_ref[jnp.maximum(i - 1, 0)])
    e_next = be_ref[jnp.minimum(i + 1, pl.num_programs(0) - 1)]
    next_is_new = jnp.logical_and(i + 1 < nu_ref[0], e_next != e)

    def tile_copies(t, slot, expert=e):
        cols = pl.ds(pl.multiple_of(t * tf, tf), tf)
        return (pltpu.make_async_copy(wg_hbm.at[expert, :, cols], stg_g.at[slot], sems.at[slot, 0]),
                pltpu.make_async_copy(wu_hbm.at[expert, :, cols], stg_u.at[slot], sems.at[slot, 1]),
                pltpu.make_async_copy(wd_hbm.at[expert, cols, :], stg_d.at[slot], sems.at[slot, 2]))

    @pl.when(i < nu_ref[0])
    def _():
        @pl.when(i == 0)
        def _():
            for cp in tile_copies(0, 0):
                cp.start()

        lo, hi = _unpack_bf16_pairs(xs_ref[...])
        xb_ref[:, :half] = lo
        xb_ref[:, half:] = hi

        def refill(t):
            slot = t % 2

            @pl.when(t + 1 < nj)
            def _():
                for cp in tile_copies(t + 1, 1 - slot):
                    cp.start()

            for cp in tile_copies(t, slot):
                cp.wait()
            cg[t] = stg_g[slot].astype(BF16)
            cu[t] = stg_u[slot].astype(BF16)
            cd[t] = stg_d[slot].astype(BF16)

        def compute(t, opening):
            xb = xb_ref[...]
            g = jnp.minimum(_dot(xb, cg[t]) + bg_ref[0, t], SWIGLU_LIMIT)
            u = jnp.clip(_dot(xb, cu[t]) + bu_ref[0, t], -SWIGLU_LIMIT, SWIGLU_LIMIT)
            act = (u + 1.0) * (g * jax.nn.sigmoid(SWIGLU_ALPHA * g))
            part = _dot(act.astype(BF16), cd[t])
            if opening:
                ys_ref[...] = part + bd_ref[0]
            else:
                ys_ref[...] += part

        @pl.when(first)
        def _():
            refill(jnp.int32(0))
            compute(0, True)

            def step(t, carry):
                refill(t)
                compute(t, False)
                return carry

            lax.fori_loop(1, nj, step, 0)

        def compute_wide(t, opening, rows):
            xb = xb_ref[0:rows, :]
            span = range(t, t + CACHED_TILES)
            wide = lambda ref: jnp.concatenate([ref[k] for k in span], axis=1)
            bias = lambda ref: jnp.concatenate([ref[0, k] for k in span], axis=1)
            g = jnp.minimum(_dot(xb, wide(cg)) + bias(bg_ref), SWIGLU_LIMIT)
            u = jnp.clip(_dot(xb, wide(cu)) + bias(bu_ref), -SWIGLU_LIMIT, SWIGLU_LIMIT)
            act = (u + 1.0) * (g * jax.nn.sigmoid(SWIGLU_ALPHA * g))
            part = _dot(act.astype(BF16), jnp.concatenate([cd[k] for k in span], axis=0))
            if opening:
                ys_ref[0:rows, :] = part + bd_ref[0]
            else:
                ys_ref[0:rows, :] += part

        tm = ys_ref.shape[0]
        short = rows_ref[i] <= tm // 2

        @pl.when(jnp.logical_and(jnp.logical_not(first), jnp.logical_not(short)))
        def _():
            for t in range(0, nj, CACHED_TILES):
                compute_wide(t, t == 0, tm)

        @pl.when(jnp.logical_and(jnp.logical_not(first), short))
        def _():
            ys_ref[tm // 2:, :] = jnp.zeros((tm - tm // 2, D_MODEL), F32)
            for t in range(0, nj, CACHED_TILES):
                compute_wide(t, t == 0, tm // 2)

        @pl.when(next_is_new)
        def _():
            for cp in tile_copies(0, 0, e_next):
                cp.start()

    @pl.when(i >= nu_ref[0])
    def _():
        ys_ref[...] = jnp.zeros(ys_ref.shape, F32)


def _experts(block_e, n_used, block_rows, xs, w_e_gate, b_e_gate, w_e_up, b_e_up, w_e_down, b_e_down):
    n_rows = xs.shape[0]
    tm, tf = MOE_TM, MOE_TF
    n_blocks = n_rows // tm
    nj = D_FF // tf
    half = D_MODEL // 2

    def blk(i, nu):
        return jnp.minimum(i, nu[0] - 1)

    hbm = pl.BlockSpec(memory_space=pl.ANY)
    grid_spec = pltpu.PrefetchScalarGridSpec(
        num_scalar_prefetch=3,
        grid=(n_blocks,),
        in_specs=[pl.BlockSpec((tm, half), lambda i, be, nu, br: (blk(i, nu), 0)),
                  pl.BlockSpec((1, nj, 1, tf), lambda i, be, nu, br: (be[blk(i, nu)], 0, 0, 0)),
                  pl.BlockSpec((1, nj, 1, tf), lambda i, be, nu, br: (be[blk(i, nu)], 0, 0, 0)),
                  pl.BlockSpec((1, 1, D_MODEL), lambda i, be, nu, br: (be[blk(i, nu)], 0, 0)),
                  hbm, hbm, hbm],
        out_specs=pl.BlockSpec((tm, D_MODEL), lambda i, be, nu, br: (i, 0)),
        scratch_shapes=[pltpu.VMEM((nj, D_MODEL, tf), BF16), pltpu.VMEM((nj, D_MODEL, tf), BF16),
                        pltpu.VMEM((nj, tf, D_MODEL), BF16),
                        pltpu.VMEM((2, D_MODEL, tf), F32), pltpu.VMEM((2, D_MODEL, tf), F32),
                        pltpu.VMEM((2, tf, D_MODEL), F32),
                        pltpu.VMEM((tm, D_MODEL), BF16), pltpu.SemaphoreType.DMA((2, 3))],
    )
    return pl.pallas_call(
        functools.partial(_expert_kernel, tf=tf),
        grid_spec=grid_spec,
        out_shape=jax.ShapeDtypeStruct((n_rows, D_MODEL), F32),
        compiler_params=pltpu.CompilerParams(dimension_semantics=("arbitrary",), vmem_limit_bytes=EXPERT_VMEM_LIMIT),
        name="experts",
    )(block_e, n_used, block_rows, xs, b_e_gate.reshape(N_EXPERTS, nj, 1, tf), b_e_up.reshape(N_EXPERTS, nj, 1, tf),
      b_e_down.reshape(N_EXPERTS, 1, D_MODEL), w_e_gate, w_e_up, w_e_down)


def _combine_kernel(dest_ref, x1_ref, gate_ref, ys_hbm, o_ref, buf, sems, *, tt, row0):
    i = pl.program_id(0)
    n_steps = pl.num_programs(0)
    slot = i % 2

    def gather(step, to_slot):
        base = (row0 + step * tt) * TOP_K

        def issue(r, carry):
            for k in range(TOP_K):
                src = dest_ref[base + r * TOP_K + k]
                pltpu.make_async_copy(ys_hbm.at[pl.ds(src, 1), :], buf.at[to_slot, k, pl.ds(r, 1), :],
                                      sems.at[to_slot]).start()
            return carry

        lax.fori_loop(0, tt, issue, 0, unroll=True)

    @pl.when(i == 0)
    def _():
        gather(0, 0)

    @pl.when(i + 1 < n_steps)
    def _():
        gather(i + 1, 1 - slot)

    for k in range(TOP_K):
        pltpu.make_async_copy(ys_hbm.at[pl.ds(0, tt), :], buf.at[slot, k], sems.at[slot]).wait()
    gate = gate_ref[...]
    acc = x1_ref[...]
    for k in range(TOP_K):
        acc = acc + gate[:, k:k + 1] * buf[slot, k]
    o_ref[...] = acc


def _combine(dest_flat, x1, gate, ys, row0, rows):
    tt = _pick(math.gcd(row0, rows), (256, 128, 64, 32))
    blk0 = row0 // tt
    grid_spec = pltpu.PrefetchScalarGridSpec(
        num_scalar_prefetch=1,
        grid=(rows // tt,),
        in_specs=[pl.BlockSpec((tt, D_MODEL), lambda i, d: (blk0 + i, 0)),
                  pl.BlockSpec((tt, LANES), lambda i, d: (blk0 + i, 0)),
                  pl.BlockSpec(memory_space=pl.ANY)],
        out_specs=pl.BlockSpec((tt, D_MODEL), lambda i, d: (i, 0)),
        scratch_shapes=[pltpu.VMEM((2, TOP_K, tt, D_MODEL), F32), pltpu.SemaphoreType.DMA((2,))],
    )
    return pl.pallas_call(
        functools.partial(_combine_kernel, tt=tt, row0=row0),
        grid_spec=grid_spec,
        out_shape=jax.ShapeDtypeStruct((rows, D_MODEL), F32),
        compiler_params=_cparams(("arbitrary",)),
        name="combine",
    )(dest_flat, x1, gate, ys)


def _moe(x1, h2, idx, gate, w_e_gate, b_e_gate, w_e_up, b_e_up, w_e_down, b_e_down, n_first):
    m = x1.shape[0]
    tm = MOE_TM
    ranks, cnt = _ranks(idx)
    counts = cnt[0, :N_EXPERTS]
    padded = (counts + tm - 1) // tm * tm
    pad_end = jnp.cumsum(padded)
    pad_start = pad_end - padded
    e_sel = idx[:, :TOP_K]
    dest = (pad_start[e_sel] + ranks[:, :TOP_K]).astype(I32).reshape(-1)
    n_blocks = -(-(m * TOP_K + N_EXPERTS * (tm - 1)) // tm)
    block_row = jnp.arange(n_blocks, dtype=I32) * tm
    block_e = jnp.minimum(jnp.sum((pad_end[None, :] <= block_row[:, None]).astype(I32), axis=1), N_EXPERTS - 1)
    n_used = (pad_end[-1] // tm).astype(I32).reshape(1)
    is_last = jnp.sum((pad_end[None, :] == block_row[:, None] + tm).astype(I32), axis=1) > 0
    fill_block = jnp.logical_or(is_last, block_row >= pad_end[-1]).astype(I32)
    xs = _dispatch(dest, fill_block, h2, tm)
    block_rows = jnp.clip((pad_start + counts)[block_e] - block_row, 0, tm).astype(I32)
    ys = _experts(block_e, n_used, block_rows, xs, w_e_gate, b_e_gate, w_e_up, b_e_up, w_e_down, b_e_down)
    return _combine(dest, x1, gate, ys, 0, n_first), _combine(dest, x1, gate, ys, n_first, m - n_first)


def kernel(x_prompt, x_sample, cache_k, cache_v, state_conv, state_ssm, norm1, w_in, q_norm, k_norm, sinks, conv_w,
           conv_b, dt_bias, a_log, d_skip, ssm_norm, w_gate, b_gate, w_attn_up, w_ssm_up, w_out, norm2, w_router,
           b_router, w_e_gate, b_e_gate, w_e_up, b_e_up, w_e_down, b_e_down):
    bp, sp, _ = x_prompt.shape
    bs, ss, _ = x_sample.shape
    tp, ts = bp * sp, bs * ss
    kv_len = cache_k.shape[2]
    l = 0

    xa, xb = x_prompt.reshape(tp, D_MODEL), x_sample.reshape(ts, D_MODEL)

    o1 = ATTN_WIDTH
    o2 = o1 + 2 * KV_WIDTH
    o3 = o2 + D_INNER
    o4 = o3 + CONV_DIM
    h = _rmsnorm(xa, xb, norm1[l], BF16)
    w_in_t = jnp.swapaxes(w_in, 1, 2)
    q_p = _matmul(h, w_in_t, l, 0, o1, F32, "proj_q")
    kv_p = _matmul(h, w_in_t, l, o1, o2 - o1, F32, "proj_kv")
    z = _matmul(h, w_in_t, l, o2, o3 - o2, BF16, "proj_z")
    xbc = _matmul(h, w_in_t, l, o3, o4 - o3, F32, "proj_xbc")
    w_dt = jnp.pad(w_in_t[l:l + 1, o4:, :], ((0, 0), (0, LANES - SSM_HEADS), (0, 0)))
    dt_raw = _matmul(h, w_dt, 0, 0, LANES, F32, "proj_dt")

    q_rot, k_rot = _qk_prep(q_p, kv_p, q_norm[l], k_norm[l], tp, sp, ss)
    sk = sinks[l].astype(F32)
    attn_p = _attn_prompt(q_rot, k_rot, kv_p, sk, bp, sp)
    ck = cache_k[l].reshape(bs, kv_len, KV_WIDTH)
    cv = cache_v[l].reshape(bs, kv_len, KV_WIDTH)
    attn_s = _attn_sample(q_rot, k_rot, kv_p, ck, cv, sk, tp, bs, ss)

    ssd_w = (conv_w[l], conv_b[l], dt_bias[l], a_log[l], d_skip[l], ssm_norm[l])
    ssm_p, hfin_p = _ssd(xbc, dt_raw, z, *ssd_w, 0, bp, sp, CHUNK)
    ssm_s, hfin_s = _ssd(xbc, dt_raw, z, *ssd_w, tp, bs, ss, ss, state_conv[l], state_ssm[l])

    merged = _merge(h, (attn_p, attn_s), (ssm_p, ssm_s), w_gate[l].astype(BF16), b_gate[l], w_attn_up[l].astype(BF16), w_ssm_up[l].astype(BF16))
    x1, h2, idx, gate = _outproj_router(merged, w_out[l].astype(BF16), xa, xb, norm2[l], w_router[l], b_router[l])
    y_p, y_s = _moe(x1, h2, idx, gate, w_e_gate[l], b_e_gate[l], w_e_up[l], b_e_up[l], w_e_down[l], b_e_down[l], tp)

    def prompt_tail(a, n):
        return jnp.stack([a[(b + 1) * sp - n:(b + 1) * sp] for b in range(bp)])

    heads = lambda a: a.reshape(a.shape[0], a.shape[1], N_KV_HEADS, HEAD_DIM)
    new_k_p = heads(prompt_tail(k_rot, WINDOW))
    new_v_p = heads(prompt_tail(kv_p, WINDOW)[:, :, KV_WIDTH:])
    ks4 = heads(k_rot[tp:].reshape(bs, ss, KV_WIDTH))
    vs4 = heads(kv_p[tp:, KV_WIDTH:].reshape(bs, ss, KV_WIDTH))
    new_k_s = jnp.concatenate([cache_k[l], ks4], axis=1)[:, -kv_len:]
    new_v_s = jnp.concatenate([cache_v[l], vs4], axis=1)[:, -kv_len:]
    xbc_s = jnp.concatenate([state_conv[l], xbc[tp:].reshape(bs, ss, CONV_DIM)], axis=1)
    return (y_p.reshape(bp, sp, D_MODEL), y_s.reshape(bs, ss, D_MODEL),
            new_k_p[None], new_v_p[None], prompt_tail(xbc, CONV_W - 1)[None], hfin_p[None],
            new_k_s[None], new_v_s[None], xbc_s[:, -(CONV_W - 1):][None], hfin_s[None])
```

```python
import functools
import math

import jax
import jax.numpy as jnp
import numpy as np
from jax import lax
from jax.experimental import pallas as pl
from jax.experimental.pallas import tpu as pltpu

F32 = jnp.float32
BF16 = jnp.bfloat16
I32 = jnp.int32

D_MODEL = 2048
CHUNK = 64
N_HEADS = 32
N_KV_HEADS = 8
HEAD_DIM = 64
Q_PER_KV = N_HEADS // N_KV_HEADS
ATTN_WIDTH = N_HEADS * HEAD_DIM
KV_WIDTH = N_KV_HEADS * HEAD_DIM
WINDOW = 128
N_PREV_CHUNKS = WINDOW // CHUNK
ROPE_THETA = 500000.0
ROT_DIM = HEAD_DIM // 4
D_INNER = 2 * D_MODEL
SSM_HEAD_DIM = 64
SSM_HEADS = D_INNER // SSM_HEAD_DIM
SSM_GROUPS = 8
D_STATE = 128
CONV_W = 4
BC_WIDTH = SSM_GROUPS * D_STATE
CONV_DIM = D_INNER + 2 * BC_WIDTH
N_EXPERTS = 32
TOP_K = 4
D_FF = D_MODEL
SWIGLU_LIMIT = 7.0
SWIGLU_ALPHA = 1.702
EPS = 1e-6
NEG_INF = -1e30
PAST_LEN = 2048

LANES = 128
VMEM_LIMIT = 56 * 1024 * 1024
EXPERT_VMEM_LIMIT = 60 * 1024 * 1024
MOE_TM = 512
MOE_TF = 256
CACHED_TILES = 2


def _pick(n, cands):
    for c in cands:
        if n % c == 0:
            return c
    return n


def _cparams(sem):
    return pltpu.CompilerParams(dimension_semantics=sem, vmem_limit_bytes=VMEM_LIMIT)


def _split2(x):
    hi = x.astype(BF16)
    lo = (x - hi.astype(F32)).astype(BF16)
    return hi, lo


def _split3(x):
    hi = x.astype(BF16)
    r = x - hi.astype(F32)
    mid = r.astype(BF16)
    lo = (r - mid.astype(F32)).astype(BF16)
    return hi, mid, lo


def _dot(a, b):
    return jnp.dot(a, b, preferred_element_type=F32)


def _dot_nt(a, b):
    return lax.dot_general(a, b, (((1,), (1,)), ((), ())), preferred_element_type=F32)


def _silu(x):
    half = 0.5 * x
    return half + half * jnp.tanh(half)


def _first_rows(n_first):
    return lambda i, *_: (jnp.minimum(i, n_first - 1), 0)


def _later_rows(n_first):
    return lambda i, *_: (jnp.maximum(i - n_first, 0), 0)


def _by_part(i, n_first, fn, first_refs, later_refs):
    @pl.when(i < n_first)
    def _():
        fn(*first_refs)

    @pl.when(i >= n_first)
    def _():
        fn(*later_refs)


def _rms_kernel(xa_ref, xb_ref, w_ref, o_ref, *, n_first):
    def body(x_ref):
        x = x_ref[...]
        ms = jnp.mean(x * x, axis=-1, keepdims=True)
        o_ref[...] = (x * lax.rsqrt(ms + EPS) * w_ref[...]).astype(o_ref.dtype)

    _by_part(pl.program_id(0), n_first, body, (xa_ref,), (xb_ref,))


def _rmsnorm(xa, xb, w, out_dtype):
    (ma, d), mb = xa.shape, xb.shape[0]
    tm = _pick(math.gcd(ma, mb), (512, 256, 128, 64, 32))
    nf = ma // tm
    return pl.pallas_call(
        functools.partial(_rms_kernel, n_first=nf),
        grid=((ma + mb) // tm,),
        in_specs=[pl.BlockSpec((tm, d), _first_rows(nf)), pl.BlockSpec((tm, d), _later_rows(nf)),
                  pl.BlockSpec((1, d), lambda i: (0, 0))],
        out_specs=pl.BlockSpec((tm, d), lambda i: (i, 0)),
        out_shape=jax.ShapeDtypeStruct((ma + mb, d), out_dtype),
        compiler_params=_cparams(("parallel",)),
        name="rmsnorm",
    )(xa, xb, w.reshape(1, d))


def _mm_kernel(x_ref, w_ref, o_ref, wb_ref):
    @pl.when(pl.program_id(1) == 0)
    def _():
        wb_ref[...] = w_ref[0].astype(BF16)

    o_ref[...] = _dot_nt(x_ref[...], wb_ref[...]).astype(o_ref.dtype)


def _matmul(x, wt, layer, row0, n, out_dtype, name):
    m, k = x.shape
    tm = _pick(m, (1024, 512, 256, 128, 64, 32))
    tn = _pick(math.gcd(n, row0), (1024, 512, 256, 128))
    rb0 = row0 // tn
    return pl.pallas_call(
        _mm_kernel,
        grid=(n // tn, m // tm),
        in_specs=[pl.BlockSpec((tm, k), lambda j, i: (i, 0)),
                  pl.BlockSpec((1, tn, k), lambda j, i: (layer, rb0 + j, 0))],
        out_specs=pl.BlockSpec((tm, tn), lambda j, i: (i, j)),
        out_shape=jax.ShapeDtypeStruct((m, n), out_dtype),
        scratch_shapes=[pltpu.VMEM((tn, k), BF16)],
        compiler_params=_cparams(("parallel", "arbitrary")),
        name=name,
    )(x, wt)


def _qk_prep_kernel(q_ref, k_ref, cos_ref, s1_ref, s2_ref, qn_ref, kn_ref, g_ref, gt_ref, qo_ref, ko_ref):
    cos = cos_ref[...]
    s1 = s1_ref[...]
    s2 = s2_ref[...]

    def norm_rope(x, nw, width):
        g = g_ref[0:width, :]
        gt = gt_ref[:, 0:width]
        sq_hi, sq_lo = _split2(x * x)
        ssum = _dot(sq_hi, g) + _dot(sq_lo, g)
        r = lax.rsqrt(ssum * (1.0 / HEAD_DIM) + EPS)
        r_hi, r_lo = _split2(r)
        y = x * (_dot(r_hi, gt) + _dot(r_lo, gt)) * nw
        outs = []
        for s in range(width // LANES):
            blk = y[:, s * LANES:(s + 1) * LANES]
            outs.append(blk * cos + pltpu.roll(blk, ROT_DIM // 2, 1) * s1
                        + pltpu.roll(blk, LANES - ROT_DIM // 2, 1) * s2)
        return jnp.concatenate(outs, axis=1)

    q = norm_rope(q_ref[...], qn_ref[...], ATTN_WIDTH)
    qo_ref[...] = (q * (HEAD_DIM ** -0.5)).astype(qo_ref.dtype)
    ko_ref[...] = norm_rope(k_ref[...], kn_ref[...], KV_WIDTH)


def _rope_tables(seq, dec_seq, tm):
    half = ROT_DIM // 2
    inv_freq = np.float32(ROPE_THETA) ** (-np.arange(half, dtype=np.float32) * np.float32(2.0) / np.float32(ROT_DIM))
    pos = np.concatenate([np.arange(seq), np.tile(PAST_LEN + np.arange(dec_seq), tm // dec_seq)]).astype(np.float32)
    ang = pos[:, None] * inv_freq.astype(np.float32)[None, :]
    cos, sin = np.cos(ang).astype(np.float32), np.sin(ang).astype(np.float32)
    rows = pos.shape[0]
    ones = np.ones((rows, HEAD_DIM - ROT_DIM), np.float32)
    zeros = np.zeros((rows, HEAD_DIM - ROT_DIM), np.float32)
    zh = np.zeros((rows, half), np.float32)
    reps = (1, LANES // HEAD_DIM)
    return (np.tile(np.concatenate([cos, cos, ones], axis=1), reps),
            np.tile(np.concatenate([zh, sin, zeros], axis=1), reps),
            np.tile(np.concatenate([-sin, zh, zeros], axis=1), reps))


def _qk_prep(q, k, q_norm, k_norm, n_prompt, seq, dec_seq):
    m = q.shape[0]
    tm = _pick(math.gcd(math.gcd(seq, m - n_prompt), 256), (256, 128, 64, 32))
    cos_t, s1_t, s2_t = (jnp.asarray(t) for t in _rope_tables(seq, dec_seq, tm))
    prompt_tiles, seq_tiles = n_prompt // tm, seq // tm
    pos_rows = pl.BlockSpec((tm, LANES), lambda i: (jnp.where(i < prompt_tiles, i % seq_tiles, seq_tiles), 0))
    head_of_lane = jnp.arange(ATTN_WIDTH) // HEAD_DIM
    g = (head_of_lane[:, None] == jnp.arange(LANES)[None, :]).astype(BF16)
    gt = g.T
    qn = jnp.tile(q_norm.astype(F32), N_HEADS).reshape(1, ATTN_WIDTH)
    kn = jnp.tile(k_norm.astype(F32), N_KV_HEADS).reshape(1, KV_WIDTH)
    row = lambda w: pl.BlockSpec((tm, w), lambda i: (i, 0))
    full = lambda a: pl.BlockSpec(a.shape, lambda i: (0, 0))
    return pl.pallas_call(
        _qk_prep_kernel,
        grid=(m // tm,),
        in_specs=[row(ATTN_WIDTH), row(KV_WIDTH), pos_rows, pos_rows, pos_rows,
                  full(qn), full(kn), full(g), full(gt)],
        out_specs=[row(ATTN_WIDTH), row(KV_WIDTH)],
        out_shape=[jax.ShapeDtypeStruct((m, ATTN_WIDTH), BF16), jax.ShapeDtypeStruct((m, KV_WIDTH), F32)],
        compiler_params=_cparams(("parallel",)),
        name="qk_prep",
    )(q, k, cos_t, s1_t, s2_t, qn, kn, g, gt)


def _attend(q, kk, vv, sinks_ref, s_ref, p_ref, l_ref):
    tq = q.shape[0]
    for j in range(N_KV_HEADS):
        heads = [Q_PER_KV * j + g for g in range(Q_PER_KV)]
        q4 = jnp.concatenate([q[:, h * HEAD_DIM:(h + 1) * HEAD_DIM] for h in heads], axis=0)
        s_ref[j] = _dot_nt(q4, kk[:, j * HEAD_DIM:(j + 1) * HEAD_DIM])
    for j in range(N_KV_HEADS):
        heads = [Q_PER_KV * j + g for g in range(Q_PER_KV)]
        s = s_ref[j]
        sink = jnp.concatenate([jnp.full((tq, 1), sinks_ref[h], F32) for h in heads], axis=0)
        m = jnp.maximum(jnp.max(s, axis=-1, keepdims=True), sink)
        p = jnp.exp(s - m)
        l_ref[j] = jnp.broadcast_to(jnp.sum(p, axis=-1, keepdims=True) + jnp.exp(sink - m), l_ref.shape[1:])
        p_ref[j] = p.astype(BF16)
    outs = []
    for j in range(N_KV_HEADS):
        o = _dot(p_ref[j], vv[:, j * HEAD_DIM:(j + 1) * HEAD_DIM]) / l_ref[j][:, 0:HEAD_DIM]
        for g in range(Q_PER_KV):
            outs.append(o[g * tq:(g + 1) * tq, :])
    return jnp.concatenate(outs, axis=1)


def _attn_prompt_kernel(sinks_ref, q_ref, ka_ref, kb_ref, va_ref, vb_ref, o_ref, st_ref, pt_ref, rl_ref):
    i = pl.program_id(1)
    tq = q_ref.shape[0]
    tk = 2 * tq
    q = q_ref[...]
    k32 = jnp.concatenate([ka_ref[...], kb_ref[...]], axis=0)
    vt = jnp.concatenate([va_ref[...], vb_ref[...]], axis=0).T.astype(BF16)

    lane_k = lax.broadcasted_iota(I32, (tk, LANES), 1)
    kc = lax.broadcasted_iota(I32, (tk, LANES), 0) // CHUNK
    k_ind = jnp.where(lane_k == 0, jnp.where(kc == 0, 1.0, 0.0),
                      jnp.where(lane_k == 1, jnp.where(kc == N_PREV_CHUNKS + 1, 1.0, 0.0),
                                jnp.where(lane_k == 2, jnp.where(kc < N_PREV_CHUNKS, 1.0, 0.0), 0.0)))
    first_step = jnp.where(i == 0, 1.0, 0.0)
    k_ind = jnp.where(lane_k == 2, k_ind * first_step, k_ind).astype(BF16)
    lane_q = lax.broadcasted_iota(I32, (tq, LANES), 1)
    qc = lax.broadcasted_iota(I32, (tq, LANES), 0) // CHUNK
    q_msk = jnp.where(lane_q == 0, jnp.where(qc == 1, NEG_INF, 0.0),
                      jnp.where(lane_q == 1, jnp.where(qc == 0, NEG_INF, 0.0),
                                jnp.where(lane_q == 2, NEG_INF, 0.0))).astype(BF16)
    low = lane_k < HEAD_DIM

    for slab in range(KV_WIDTH // LANES):
        ks = k32[:, slab * LANES:(slab + 1) * LANES]
        kr = pltpu.roll(ks, HEAD_DIM, 1)
        for jj in range(2):
            j = 2 * slab + jj
            in_low = jnp.where(low, ks if jj == 0 else kr, 0.0).astype(BF16)
            in_high = jnp.where(low, 0.0, kr if jj == 0 else ks).astype(BF16)
            k_ext = (jnp.concatenate([in_low, k_ind], axis=1), jnp.concatenate([in_high, k_ind], axis=1))
            for g in range(Q_PER_KV):
                h = Q_PER_KV * j + g
                pair = h // 2
                q_ext = jnp.concatenate([q[:, pair * LANES:(pair + 1) * LANES], q_msk], axis=1)
                st_ref[h] = _dot_nt(k_ext[h % 2], q_ext)

    for h in range(N_HEADS):
        st = st_ref[h]
        sink = sinks_ref[h]
        m = jnp.maximum(jnp.max(st, axis=0, keepdims=True), sink)
        pt = jnp.exp(st - m)
        rl_ref[h:h + 1, :] = 1.0 / (jnp.sum(pt, axis=0, keepdims=True) + jnp.exp(sink - m))
        pt_ref[h] = pt.astype(BF16)

    for pair in range(N_HEADS // 2):
        pieces = []
        for h in (2 * pair, 2 * pair + 1):
            j = h // Q_PER_KV
            pieces.append(_dot(vt[j * HEAD_DIM:(j + 1) * HEAD_DIM, :], pt_ref[h]) * rl_ref[h:h + 1, :])
        o_ref[:, pair * LANES:(pair + 1) * LANES] = jnp.concatenate(pieces, axis=0).T.astype(o_ref.dtype)


def _attn_prompt(q, k, kv, sinks, bsz, seq):
    tq = 2 * CHUNK
    nb = seq // tq

    def kv_spec(back, col):
        return pl.BlockSpec((tq, KV_WIDTH), lambda b, i, s: (b * nb + jnp.maximum(i - back, 0), col))

    grid_spec = pltpu.PrefetchScalarGridSpec(
        num_scalar_prefetch=1,
        grid=(bsz, nb),
        in_specs=[pl.BlockSpec((tq, ATTN_WIDTH), lambda b, i, s: (b * nb + i, 0)),
                  kv_spec(1, 0), kv_spec(0, 0), kv_spec(1, 1), kv_spec(0, 1)],
        out_specs=pl.BlockSpec((tq, ATTN_WIDTH), lambda b, i, s: (b * nb + i, 0)),
        scratch_shapes=[pltpu.VMEM((N_HEADS, 2 * tq, tq), F32), pltpu.VMEM((N_HEADS, 2 * tq, tq), BF16),
                        pltpu.VMEM((N_HEADS, tq), F32)],
    )
    return pl.pallas_call(
        _attn_prompt_kernel,
        grid_spec=grid_spec,
        out_shape=jax.ShapeDtypeStruct((bsz * seq, ATTN_WIDTH), BF16),
        compiler_params=_cparams(("parallel", "parallel")),
        name="attn_prompt",
    )(sinks, q, k, k, kv, kv)


def _attn_sample_kernel(sinks_ref, q_ref, kn_ref, vn_ref, kc_ref, vc_ref, o_ref, s_ref, p_ref, l_ref):
    kk = jnp.concatenate([kc_ref[0], kn_ref[...]], axis=0).astype(BF16)
    vv = jnp.concatenate([vc_ref[0], vn_ref[...]], axis=0).astype(BF16)
    o_ref[...] = _attend(q_ref[...], kk, vv, sinks_ref, s_ref, p_ref, l_ref).astype(o_ref.dtype)


def _attn_sample(q, k, kv, cache_k, cache_v, sinks, row0, bsz, seq):
    blk0 = row0 // seq
    kv_len = cache_k.shape[1]
    new = lambda w, col: pl.BlockSpec((seq, w), lambda b, s: (blk0 + b, col))
    cache = pl.BlockSpec((1, kv_len, KV_WIDTH), lambda b, s: (b, 0, 0))
    grid_spec = pltpu.PrefetchScalarGridSpec(
        num_scalar_prefetch=1,
        grid=(bsz,),
        in_specs=[new(ATTN_WIDTH, 0), new(KV_WIDTH, 0), new(KV_WIDTH, 1), cache, cache],
        out_specs=pl.BlockSpec((seq, ATTN_WIDTH), lambda b, s: (b, 0)),
        scratch_shapes=[pltpu.VMEM((N_KV_HEADS, Q_PER_KV * seq, kv_len + seq), F32),
                        pltpu.VMEM((N_KV_HEADS, Q_PER_KV * seq, kv_len + seq), BF16),
                        pltpu.VMEM((N_KV_HEADS, Q_PER_KV * seq, LANES), F32)],
    )
    return pl.pallas_call(
        _attn_sample_kernel,
        grid_spec=grid_spec,
        out_shape=jax.ShapeDtypeStruct((bsz * seq, ATTN_WIDTH), BF16),
        compiler_params=_cparams(("parallel",)),
        name="attn_sample",
    )(sinks, q, k, kv, cache_k, cache_v)


def _pad_rows(x, rows):
    if x.shape[0] == rows:
        return x
    return jnp.concatenate([x, jnp.zeros((rows - x.shape[0], x.shape[1]), x.dtype)], axis=0)


def _ssd_kernel(*refs, clen, has_past):
    if has_past:
        (xbc_ref, dt_ref, z_ref, cw_ref, cb_ref, dtb_ref, alog_ref, dskip_ref, nw_ref, cpast_ref, hpast_ref,
         y_ref, hout_ref, xpad, u_ref, g_ref, ht_ref, ca_ref, xdt_ref, yst_ref, xw_ref, cb2_ref) = refs
    else:
        (xbc_ref, dt_ref, z_ref, cw_ref, cb_ref, dtb_ref, alog_ref, dskip_ref, nw_ref,
         y_ref, hout_ref, xpad, u_ref, g_ref, ht_ref, ca_ref, xdt_ref, yst_ref, xw_ref, cb2_ref) = refs
    L = clen
    c = pl.program_id(1)
    nc = pl.num_programs(1)
    n_pairs = SSM_HEADS // 2
    pairs_per_group = n_pairs // SSM_GROUPS
    gw = D_INNER // SSM_GROUPS

    @pl.when(c == 0)
    def _init():
        if has_past:
            for cidx in range(CONV_DIM // LANES):
                xpad[cidx, 5:8, :] = cpast_ref[0, :, cidx * LANES:(cidx + 1) * LANES]
            for p in range(n_pairs):
                both = jnp.concatenate([hpast_ref[0, 2 * p], hpast_ref[0, 2 * p + 1]], axis=0)
                ht_ref[:, p * LANES:(p + 1) * LANES] = both.T
        else:
            xpad[:, 0:8, :] = jnp.zeros((CONV_DIM // LANES, 8, LANES), F32)
            ht_ref[...] = jnp.zeros(ht_ref.shape, F32)

    n_col = CONV_DIM // LANES
    for cidx in range(n_col):
        xpad[cidx, 8:8 + L, :] = xbc_ref[:, cidx * LANES:(cidx + 1) * LANES]
    for cidx in range(n_col):
        sl = slice(cidx * LANES, (cidx + 1) * LANES)
        acc = cb_ref[:, sl] + xpad[cidx, 8:8 + L, :] * cw_ref[3:4, sl]
        acc = acc + xpad[cidx, 7:7 + L, :] * cw_ref[2:3, sl]
        acc = acc + xpad[cidx, 6:6 + L, :] * cw_ref[1:2, sl]
        acc = acc + xpad[cidx, 5:5 + L, :] * cw_ref[0:1, sl]
        u_ref[:, sl] = _silu(acc)
    for cidx in range(n_col):
        xpad[cidx, 5:8, :] = xpad[cidx, 5 + L:8 + L, :]

    dtx = dt_ref[...] + dtb_ref[...]
    dt = jnp.maximum(dtx, 0.0) + jnp.log(1.0 + jnp.exp(-jnp.abs(dtx)))
    loga = dt * (-jnp.exp(alog_ref[...]))
    t_idx = lax.broadcasted_iota(I32, (L, L), 0)
    s_idx = lax.broadcasted_iota(I32, (L, L), 1)
    incl = (s_idx <= t_idx).astype(BF16)
    acum = sum(_dot(incl, part) for part in _split3(loga))
    acum_t = _pad_rows(acum, LANES).T[:, 0:L]

    lane_m = lax.broadcasted_iota(I32, (L, 2 * L), 1)
    row_m = lax.broadcasted_iota(I32, (L, 2 * L), 0)
    left_m = lane_m < L
    causal_m = jnp.where(left_m, lane_m, lane_m - L) <= row_m
    left = lax.broadcasted_iota(I32, (L, LANES), 1) < SSM_HEAD_DIM
    b_of = lambda grp: u_ref[:, D_INNER + grp * D_STATE:D_INNER + (grp + 1) * D_STATE]
    c_of = lambda grp: u_ref[:, D_INNER + BC_WIDTH + grp * D_STATE:D_INNER + BC_WIDTH + (grp + 1) * D_STATE]
    pair_lanes = lambda p: slice(p * LANES, (p + 1) * LANES)
    group_lanes = lambda grp: slice(grp * gw, (grp + 1) * gw)

    for p in range(n_pairs):
        h0, h1, sl = 2 * p, 2 * p + 1, pair_lanes(p)
        ca_ref[:, sl] = jnp.where(left, acum[:, h0:h0 + 1], acum[:, h1:h1 + 1])
        xdt_ref[:, sl] = u_ref[:, sl] * jnp.where(left, dt[:, h0:h0 + 1], dt[:, h1:h1 + 1])

    for grp in range(SSM_GROUPS):
        c_bf = c_of(grp).astype(BF16)
        cb = _dot_nt(c_bf, b_of(grp).astype(BF16))
        cb2_ref[grp] = jnp.concatenate([cb, cb], axis=1)
        yst_ref[:, group_lanes(grp)] = _dot(c_bf, ht_ref[:, group_lanes(grp)].astype(BF16))

    for p in range(n_pairs):
        h0, h1, sl = 2 * p, 2 * p + 1, pair_lanes(p)
        ca = ca_ref[:, sl]
        ca_m = ca if L == SSM_HEAD_DIM else jnp.where(left_m, acum[:, h0:h0 + 1], acum[:, h1:h1 + 1])
        row_a = jnp.concatenate([acum_t[h0:h0 + 1, :], acum_t[h1:h1 + 1, :]], axis=1)
        dec = jnp.exp(jnp.where(causal_m, ca_m - row_a, NEG_INF))
        m_pair = (cb2_ref[p // pairs_per_group] * dec).astype(BF16)
        xdt = xdt_ref[:, sl]
        x_bd = jnp.concatenate([jnp.where(left, xdt, 0.0), jnp.where(left, 0.0, xdt)], axis=0).astype(BF16)
        y = _dot(m_pair, x_bd) + yst_ref[:, sl] * jnp.exp(ca) + u_ref[:, sl] * dskip_ref[:, sl]
        g_ref[:, sl] = y * _silu(z_ref[:, sl].astype(F32))
        xw_ref[:, sl] = (xdt * jnp.exp(ca[L - 1:L, :] - ca)).astype(BF16)

    for grp in range(SSM_GROUPS):
        gsl = group_lanes(grp)
        gg = g_ref[:, gsl]
        ms = jnp.mean(gg * gg, axis=-1, keepdims=True)
        y_ref[:, gsl] = (gg * lax.rsqrt(ms + EPS) * nw_ref[:, gsl]).astype(y_ref.dtype)
        bt_bf = _pad_rows(b_of(grp), LANES).T[:, 0:L].astype(BF16)
        ht_ref[:, gsl] = ht_ref[:, gsl] * jnp.exp(ca_ref[L - 1:L, gsl]) + _dot(bt_bf, xw_ref[:, gsl])

    @pl.when(c == nc - 1)
    def _fin():
        for p in range(n_pairs):
            both = ht_ref[:, p * LANES:(p + 1) * LANES].T
            hout_ref[0, 2 * p] = both[0:SSM_HEAD_DIM, :]
            hout_ref[0, 2 * p + 1] = both[SSM_HEAD_DIM:2 * SSM_HEAD_DIM, :]


def _ssd(xbc, dt_raw, z, conv_w, conv_b, dt_bias, a_log, d_skip, ssm_norm, row0, bsz, seq, clen,
         conv_past=None, ssm_past=None):
    nc = seq // clen
    blk0 = row0 // clen
    has_past = conv_past is not None
    pad = lambda a: jnp.pad(a.astype(F32), (0, LANES - SSM_HEADS)).reshape(1, LANES)
    params = [conv_w.astype(F32), conv_b.astype(F32).reshape(1, CONV_DIM), pad(dt_bias), pad(a_log),
              jnp.repeat(d_skip.astype(F32), SSM_HEAD_DIM).reshape(1, D_INNER), ssm_norm.astype(F32).reshape(1, D_INNER)]
    rows = lambda w: pl.BlockSpec((clen, w), lambda b, c: (blk0 + b * nc + c, 0))
    full = lambda a: pl.BlockSpec(a.shape, lambda b, c: (0,) * a.ndim)
    in_specs = [rows(CONV_DIM), rows(LANES), rows(D_INNER)] + [full(a) for a in params]
    args = [xbc, dt_raw, z] + params
    if has_past:
        in_specs += [pl.BlockSpec((1, CONV_W - 1, CONV_DIM), lambda b, c: (b, 0, 0)),
                     pl.BlockSpec((1, SSM_HEADS, SSM_HEAD_DIM, D_STATE), lambda b, c: (b, 0, 0, 0))]
        args += [conv_past.astype(F32), ssm_past.astype(F32)]
    return pl.pallas_call(
        functools.partial(_ssd_kernel, clen=clen, has_past=has_past),
        grid=(bsz, nc),
        in_specs=in_specs,
        out_specs=[pl.BlockSpec((clen, D_INNER), lambda b, c: (b * nc + c, 0)),
                   pl.BlockSpec((1, SSM_HEADS, SSM_HEAD_DIM, D_STATE), lambda b, c: (b, 0, 0, 0))],
        out_shape=[jax.ShapeDtypeStruct((bsz * seq, D_INNER), BF16),
                   jax.ShapeDtypeStruct((bsz, SSM_HEADS, SSM_HEAD_DIM, D_STATE), F32)],
        scratch_shapes=[pltpu.VMEM((CONV_DIM // LANES, 8 + clen, LANES), F32), pltpu.VMEM((clen, CONV_DIM), F32),
                        pltpu.VMEM((clen, D_INNER), F32), pltpu.VMEM((D_STATE, D_INNER), F32),
                        pltpu.VMEM((clen, D_INNER), F32), pltpu.VMEM((clen, D_INNER), F32),
                        pltpu.VMEM((clen, D_INNER), F32), pltpu.VMEM((clen, D_INNER), BF16),
                        pltpu.VMEM((SSM_GROUPS, clen, 2 * clen), F32)],
        compiler_params=_cparams(("parallel", "arbitrary")),
        name="ssd_past" if has_past else "ssd_prompt",
    )(*args)


def _merge_kernel(h_ref, a1_ref, a2_ref, s1_ref, s2_ref, wga_ref, wgs_ref, bga_ref, bgs_ref, wa_ref, ws_ref, o_ref,
                  *, n_first):
    def body(a_ref, s_ref):
        h = h_ref[...]
        g_a = jax.nn.sigmoid(_dot(h, wga_ref[...]) + bga_ref[...])
        g_s = jax.nn.sigmoid(_dot(h, wgs_ref[...]) + bgs_ref[...])
        o_ref[...] = (g_a * _dot(a_ref[...], wa_ref[...]) + g_s * _dot(s_ref[...], ws_ref[...])).astype(o_ref.dtype)

    _by_part(pl.program_id(0), n_first, body, (a1_ref, s1_ref), (a2_ref, s2_ref))


def _merge(h, attn_parts, ssm_parts, w_gate, b_gate, w_attn_up, w_ssm_up):
    m = h.shape[0]
    m1, m2 = attn_parts[0].shape[0], attn_parts[1].shape[0]
    tm = _pick(math.gcd(m1, m2), (512, 256, 128, 64, 32))
    nf = m1 // tm
    tn = 512
    nj = D_MODEL // tn
    bg = b_gate.astype(F32).reshape(1, 2 * D_MODEL)
    return pl.pallas_call(
        functools.partial(_merge_kernel, n_first=nf),
        grid=(m // tm, nj),
        in_specs=[pl.BlockSpec((tm, D_MODEL), lambda i, j: (i, 0)),
                  pl.BlockSpec((tm, ATTN_WIDTH), _first_rows(nf)), pl.BlockSpec((tm, ATTN_WIDTH), _later_rows(nf)),
                  pl.BlockSpec((tm, D_INNER), _first_rows(nf)), pl.BlockSpec((tm, D_INNER), _later_rows(nf)),
                  pl.BlockSpec((D_MODEL, tn), lambda i, j: (0, j)),
                  pl.BlockSpec((D_MODEL, tn), lambda i, j: (0, j + nj)),
                  pl.BlockSpec((1, tn), lambda i, j: (0, j)),
                  pl.BlockSpec((1, tn), lambda i, j: (0, j + nj)),
                  pl.BlockSpec((ATTN_WIDTH, tn), lambda i, j: (0, j)),
                  pl.BlockSpec((D_INNER, tn), lambda i, j: (0, j))],
        out_specs=pl.BlockSpec((tm, tn), lambda i, j: (i, j)),
        out_shape=jax.ShapeDtypeStruct((m, D_MODEL), BF16),
        compiler_params=_cparams(("parallel", "parallel")),
        name="merge",
    )(h, attn_parts[0], attn_parts[1], ssm_parts[0], ssm_parts[1], w_gate, w_gate, bg, bg, w_attn_up, w_ssm_up)


def _outproj_router_kernel(mg_ref, wo_ref, xa_ref, xb_ref, n2_ref, wr_ref, br_ref, x1_ref, h2_ref, idx_ref, gate_ref,
                           *, n_first):
    def residual(x_ref):
        x1_ref[...] = x_ref[...] + _dot(mg_ref[...], wo_ref[...])

    _by_part(pl.program_id(0), n_first, residual, (xa_ref,), (xb_ref,))
    x1 = x1_ref[...]
    ms = jnp.mean(x1 * x1, axis=-1, keepdims=True)
    h2 = x1 * lax.rsqrt(ms + EPS) * n2_ref[...]
    h2_ref[...] = _pack_bf16_pairs(h2)
    h_hi, h_lo = _split2(h2)
    w_hi, w_lo = _split2(wr_ref[...])
    logits = _dot(h_hi, w_hi) + (_dot(h_hi, w_lo) + _dot(h_lo, w_hi)) + br_ref[...]
    tm = logits.shape[0]
    lane = lax.broadcasted_iota(I32, (tm, LANES), 1)
    logits = jnp.where(lane < N_EXPERTS, logits, -jnp.inf)
    idx_out = jnp.zeros((tm, LANES), I32)
    val_out = jnp.zeros((tm, LANES), F32)
    top = None
    for k in range(TOP_K):
        v = jnp.max(logits, axis=-1, keepdims=True)
        i = jnp.min(jnp.where(logits == v, lane, LANES), axis=-1, keepdims=True)
        if k == 0:
            top = v
        idx_out = jnp.where(lane == k, i, idx_out)
        val_out = jnp.where(lane == k, jnp.exp(v - top), val_out)
        logits = jnp.where(lane == i, -jnp.inf, logits)
    idx_ref[...] = idx_out
    gate_ref[...] = val_out / jnp.sum(val_out, axis=-1, keepdims=True)


def _outproj_router(merged, w_out, xa, xb, norm2, w_router, b_router):
    m = merged.shape[0]
    tm = _pick(math.gcd(xa.shape[0], xb.shape[0]), (512, 256, 128, 64, 32))
    nf = xa.shape[0] // tm
    wr = jnp.pad(w_router.astype(F32), ((0, 0), (0, LANES - N_EXPERTS)))
    br = jnp.pad(b_router.astype(F32), (0, LANES - N_EXPERTS)).reshape(1, LANES)
    row = lambda w: pl.BlockSpec((tm, w), lambda i: (i, 0))
    full = lambda r, c: pl.BlockSpec((r, c), lambda i: (0, 0))
    return pl.pallas_call(
        functools.partial(_outproj_router_kernel, n_first=nf),
        grid=(m // tm,),
        in_specs=[row(D_MODEL), full(D_MODEL, D_MODEL),
                  pl.BlockSpec((tm, D_MODEL), _first_rows(nf)), pl.BlockSpec((tm, D_MODEL), _later_rows(nf)),
                  full(1, D_MODEL), full(D_MODEL, LANES), full(1, LANES)],
        out_specs=[row(D_MODEL), row(D_MODEL // 2), row(LANES), row(LANES)],
        out_shape=[jax.ShapeDtypeStruct((m, D_MODEL), F32), jax.ShapeDtypeStruct((m, D_MODEL // 2), I32),
                   jax.ShapeDtypeStruct((m, LANES), I32), jax.ShapeDtypeStruct((m, LANES), F32)],
        compiler_params=_cparams(("parallel",)),
        name="outproj_router",
    )(merged, w_out, xa, xb, norm2.astype(F32).reshape(1, D_MODEL), wr, br)


def _rank_kernel(idx_ref, rank_ref, cnt_ref, base_ref):
    i = pl.program_id(0)
    tt = idx_ref.shape[0]

    @pl.when(i == 0)
    def _():
        base_ref[...] = jnp.zeros(base_ref.shape, F32)

    idx = idx_ref[...]
    lane = lax.broadcasted_iota(I32, (tt, LANES), 1)
    sel = [lane == idx[:, k:k + 1] for k in range(TOP_K)]
    onehot = jnp.zeros((tt, LANES), F32)
    for k in range(TOP_K):
        onehot = jnp.where(sel[k], 1.0, onehot)
    r_idx = lax.broadcasted_iota(I32, (tt, tt), 0)
    c_idx = lax.broadcasted_iota(I32, (tt, tt), 1)
    before = (c_idx < r_idx).astype(BF16)
    rank_all = _dot(before, onehot.astype(BF16)) + base_ref[0:1, :]
    out = jnp.zeros((tt, LANES), F32)
    for k in range(TOP_K):
        out = jnp.where(lane == k, jnp.sum(jnp.where(sel[k], rank_all, 0.0), axis=-1, keepdims=True), out)
    rank_ref[...] = out.astype(I32)
    base_ref[0:1, :] = base_ref[0:1, :] + jnp.sum(onehot, axis=0, keepdims=True)
    cnt_ref[...] = base_ref[...].astype(I32)


def _ranks(idx):
    m = idx.shape[0]
    tt = _pick(m, (512, 256, 128, 64, 32))
    return pl.pallas_call(
        _rank_kernel,
        grid=(m // tt,),
        in_specs=[pl.BlockSpec((tt, LANES), lambda i: (i, 0))],
        out_specs=[pl.BlockSpec((tt, LANES), lambda i: (i, 0)), pl.BlockSpec((8, LANES), lambda i: (0, 0))],
        out_shape=[jax.ShapeDtypeStruct((m, LANES), I32), jax.ShapeDtypeStruct((8, LANES), I32)],
        scratch_shapes=[pltpu.VMEM((8, LANES), F32)],
        compiler_params=_cparams(("arbitrary",)),
        name="expert_ranks",
    )(idx)


def _dispatch_kernel(dest_ref, fill_ref, h_ref, xs_hbm, zero_ref, sem, zsem, *, tt, tm, n_blocks):
    @pl.when(pl.program_id(0) == 0)
    def _():
        zero_ref[...] = jnp.zeros(zero_ref.shape, zero_ref.dtype)

        def block_copy(b):
            return pltpu.make_async_copy(zero_ref, xs_hbm.at[pl.ds(pl.multiple_of(b * tm, tm), tm), :], zsem)

        def start(b, carry):
            @pl.when(fill_ref[b] != 0)
            def _():
                block_copy(b).start()

            return carry

        def done(b, carry):
            @pl.when(fill_ref[b] != 0)
            def _():
                block_copy(b).wait()

            return carry

        lax.fori_loop(0, n_blocks, start, 0)
        lax.fori_loop(0, n_blocks, done, 0)

    base = pl.program_id(0) * (tt * TOP_K)

    for r in range(tt):
        for k in range(TOP_K):
            dst = dest_ref[base + r * TOP_K + k]
            pltpu.make_async_copy(h_ref.at[pl.ds(r, 1), :], xs_hbm.at[pl.ds(dst, 1), :], sem).start(priority=k % 2)
    for _ in range(TOP_K):
        pltpu.make_async_copy(h_ref, xs_hbm.at[pl.ds(0, tt), :], sem).wait()


def _dispatch(dest_flat, fill_block, h2, tm):
    m = h2.shape[0]
    n_blocks = fill_block.shape[0]
    tt = _pick(m, (256, 128, 64, 32))
    width = h2.shape[1]
    grid_spec = pltpu.PrefetchScalarGridSpec(
        num_scalar_prefetch=2,
        grid=(m // tt,),
        in_specs=[pl.BlockSpec((tt, width), lambda i, d, f: (i, 0))],
        out_specs=pl.BlockSpec(memory_space=pl.ANY),
        scratch_shapes=[pltpu.VMEM((tm, width), h2.dtype), pltpu.SemaphoreType.DMA, pltpu.SemaphoreType.DMA],
    )
    return pl.pallas_call(
        functools.partial(_dispatch_kernel, tt=tt, tm=tm, n_blocks=n_blocks),
        grid_spec=grid_spec,
        out_shape=jax.ShapeDtypeStruct((n_blocks * tm, width), h2.dtype),
        compiler_params=_cparams(("arbitrary",)),
        name="dispatch",
    )(dest_flat, fill_block, h2)


def _pack_bf16_pairs(x):
    c = x.shape[1] // 2
    bits = lax.bitcast_convert_type(x.astype(BF16).astype(F32), I32)
    return lax.shift_right_logical(bits[:, :c], 16) | bits[:, c:]


def _unpack_bf16_pairs(p):
    lo = lax.bitcast_convert_type(lax.shift_left(p, 16), F32).astype(BF16)
    hi = lax.bitcast_convert_type(p & jnp.int32(-65536), F32).astype(BF16)
    return lo, hi


def _expert_kernel(be_ref, nu_ref, rows_ref, xs_ref, bg_ref, bu_ref, bd_ref, wg_hbm, wu_hbm, wd_hbm, ys_ref,
                   cg, cu, cd, stg_g, stg_u, stg_d, xb_ref, sems, *, tf):
    i = pl.program_id(0)
    nj = D_FF // tf
    half = D_MODEL // 2
    e = be_ref[i]
    first = jnp.logical_or(i == 0, e != be_ref[jnp.maximum(i - 1, 0)])
    e_next = be_ref[jnp.minimum(i + 1, pl.num_programs(0) - 1)]
    next_is_new = jnp.logical_and(i + 1 < nu_ref[0], e_next != e)

    def tile_copies(t, slot, expert=e):
        cols = pl.ds(pl.multiple_of(t * tf, tf), tf)
        return (pltpu.make_async_copy(wg_hbm.at[expert, :, cols], stg_g.at[slot], sems.at[slot, 0]),
                pltpu.make_async_copy(wu_hbm.at[expert, :, cols], stg_u.at[slot], sems.at[slot, 1]),
                pltpu.make_async_copy(wd_hbm.at[expert, cols, :], stg_d.at[slot], sems.at[slot, 2]))

    @pl.when(i < nu_ref[0])
    def _():
        @pl.when(i == 0)
        def _():
            for cp in tile_copies(0, 0):
                cp.start()

        lo, hi = _unpack_bf16_pairs(xs_ref[...])
        xb_ref[:, :half] = lo
        xb_ref[:, half:] = hi

        def refill(t):
            slot = t % 2

            @pl.when(t + 1 < nj)
            def _():
                for cp in tile_copies(t + 1, 1 - slot):
                    cp.start()

            for cp in tile_copies(t, slot):
                cp.wait()
            cg[t] = stg_g[slot].astype(BF16)
            cu[t] = stg_u[slot].astype(BF16)
            cd[t] = stg_d[slot].astype(BF16)

        def compute(t, opening):
            xb = xb_ref[...]
            g = jnp.minimum(_dot(xb, cg[t]) + bg_ref[0, t], SWIGLU_LIMIT)
            u = jnp.clip(_dot(xb, cu[t]) + bu_ref[0, t], -SWIGLU_LIMIT, SWIGLU_LIMIT)
            act = (u + 1.0) * (g * jax.nn.sigmoid(SWIGLU_ALPHA * g))
            part = _dot(act.astype(BF16), cd[t])
            if opening:
                ys_ref[...] = part + bd_ref[0]
            else:
                ys_ref[...] += part

        @pl.when(first)
        def _():
            refill(jnp.int32(0))
            compute(0, True)

            def step(t, carry):
                refill(t)
                compute(t, False)
                return carry

            lax.fori_loop(1, nj, step, 0)

        def compute_wide(t, opening, rows):
            xb = xb_ref[0:rows, :]
            span = range(t, t + CACHED_TILES)
            wide = lambda ref: jnp.concatenate([ref[k] for k in span], axis=1)
            bias = lambda ref: jnp.concatenate([ref[0, k] for k in span], axis=1)
            g = jnp.minimum(_dot(xb, wide(cg)) + bias(bg_ref), SWIGLU_LIMIT)
            u = jnp.clip(_dot(xb, wide(cu)) + bias(bu_ref), -SWIGLU_LIMIT, SWIGLU_LIMIT)
            act = (u + 1.0) * (g * jax.nn.sigmoid(SWIGLU_ALPHA * g))
            part = _dot(act.astype(BF16), jnp.concatenate([cd[k] for k in span], axis=0))
            if opening:
                ys_ref[0:rows, :] = part + bd_ref[0]
            else:
                ys_ref[0:rows, :] += part

        tm = ys_ref.shape[0]
        short = rows_ref[i] <= tm // 2

        @pl.when(jnp.logical_and(jnp.logical_not(first), jnp.logical_not(short)))
        def _():
            for t in range(0, nj, CACHED_TILES):
                compute_wide(t, t == 0, tm)

        @pl.when(jnp.logical_and(jnp.logical_not(first), short))
        def _():
            ys_ref[tm // 2:, :] = jnp.zeros((tm - tm // 2, D_MODEL), F32)
            for t in range(0, nj, CACHED_TILES):
                compute_wide(t, t == 0, tm // 2)

        @pl.when(next_is_new)
        def _():
            for cp in tile_copies(0, 0, e_next):
                cp.start()

    @pl.when(i >= nu_ref[0])
    def _():
        ys_ref[...] = jnp.zeros(ys_ref.shape, F32)


def _experts(block_e, n_used, block_rows, xs, w_e_gate, b_e_gate, w_e_up, b_e_up, w_e_down, b_e_down):
    n_rows = xs.shape[0]
    tm, tf = MOE_TM, MOE_TF
    n_blocks = n_rows // tm
    nj = D_FF // tf
    half = D_MODEL // 2

    def blk(i, nu):
        return jnp.minimum(i, nu[0] - 1)

    hbm = pl.BlockSpec(memory_space=pl.ANY)
    grid_spec = pltpu.PrefetchScalarGridSpec(
        num_scalar_prefetch=3,
        grid=(n_blocks,),
        in_specs=[pl.BlockSpec((tm, half), lambda i, be, nu, br: (blk(i, nu), 0)),
                  pl.BlockSpec((1, nj, 1, tf), lambda i, be, nu, br: (be[blk(i, nu)], 0, 0, 0)),
                  pl.BlockSpec((1, nj, 1, tf), lambda i, be, nu, br: (be[blk(i, nu)], 0, 0, 0)),
                  pl.BlockSpec((1, 1, D_MODEL), lambda i, be, nu, br: (be[blk(i, nu)], 0, 0)),
                  hbm, hbm, hbm],
        out_specs=pl.BlockSpec((tm, D_MODEL), lambda i, be, nu, br: (i, 0)),
        scratch_shapes=[pltpu.VMEM((nj, D_MODEL, tf), BF16), pltpu.VMEM((nj, D_MODEL, tf), BF16),
                        pltpu.VMEM((nj, tf, D_MODEL), BF16),
                        pltpu.VMEM((2, D_MODEL, tf), F32), pltpu.VMEM((2, D_MODEL, tf), F32),
                        pltpu.VMEM((2, tf, D_MODEL), F32),
                        pltpu.VMEM((tm, D_MODEL), BF16), pltpu.SemaphoreType.DMA((2, 3))],
    )
    return pl.pallas_call(
        functools.partial(_expert_kernel, tf=tf),
        grid_spec=grid_spec,
        out_shape=jax.ShapeDtypeStruct((n_rows, D_MODEL), F32),
        compiler_params=pltpu.CompilerParams(dimension_semantics=("arbitrary",), vmem_limit_bytes=EXPERT_VMEM_LIMIT),
        name="experts",
    )(block_e, n_used, block_rows, xs, b_e_gate.reshape(N_EXPERTS, nj, 1, tf), b_e_up.reshape(N_EXPERTS, nj, 1, tf),
      b_e_down.reshape(N_EXPERTS, 1, D_MODEL), w_e_gate, w_e_up, w_e_down)


def _combine_kernel(dest_ref, x1_ref, gate_ref, ys_hbm, o_ref, buf, sems, *, tt, row0):
    i = pl.program_id(0)
    n_steps = pl.num_programs(0)
    slot = i % 2

    def gather(step, to_slot):
        base = (row0 + step * tt) * TOP_K

        for r in range(tt):
            for k in range(TOP_K):
                src = dest_ref[base + r * TOP_K + k]
                pltpu.make_async_copy(ys_hbm.at[pl.ds(src, 1), :], buf.at[to_slot, k, pl.ds(r, 1), :],
                                      sems.at[to_slot]).start(priority=k % 2)

    @pl.when(i == 0)
    def _():
        gather(0, 0)

    @pl.when(i + 1 < n_steps)
    def _():
        gather(i + 1, 1 - slot)

    for k in range(TOP_K):
        pltpu.make_async_copy(ys_hbm.at[pl.ds(0, tt), :], buf.at[slot, k], sems.at[slot]).wait()
    gate = gate_ref[...]
    acc = x1_ref[...]
    for k in range(TOP_K):
        acc = acc + gate[:, k:k + 1] * buf[slot, k]
    o_ref[...] = acc


def _combine(dest_flat, x1, gate, ys, row0, rows):
    tt = _pick(math.gcd(row0, rows), (256, 128, 64, 32))
    blk0 = row0 // tt
    grid_spec = pltpu.PrefetchScalarGridSpec(
        num_scalar_prefetch=1,
        grid=(rows // tt,),
        in_specs=[pl.BlockSpec((tt, D_MODEL), lambda i, d: (blk0 + i, 0)),
                  pl.BlockSpec((tt, LANES), lambda i, d: (blk0 + i, 0)),
                  pl.BlockSpec(memory_space=pl.ANY)],
        out_specs=pl.BlockSpec((tt, D_MODEL), lambda i, d: (i, 0)),
        scratch_shapes=[pltpu.VMEM((2, TOP_K, tt, D_MODEL), F32), pltpu.SemaphoreType.DMA((2,))],
    )
    return pl.pallas_call(
        functools.partial(_combine_kernel, tt=tt, row0=row0),
        grid_spec=grid_spec,
        out_shape=jax.ShapeDtypeStruct((rows, D_MODEL), F32),
        compiler_params=_cparams(("arbitrary",)),
        name="combine",
    )(dest_flat, x1, gate, ys)


def _moe(x1, h2, idx, gate, w_e_gate, b_e_gate, w_e_up, b_e_up, w_e_down, b_e_down, n_first):
    m = x1.shape[0]
    tm = MOE_TM
    ranks, cnt = _ranks(idx)
    counts = cnt[0, :N_EXPERTS]
    padded = (counts + tm - 1) // tm * tm
    pad_end = jnp.cumsum(padded)
    pad_start = pad_end - padded
    e_sel = idx[:, :TOP_K]
    dest = (pad_start[e_sel] + ranks[:, :TOP_K]).astype(I32).reshape(-1)
    n_blocks = -(-(m * TOP_K + N_EXPERTS * (tm - 1)) // tm)
    block_row = jnp.arange(n_blocks, dtype=I32) * tm
    block_e = jnp.minimum(jnp.sum((pad_end[None, :] <= block_row[:, None]).astype(I32), axis=1), N_EXPERTS - 1)
    n_used = (pad_end[-1] // tm).astype(I32).reshape(1)
    is_last = jnp.sum((pad_end[None, :] == block_row[:, None] + tm).astype(I32), axis=1) > 0
    fill_block = jnp.logical_or(is_last, block_row >= pad_end[-1]).astype(I32)
    xs = _dispatch(dest, fill_block, h2, tm)
    block_rows = jnp.clip((pad_start + counts)[block_e] - block_row, 0, tm).astype(I32)
    ys = _experts(block_e, n_used, block_rows, xs, w_e_gate, b_e_gate, w_e_up, b_e_up, w_e_down, b_e_down)
    return _combine(dest, x1, gate, ys, 0, n_first), _combine(dest, x1, gate, ys, n_first, m - n_first)


def kernel(x_prompt, x_sample, cache_k, cache_v, state_conv, state_ssm, norm1, w_in, q_norm, k_norm, sinks, conv_w,
           conv_b, dt_bias, a_log, d_skip, ssm_norm, w_gate, b_gate, w_attn_up, w_ssm_up, w_out, norm2, w_router,
           b_router, w_e_gate, b_e_gate, w_e_up, b_e_up, w_e_down, b_e_down):
    bp, sp, _ = x_prompt.shape
    bs, ss, _ = x_sample.shape
    tp, ts = bp * sp, bs * ss
    kv_len = cache_k.shape[2]
    l = 0

    xa, xb = x_prompt.reshape(tp, D_MODEL), x_sample.reshape(ts, D_MODEL)

    o1 = ATTN_WIDTH
    o2 = o1 + 2 * KV_WIDTH
    o3 = o2 + D_INNER
    o4 = o3 + CONV_DIM
    h = _rmsnorm(xa, xb, norm1[l], BF16)
    w_in_t = jnp.swapaxes(w_in, 1, 2)
    q_p = _matmul(h, w_in_t, l, 0, o1, F32, "proj_q")
    kv_p = _matmul(h, w_in_t, l, o1, o2 - o1, F32, "proj_kv")
    z = _matmul(h, w_in_t, l, o2, o3 - o2, BF16, "proj_z")
    xbc = _matmul(h, w_in_t, l, o3, o4 - o3, F32, "proj_xbc")
    w_dt = jnp.pad(w_in_t[l:l + 1, o4:, :], ((0, 0), (0, LANES - SSM_HEADS), (0, 0)))
    dt_raw = _matmul(h, w_dt, 0, 0, LANES, F32, "proj_dt")

    q_rot, k_rot = _qk_prep(q_p, kv_p, q_norm[l], k_norm[l], tp, sp, ss)
    sk = sinks[l].astype(F32)
    attn_p = _attn_prompt(q_rot, k_rot, kv_p, sk, bp, sp)
    ck = cache_k[l].reshape(bs, kv_len, KV_WIDTH)
    cv = cache_v[l].reshape(bs, kv_len, KV_WIDTH)
    attn_s = _attn_sample(q_rot, k_rot, kv_p, ck, cv, sk, tp, bs, ss)

    ssd_w = (conv_w[l], conv_b[l], dt_bias[l], a_log[l], d_skip[l], ssm_norm[l])
    ssm_p, hfin_p = _ssd(xbc, dt_raw, z, *ssd_w, 0, bp, sp, CHUNK)
    ssm_s, hfin_s = _ssd(xbc, dt_raw, z, *ssd_w, tp, bs, ss, ss, state_conv[l], state_ssm[l])

    merged = _merge(h, (attn_p, attn_s), (ssm_p, ssm_s), w_gate[l].astype(BF16), b_gate[l], w_attn_up[l].astype(BF16), w_ssm_up[l].astype(BF16))
    x1, h2, idx, gate = _outproj_router(merged, w_out[l].astype(BF16), xa, xb, norm2[l], w_router[l], b_router[l])
    y_p, y_s = _moe(x1, h2, idx, gate, w_e_gate[l], b_e_gate[l], w_e_up[l], b_e_up[l], w_e_down[l], b_e_down[l], tp)

    def prompt_tail(a, n):
        return jnp.stack([a[(b + 1) * sp - n:(b + 1) * sp] for b in range(bp)])

    heads = lambda a: a.reshape(a.shape[0], a.shape[1], N_KV_HEADS, HEAD_DIM)
    new_k_p = heads(prompt_tail(k_rot, WINDOW))
    new_v_p = heads(prompt_tail(kv_p, WINDOW)[:, :, KV_WIDTH:])
    ks4 = heads(k_rot[tp:].reshape(bs, ss, KV_WIDTH))
    vs4 = heads(kv_p[tp:, KV_WIDTH:].reshape(bs, ss, KV_WIDTH))
    new_k_s = jnp.concatenate([cache_k[l], ks4], axis=1)[:, -kv_len:]
    new_v_s = jnp.concatenate([cache_v[l], vs4], axis=1)[:, -kv_len:]
    xbc_s = jnp.concatenate([state_conv[l], xbc[tp:].reshape(bs, ss, CONV_DIM)], axis=1)
    return (y_p.reshape(bp, sp, D_MODEL), y_s.reshape(bs, ss, D_MODEL),
            new_k_p[None], new_v_p[None], prompt_tail(xbc, CONV_W - 1)[None], hfin_p[None],
            new_k_s[None], new_v_s[None], xbc_s[:, -(CONV_W - 1):][None], hfin_s[None])
```

```python
import functools
import math

import jax
import jax.numpy as jnp
import numpy as np
from jax import lax
from jax.experimental import pallas as pl
from jax.experimental.pallas import tpu as pltpu

F32 = jnp.float32
BF16 = jnp.bfloat16
I32 = jnp.int32

D_MODEL = 2048
CHUNK = 64
N_HEADS = 32
N_KV_HEADS = 8
HEAD_DIM = 64
Q_PER_KV = N_HEADS // N_KV_HEADS
ATTN_WIDTH = N_HEADS * HEAD_DIM
KV_WIDTH = N_KV_HEADS * HEAD_DIM
WINDOW = 128
N_PREV_CHUNKS = WINDOW // CHUNK
ROPE_THETA = 500000.0
ROT_DIM = HEAD_DIM // 4
D_INNER = 2 * D_MODEL
SSM_HEAD_DIM = 64
SSM_HEADS = D_INNER // SSM_HEAD_DIM
SSM_GROUPS = 8
D_STATE = 128
CONV_W = 4
BC_WIDTH = SSM_GROUPS * D_STATE
CONV_DIM = D_INNER + 2 * BC_WIDTH
N_EXPERTS = 32
TOP_K = 4
D_FF = D_MODEL
SWIGLU_LIMIT = 7.0
SWIGLU_ALPHA = 1.702
EPS = 1e-6
NEG_INF = -1e30
PAST_LEN = 2048

LANES = 128
VMEM_LIMIT = 56 * 1024 * 1024
EXPERT_VMEM_LIMIT = 60 * 1024 * 1024
MOE_TM = 512
MOE_TF = 256
CACHED_TILES = 2


def _pick(n, cands):
    for c in cands:
        if n % c == 0:
            return c
    return n


def _cparams(sem):
    return pltpu.CompilerParams(dimension_semantics=sem, vmem_limit_bytes=VMEM_LIMIT)


def _split2(x):
    hi = x.astype(BF16)
    lo = (x - hi.astype(F32)).astype(BF16)
    return hi, lo


def _split3(x):
    hi = x.astype(BF16)
    r = x - hi.astype(F32)
    mid = r.astype(BF16)
    lo = (r - mid.astype(F32)).astype(BF16)
    return hi, mid, lo


def _dot(a, b):
    return jnp.dot(a, b, preferred_element_type=F32)


def _dot_nt(a, b):
    return lax.dot_general(a, b, (((1,), (1,)), ((), ())), preferred_element_type=F32)


def _silu(x):
    half = 0.5 * x
    return half + half * jnp.tanh(half)


def _first_rows(n_first):
    return lambda i, *_: (jnp.minimum(i, n_first - 1), 0)


def _later_rows(n_first):
    return lambda i, *_: (jnp.maximum(i - n_first, 0), 0)


def _by_part(i, n_first, fn, first_refs, later_refs):
    @pl.when(i < n_first)
    def _():
        fn(*first_refs)

    @pl.when(i >= n_first)
    def _():
        fn(*later_refs)


def _rms_kernel(xa_ref, xb_ref, w_ref, o_ref, *, n_first):
    def body(x_ref):
        x = x_ref[...]
        ms = jnp.mean(x * x, axis=-1, keepdims=True)
        o_ref[...] = (x * lax.rsqrt(ms + EPS) * w_ref[...]).astype(o_ref.dtype)

    _by_part(pl.program_id(0), n_first, body, (xa_ref,), (xb_ref,))


def _rmsnorm(xa, xb, w, out_dtype):
    (ma, d), mb = xa.shape, xb.shape[0]
    tm = _pick(math.gcd(ma, mb), (512, 256, 128, 64, 32))
    nf = ma // tm
    return pl.pallas_call(
        functools.partial(_rms_kernel, n_first=nf),
        grid=((ma + mb) // tm,),
        in_specs=[pl.BlockSpec((tm, d), _first_rows(nf)), pl.BlockSpec((tm, d), _later_rows(nf)),
                  pl.BlockSpec((1, d), lambda i: (0, 0))],
        out_specs=pl.BlockSpec((tm, d), lambda i: (i, 0)),
        out_shape=jax.ShapeDtypeStruct((ma + mb, d), out_dtype),
        compiler_params=_cparams(("parallel",)),
        name="rmsnorm",
    )(xa, xb, w.reshape(1, d))


def _mm_kernel(x_ref, w_ref, o_ref, wb_ref):
    @pl.when(pl.program_id(1) == 0)
    def _():
        wb_ref[...] = w_ref[0].astype(BF16)

    o_ref[...] = _dot_nt(x_ref[...], wb_ref[...]).astype(o_ref.dtype)


def _matmul(x, wt, layer, row0, n, out_dtype, name):
    m, k = x.shape
    tm = _pick(m, (1024, 512, 256, 128, 64, 32))
    tn = _pick(math.gcd(n, row0), (1024, 512, 256, 128))
    rb0 = row0 // tn
    return pl.pallas_call(
        _mm_kernel,
        grid=(n // tn, m // tm),
        in_specs=[pl.BlockSpec((tm, k), lambda j, i: (i, 0)),
                  pl.BlockSpec((1, tn, k), lambda j, i: (layer, rb0 + j, 0))],
        out_specs=pl.BlockSpec((tm, tn), lambda j, i: (i, j)),
        out_shape=jax.ShapeDtypeStruct((m, n), out_dtype),
        scratch_shapes=[pltpu.VMEM((tn, k), BF16)],
        compiler_params=_cparams(("parallel", "arbitrary")),
        name=name,
    )(x, wt)


def _qk_prep_kernel(q_ref, k_ref, cos_ref, s1_ref, s2_ref, qn_ref, kn_ref, g_ref, gt_ref, qo_ref, ko_ref):
    cos = cos_ref[...]
    s1 = s1_ref[...]
    s2 = s2_ref[...]

    def norm_rope(x, nw, width):
        g = g_ref[0:width, :]
        gt = gt_ref[:, 0:width]
        sq_hi, sq_lo = _split2(x * x)
        ssum = _dot(sq_hi, g) + _dot(sq_lo, g)
        r = lax.rsqrt(ssum * (1.0 / HEAD_DIM) + EPS)
        r_hi, r_lo = _split2(r)
        y = x * (_dot(r_hi, gt) + _dot(r_lo, gt)) * nw
        outs = []
        for s in range(width // LANES):
            blk = y[:, s * LANES:(s + 1) * LANES]
            outs.append(blk * cos + pltpu.roll(blk, ROT_DIM // 2, 1) * s1
                        + pltpu.roll(blk, LANES - ROT_DIM // 2, 1) * s2)
        return jnp.concatenate(outs, axis=1)

    q = norm_rope(q_ref[...], qn_ref[...], ATTN_WIDTH)
    qo_ref[...] = (q * (HEAD_DIM ** -0.5)).astype(qo_ref.dtype)
    ko_ref[...] = norm_rope(k_ref[...], kn_ref[...], KV_WIDTH)


def _rope_tables(seq, dec_seq, tm):
    half = ROT_DIM // 2
    inv_freq = np.float32(ROPE_THETA) ** (-np.arange(half, dtype=np.float32) * np.float32(2.0) / np.float32(ROT_DIM))
    pos = np.concatenate([np.arange(seq), np.tile(PAST_LEN + np.arange(dec_seq), tm // dec_seq)]).astype(np.float32)
    ang = pos[:, None] * inv_freq.astype(np.float32)[None, :]
    cos, sin = np.cos(ang).astype(np.float32), np.sin(ang).astype(np.float32)
    rows = pos.shape[0]
    ones = np.ones((rows, HEAD_DIM - ROT_DIM), np.float32)
    zeros = np.zeros((rows, HEAD_DIM - ROT_DIM), np.float32)
    zh = np.zeros((rows, half), np.float32)
    reps = (1, LANES // HEAD_DIM)
    return (np.tile(np.concatenate([cos, cos, ones], axis=1), reps),
            np.tile(np.concatenate([zh, sin, zeros], axis=1), reps),
            np.tile(np.concatenate([-sin, zh, zeros], axis=1), reps))


def _qk_prep(q, k, q_norm, k_norm, n_prompt, seq, dec_seq):
    m = q.shape[0]
    tm = _pick(math.gcd(math.gcd(seq, m - n_prompt), 256), (256, 128, 64, 32))
    cos_t, s1_t, s2_t = (jnp.asarray(t) for t in _rope_tables(seq, dec_seq, tm))
    prompt_tiles, seq_tiles = n_prompt // tm, seq // tm
    pos_rows = pl.BlockSpec((tm, LANES), lambda i: (jnp.where(i < prompt_tiles, i % seq_tiles, seq_tiles), 0))
    head_of_lane = jnp.arange(ATTN_WIDTH) // HEAD_DIM
    g = (head_of_lane[:, None] == jnp.arange(LANES)[None, :]).astype(BF16)
    gt = g.T
    qn = jnp.tile(q_norm.astype(F32), N_HEADS).reshape(1, ATTN_WIDTH)
    kn = jnp.tile(k_norm.astype(F32), N_KV_HEADS).reshape(1, KV_WIDTH)
    row = lambda w: pl.BlockSpec((tm, w), lambda i: (i, 0))
    full = lambda a: pl.BlockSpec(a.shape, lambda i: (0, 0))
    return pl.pallas_call(
        _qk_prep_kernel,
        grid=(m // tm,),
        in_specs=[row(ATTN_WIDTH), row(KV_WIDTH), pos_rows, pos_rows, pos_rows,
                  full(qn), full(kn), full(g), full(gt)],
        out_specs=[row(ATTN_WIDTH), row(KV_WIDTH)],
        out_shape=[jax.ShapeDtypeStruct((m, ATTN_WIDTH), BF16), jax.ShapeDtypeStruct((m, KV_WIDTH), F32)],
        compiler_params=_cparams(("parallel",)),
        name="qk_prep",
    )(q, k, cos_t, s1_t, s2_t, qn, kn, g, gt)


def _attend(q, kk, vv, sinks_ref, s_ref, p_ref, l_ref):
    tq = q.shape[0]
    for j in range(N_KV_HEADS):
        heads = [Q_PER_KV * j + g for g in range(Q_PER_KV)]
        q4 = jnp.concatenate([q[:, h * HEAD_DIM:(h + 1) * HEAD_DIM] for h in heads], axis=0)
        s_ref[j] = _dot_nt(q4, kk[:, j * HEAD_DIM:(j + 1) * HEAD_DIM])
    for j in range(N_KV_HEADS):
        heads = [Q_PER_KV * j + g for g in range(Q_PER_KV)]
        s = s_ref[j]
        sink = jnp.concatenate([jnp.full((tq, 1), sinks_ref[h], F32) for h in heads], axis=0)
        m = jnp.maximum(jnp.max(s, axis=-1, keepdims=True), sink)
        p = jnp.exp(s - m)
        l_ref[j] = jnp.broadcast_to(jnp.sum(p, axis=-1, keepdims=True) + jnp.exp(sink - m), l_ref.shape[1:])
        p_ref[j] = p.astype(BF16)
    outs = []
    for j in range(N_KV_HEADS):
        o = _dot(p_ref[j], vv[:, j * HEAD_DIM:(j + 1) * HEAD_DIM]) / l_ref[j][:, 0:HEAD_DIM]
        for g in range(Q_PER_KV):
            outs.append(o[g * tq:(g + 1) * tq, :])
    return jnp.concatenate(outs, axis=1)


def _attn_prompt_kernel(sinks_ref, q_ref, ka_ref, kb_ref, va_ref, vb_ref, o_ref, st_ref, pt_ref, rl_ref):
    i = pl.program_id(1)
    tq = q_ref.shape[0]
    tk = 2 * tq
    q = q_ref[...]
    k32 = jnp.concatenate([ka_ref[...], kb_ref[...]], axis=0)
    vt = jnp.concatenate([va_ref[...], vb_ref[...]], axis=0).T.astype(BF16)

    lane_k = lax.broadcasted_iota(I32, (tk, LANES), 1)
    kc = lax.broadcasted_iota(I32, (tk, LANES), 0) // CHUNK
    k_ind = jnp.where(lane_k == 0, jnp.where(kc == 0, 1.0, 0.0),
                      jnp.where(lane_k == 1, jnp.where(kc == N_PREV_CHUNKS + 1, 1.0, 0.0),
                                jnp.where(lane_k == 2, jnp.where(kc < N_PREV_CHUNKS, 1.0, 0.0), 0.0)))
    first_step = jnp.where(i == 0, 1.0, 0.0)
    k_ind = jnp.where(lane_k == 2, k_ind * first_step, k_ind).astype(BF16)
    lane_q = lax.broadcasted_iota(I32, (tq, LANES), 1)
    qc = lax.broadcasted_iota(I32, (tq, LANES), 0) // CHUNK
    q_msk = jnp.where(lane_q == 0, jnp.where(qc == 1, NEG_INF, 0.0),
                      jnp.where(lane_q == 1, jnp.where(qc == 0, NEG_INF, 0.0),
                                jnp.where(lane_q == 2, NEG_INF, 0.0))).astype(BF16)
    low = lane_k < HEAD_DIM

    for slab in range(KV_WIDTH // LANES):
        ks = k32[:, slab * LANES:(slab + 1) * LANES]
        kr = pltpu.roll(ks, HEAD_DIM, 1)
        for jj in range(2):
            j = 2 * slab + jj
            in_low = jnp.where(low, ks if jj == 0 else kr, 0.0).astype(BF16)
            in_high = jnp.where(low, 0.0, kr if jj == 0 else ks).astype(BF16)
            k_ext = (jnp.concatenate([in_low, k_ind], axis=1), jnp.concatenate([in_high, k_ind], axis=1))
            for g in range(Q_PER_KV):
                h = Q_PER_KV * j + g
                pair = h // 2
                q_ext = jnp.concatenate([q[:, pair * LANES:(pair + 1) * LANES], q_msk], axis=1)
                st_ref[h] = _dot_nt(k_ext[h % 2], q_ext)

    for h in range(N_HEADS):
        st = st_ref[h]
        sink = sinks_ref[h]
        m = jnp.maximum(jnp.max(st, axis=0, keepdims=True), sink)
        pt = jnp.exp(st - m)
        rl_ref[h:h + 1, :] = 1.0 / (jnp.sum(pt, axis=0, keepdims=True) + jnp.exp(sink - m))
        pt_ref[h] = pt.astype(BF16)

    for pair in range(N_HEADS // 2):
        pieces = []
        for h in (2 * pair, 2 * pair + 1):
            j = h // Q_PER_KV
            pieces.append(_dot(vt[j * HEAD_DIM:(j + 1) * HEAD_DIM, :], pt_ref[h]) * rl_ref[h:h + 1, :])
        o_ref[:, pair * LANES:(pair + 1) * LANES] = jnp.concatenate(pieces, axis=0).T.astype(o_ref.dtype)


def _attn_prompt(q, k, kv, sinks, bsz, seq):
    tq = 2 * CHUNK
    nb = seq // tq

    def kv_spec(back, col):
        return pl.BlockSpec((tq, KV_WIDTH), lambda b, i, s: (b * nb + jnp.maximum(i - back, 0), col))

    grid_spec = pltpu.PrefetchScalarGridSpec(
        num_scalar_prefetch=1,
        grid=(bsz, nb),
        in_specs=[pl.BlockSpec((tq, ATTN_WIDTH), lambda b, i, s: (b * nb + i, 0)),
                  kv_spec(1, 0), kv_spec(0, 0), kv_spec(1, 1), kv_spec(0, 1)],
        out_specs=pl.BlockSpec((tq, ATTN_WIDTH), lambda b, i, s: (b * nb + i, 0)),
        scratch_shapes=[pltpu.VMEM((N_HEADS, 2 * tq, tq), F32), pltpu.VMEM((N_HEADS, 2 * tq, tq), BF16),
                        pltpu.VMEM((N_HEADS, tq), F32)],
    )
    return pl.pallas_call(
        _attn_prompt_kernel,
        grid_spec=grid_spec,
        out_shape=jax.ShapeDtypeStruct((bsz * seq, ATTN_WIDTH), BF16),
        compiler_params=_cparams(("parallel", "parallel")),
        name="attn_prompt",
    )(sinks, q, k, k, kv, kv)


def _attn_sample_kernel(sinks_ref, q_ref, kn_ref, vn_ref, kc_ref, vc_ref, o_ref, s_ref, p_ref, l_ref):
    kk = jnp.concatenate([kc_ref[0], kn_ref[...]], axis=0).astype(BF16)
    vv = jnp.concatenate([vc_ref[0], vn_ref[...]], axis=0).astype(BF16)
    o_ref[...] = _attend(q_ref[...], kk, vv, sinks_ref, s_ref, p_ref, l_ref).astype(o_ref.dtype)


def _attn_sample(q, k, kv, cache_k, cache_v, sinks, row0, bsz, seq):
    blk0 = row0 // seq
    kv_len = cache_k.shape[1]
    new = lambda w, col: pl.BlockSpec((seq, w), lambda b, s: (blk0 + b, col))
    cache = pl.BlockSpec((1, kv_len, KV_WIDTH), lambda b, s: (b, 0, 0))
    grid_spec = pltpu.PrefetchScalarGridSpec(
        num_scalar_prefetch=1,
        grid=(bsz,),
        in_specs=[new(ATTN_WIDTH, 0), new(KV_WIDTH, 0), new(KV_WIDTH, 1), cache, cache],
        out_specs=pl.BlockSpec((seq, ATTN_WIDTH), lambda b, s: (b, 0)),
        scratch_shapes=[pltpu.VMEM((N_KV_HEADS, Q_PER_KV * seq, kv_len + seq), F32),
                        pltpu.VMEM((N_KV_HEADS, Q_PER_KV * seq, kv_len + seq), BF16),
                        pltpu.VMEM((N_KV_HEADS, Q_PER_KV * seq, LANES), F32)],
    )
    return pl.pallas_call(
        _attn_sample_kernel,
        grid_spec=grid_spec,
        out_shape=jax.ShapeDtypeStruct((bsz * seq, ATTN_WIDTH), BF16),
        compiler_params=_cparams(("parallel",)),
        name="attn_sample",
    )(sinks, q, k, kv, cache_k, cache_v)


def _pad_rows(x, rows):
    if x.shape[0] == rows:
        return x
    return jnp.concatenate([x, jnp.zeros((rows - x.shape[0], x.shape[1]), x.dtype)], axis=0)


def _ssd_kernel(*refs, clen, has_past):
    if has_past:
        (xbc_ref, dt_ref, z_ref, cw_ref, cb_ref, dtb_ref, alog_ref, dskip_ref, nw_ref, cpast_ref, hpast_ref,
         y_ref, hout_ref, xpad, u_ref, g_ref, ht_ref, ca_ref, xdt_ref, yst_ref, xw_ref, cb2_ref) = refs
    else:
        (xbc_ref, dt_ref, z_ref, cw_ref, cb_ref, dtb_ref, alog_ref, dskip_ref, nw_ref,
         y_ref, hout_ref, xpad, u_ref, g_ref, ht_ref, ca_ref, xdt_ref, yst_ref, xw_ref, cb2_ref) = refs
    L = clen
    c = pl.program_id(1)
    nc = pl.num_programs(1)
    n_pairs = SSM_HEADS // 2
    pairs_per_group = n_pairs // SSM_GROUPS
    gw = D_INNER // SSM_GROUPS

    @pl.when(c == 0)
    def _init():
        if has_past:
            for cidx in range(CONV_DIM // LANES):
                xpad[cidx, 5:8, :] = cpast_ref[0, :, cidx * LANES:(cidx + 1) * LANES]
            for p in range(n_pairs):
                both = jnp.concatenate([hpast_ref[0, 2 * p], hpast_ref[0, 2 * p + 1]], axis=0)
                ht_ref[:, p * LANES:(p + 1) * LANES] = both.T
        else:
            xpad[:, 0:8, :] = jnp.zeros((CONV_DIM // LANES, 8, LANES), F32)
            ht_ref[...] = jnp.zeros(ht_ref.shape, F32)

    n_col = CONV_DIM // LANES
    for cidx in range(n_col):
        xpad[cidx, 8:8 + L, :] = xbc_ref[:, cidx * LANES:(cidx + 1) * LANES]
    for cidx in range(n_col):
        sl = slice(cidx * LANES, (cidx + 1) * LANES)
        acc = cb_ref[:, sl] + xpad[cidx, 8:8 + L, :] * cw_ref[3:4, sl]
        acc = acc + xpad[cidx, 7:7 + L, :] * cw_ref[2:3, sl]
        acc = acc + xpad[cidx, 6:6 + L, :] * cw_ref[1:2, sl]
        acc = acc + xpad[cidx, 5:5 + L, :] * cw_ref[0:1, sl]
        u_ref[:, sl] = _silu(acc)
    for cidx in range(n_col):
        xpad[cidx, 5:8, :] = xpad[cidx, 5 + L:8 + L, :]

    dtx = dt_ref[...] + dtb_ref[...]
    dt = jnp.maximum(dtx, 0.0) + jnp.log(1.0 + jnp.exp(-jnp.abs(dtx)))
    loga = dt * (-jnp.exp(alog_ref[...]))
    t_idx = lax.broadcasted_iota(I32, (L, L), 0)
    s_idx = lax.broadcasted_iota(I32, (L, L), 1)
    incl = (s_idx <= t_idx).astype(BF16)
    acum = sum(_dot(incl, part) for part in _split3(loga))
    acum_t = _pad_rows(acum, LANES).T[:, 0:L]

    lane_m = lax.broadcasted_iota(I32, (L, 2 * L), 1)
    row_m = lax.broadcasted_iota(I32, (L, 2 * L), 0)
    left_m = lane_m < L
    causal_m = jnp.where(left_m, lane_m, lane_m - L) <= row_m
    left = lax.broadcasted_iota(I32, (L, LANES), 1) < SSM_HEAD_DIM
    b_of = lambda grp: u_ref[:, D_INNER + grp * D_STATE:D_INNER + (grp + 1) * D_STATE]
    c_of = lambda grp: u_ref[:, D_INNER + BC_WIDTH + grp * D_STATE:D_INNER + BC_WIDTH + (grp + 1) * D_STATE]
    pair_lanes = lambda p: slice(p * LANES, (p + 1) * LANES)
    group_lanes = lambda grp: slice(grp * gw, (grp + 1) * gw)

    for p in range(n_pairs):
        h0, h1, sl = 2 * p, 2 * p + 1, pair_lanes(p)
        ca_ref[:, sl] = jnp.where(left, acum[:, h0:h0 + 1], acum[:, h1:h1 + 1])
        xdt_ref[:, sl] = u_ref[:, sl] * jnp.where(left, dt[:, h0:h0 + 1], dt[:, h1:h1 + 1])

    for grp in range(SSM_GROUPS):
        c_bf = c_of(grp).astype(BF16)
        cb = _dot_nt(c_bf, b_of(grp).astype(BF16))
        cb2_ref[grp] = jnp.concatenate([cb, cb], axis=1)
        yst_ref[:, group_lanes(grp)] = _dot(c_bf, ht_ref[:, group_lanes(grp)].astype(BF16))

    for p in range(n_pairs):
        h0, h1, sl = 2 * p, 2 * p + 1, pair_lanes(p)
        ca = ca_ref[:, sl]
        ca_m = ca if L == SSM_HEAD_DIM else jnp.where(left_m, acum[:, h0:h0 + 1], acum[:, h1:h1 + 1])
        row_a = jnp.concatenate([acum_t[h0:h0 + 1, :], acum_t[h1:h1 + 1, :]], axis=1)
        dec = jnp.exp(jnp.where(causal_m, ca_m - row_a, NEG_INF))
        m_pair = (cb2_ref[p // pairs_per_group] * dec).astype(BF16)
        xdt = xdt_ref[:, sl]
        x_bd = jnp.concatenate([jnp.where(left, xdt, 0.0), jnp.where(left, 0.0, xdt)], axis=0).astype(BF16)
        y = _dot(m_pair, x_bd) + yst_ref[:, sl] * jnp.exp(ca) + u_ref[:, sl] * dskip_ref[:, sl]
        g_ref[:, sl] = y * _silu(z_ref[:, sl].astype(F32))
        xw_ref[:, sl] = (xdt * jnp.exp(ca[L - 1:L, :] - ca)).astype(BF16)

    for grp in range(SSM_GROUPS):
        gsl = group_lanes(grp)
        gg = g_ref[:, gsl]
        ms = jnp.mean(gg * gg, axis=-1, keepdims=True)
        y_ref[:, gsl] = (gg * lax.rsqrt(ms + EPS) * nw_ref[:, gsl]).astype(y_ref.dtype)
        bt_bf = _pad_rows(b_of(grp), LANES).T[:, 0:L].astype(BF16)
        ht_ref[:, gsl] = ht_ref[:, gsl] * jnp.exp(ca_ref[L - 1:L, gsl]) + _dot(bt_bf, xw_ref[:, gsl])

    @pl.when(c == nc - 1)
    def _fin():
        for p in range(n_pairs):
            both = ht_ref[:, p * LANES:(p + 1) * LANES].T
            hout_ref[0, 2 * p] = both[0:SSM_HEAD_DIM, :]
            hout_ref[0, 2 * p + 1] = both[SSM_HEAD_DIM:2 * SSM_HEAD_DIM, :]


def _ssd(xbc, dt_raw, z, conv_w, conv_b, dt_bias, a_log, d_skip, ssm_norm, row0, bsz, seq, clen,
         conv_past=None, ssm_past=None):
    nc = seq // clen
    blk0 = row0 // clen
    has_past = conv_past is not None
    pad = lambda a: jnp.pad(a.astype(F32), (0, LANES - SSM_HEADS)).reshape(1, LANES)
    params = [conv_w.astype(F32), conv_b.astype(F32).reshape(1, CONV_DIM), pad(dt_bias), pad(a_log),
              jnp.repeat(d_skip.astype(F32), SSM_HEAD_DIM).reshape(1, D_INNER), ssm_norm.astype(F32).reshape(1, D_INNER)]
    rows = lambda w: pl.BlockSpec((clen, w), lambda b, c: (blk0 + b * nc + c, 0))
    full = lambda a: pl.BlockSpec(a.shape, lambda b, c: (0,) * a.ndim)
    in_specs = [rows(CONV_DIM), rows(LANES), rows(D_INNER)] + [full(a) for a in params]
    args = [xbc, dt_raw, z] + params
    if has_past:
        in_specs += [pl.BlockSpec((1, CONV_W - 1, CONV_DIM), lambda b, c: (b, 0, 0)),
                     pl.BlockSpec((1, SSM_HEADS, SSM_HEAD_DIM, D_STATE), lambda b, c: (b, 0, 0, 0))]
        args += [conv_past.astype(F32), ssm_past.astype(F32)]
    return pl.pallas_call(
        functools.partial(_ssd_kernel, clen=clen, has_past=has_past),
        grid=(bsz, nc),
        in_specs=in_specs,
        out_specs=[pl.BlockSpec((clen, D_INNER), lambda b, c: (b * nc + c, 0)),
                   pl.BlockSpec((1, SSM_HEADS, SSM_HEAD_DIM, D_STATE), lambda b, c: (b, 0, 0, 0))],
        out_shape=[jax.ShapeDtypeStruct((bsz * seq, D_INNER), BF16),
                   jax.ShapeDtypeStruct((bsz, SSM_HEADS, SSM_HEAD_DIM, D_STATE), F32)],
        scratch_shapes=[pltpu.VMEM((CONV_DIM // LANES, 8 + clen, LANES), F32), pltpu.VMEM((clen, CONV_DIM), F32),
                        pltpu.VMEM((clen, D_INNER), F32), pltpu.VMEM((D_STATE, D_INNER), F32),
                        pltpu.VMEM((clen, D_INNER), F32), pltpu.VMEM((clen, D_INNER), F32),
                        pltpu.VMEM((clen, D_INNER), F32), pltpu.VMEM((clen, D_INNER), BF16),
                        pltpu.VMEM((SSM_GROUPS, clen, 2 * clen), F32)],
        compiler_params=_cparams(("parallel", "arbitrary")),
        name="ssd_past" if has_past else "ssd_prompt",
    )(*args)


def _merge_kernel(h_ref, a1_ref, a2_ref, s1_ref, s2_ref, wga_ref, wgs_ref, bga_ref, bgs_ref, wa_ref, ws_ref, o_ref,
                  *, n_first):
    def body(a_ref, s_ref):
        h = h_ref[...]
        g_a = jax.nn.sigmoid(_dot(h, wga_ref[...]) + bga_ref[...])
        g_s = jax.nn.sigmoid(_dot(h, wgs_ref[...]) + bgs_ref[...])
        o_ref[...] = (g_a * _dot(a_ref[...], wa_ref[...]) + g_s * _dot(s_ref[...], ws_ref[...])).astype(o_ref.dtype)

    _by_part(pl.program_id(0), n_first, body, (a1_ref, s1_ref), (a2_ref, s2_ref))


def _merge(h, attn_parts, ssm_parts, w_gate, b_gate, w_attn_up, w_ssm_up):
    m = h.shape[0]
    m1, m2 = attn_parts[0].shape[0], attn_parts[1].shape[0]
    tm = _pick(math.gcd(m1, m2), (512, 256, 128, 64, 32))
    nf = m1 // tm
    tn = 512
    nj = D_MODEL // tn
    bg = b_gate.astype(F32).reshape(1, 2 * D_MODEL)
    return pl.pallas_call(
        functools.partial(_merge_kernel, n_first=nf),
        grid=(m // tm, nj),
        in_specs=[pl.BlockSpec((tm, D_MODEL), lambda i, j: (i, 0)),
                  pl.BlockSpec((tm, ATTN_WIDTH), _first_rows(nf)), pl.BlockSpec((tm, ATTN_WIDTH), _later_rows(nf)),
                  pl.BlockSpec((tm, D_INNER), _first_rows(nf)), pl.BlockSpec((tm, D_INNER), _later_rows(nf)),
                  pl.BlockSpec((D_MODEL, tn), lambda i, j: (0, j)),
                  pl.BlockSpec((D_MODEL, tn), lambda i, j: (0, j + nj)),
                  pl.BlockSpec((1, tn), lambda i, j: (0, j)),
                  pl.BlockSpec((1, tn), lambda i, j: (0, j + nj)),
                  pl.BlockSpec((ATTN_WIDTH, tn), lambda i, j: (0, j)),
                  pl.BlockSpec((D_INNER, tn), lambda i, j: (0, j))],
        out_specs=pl.BlockSpec((tm, tn), lambda i, j: (i, j)),
        out_shape=jax.ShapeDtypeStruct((m, D_MODEL), BF16),
        compiler_params=_cparams(("parallel", "parallel")),
        name="merge",
    )(h, attn_parts[0], attn_parts[1], ssm_parts[0], ssm_parts[1], w_gate, w_gate, bg, bg, w_attn_up, w_ssm_up)


def _outproj_router_kernel(mg_ref, wo_ref, xa_ref, xb_ref, n2_ref, wr_ref, br_ref, x1_ref, h2_ref, idx_ref, gate_ref,
                           *, n_first):
    def residual(x_ref):
        x1_ref[...] = x_ref[...] + _dot(mg_ref[...], wo_ref[...])

    _by_part(pl.program_id(0), n_first, residual, (xa_ref,), (xb_ref,))
    x1 = x1_ref[...]
    ms = jnp.mean(x1 * x1, axis=-1, keepdims=True)
    h2 = x1 * lax.rsqrt(ms + EPS) * n2_ref[...]
    h2_ref[...] = _pack_bf16_pairs(h2)
    h_hi, h_lo = _split2(h2)
    w_hi, w_lo = _split2(wr_ref[...])
    logits = _dot(h_hi, w_hi) + (_dot(h_hi, w_lo) + _dot(h_lo, w_hi)) + br_ref[...]
    tm = logits.shape[0]
    lane = lax.broadcasted_iota(I32, (tm, LANES), 1)
    logits = jnp.where(lane < N_EXPERTS, logits, -jnp.inf)
    idx_out = jnp.zeros((tm, LANES), I32)
    val_out = jnp.zeros((tm, LANES), F32)
    top = None
    for k in range(TOP_K):
        v = jnp.max(logits, axis=-1, keepdims=True)
        i = jnp.min(jnp.where(logits == v, lane, LANES), axis=-1, keepdims=True)
        if k == 0:
            top = v
        idx_out = jnp.where(lane == k, i, idx_out)
        val_out = jnp.where(lane == k, jnp.exp(v - top), val_out)
        logits = jnp.where(lane == i, -jnp.inf, logits)
    idx_ref[...] = idx_out
    gate_ref[...] = val_out / jnp.sum(val_out, axis=-1, keepdims=True)


def _outproj_router(merged, w_out, xa, xb, norm2, w_router, b_router):
    m = merged.shape[0]
    tm = _pick(math.gcd(xa.shape[0], xb.shape[0]), (512, 256, 128, 64, 32))
    nf = xa.shape[0] // tm
    wr = jnp.pad(w_router.astype(F32), ((0, 0), (0, LANES - N_EXPERTS)))
    br = jnp.pad(b_router.astype(F32), (0, LANES - N_EXPERTS)).reshape(1, LANES)
    row = lambda w: pl.BlockSpec((tm, w), lambda i: (i, 0))
    full = lambda r, c: pl.BlockSpec((r, c), lambda i: (0, 0))
    return pl.pallas_call(
        functools.partial(_outproj_router_kernel, n_first=nf),
        grid=(m // tm,),
        in_specs=[row(D_MODEL), full(D_MODEL, D_MODEL),
                  pl.BlockSpec((tm, D_MODEL), _first_rows(nf)), pl.BlockSpec((tm, D_MODEL), _later_rows(nf)),
                  full(1, D_MODEL), full(D_MODEL, LANES), full(1, LANES)],
        out_specs=[row(D_MODEL), row(D_MODEL // 2), row(LANES), row(LANES)],
        out_shape=[jax.ShapeDtypeStruct((m, D_MODEL), F32), jax.ShapeDtypeStruct((m, D_MODEL // 2), I32),
                   jax.ShapeDtypeStruct((m, LANES), I32), jax.ShapeDtypeStruct((m, LANES), F32)],
        compiler_params=_cparams(("parallel",)),
        name="outproj_router",
    )(merged, w_out, xa, xb, norm2.astype(F32).reshape(1, D_MODEL), wr, br)


def _rank_kernel(idx_ref, rank_ref, cnt_ref, base_ref):
    i = pl.program_id(0)
    tt = idx_ref.shape[0]

    @pl.when(i == 0)
    def _():
        base_ref[...] = jnp.zeros(base_ref.shape, F32)

    idx = idx_ref[...]
    lane = lax.broadcasted_iota(I32, (tt, LANES), 1)
    sel = [lane == idx[:, k:k + 1] for k in range(TOP_K)]
    onehot = jnp.zeros((tt, LANES), F32)
    for k in range(TOP_K):
        onehot = jnp.where(sel[k], 1.0, onehot)
    r_idx = lax.broadcasted_iota(I32, (tt, tt), 0)
    c_idx = lax.broadcasted_iota(I32, (tt, tt), 1)
    before = (c_idx < r_idx).astype(BF16)
    rank_all = _dot(before, onehot.astype(BF16)) + base_ref[0:1, :]
    out = jnp.zeros((tt, LANES), F32)
    for k in range(TOP_K):
        out = jnp.where(lane == k, jnp.sum(jnp.where(sel[k], rank_all, 0.0), axis=-1, keepdims=True), out)
    rank_ref[...] = out.astype(I32)
    base_ref[0:1, :] = base_ref[0:1, :] + jnp.sum(onehot, axis=0, keepdims=True)
    cnt_ref[...] = base_ref[...].astype(I32)


def _ranks(idx):
    m = idx.shape[0]
    tt = _pick(m, (512, 256, 128, 64, 32))
    return pl.pallas_call(
        _rank_kernel,
        grid=(m // tt,),
        in_specs=[pl.BlockSpec((tt, LANES), lambda i: (i, 0))],
        out_specs=[pl.BlockSpec((tt, LANES), lambda i: (i, 0)), pl.BlockSpec((8, LANES), lambda i: (0, 0))],
        out_shape=[jax.ShapeDtypeStruct((m, LANES), I32), jax.ShapeDtypeStruct((8, LANES), I32)],
        scratch_shapes=[pltpu.VMEM((8, LANES), F32)],
        compiler_params=_cparams(("arbitrary",)),
        name="expert_ranks",
    )(idx)


def _dispatch_kernel(dest_ref, fill_ref, h_ref, xs_hbm, zero_ref, sem, zsem, *, tt, tm, n_blocks):
    @pl.when(pl.program_id(0) == 0)
    def _():
        zero_ref[...] = jnp.zeros(zero_ref.shape, zero_ref.dtype)

        def block_copy(b):
            return pltpu.make_async_copy(zero_ref, xs_hbm.at[pl.ds(pl.multiple_of(b * tm, tm), tm), :], zsem)

        def start(b, carry):
            @pl.when(fill_ref[b] != 0)
            def _():
                block_copy(b).start()

            return carry

        def done(b, carry):
            @pl.when(fill_ref[b] != 0)
            def _():
                block_copy(b).wait()

            return carry

        lax.fori_loop(0, n_blocks, start, 0)
        lax.fori_loop(0, n_blocks, done, 0)

    base = pl.program_id(0) * (tt * TOP_K)

    for r in range(tt):
        for k in range(TOP_K):
            dst = dest_ref[base + r * TOP_K + k]
            pltpu.make_async_copy(h_ref.at[pl.ds(r, 1), :], xs_hbm.at[pl.ds(dst, 1), :], sem).start(priority=k % 2)
    for _ in range(TOP_K):
        pltpu.make_async_copy(h_ref, xs_hbm.at[pl.ds(0, tt), :], sem).wait()


def _dispatch(dest_flat, fill_block, h2, tm):
    m = h2.shape[0]
    n_blocks = fill_block.shape[0]
    tt = _pick(m, (256, 128, 64, 32))
    width = h2.shape[1]
    grid_spec = pltpu.PrefetchScalarGridSpec(
        num_scalar_prefetch=2,
        grid=(m // tt,),
        in_specs=[pl.BlockSpec((tt, width), lambda i, d, f: (i, 0))],
        out_specs=pl.BlockSpec(memory_space=pl.ANY),
        scratch_shapes=[pltpu.VMEM((tm, width), h2.dtype), pltpu.SemaphoreType.DMA, pltpu.SemaphoreType.DMA],
    )
    return pl.pallas_call(
        functools.partial(_dispatch_kernel, tt=tt, tm=tm, n_blocks=n_blocks),
        grid_spec=grid_spec,
        out_shape=jax.ShapeDtypeStruct((n_blocks * tm, width), h2.dtype),
        compiler_params=_cparams(("arbitrary",)),
        name="dispatch",
    )(dest_flat, fill_block, h2)


def _pack_bf16_pairs(x):
    c = x.shape[1] // 2
    bits = lax.bitcast_convert_type(x.astype(BF16).astype(F32), I32)
    return lax.shift_right_logical(bits[:, :c], 16) | bits[:, c:]


def _unpack_bf16_pairs(p):
    lo = lax.bitcast_convert_type(lax.shift_left(p, 16), F32).astype(BF16)
    hi = lax.bitcast_convert_type(p & jnp.int32(-65536), F32).astype(BF16)
    return lo, hi


def _expert_kernel(be_ref, nu_ref, rows_ref, xs_ref, bg_ref, bu_ref, bd_ref, wg_hbm, wu_hbm, wd_hbm, ys_ref,
                   cg, cu, cd, stg_g, stg_u, stg_d, xb_ref, sems, *, tf):
    i = pl.program_id(0)
    nj = D_FF // tf
    half = D_MODEL // 2
    e = be_ref[i]
    first = jnp.logical_or(i == 0, e != be_ref[jnp.maximum(i - 1, 0)])
    e_next = be_ref[jnp.minimum(i + 1, pl.num_programs(0) - 1)]
    next_is_new = jnp.logical_and(i + 1 < nu_ref[0], e_next != e)

    def tile_copies(t, slot, expert=e):
        cols = pl.ds(pl.multiple_of(t * tf, tf), tf)
        return (pltpu.make_async_copy(wg_hbm.at[expert, :, cols], stg_g.at[slot], sems.at[slot, 0]),
                pltpu.make_async_copy(wu_hbm.at[expert, :, cols], stg_u.at[slot], sems.at[slot, 1]),
                pltpu.make_async_copy(wd_hbm.at[expert, cols, :], stg_d.at[slot], sems.at[slot, 2]))

    @pl.when(i < nu_ref[0])
    def _():
        @pl.when(i == 0)
        def _():
            for cp in tile_copies(0, 0):
                cp.start()

        lo, hi = _unpack_bf16_pairs(xs_ref[...])
        xb_ref[:, :half] = lo
        xb_ref[:, half:] = hi

        def refill(t):
            slot = t % 2

            @pl.when(t + 1 < nj)
            def _():
                for cp in tile_copies(t + 1, 1 - slot):
                    cp.start()

            for cp in tile_copies(t, slot):
                cp.wait()
            cg[t] = stg_g[slot].astype(BF16)
            cu[t] = stg_u[slot].astype(BF16)
            cd[t] = stg_d[slot].astype(BF16)

        def compute(t, opening):
            xb = xb_ref[...]
            g = jnp.minimum(_dot(xb, cg[t]) + bg_ref[0, t], SWIGLU_LIMIT)
            u = jnp.clip(_dot(xb, cu[t]) + bu_ref[0, t], -SWIGLU_LIMIT, SWIGLU_LIMIT)
            act = (u + 1.0) * (g * jax.nn.sigmoid(SWIGLU_ALPHA * g))
            part = _dot(act.astype(BF16), cd[t])
            if opening:
                ys_ref[...] = part + bd_ref[0]
            else:
                ys_ref[...] += part

        @pl.when(first)
        def _():
            refill(jnp.int32(0))
            compute(0, True)

            def step(t, carry):
                refill(t)
                compute(t, False)
                return carry

            lax.fori_loop(1, nj, step, 0)

        def compute_wide(t, opening, rows):
            xb = xb_ref[0:rows, :]
            span = range(t, t + CACHED_TILES)
            wide = lambda ref: jnp.concatenate([ref[k] for k in span], axis=1)
            bias = lambda ref: jnp.concatenate([ref[0, k] for k in span], axis=1)
            g = jnp.minimum(_dot(xb, wide(cg)) + bias(bg_ref), SWIGLU_LIMIT)
            u = jnp.clip(_dot(xb, wide(cu)) + bias(bu_ref), -SWIGLU_LIMIT, SWIGLU_LIMIT)
            act = (u + 1.0) * (g * jax.nn.sigmoid(SWIGLU_ALPHA * g))
            part = _dot(act.astype(BF16), jnp.concatenate([cd[k] for k in span], axis=0))
            if opening:
                ys_ref[0:rows, :] = part + bd_ref[0]
            else:
                ys_ref[0:rows, :] += part

        tm = ys_ref.shape[0]
        short = rows_ref[i] <= tm // 2

        @pl.when(jnp.logical_and(jnp.logical_not(first), jnp.logical_not(short)))
        def _():
            for t in range(0, nj, CACHED_TILES):
                compute_wide(t, t == 0, tm)

        @pl.when(jnp.logical_and(jnp.logical_not(first), short))
        def _():
            ys_ref[tm // 2:, :] = jnp.zeros((tm - tm // 2, D_MODEL), F32)
            for t in range(0, nj, CACHED_TILES):
                compute_wide(t, t == 0, tm // 2)

        @pl.when(next_is_new)
        def _():
            for cp in tile_copies(0, 0, e_next):
                cp.start()

    @pl.when(i >= nu_ref[0])
    def _():
        ys_ref[...] = jnp.zeros(ys_ref.shape, F32)


def _experts(block_e, n_used, block_rows, xs, w_e_gate, b_e_gate, w_e_up, b_e_up, w_e_down, b_e_down):
    n_rows = xs.shape[0]
    tm, tf = MOE_TM, MOE_TF
    n_blocks = n_rows // tm
    nj = D_FF // tf
    half = D_MODEL // 2

    def blk(i, nu):
        return jnp.minimum(i, nu[0] - 1)

    hbm = pl.BlockSpec(memory_space=pl.ANY)
    grid_spec = pltpu.PrefetchScalarGridSpec(
        num_scalar_prefetch=3,
        grid=(n_blocks,),
        in_specs=[pl.BlockSpec((tm, half), lambda i, be, nu, br: (blk(i, nu), 0)),
                  pl.BlockSpec((1, nj, 1, tf), lambda i, be, nu, br: (be[blk(i, nu)], 0, 0, 0)),
                  pl.BlockSpec((1, nj, 1, tf), lambda i, be, nu, br: (be[blk(i, nu)], 0, 0, 0)),
                  pl.BlockSpec((1, 1, D_MODEL), lambda i, be, nu, br: (be[blk(i, nu)], 0, 0)),
                  hbm, hbm, hbm],
        out_specs=pl.BlockSpec((tm, D_MODEL), lambda i, be, nu, br: (i, 0)),
        scratch_shapes=[pltpu.VMEM((nj, D_MODEL, tf), BF16), pltpu.VMEM((nj, D_MODEL, tf), BF16),
                        pltpu.VMEM((nj, tf, D_MODEL), BF16),
                        pltpu.VMEM((2, D_MODEL, tf), F32), pltpu.VMEM((2, D_MODEL, tf), F32),
                        pltpu.VMEM((2, tf, D_MODEL), F32),
                        pltpu.VMEM((tm, D_MODEL), BF16), pltpu.SemaphoreType.DMA((2, 3))],
    )
    return pl.pallas_call(
        functools.partial(_expert_kernel, tf=tf),
        grid_spec=grid_spec,
        out_shape=jax.ShapeDtypeStruct((n_rows, D_MODEL), F32),
        compiler_params=pltpu.CompilerParams(dimension_semantics=("arbitrary",), vmem_limit_bytes=EXPERT_VMEM_LIMIT),
        name="experts",
    )(block_e, n_used, block_rows, xs, b_e_gate.reshape(N_EXPERTS, nj, 1, tf), b_e_up.reshape(N_EXPERTS, nj, 1, tf),
      b_e_down.reshape(N_EXPERTS, 1, D_MODEL), w_e_gate, w_e_up, w_e_down)


def _combine_kernel(dest_ref, x1_ref, gate_ref, ys_hbm, o_ref, buf, sems, *, tt, row0):
    i = pl.program_id(0)
    n_steps = pl.num_programs(0)
    slot = i % 2

    def gather(step, to_slot):
        base = (row0 + step * tt) * TOP_K

        for r in range(tt):
            for k in range(TOP_K):
                src = dest_ref[base + r * TOP_K + k]
                pltpu.make_async_copy(ys_hbm.at[pl.ds(src, 1), :], buf.at[to_slot, k, pl.ds(r, 1), :],
                                      sems.at[to_slot]).start(priority=k % 2)

    @pl.when(i == 0)
    def _():
        gather(0, 0)

    @pl.when(jnp.logical_and(i + 1 < n_steps, slot == 0))
    def _():
        gather(i + 1, 1)

    @pl.when(jnp.logical_and(i + 1 < n_steps, slot == 1))
    def _():
        gather(i + 1, 0)

    for k in range(TOP_K):
        pltpu.make_async_copy(ys_hbm.at[pl.ds(0, tt), :], buf.at[slot, k], sems.at[slot]).wait()
    gate = gate_ref[...]
    acc = x1_ref[...]
    for k in range(TOP_K):
        acc = acc + gate[:, k:k + 1] * buf[slot, k]
    o_ref[...] = acc


def _combine(dest_flat, x1, gate, ys, row0, rows):
    tt = _pick(math.gcd(row0, rows), (256, 128, 64, 32))
    blk0 = row0 // tt
    grid_spec = pltpu.PrefetchScalarGridSpec(
        num_scalar_prefetch=1,
        grid=(rows // tt,),
        in_specs=[pl.BlockSpec((tt, D_MODEL), lambda i, d: (blk0 + i, 0)),
                  pl.BlockSpec((tt, LANES), lambda i, d: (blk0 + i, 0)),
                  pl.BlockSpec(memory_space=pl.ANY)],
        out_specs=pl.BlockSpec((tt, D_MODEL), lambda i, d: (i, 0)),
        scratch_shapes=[pltpu.VMEM((2, TOP_K, tt, D_MODEL), F32), pltpu.SemaphoreType.DMA((2,))],
    )
    return pl.pallas_call(
        functools.partial(_combine_kernel, tt=tt, row0=row0),
        grid_spec=grid_spec,
        out_shape=jax.ShapeDtypeStruct((rows, D_MODEL), F32),
        compiler_params=_cparams(("arbitrary",)),
        name="combine",
    )(dest_flat, x1, gate, ys)


def _moe(x1, h2, idx, gate, w_e_gate, b_e_gate, w_e_up, b_e_up, w_e_down, b_e_down, n_first):
    m = x1.shape[0]
    tm = MOE_TM
    ranks, cnt = _ranks(idx)
    counts = cnt[0, :N_EXPERTS]
    padded = (counts + tm - 1) // tm * tm
    pad_end = jnp.cumsum(padded)
    pad_start = pad_end - padded
    e_sel = idx[:, :TOP_K]
    dest = (pad_start[e_sel] + ranks[:, :TOP_K]).astype(I32).reshape(-1)
    n_blocks = -(-(m * TOP_K + N_EXPERTS * (tm - 1)) // tm)
    block_row = jnp.arange(n_blocks, dtype=I32) * tm
    block_e = jnp.minimum(jnp.sum((pad_end[None, :] <= block_row[:, None]).astype(I32), axis=1), N_EXPERTS - 1)
    n_used = (pad_end[-1] // tm).astype(I32).reshape(1)
    is_last = jnp.sum((pad_end[None, :] == block_row[:, None] + tm).astype(I32), axis=1) > 0
    fill_block = jnp.logical_or(is_last, block_row >= pad_end[-1]).astype(I32)
    xs = _dispatch(dest, fill_block, h2, tm)
    block_rows = jnp.clip((pad_start + counts)[block_e] - block_row, 0, tm).astype(I32)
    ys = _experts(block_e, n_used, block_rows, xs, w_e_gate, b_e_gate, w_e_up, b_e_up, w_e_down, b_e_down)
    return _combine(dest, x1, gate, ys, 0, n_first), _combine(dest, x1, gate, ys, n_first, m - n_first)


def kernel(x_prompt, x_sample, cache_k, cache_v, state_conv, state_ssm, norm1, w_in, q_norm, k_norm, sinks, conv_w,
           conv_b, dt_bias, a_log, d_skip, ssm_norm, w_gate, b_gate, w_attn_up, w_ssm_up, w_out, norm2, w_router,
           b_router, w_e_gate, b_e_gate, w_e_up, b_e_up, w_e_down, b_e_down):
    bp, sp, _ = x_prompt.shape
    bs, ss, _ = x_sample.shape
    tp, ts = bp * sp, bs * ss
    kv_len = cache_k.shape[2]
    l = 0

    xa, xb = x_prompt.reshape(tp, D_MODEL), x_sample.reshape(ts, D_MODEL)

    o1 = ATTN_WIDTH
    o2 = o1 + 2 * KV_WIDTH
    o3 = o2 + D_INNER
    o4 = o3 + CONV_DIM
    h = _rmsnorm(xa, xb, norm1[l], BF16)
    w_in_t = jnp.swapaxes(w_in, 1, 2)
    q_p = _matmul(h, w_in_t, l, 0, o1, F32, "proj_q")
    kv_p = _matmul(h, w_in_t, l, o1, o2 - o1, F32, "proj_kv")
    z = _matmul(h, w_in_t, l, o2, o3 - o2, BF16, "proj_z")
    xbc = _matmul(h, w_in_t, l, o3, o4 - o3, F32, "proj_xbc")
    w_dt = jnp.pad(w_in_t[l:l + 1, o4:, :], ((0, 0), (0, LANES - SSM_HEADS), (0, 0)))
    dt_raw = _matmul(h, w_dt, 0, 0, LANES, F32, "proj_dt")

    q_rot, k_rot = _qk_prep(q_p, kv_p, q_norm[l], k_norm[l], tp, sp, ss)
    sk = sinks[l].astype(F32)
    attn_p = _attn_prompt(q_rot, k_rot, kv_p, sk, bp, sp)
    ck = cache_k[l].reshape(bs, kv_len, KV_WIDTH)
    cv = cache_v[l].reshape(bs, kv_len, KV_WIDTH)
    attn_s = _attn_sample(q_rot, k_rot, kv_p, ck, cv, sk, tp, bs, ss)

    ssd_w = (conv_w[l], conv_b[l], dt_bias[l], a_log[l], d_skip[l], ssm_norm[l])
    ssm_p, hfin_p = _ssd(xbc, dt_raw, z, *ssd_w, 0, bp, sp, CHUNK)
    ssm_s, hfin_s = _ssd(xbc, dt_raw, z, *ssd_w, tp, bs, ss, ss, state_conv[l], state_ssm[l])

    merged = _merge(h, (attn_p, attn_s), (ssm_p, ssm_s), w_gate[l].astype(BF16), b_gate[l], w_attn_up[l].astype(BF16), w_ssm_up[l].astype(BF16))
    x1, h2, idx, gate = _outproj_router(merged, w_out[l].astype(BF16), xa, xb, norm2[l], w_router[l], b_router[l])
    y_p, y_s = _moe(x1, h2, idx, gate, w_e_gate[l], b_e_gate[l], w_e_up[l], b_e_up[l], w_e_down[l], b_e_down[l], tp)

    def prompt_tail(a, n):
        return jnp.stack([a[(b + 1) * sp - n:(b + 1) * sp] for b in range(bp)])

    heads = lambda a: a.reshape(a.shape[0], a.shape[1], N_KV_HEADS, HEAD_DIM)
    new_k_p = heads(prompt_tail(k_rot, WINDOW))
    new_v_p = heads(prompt_tail(kv_p, WINDOW)[:, :, KV_WIDTH:])
    ks4 = heads(k_rot[tp:].reshape(bs, ss, KV_WIDTH))
    vs4 = heads(kv_p[tp:, KV_WIDTH:].reshape(bs, ss, KV_WIDTH))
    new_k_s = jnp.concatenate([cache_k[l], ks4], axis=1)[:, -kv_len:]
    new_v_s = jnp.concatenate([cache_v[l], vs4], axis=1)[:, -kv_len:]
    xbc_s = jnp.concatenate([state_conv[l], xbc[tp:].reshape(bs, ss, CONV_DIM)], axis=1)
    return (y_p.reshape(bp, sp, D_MODEL), y_s.reshape(bs, ss, D_MODEL),
            new_k_p[None], new_v_p[None], prompt_tail(xbc, CONV_W - 1)[None], hfin_p[None],
            new_k_s[None], new_v_s[None], xbc_s[:, -(CONV_W - 1):][None], hfin_s[None])
```
